```python
import math
import jax, jax.numpy as jnp
from jax import lax
import numpy as np

D_MODEL = 1024
BATCH = 16
SEQ = 256
DEPTH = 2
DEC_BATCH = 2
DEC_SEQ = 1024
PAST_LEN = 512

GRID_W = 64
POS_BASE = 10000.0
EPS = 1e-6
N_DIR = 2
HY_CH = 256
HY_ORDER = 2
HY_EMB = 33
HY_FILTER_W = 64
ML_HEADS = 4
ML_DH = 128
ML_W = ML_HEADS * ML_DH
ML_CHUNK = 64
ML_GATES = 2 * N_DIR * ML_HEADS
S5_CH = 256
S5_GROUP = 16
S5_G = S5_CH // S5_GROUP
S5_P = 64
MIX_W = HY_CH + ML_W + S5_CH
SHORT_W = 3
OFF_HY = 0
OFF_QK = OFF_HY + (HY_ORDER + 1) * HY_CH
OFF_V = OFF_QK + 2 * ML_W
OFF_O = OFF_V + ML_W
OFF_G = OFF_O + ML_W
OFF_S5 = OFF_G + ML_GATES
IN_W = OFF_S5 + S5_CH
N_EXPERTS = 16
N_EXPERT_GROUPS = 4
EXPERTS_PER_GROUP = N_EXPERTS // N_EXPERT_GROUPS
TOP_K = 2
D_EXPERT = 512

F32 = jnp.float32

kernel_name = "hybrid_hyena_mlstm_s5_moe_diffusion_step"


def rms_norm(x, g):
    xf = x.astype(F32)
    y = xf * lax.rsqrt(jnp.mean(xf * xf, axis=-1, keepdims=True) + EPS)
    return (y * g.astype(F32)).astype(x.dtype)


def centred_dwconv(x, w):
    k = w.shape[0]
    pad = k // 2
    L = x.shape[1]
    xp = jnp.pad(x, ((0, 0), (pad, pad), (0, 0)))
    out = xp[:, 0:L] * w[0]
    for j in range(1, k):
        out = out + xp[:, j:j + L] * w[j]
    return out


def grid_pos_embed(n_tokens, dtype):
    rows = n_tokens // GRID_W
    r = jnp.repeat(jnp.arange(rows, dtype=F32), GRID_W)
    col = jnp.tile(jnp.arange(GRID_W, dtype=F32), rows)
    quarter = D_MODEL // 4
    freq = jnp.exp(-math.log(POS_BASE) * jnp.arange(quarter, dtype=F32) / quarter)
    ar = r[:, None] * freq[None]
    ac = col[:, None] * freq[None]
    emb = jnp.concatenate([jnp.sin(ar), jnp.cos(ar), jnp.sin(ac), jnp.cos(ac)], axis=-1)
    return emb.astype(dtype)


def hyena_filters(L, fw1, fb1, fw2, fb2, fw3, log_decay):
    t = jnp.linspace(0.0, 1.0, L, dtype=F32)
    bands = (HY_EMB - 1) // 2
    f = jnp.linspace(1e-4, bands - 1, bands, dtype=F32)
    w = 2.0 * math.pi * jnp.arange(L, dtype=F32) / L
    ang = w[:, None] * f[None, :]
    z = jnp.concatenate([t[:, None], jnp.cos(ang), -jnp.sin(ang)], axis=-1)
    hdn = jnp.sin(z @ fw1.astype(F32) + fb1.astype(F32))
    hdn = jnp.sin(hdn @ fw2.astype(F32) + fb2.astype(F32))
    filt = (hdn @ fw3.astype(F32)).reshape(L, HY_ORDER, N_DIR, HY_CH)
    window = jnp.exp(-t[:, None, None, None] * jnp.exp(log_decay.astype(F32))[None])
    return filt * window


def two_sided_filter(h_fwd, h_bwd):
    zero = jnp.zeros_like(h_fwd[:1])
    return jnp.concatenate([h_fwd, zero, h_bwd[1:][::-1]], axis=0)


def fft_long_conv(z, filt):
    L = z.shape[1]
    zf = jnp.fft.rfft(z, n=2 * L, axis=1)
    ff = jnp.fft.rfft(filt, n=2 * L, axis=0)
    return jnp.fft.irfft(zf * ff[None], n=2 * L, axis=1)[:, :L]


def hyena_mixer(u, short_w, fw1, fb1, fw2, fb2, fw3, log_decay, bias):
    L = u.shape[1]
    u = centred_dwconv(u.astype(F32), short_w.astype(F32))
    v, x1, x2 = jnp.split(u, 3, axis=-1)
    filt = hyena_filters(L, fw1, fb1, fw2, fb2, fw3, log_decay)
    bias = bias.astype(F32)
    z = v
    for o, gate in enumerate((x1, x2)):
        h = two_sided_filter(filt[:, o, 0], filt[:, o, 1])
        z = gate * (fft_long_conv(z, h) + bias[o] * z)
    return z


def mlstm_chunked(q, k, v, ig, lf, C0, n0, m0):
    B, H, L, Dh = q.shape
    T = ML_CHUNK
    nc = L // T

    def chunks(a):
        a = a.reshape(a.shape[:2] + (nc, T) + a.shape[3:])
        return jnp.moveaxis(a, 2, 0)

    causal = jnp.tril(jnp.ones((T, T), dtype=bool))

    def step(carry, inp):
        C, n, m = carry
        qc, kc, vc, ic, fc = inp
        b = jnp.cumsum(fc, axis=-1)
        d = jnp.where(causal, b[..., :, None] - b[..., None, :] + ic[..., None, :], -jnp.inf)
        inter = b + m[..., None]
        m_row = jnp.maximum(inter, jnp.max(d, axis=-1))
        w_intra = jnp.exp(d - m_row[..., None])
        w_state = jnp.exp(inter - m_row)
        s = jnp.einsum('bhtd,bhsd->bhts', qc, kc) * w_intra
        num = jnp.einsum('bhts,bhse->bhte', s, vc) + w_state[..., None] * jnp.einsum('bhtd,bhde->bhte', qc, C)
        den = jnp.sum(s, axis=-1) + w_state * jnp.einsum('bhtd,bhd->bht', qc, n)
        h = num / jnp.maximum(jnp.abs(den), jnp.exp(-m_row))[..., None]
        b_end = b[..., -1]
        g = b_end[..., None] - b + ic
        m_new = jnp.maximum(b_end + m, jnp.max(g, axis=-1))
        wg = jnp.exp(g - m_new[..., None])
        decay = jnp.exp(b_end + m - m_new)
        C = decay[..., None, None] * C + jnp.einsum('bhs,bhsd,bhse->bhde', wg, kc, vc)
        n = decay[..., None] * n + jnp.einsum('bhs,bhsd->bhd', wg, kc)
        return (C, n, m_new), h

    carry0 = (C0.astype(F32), n0.astype(F32), m0.astype(F32))
    (C, n, m), hs = lax.scan(step, carry0, (chunks(q), chunks(k), chunks(v), chunks(ig), chunks(lf)))
    h = jnp.moveaxis(hs, 0, 2).reshape(B, H, L, Dh)
    return h, C, n, m


def mlstm_mixer(qk, v, o, gates, short_w, gate_bias, norm_g, C0, n0, m0):
    B, L, _ = v.shape
    qk = jax.nn.silu(centred_dwconv(qk.astype(F32), short_w.astype(F32)))
    q, k = jnp.split(qk, 2, axis=-1)

    def heads(a):
        return a.reshape(B, L, ML_HEADS, ML_DH).transpose(0, 2, 1, 3)

    q = heads(q)
    k = heads(k) * (ML_DH ** -0.5)
    vv = heads(v.astype(F32))
    pre = gates.astype(F32).reshape(B, L, 2, N_DIR, ML_HEADS) + gate_bias.astype(F32)
    pre = pre.transpose(2, 3, 0, 4, 1)
    h_sum = jnp.zeros_like(q)
    fin_C, fin_n, fin_m = [], [], []
    for d in range(N_DIR):
        ig = pre[0, d]
        lf = jax.nn.log_sigmoid(pre[1, d])
        qd, kd, vd = q, k, vv
        if d == 1:
            qd, kd, vd = jnp.flip(qd, 2), jnp.flip(kd, 2), jnp.flip(vd, 2)
            ig, lf = jnp.flip(ig, -1), jnp.flip(lf, -1)
        h, C, n, m = mlstm_chunked(qd, kd, vd, ig, lf, C0[:, d], n0[:, d], m0[:, d])
        if d == 1:
            h = jnp.flip(h, 2)
        h_sum = h_sum + h
        fin_C.append(C)
        fin_n.append(n)
        fin_m.append(m)
    hn = h_sum * lax.rsqrt(jnp.mean(h_sum * h_sum, axis=-1, keepdims=True) + EPS)
    hn = hn.transpose(0, 2, 1, 3).reshape(B, L, ML_W) * norm_g.astype(F32)
    out = jax.nn.sigmoid(o.astype(F32)) * hn
    return out, jnp.stack(fin_C, axis=1), jnp.stack(fin_n, axis=1), jnp.stack(fin_m, axis=1)


def _ssm_combine(left, right):
    a1, b1 = left
    a2, b2 = right
    return a1 * a2, a2 * b1 + b2


def s5_mixer(u, a_re, a_im, log_dt, b_re, b_im, c_re, c_im, d_skip, w_glu, s0_re, s0_im):
    B, L, _ = u.shape
    uf = u.astype(F32)
    ug = uf.reshape(B, L, S5_G, S5_GROUP).astype(jnp.complex64)
    y = uf * d_skip.astype(F32)
    finals = []
    for d in range(N_DIR):
        lam = lax.complex(a_re[d].astype(F32), a_im[d].astype(F32))
        dt = jnp.exp(log_dt[d].astype(F32))[:, None]
        lam_bar = jnp.exp(lam * dt)
        b_bar = ((lam_bar - 1.0) / lam)[..., None] * lax.complex(b_re[d].astype(F32), b_im[d].astype(F32))
        cmat = lax.complex(c_re[d].astype(F32), c_im[d].astype(F32))
        s0 = lax.complex(s0_re[:, d].astype(F32), s0_im[:, d].astype(F32))
        ud = jnp.flip(ug, 1) if d == 1 else ug
        bu = jnp.einsum('gpc,blgc->blgp', b_bar, ud)
        a = jnp.broadcast_to(lam_bar, bu.shape)
        acc, st = lax.associative_scan(_ssm_combine, (a, bu), axis=1)
        st = st + acc * s0[:, None]
        yd = jnp.einsum('gcp,blgp->blgc', cmat, st).real.reshape(B, L, S5_CH)
        if d == 1:
            yd = jnp.flip(yd, 1)
        y = y + yd
        finals.append(st[:, -1])
    g = jax.nn.gelu(y)
    out = g * jax.nn.sigmoid(g @ w_glu.astype(F32))
    fin = jnp.stack(finals, axis=1)
    return out, jnp.real(fin), jnp.imag(fin)


def grouped_moe(h, router_w, router_b, w_gate, w_up, w_down):
    B, L, D = h.shape
    t = h.reshape(B * L, D)
    probs = jax.nn.softmax(t.astype(F32) @ router_w.astype(F32), axis=-1)
    sel = probs + router_b.astype(F32)
    grp = sel.reshape(-1, N_EXPERT_GROUPS, EXPERTS_PER_GROUP)
    grp_score = jnp.sum(lax.top_k(grp, TOP_K)[0], axis=-1)
    best = jnp.argmax(grp_score, axis=-1)
    in_group = jax.nn.one_hot(best, N_EXPERT_GROUPS, dtype=bool)[:, :, None]
    masked = jnp.where(in_group, grp, -jnp.inf).reshape(-1, N_EXPERTS)
    _, idx = lax.top_k(masked, TOP_K)
    w = jnp.take_along_axis(probs, idx, axis=-1)
    w = w / jnp.sum(w, axis=-1, keepdims=True)
    combine = jnp.sum(jax.nn.one_hot(idx, N_EXPERTS, dtype=F32) * w[..., None], axis=1)
    hid = jax.nn.silu(jnp.einsum('td,edf->tef', t, w_gate)) * jnp.einsum('td,edf->tef', t, w_up)
    hid = hid * combine[..., None].astype(hid.dtype)
    out = jnp.einsum('tef,efd->td', hid, w_down)
    return out.reshape(B, L, D)


def trunk_layer(x, mod, lp, router_w, router_b, st):
    shift1, scale1, gate1, shift2, scale2, gate2 = jnp.split(mod[:, None, :].astype(x.dtype), 6, axis=-1)
    h = rms_norm(x, lp['norm1']) * (1.0 + scale1) + shift1
    proj = jnp.einsum('bld,de->ble', h, lp['w_in'])
    y_hy = hyena_mixer(proj[..., OFF_HY:OFF_QK], lp['hy_short'], lp['hy_fw1'], lp['hy_fb1'],
                       lp['hy_fw2'], lp['hy_fb2'], lp['hy_fw3'], lp['hy_log_decay'], lp['hy_bias'])
    y_ml, fin_C, fin_n, fin_m = mlstm_mixer(proj[..., OFF_QK:OFF_V], proj[..., OFF_V:OFF_O],
                                            proj[..., OFF_O:OFF_G], proj[..., OFF_G:OFF_S5],
                                            lp['ml_short'], lp['ml_gate_bias'], lp['ml_norm_g'],
                                            st[0], st[1], st[2])
    y_s5, fin_re, fin_im = s5_mixer(proj[..., OFF_S5:IN_W], lp['s5_a_re'], lp['s5_a_im'], lp['s5_log_dt'],
                                    lp['s5_b_re'], lp['s5_b_im'], lp['s5_c_re'], lp['s5_c_im'],
                                    lp['s5_d'], lp['s5_w_glu'], st[3], st[4])
    mixed = jnp.concatenate([y_hy, y_ml, y_s5], axis=-1).astype(x.dtype)
    x = x + gate1 * jnp.einsum('ble,ed->bld', mixed, lp['w_out'])
    h = rms_norm(x, lp['norm2']) * (1.0 + scale2) + shift2
    x = x + gate2 * grouped_moe(h, router_w, router_b, lp['moe_w_gate'], lp['moe_w_up'], lp['moe_w_down'])
    return x, (fin_C, fin_n, fin_m, fin_re, fin_im)


def setup_inputs(seed: int = 0) -> dict:
    key = jax.random.key(seed)
    it = iter(jax.random.split(key, 64))

    def nrm(shape, scale):
        return scale * jax.random.normal(next(it), shape, F32)

    D = D_MODEL
    inp = {}
    inp['x_prompt'] = nrm((BATCH, SEQ, D), 1.0)
    inp['x_sample'] = nrm((DEC_BATCH, DEC_SEQ, D), 1.0)
    inp['c'] = nrm((DEC_BATCH, D), 1.0)
    inp['state_mlstm_C'] = nrm((DEC_BATCH, DEPTH, N_DIR, ML_HEADS, ML_DH, ML_DH), 0.1)
    inp['state_mlstm_n'] = nrm((DEC_BATCH, DEPTH, N_DIR, ML_HEADS, ML_DH), 0.1)
    inp['state_mlstm_m'] = nrm((DEC_BATCH, DEPTH, N_DIR, ML_HEADS), 0.5)
    inp['state_s5_re'] = nrm((DEC_BATCH, DEPTH, N_DIR, S5_G, S5_P), 0.3)
    inp['state_s5_im'] = nrm((DEC_BATCH, DEPTH, N_DIR, S5_G, S5_P), 0.3)
    inp['c_ctx'] = nrm((D,), 1.0)
    inp['w_ada'] = nrm((DEPTH, D, 6 * D), 0.5 * D ** -0.5)
    inp['b_ada'] = nrm((DEPTH, 6 * D), 0.02)
    inp['norm1_g'] = 1.0 + nrm((DEPTH, D), 0.02)
    inp['norm2_g'] = 1.0 + nrm((DEPTH, D), 0.02)
    inp['final_g'] = 1.0 + nrm((D,), 0.02)
    inp['w_in'] = nrm((DEPTH, D, IN_W), D ** -0.5)
    inp['w_out'] = nrm((DEPTH, MIX_W, D), MIX_W ** -0.5)
    inp['hy_short'] = nrm((DEPTH, SHORT_W, (HY_ORDER + 1) * HY_CH), SHORT_W ** -0.5)
    inp['hy_fw1'] = nrm((DEPTH, HY_EMB, HY_FILTER_W), 2.0 * HY_EMB ** -0.5)
    inp['hy_fb1'] = nrm((DEPTH, HY_FILTER_W), 0.1)
    inp['hy_fw2'] = nrm((DEPTH, HY_FILTER_W, HY_FILTER_W), 2.0 * HY_FILTER_W ** -0.5)
    inp['hy_fb2'] = nrm((DEPTH, HY_FILTER_W), 0.1)
    inp['hy_fw3'] = nrm((DEPTH, HY_FILTER_W, HY_ORDER * N_DIR * HY_CH), 0.004)
    inp['hy_log_decay'] = jnp.broadcast_to(jnp.linspace(math.log(0.5), math.log(30.0), HY_CH, dtype=F32),
                                           (DEPTH, HY_ORDER, N_DIR, HY_CH)) + nrm((DEPTH, HY_ORDER, N_DIR, HY_CH), 0.05)
    inp['hy_bias'] = nrm((DEPTH, HY_ORDER, HY_CH), 0.5)
    inp['ml_short'] = nrm((DEPTH, SHORT_W, 2 * ML_W), SHORT_W ** -0.5)
    inp['ml_gate_bias'] = jnp.stack([nrm((DEPTH, N_DIR, ML_HEADS), 0.1),
                                     jnp.linspace(3.0, 6.0, ML_HEADS, dtype=F32) + nrm((DEPTH, N_DIR, ML_HEADS), 0.1)],
                                    axis=1)
    inp['ml_norm_g'] = 1.0 + nrm((DEPTH, ML_W), 0.02)
    inp['s5_a_re'] = -0.5 + nrm((DEPTH, N_DIR, S5_G, S5_P), 0.01)
    inp['s5_a_im'] = math.pi * jnp.arange(S5_P, dtype=F32) + nrm((DEPTH, N_DIR, S5_G, S5_P), 0.01)
    inp['s5_log_dt'] = jax.random.uniform(next(it), (DEPTH, N_DIR, S5_G), F32, math.log(1e-3), math.log(1e-1))
    inp['s5_b_re'] = nrm((DEPTH, N_DIR, S5_G, S5_P, S5_GROUP), (2 * S5_GROUP) ** -0.5)
    inp['s5_b_im'] = nrm((DEPTH, N_DIR, S5_G, S5_P, S5_GROUP), (2 * S5_GROUP) ** -0.5)
    inp['s5_c_re'] = nrm((DEPTH, N_DIR, S5_G, S5_GROUP, S5_P), S5_P ** -0.5)
    inp['s5_c_im'] = nrm((DEPTH, N_DIR, S5_G, S5_GROUP, S5_P), S5_P ** -0.5)
    inp['s5_d'] = nrm((DEPTH, S5_CH), 1.0)
    inp['s5_w_glu'] = nrm((DEPTH, S5_CH, S5_CH), S5_CH ** -0.5)
    inp['router_w'] = nrm((D, N_EXPERTS), D ** -0.5)
    inp['router_b'] = nrm((N_EXPERTS,), 0.01)
    inp['moe_w_gate'] = nrm((DEPTH, N_EXPERTS, D, D_EXPERT), D ** -0.5)
    inp['moe_w_up'] = nrm((DEPTH, N_EXPERTS, D, D_EXPERT), D ** -0.5)
    inp['moe_w_down'] = nrm((DEPTH, N_EXPERTS, D_EXPERT, D), D_EXPERT ** -0.5)
    return inp


def reference(x_prompt, x_sample, c, state_mlstm_C, state_mlstm_n, state_mlstm_m, state_s5_re, state_s5_im,
              c_ctx, w_ada, b_ada, norm1_g, norm2_g, final_g, w_in, w_out,
              hy_short, hy_fw1, hy_fb1, hy_fw2, hy_fb2, hy_fw3, hy_log_decay, hy_bias,
              ml_short, ml_gate_bias, ml_norm_g,
              s5_a_re, s5_a_im, s5_log_dt, s5_b_re, s5_b_im, s5_c_re, s5_c_im, s5_d, s5_w_glu,
              router_w, router_b, moe_w_gate, moe_w_up, moe_w_down):
    bp = x_prompt.shape[0]
    ctx_state0 = (jnp.zeros((bp, N_DIR, ML_HEADS, ML_DH, ML_DH), F32),
                  jnp.zeros((bp, N_DIR, ML_HEADS, ML_DH), F32),
                  jnp.zeros((bp, N_DIR, ML_HEADS), F32),
                  jnp.zeros((bp, N_DIR, S5_G, S5_P), F32),
                  jnp.zeros((bp, N_DIR, S5_G, S5_P), F32))
    xp = x_prompt
    xs = x_sample + grid_pos_embed(x_sample.shape[1], x_sample.dtype)[None]
    new_C, new_n, new_m, new_re, new_im = [], [], [], [], []
    for l in range(DEPTH):
        lp = dict(norm1=norm1_g[l], norm2=norm2_g[l], w_in=w_in[l], w_out=w_out[l],
                  hy_short=hy_short[l], hy_fw1=hy_fw1[l], hy_fb1=hy_fb1[l], hy_fw2=hy_fw2[l],
                  hy_fb2=hy_fb2[l], hy_fw3=hy_fw3[l], hy_log_decay=hy_log_decay[l], hy_bias=hy_bias[l],
                  ml_short=ml_short[l], ml_gate_bias=ml_gate_bias[l], ml_norm_g=ml_norm_g[l],
                  s5_a_re=s5_a_re[l], s5_a_im=s5_a_im[l], s5_log_dt=s5_log_dt[l], s5_b_re=s5_b_re[l],
                  s5_b_im=s5_b_im[l], s5_c_re=s5_c_re[l], s5_c_im=s5_c_im[l], s5_d=s5_d[l],
                  s5_w_glu=s5_w_glu[l], moe_w_gate=moe_w_gate[l], moe_w_up=moe_w_up[l],
                  moe_w_down=moe_w_down[l])
        mod_ctx = (jax.nn.silu(c_ctx) @ w_ada[l] + b_ada[l])[None]
        mod_lat = jax.nn.silu(c) @ w_ada[l] + b_ada[l]
        xp, st = trunk_layer(xp, mod_ctx, lp, router_w, router_b, ctx_state0)
        new_C.append(st[0])
        new_n.append(st[1])
        new_m.append(st[2])
        new_re.append(st[3])
        new_im.append(st[4])
        lat_state0 = (state_mlstm_C[:, l], state_mlstm_n[:, l], state_mlstm_m[:, l],
                      state_s5_re[:, l], state_s5_im[:, l])
        xs, _ = trunk_layer(xs, mod_lat, lp, router_w, router_b, lat_state0)
    y_prompt = rms_norm(xp, final_g)
    y_sample = rms_norm(xs, final_g)
    new_mlstm_C = jnp.stack(new_C, axis=1)
    new_mlstm_n = jnp.stack(new_n, axis=1)
    new_mlstm_m = jnp.stack(new_m, axis=1)
    new_s5_re = jnp.stack(new_re, axis=1)
    new_s5_im = jnp.stack(new_im, axis=1)
    return (y_prompt, y_sample, new_mlstm_C, new_mlstm_n, new_mlstm_m, new_s5_re, new_s5_im)
```

```python
import functools
import math

import numpy as np
import jax
import jax.numpy as jnp
from jax import lax
from jax.experimental import pallas as pl
from jax.experimental.pallas import tpu as pltpu

F32 = jnp.float32
BF16 = jnp.bfloat16
HIGHEST = lax.Precision.HIGHEST

D = 1024
N_CTX_SEQ, L_CTX = 16, 256
N_LAT_SEQ, L_LAT = 2, 1024
T_CTX = N_CTX_SEQ * L_CTX
T_LAT = N_LAT_SEQ * L_LAT
T_ALL = T_CTX + T_LAT
DEPTH = 2
EPS = 1e-6
GRID_W = 64
POS_BASE = 10000.0
HY_CH = 256
HY_EMB = 33
HY_FILTER_W = 64
ML_HEADS = 4
ML_DH = 128
ML_W = ML_HEADS * ML_DH
S5_CH = 256
S5_G = 16
S5_GROUP = 16
S5_P = 64
S5_STATE = S5_G * S5_P
N_EXPERTS = 16
N_GROUPS = 4
GROUP_SIZE = N_EXPERTS // N_GROUPS
D_EXPERT = 512
OFF_HY = 0
OFF_QK = 3 * HY_CH
OFF_V = OFF_QK + 2 * ML_W
OFF_O = OFF_V + ML_W
OFF_G = OFF_O + ML_W
OFF_S5 = OFF_G + 16
IN_W = OFF_S5 + S5_CH
LANES = 128
GATE_PAD = LANES
IN_W_PAD = 3 * HY_CH + 2 * ML_W + ML_W + ML_W + S5_CH + GATE_PAD

TM = 512
TM_MOE = 1024
ML_CHUNK = 256
S5_ROWS = 8
S5_STEPS = 256
VMEM_LIMIT = 56 * 1024 * 1024


def _cparams(sem, vmem=VMEM_LIMIT):
    if sem is None:
        return pltpu.CompilerParams(vmem_limit_bytes=vmem)
    return pltpu.CompilerParams(dimension_semantics=sem, vmem_limit_bytes=vmem)


def _bdot(a, b):
    return jnp.dot(a.astype(BF16), b.astype(BF16), preferred_element_type=F32)


def _silu(x):
    return x * jax.nn.sigmoid(x)


def _rms(x, g):
    return x * lax.rsqrt(jnp.mean(x * x, axis=-1, keepdims=True) + EPS) * g


def _log_sigmoid(x):
    return jnp.minimum(x, 0.0) - jnp.log1p(jnp.exp(-jnp.abs(x)))


def _conv3(u, w, n_rows, seq_len):
    row = lax.broadcasted_iota(jnp.int32, (n_rows, 1), 0) % seq_len
    prev = jnp.where(row == 0, 0.0, pltpu.roll(u, 1, 0))
    nxt = jnp.where(row == seq_len - 1, 0.0, pltpu.roll(u, n_rows - 1, 0))
    return prev * w[0:1] + u * w[1:2] + nxt * w[2:3]


def _ada_kernel(c_ref, w_ref, b_ref, o_ref):
    o_ref[0] = _bdot(_silu(c_ref[...]), w_ref[0]) + b_ref[0]


def _ada(cc, w_ada, b_ada):
    tn = 1536
    return pl.pallas_call(
        _ada_kernel,
        grid=(DEPTH, 6 * D // tn),
        in_specs=[pl.BlockSpec((8, D), lambda l, j: (0, 0)),
                  pl.BlockSpec((1, D, tn), lambda l, j: (l, 0, j)),
                  pl.BlockSpec((1, 1, tn), lambda l, j: (l, 0, j))],
        out_specs=pl.BlockSpec((1, 8, tn), lambda l, j: (l, 0, j)),
        out_shape=jax.ShapeDtypeStruct((DEPTH, 8, 6 * D), F32),
        compiler_params=_cparams(("arbitrary", "arbitrary")),
        name="ada_mod",
    )(cc, w_ada, b_ada.reshape(DEPTH, 1, 6 * D))


def _mod_row(i, tm):
    n_ctx = T_CTX // tm
    return jnp.where(i < n_ctx, 0, 1 + (i - n_ctx) // (L_LAT // tm))


_SEG = ((0, 3 * HY_CH), (3 * HY_CH, 2 * ML_W), (3 * HY_CH + 2 * ML_W, ML_W),
        (3 * HY_CH + 3 * ML_W, ML_W), (3 * HY_CH + 4 * ML_W, S5_CH),
        (3 * HY_CH + 4 * ML_W + S5_CH, GATE_PAD))


def _inproj_kernel(x_ref, g_ref, mod_ref, w_ref, hy_ref, qk_ref, v_ref, o_ref, s5_ref, gt_ref):
    mod = mod_ref[0]
    h = _rms(x_ref[...], g_ref[...]) * (1.0 + mod[1:2]) + mod[0:1]
    hb = h.astype(BF16)
    for (a, w), ref in zip(_SEG, (hy_ref, qk_ref, v_ref, o_ref, s5_ref, gt_ref)):
        ref[...] = jnp.dot(hb, w_ref[:, a:a + w], preferred_element_type=F32)


def _inproj(x, g, mod, w_pad):
    widths = [w for _, w in _SEG]
    return pl.pallas_call(
        _inproj_kernel,
        grid=(T_ALL // TM,),
        in_specs=[pl.BlockSpec((TM, D), lambda i: (i, 0)),
                  pl.BlockSpec((1, D), lambda i: (0, 0)),
                  pl.BlockSpec((1, 6, D), lambda i: (_mod_row(i, TM), 0, 0)),
                  pl.BlockSpec((D, IN_W_PAD), lambda i: (0, 0))],
        out_specs=[pl.BlockSpec((TM, w), lambda i: (i, 0)) for w in widths],
        out_shape=[jax.ShapeDtypeStruct((T_ALL, w), F32) for w in widths],
        compiler_params=_cparams(("arbitrary",)),
        name="norm_inproj",
    )(x, g.reshape(1, D), mod, w_pad)


@functools.lru_cache(None)
def _dft_mats(L):
    n = 2 * L
    k = np.arange(L)[:, None]
    t = np.arange(L)[None, :]
    ang = 2.0 * np.pi * ((k * t) % n) / n
    top = np.cos(ang)
    bot = -np.sin(ang)
    bot[0] = np.cos(np.pi * np.arange(L))
    fwd = np.concatenate([top, bot], 0)
    s = np.full((n, 1), 2.0 / n)
    s[0] = s[L] = 1.0 / n
    inv = (fwd * s).T
    return fwd.astype(np.float32), inv.astype(np.float32)


@functools.lru_cache(None)
def _hy_positions(L):
    t = np.linspace(0.0, 1.0, L)
    bands = (HY_EMB - 1) // 2
    f = np.linspace(1e-4, bands - 1, bands)
    w = 2.0 * np.pi * np.arange(L) / L
    ang = w[:, None] * f[None, :]
    z = np.concatenate([t[:, None], np.cos(ang), -np.sin(ang)], -1)
    zp = np.zeros((L, LANES))
    zp[:, :HY_EMB] = z
    return zp.astype(np.float32), t[:, None].astype(np.float32)


def _hy_filter_kernel(z_ref, t_ref, w1_ref, b1_ref, w2_ref, b2_ref, w3_ref, ld_ref, f_ref,
                      p_ref, q_ref, r_ref, *, L):
    h = jnp.sin(jnp.dot(z_ref[...], w1_ref[...], precision=HIGHEST, preferred_element_type=F32) + b1_ref[...])
    h = jnp.sin(jnp.dot(h, w2_ref[...], precision=HIGHEST, preferred_element_type=F32) + b2_ref[...])
    filt = jnp.dot(h, w3_ref[...], precision=HIGHEST, preferred_element_type=F32)
    filt = filt * jnp.exp(-t_ref[...] * jnp.exp(ld_ref[...]))
    c = HY_CH
    h_fwd = jnp.concatenate([filt[:, 0:c], filt[:, 2 * c:3 * c]], axis=1)
    h_bwd = jnp.concatenate([filt[:, c:2 * c], filt[:, 3 * c:4 * c]], axis=1)
    row = lax.broadcasted_iota(jnp.int32, (L, 1), 0)
    h_bwd = jnp.where(row == 0, 0.0, h_bwd)
    a = jnp.dot(f_ref[...], h_fwd.astype(BF16), preferred_element_type=F32)
    b = jnp.dot(f_ref[...], h_bwd.astype(BF16), preferred_element_type=F32)
    re = a[:L] + b[:L]
    im = a[L:] - b[L:]
    nyq = a[L:L + 1] + b[L:L + 1]
    p_ref[...] = re
    q_ref[...] = jnp.where(row == 0, 0.0, im)
    r_ref[...] = jnp.where(row == 0, nyq, re)


def _hy_filter(L, w1p, b1, w2, b2, w3, ld):
    z, t = _hy_positions(L)
    fwd = jnp.asarray(_dft_mats(L)[0]).astype(BF16)
    out = jax.ShapeDtypeStruct((L, 2 * HY_CH), F32)
    return pl.pallas_call(
        functools.partial(_hy_filter_kernel, L=L),
        out_shape=[out, out, out],
        compiler_params=_cparams(None),
        name=f"hyena_filter_{L}",
    )(z, t, w1p, b1, w2, b2, w3, ld, fwd)


def _hyena_kernel(u_ref, sw_ref, bias_ref, p_ref, q_ref, r_ref, f_ref, g_ref, o_ref, *, L):
    c = HY_CH
    u = _conv3(u_ref[...], sw_ref[...], L, L)
    z = u[:, 0:c]
    for o in range(2):
        gate = u[:, (o + 1) * c:(o + 2) * c]
        zf = jnp.dot(f_ref[...], z.astype(BF16), preferred_element_type=F32)
        a, b = zf[:L], zf[L:]
        p = p_ref[:, o * c:(o + 1) * c]
        q = q_ref[:, o * c:(o + 1) * c]
        r = r_ref[:, o * c:(o + 1) * c]
        y_re = (a * p - b * q).astype(BF16)
        y_im = (a * q + b * r).astype(BF16)
        y = (jnp.dot(g_ref[:, :L], y_re, preferred_element_type=F32)
             + jnp.dot(g_ref[:, L:], y_im, preferred_element_type=F32))
        z = gate * (y + bias_ref[o:o + 1, :] * z)
    o_ref[...] = z


def _hyena(u_hy, L, n_seq, row_block0, sw, bias, p, q, r):
    fwd, inv = (jnp.asarray(m).astype(BF16) for m in _dft_mats(L))
    full = lambda a: pl.BlockSpec(a.shape, lambda b: (0,) * a.ndim)
    return pl.pallas_call(
        functools.partial(_hyena_kernel, L=L),
        grid=(n_seq,),
        in_specs=[pl.BlockSpec((L, 3 * HY_CH), lambda b: (row_block0 + b, 0)),
                  full(sw), full(bias), full(p), full(q), full(r), full(fwd), full(inv)],
        out_specs=pl.BlockSpec((L, HY_CH), lambda b: (b, 0)),
        out_shape=jax.ShapeDtypeStruct((n_seq * L, HY_CH), F32),
        compiler_params=_cparams(("arbitrary",)),
        name=f"hyena_{L}",
    )(u_hy, sw, bias, p, q, r, fwd, inv)


def _cumsum_rows(x, n, reverse):
    row = lax.broadcasted_iota(jnp.int32, (n, 1), 0)
    s = 1
    while s < n:
        if reverse:
            x = x + jnp.where(row < n - s, pltpu.roll(x, n - s, 0), 0.0)
        else:
            x = x + jnp.where(row >= s, pltpu.roll(x, s, 0), 0.0)
        s *= 2
    return x


def _cumsum_lanes(x, n, reverse):
    col = lax.broadcasted_iota(jnp.int32, (1, n), 1)
    s = 1
    while s < n:
        if reverse:
            x = x + jnp.where(col < n - s, pltpu.roll(x, n - s, 1), 0.0)
        else:
            x = x + jnp.where(col >= s, pltpu.roll(x, s, 1), 0.0)
        s *= 2
    return x


def _mlstm_kernel(*refs, L, has_state):
    if has_state:
        (qk_ref, v_ref, o_ref, g_ref, gt_ref, sw_ref, gb_ref, gbt_ref, ng_ref, c0_ref, n0_ref, m0_ref,
         y_ref, cout_ref, nout_ref, mout_ref, q_s, k_s, h_s, c_s, n_s, m_s) = refs
    else:
        (qk_ref, v_ref, o_ref, g_ref, gt_ref, sw_ref, gb_ref, gbt_ref, ng_ref,
         y_ref, cout_ref, nout_ref, mout_ref, q_s, k_s, h_s, c_s, n_s, m_s) = refs
    tc = ML_CHUNK
    nc = L // tc
    nh = ML_HEADS
    dh = ML_DH

    qk = _silu(_conv3(qk_ref[...], sw_ref[...], L, L))
    q_s[...] = qk[:, :ML_W].astype(BF16)
    k_s[...] = (qk[:, ML_W:] * (dh ** -0.5)).astype(BF16)

    if has_state:
        c_s[...] = c0_ref[0]
        n_s[...] = n0_ref[0]
        m_s[...] = m0_ref[0]
    else:
        c_s[...] = jnp.zeros_like(c_s)
        n_s[...] = jnp.zeros_like(n_s)
        m_s[...] = jnp.zeros_like(m_s)

    ri = lax.broadcasted_iota(jnp.int32, (tc, tc), 0)
    ci = lax.broadcasted_iota(jnp.int32, (tc, tc), 1)

    for d in range(2):
        rev = d == 1
        mask = (ci >= ri) if rev else (ci <= ri)
        edge = 0 if rev else tc - 1

        def chunk(j, carry, d=d, rev=rev, mask=mask, edge=edge):
            cidx = (nc - 1 - j) if rev else j
            r0 = pl.multiple_of(cidx * tc, tc)
            pre = g_ref[pl.ds(r0, tc), :] + gb_ref[...]
            pre_t = gt_ref[cidx] + gbt_ref[...]
            cum = _cumsum_rows(_log_sigmoid(pre), tc, rev)
            cum_t = _cumsum_lanes(_log_sigmoid(pre_t), tc, rev)
            for h in range(nh):
                col = d * nh + h
                s_idx = d * nh + h
                ig_col = pre[:, col:col + 1]
                ig_row = pre_t[col:col + 1, :]
                b_col = cum[:, 8 + col:9 + col]
                b_row = cum_t[8 + col:9 + col, :]
                b_end = b_col[edge:edge + 1, :]
                m_prev = m_s[s_idx:s_idx + 1, 0:1]
                dmat = jnp.where(mask, b_col - b_row + ig_row, -jnp.inf)
                inter = b_col + m_prev
                m_row = jnp.maximum(inter, jnp.max(dmat, axis=1, keepdims=True))
                w_intra = jnp.exp(dmat - m_row)
                w_state = jnp.exp(inter - m_row)
                qh = q_s[pl.ds(r0, tc), h * dh:(h + 1) * dh]
                kh = k_s[pl.ds(r0, tc), h * dh:(h + 1) * dh]
                vh = v_ref[pl.ds(r0, tc), h * dh:(h + 1) * dh].astype(BF16)
                c_prev = c_s[s_idx]
                n_prev = n_s[s_idx:s_idx + 1, :]
                s = lax.dot_general(qh, kh, (((1,), (1,)), ((), ())), preferred_element_type=F32) * w_intra
                num = (jnp.dot(s.astype(BF16), vh, preferred_element_type=F32)
                       + w_state * jnp.dot(qh, c_prev.astype(BF16), preferred_element_type=F32))
                den = (jnp.sum(s, axis=1, keepdims=True)
                       + w_state * jnp.sum(qh.astype(F32) * n_prev, axis=1, keepdims=True))
                hout = num / jnp.maximum(jnp.abs(den), jnp.exp(-m_row))
                if d == 0:
                    h_s[pl.ds(r0, tc), h * dh:(h + 1) * dh] = hout
                else:
                    h_s[pl.ds(r0, tc), h * dh:(h + 1) * dh] += hout
                g_col = b_end - b_col + ig_col
                m_new = jnp.maximum(b_end + m_prev, jnp.max(g_col, axis=0, keepdims=True))
                wg = jnp.exp(g_col - m_new)
                decay = jnp.exp(b_end + m_prev - m_new)
                kw = kh.astype(F32) * wg
                c_s[s_idx] = decay * c_prev + lax.dot_general(
                    kw.astype(BF16), vh, (((0,), (0,)), ((), ())), preferred_element_type=F32)
                n_s[s_idx:s_idx + 1, :] = decay * n_prev + jnp.sum(kw, axis=0, keepdims=True)
                m_s[s_idx:s_idx + 1, :] = jnp.broadcast_to(m_new, (1, LANES))
            return carry

        lax.fori_loop(0, nc, chunk, 0)

    for h in range(nh):
        hs = h_s[:, h * dh:(h + 1) * dh]
        hn = hs * lax.rsqrt(jnp.mean(hs * hs, axis=1, keepdims=True) + EPS)
        y_ref[:, h * dh:(h + 1) * dh] = (jax.nn.sigmoid(o_ref[:, h * dh:(h + 1) * dh])
                                          * (hn * ng_ref[:, h * dh:(h + 1) * dh]))
    cout_ref[0] = c_s[...]
    nout_ref[0] = n_s[...]
    mout_ref[0] = m_s[...]


def _mlstm(qk, v, o, gates, gates_t, L, n_seq, row_block0, sw, gb, gbt, ng, state):
    nc = L // ML_CHUNK
    has_state = state is not None
    full = lambda a: pl.BlockSpec(a.shape, lambda b: (0,) * a.ndim)
    in_specs = [pl.BlockSpec((L, 2 * ML_W), lambda b: (row_block0 + b, 0)),
                pl.BlockSpec((L, ML_W), lambda b: (row_block0 + b, 0)),
                pl.BlockSpec((L, ML_W), lambda b: (row_block0 + b, 0)),
                pl.BlockSpec((L, GATE_PAD), lambda b: (row_block0 + b, 0)),
                pl.BlockSpec((nc, 16, ML_CHUNK), lambda b: (row_block0 + b, 0, 0)),
                full(sw), full(gb), full(gbt), full(ng)]
    args = [qk, v, o, gates, gates_t, sw, gb, gbt, ng]
    if has_state:
        c0, n0, m0 = state
        in_specs += [pl.BlockSpec((1, 2 * ML_HEADS, ML_DH, ML_DH), lambda b: (b, 0, 0, 0)),
                     pl.BlockSpec((1, 2 * ML_HEADS, ML_DH), lambda b: (b, 0, 0)),
                     pl.BlockSpec((1, 2 * ML_HEADS, LANES), lambda b: (b, 0, 0))]
        args += [c0, n0, m0]
    return pl.pallas_call(
        functools.partial(_mlstm_kernel, L=L, has_state=has_state),
        grid=(n_seq,),
        in_specs=in_specs,
        out_specs=[pl.BlockSpec((L, ML_W), lambda b: (b, 0)),
                   pl.BlockSpec((1, 2 * ML_HEADS, ML_DH, ML_DH), lambda b: (b, 0, 0, 0)),
                   pl.BlockSpec((1, 2 * ML_HEADS, ML_DH), lambda b: (b, 0, 0)),
                   pl.BlockSpec((1, 2 * ML_HEADS, LANES), lambda b: (b, 0, 0))],
        out_shape=[jax.ShapeDtypeStruct((n_seq * L, ML_W), F32),
                   jax.ShapeDtypeStruct((n_seq, 2 * ML_HEADS, ML_DH, ML_DH), F32),
                   jax.ShapeDtypeStruct((n_seq, 2 * ML_HEADS, ML_DH), F32),
                   jax.ShapeDtypeStruct((n_seq, 2 * ML_HEADS, LANES), F32)],
        scratch_shapes=[pltpu.VMEM((L, ML_W), BF16), pltpu.VMEM((L, ML_W), BF16),
                        pltpu.VMEM((L, ML_W), F32),
                        pltpu.VMEM((2 * ML_HEADS, ML_DH, ML_DH), F32),
                        pltpu.VMEM((2 * ML_HEADS, ML_DH), F32),
                        pltpu.VMEM((2 * ML_HEADS, LANES), F32)],
        compiler_params=_cparams(("arbitrary",)),
        name=f"mlstm_{L}",
    )(*args)


def _cmul(ar, ai, br, bi):
    return ar * br - ai * bi, ar * bi + ai * br


def _s5_kernel(*refs, segmented):
    if segmented:
        (u_ref, b_ref, c_ref, lam_ref, dsk_ref, wglu_ref, s0_ref, y_ref, sbuf, yacc, pw) = refs
    else:
        (u_ref, b_ref, c_ref, lam_ref, dsk_ref, wglu_ref, y_ref, fin_ref, sbuf, yacc) = refs
    n = S5_STATE
    rows = S5_ROWS
    steps = S5_STEPS
    blk = 256
    n_blk = steps * rows // blk
    n_seg = 4

    yacc[...] = u_ref[0] * dsk_ref[...]
    ub = u_ref[0].astype(BF16)

    for d in range(2):
        rev = d == 1
        for i in range(n_blk):
            sbuf[i * blk:(i + 1) * blk, :] = jnp.dot(ub[i * blk:(i + 1) * blk], b_ref[d],
                                                     preferred_element_type=F32)
        lam = lam_ref[d]
        lr = jnp.broadcast_to(lam[:, :n], (rows, n))
        li = jnp.broadcast_to(lam[:, n:], (rows, n))

        def step(i, carry, rev=rev, lr=lr, li=li):
            sr, si = carry
            t = (steps - 1 - i) if rev else i
            off = pl.multiple_of(t * rows, rows)
            pr, pi = _cmul(lr, li, sr, si)
            nr = pr + sbuf[pl.ds(off, rows), 0:n]
            ni = pi + sbuf[pl.ds(off, rows), n:2 * n]
            sbuf[pl.ds(off, rows), 0:n] = nr
            sbuf[pl.ds(off, rows), n:2 * n] = ni
            return nr, ni

        zero = jnp.zeros((rows, n), F32)
        sr, si = lax.fori_loop(0, steps, step, (zero, zero), unroll=2)

        if not segmented:
            fin_ref[0, d, :, 0:n] = sr
            fin_ref[0, d, :, n:2 * n] = si
        else:
            lam_r, lam_i = lam[:, :n], lam[:, n:]
            row8 = lax.broadcasted_iota(jnp.int32, (rows, 1), 0)
            cr, ci = lam_r, lam_i
            acc_r = jnp.broadcast_to(cr, (rows, n))
            acc_i = jnp.broadcast_to(ci, (rows, n))
            for j in range(1, rows):
                cr, ci = _cmul(cr, ci, lam_r, lam_i)
                acc_r = jnp.where(row8 >= j, jnp.broadcast_to(cr, (rows, n)), acc_r)
                acc_i = jnp.where(row8 >= j, jnp.broadcast_to(ci, (rows, n)), acc_i)
            pw[0:rows, 0:n] = acc_r
            pw[0:rows, n:2 * n] = acc_i
            size = rows
            while size < steps:
                tr = pw[size - 1:size, 0:n]
                ti = pw[size - 1:size, n:2 * n]
                xr, xi = _cmul(pw[0:size, 0:n], pw[0:size, n:2 * n], tr, ti)
                pw[size:2 * size, 0:n] = xr
                pw[size:2 * size, n:2 * n] = xi
                size *= 2
            end_off = 0 if rev else (steps - 1) * rows
            loc_r = sbuf[end_off:end_off + rows, 0:n]
            loc_i = sbuf[end_off:end_off + rows, n:2 * n]
            pl_r = pw[steps - 1:steps, 0:n]
            pl_i = pw[steps - 1:steps, n:2 * n]
            s0r = s0_ref[d, :, 0:n]
            s0i = s0_ref[d, :, n:2 * n]
            seg = row8 // 2
            first = (seg == n_seg - 1) if rev else (seg == 0)
            shift = (rows - 2) if rev else 2
            cin_r, cin_i = s0r, s0i
            for _ in range(n_seg - 1):
                fr, fi = _cmul(jnp.broadcast_to(pl_r, (rows, n)), jnp.broadcast_to(pl_i, (rows, n)), cin_r, cin_i)
                tru_r = loc_r + fr
                tru_i = loc_i + fi
                cin_r = jnp.where(first, s0r, pltpu.roll(tru_r, shift, 0))
                cin_i = jnp.where(first, s0i, pltpu.roll(tru_i, shift, 0))

            def fix(tb, carry, rev=rev, cin_r=cin_r, cin_i=cin_i):
                pb = (steps // rows - 1 - tb) if rev else tb
                poff = pl.multiple_of(pb * rows, rows)
                p_r = pw[pl.ds(poff, rows), 0:n]
                p_i = pw[pl.ds(poff, rows), n:2 * n]
                for j in range(rows):
                    jj = rows - 1 - j if rev else j
                    off = pl.multiple_of((tb * rows + j) * rows, rows)
                    fr, fi = _cmul(jnp.broadcast_to(p_r[jj:jj + 1], (rows, n)),
                                   jnp.broadcast_to(p_i[jj:jj + 1], (rows, n)), cin_r, cin_i)
                    sbuf[pl.ds(off, rows), 0:n] += fr
                    sbuf[pl.ds(off, rows), n:2 * n] += fi
                return carry

            lax.fori_loop(0, steps // rows, fix, 0)

        for i in range(n_blk):
            yacc[i * blk:(i + 1) * blk, :] += jnp.dot(sbuf[i * blk:(i + 1) * blk, :].astype(BF16), c_ref[d],
                                                      preferred_element_type=F32)

    g = jax.nn.gelu(yacc[...], approximate=True)
    y_ref[0] = g * jax.nn.sigmoid(_bdot(g, wglu_ref[...]))


def _s5(u_tm, bmat, cmat, lam, dskip, wglu, s0):
    n_grp = u_tm.shape[0]
    n_rows = S5_STEPS * S5_ROWS
    segmented = s0 is not None
    full = lambda a: pl.BlockSpec(a.shape, lambda g: (0,) * a.ndim)
    in_specs = [pl.BlockSpec((1, n_rows, S5_CH), lambda g: (g, 0, 0)),
                full(bmat), full(cmat), full(lam), full(dskip), full(wglu)]
    args = [u_tm, bmat, cmat, lam, dskip, wglu]
    out_specs = [pl.BlockSpec((1, n_rows, S5_CH), lambda g: (g, 0, 0))]
    out_shape = [jax.ShapeDtypeStruct((n_grp, n_rows, S5_CH), F32)]
    scratch = [pltpu.VMEM((n_rows, 2 * S5_STATE), F32), pltpu.VMEM((n_rows, S5_CH), F32)]
    if segmented:
        in_specs.append(full(s0))
        args.append(s0)
        scratch.append(pltpu.VMEM((S5_STEPS, 2 * S5_STATE), F32))
    else:
        out_specs.append(pl.BlockSpec((1, 2, S5_ROWS, 2 * S5_STATE), lambda g: (g, 0, 0, 0)))
        out_shape.append(jax.ShapeDtypeStruct((n_grp, 2, S5_ROWS, 2 * S5_STATE), F32))
    return pl.pallas_call(
        functools.partial(_s5_kernel, segmented=segmented),
        grid=(n_grp,),
        in_specs=in_specs,
        out_specs=out_specs,
        out_shape=out_shape,
        scratch_shapes=scratch,
        compiler_params=_cparams(("arbitrary",)),
        name="s5_seg" if segmented else "s5_ctx",
    )(*args)


def _s5_params(a_re, a_im, log_dt, b_re, b_im, c_re, c_im):
    dt = jnp.exp(log_dt)[:, :, None]
    mag = jnp.exp(a_re * dt)
    lb_re = mag * jnp.cos(a_im * dt)
    lb_im = mag * jnp.sin(a_im * dt)
    den = a_re * a_re + a_im * a_im
    nr, ni = lb_re - 1.0, lb_im
    k_re = (nr * a_re + ni * a_im) / den
    k_im = (ni * a_re - nr * a_im) / den
    bb_re = k_re[..., None] * b_re - k_im[..., None] * b_im
    bb_im = k_re[..., None] * b_im + k_im[..., None] * b_re
    eye = jnp.eye(S5_G, dtype=F32)
    blk_b = lambda m: jnp.einsum('dgpc,gh->dgchp', m, eye).reshape(2, S5_CH, S5_STATE)
    blk_c = lambda m: jnp.einsum('dgcp,gh->dgphc', m, eye).reshape(2, S5_STATE, S5_CH)
    bmat = jnp.concatenate([blk_b(bb_re), blk_b(bb_im)], axis=2).astype(BF16)
    cmat = jnp.concatenate([blk_c(c_re), -blk_c(c_im)], axis=1).astype(BF16)
    lam = jnp.concatenate([lb_re.reshape(2, 1, S5_STATE), lb_im.reshape(2, 1, S5_STATE)], axis=2)
    return bmat, cmat, lam


def _outproj_kernel(x_ref, hy_ref, ml_ref, s5_ref, w_ref, mod_ref, g_ref, rw_ref, rb_ref,
                    xn_ref, h2_ref, comb_ref):
    mod = mod_ref[0]
    a, b = HY_CH, HY_CH + ML_W
    mix = (jnp.dot(hy_ref[...].astype(BF16), w_ref[0:a, :], preferred_element_type=F32)
           + jnp.dot(ml_ref[...].astype(BF16), w_ref[a:b, :], preferred_element_type=F32)
           + jnp.dot(s5_ref[...].astype(BF16), w_ref[b:, :], preferred_element_type=F32))
    xn = x_ref[...] + mod[2:3] * mix
    xn_ref[...] = xn
    h2 = _rms(xn, g_ref[...]) * (1.0 + mod[4:5]) + mod[3:4]
    h2_ref[...] = h2.astype(BF16)
    logits = lax.dot_general(rw_ref[...], h2, (((1,), (1,)), ((), ())),
                             precision=HIGHEST, preferred_element_type=F32)
    ex = jnp.exp(logits - jnp.max(logits, axis=0, keepdims=True))
    probs = ex / jnp.sum(ex, axis=0, keepdims=True)
    sel = probs + rb_ref[...]
    best = None
    best_score = None
    for g in range(N_GROUPS):
        r = [sel[g * GROUP_SIZE + i:g * GROUP_SIZE + i + 1, :] for i in range(GROUP_SIZE)]
        score = None
        for i in range(GROUP_SIZE):
            for j in range(i + 1, GROUP_SIZE):
                pair = r[i] + r[j]
                score = pair if score is None else jnp.maximum(score, pair)
        if g == 0:
            best, best_score = jnp.zeros_like(score, dtype=jnp.int32), score
        else:
            upd = score > best_score
            best = jnp.where(upd, g, best)
            best_score = jnp.where(upd, score, best_score)
    eid = lax.broadcasted_iota(jnp.int32, (N_EXPERTS, 1), 0)
    masked = jnp.where(eid // GROUP_SIZE == best, sel, -jnp.inf)
    m1 = jnp.max(masked, axis=0, keepdims=True)
    i1 = jnp.min(jnp.where(masked == m1, eid, N_EXPERTS), axis=0, keepdims=True)
    masked2 = jnp.where(eid == i1, -jnp.inf, masked)
    m2 = jnp.max(masked2, axis=0, keepdims=True)
    i2 = jnp.min(jnp.where(masked2 == m2, eid, N_EXPERTS), axis=0, keepdims=True)
    p1 = jnp.sum(jnp.where(eid == i1, probs, 0.0), axis=0, keepdims=True)
    p2 = jnp.sum(jnp.where(eid == i2, probs, 0.0), axis=0, keepdims=True)
    tot = p1 + p2
    comb_ref[...] = jnp.where(eid == i1, p1 / tot, 0.0) + jnp.where(eid == i2, p2 / tot, 0.0)


def _outproj(x, y_hy, y_ml, y_s5, w_out, mod, g2, rw_t, rb):
    full = lambda a: pl.BlockSpec(a.shape, lambda i: (0,) * a.ndim)
    return pl.pallas_call(
        _outproj_kernel,
        grid=(T_ALL // TM,),
        in_specs=[pl.BlockSpec((TM, D), lambda i: (i, 0)),
                  pl.BlockSpec((TM, HY_CH), lambda i: (i, 0)),
                  pl.BlockSpec((TM, ML_W), lambda i: (i, 0)),
                  pl.BlockSpec((TM, S5_CH), lambda i: (i, 0)),
                  full(w_out),
                  pl.BlockSpec((1, 6, D), lambda i: (_mod_row(i, TM), 0, 0)),
                  full(g2), full(rw_t), full(rb)],
        out_specs=[pl.BlockSpec((TM, D), lambda i: (i, 0)),
                   pl.BlockSpec((TM, D), lambda i: (i, 0)),
                   pl.BlockSpec((N_EXPERTS, TM), lambda i: (0, i))],
        out_shape=[jax.ShapeDtypeStruct((T_ALL, D), F32),
                   jax.ShapeDtypeStruct((T_ALL, D), BF16),
                   jax.ShapeDtypeStruct((N_EXPERTS, T_ALL), F32)],
        compiler_params=_cparams(("arbitrary",)),
        name="outproj_router",
    )(x, y_hy, y_ml, y_s5, w_out, mod, g2, rw_t, rb)


def _moe_kernel(h_ref, comb_ref, wg_ref, wu_ref, wd_ref, xn_ref, mod_ref, fg_ref, *rest, final):
    if final:
        out_ref, y_ref, acc = rest
    else:
        out_ref, acc = rest
    e = pl.program_id(1)
    lane = lax.broadcasted_iota(jnp.int32, (1, N_EXPERTS), 1)
    wg = wg_ref[0, 0].astype(BF16)
    wu = wu_ref[0, 0].astype(BF16)
    wd = wd_ref[0, 0].astype(BF16)
    sub = 256
    for i in range(TM_MOE // sub):
        rs = slice(i * sub, (i + 1) * sub)
        hb = h_ref[rs, :]
        ce = jnp.sum(jnp.where(lane == e, comb_ref[rs, :], 0.0), axis=1, keepdims=True)
        hid = _silu(jnp.dot(hb, wg, preferred_element_type=F32)) * jnp.dot(hb, wu, preferred_element_type=F32)
        part = jnp.dot((hid * ce).astype(BF16), wd, preferred_element_type=F32)

        @pl.when(e == 0)
        def _():
            acc[rs, :] = part

        @pl.when(e > 0)
        def _():
            acc[rs, :] += part

    @pl.when(e == N_EXPERTS - 1)
    def _():
        out = xn_ref[...] + mod_ref[0][5:6] * acc[...]
        out_ref[...] = out
        if final:
            y_ref[...] = _rms(out, fg_ref[...])


def _moe(h2, comb, wg, wu, wd, xn, mod, fg, l, final):
    tm = TM_MOE
    last = N_EXPERTS - 1
    out_specs = [pl.BlockSpec((tm, D), lambda i, e: (i, 0))]
    out_shape = [jax.ShapeDtypeStruct((T_ALL, D), F32)]
    if final:
        out_specs.append(pl.BlockSpec((tm, D), lambda i, e: (i, 0)))
        out_shape.append(jax.ShapeDtypeStruct((T_ALL, D), F32))
    return pl.pallas_call(
        functools.partial(_moe_kernel, final=final),
        grid=(T_ALL // tm, N_EXPERTS),
        in_specs=[pl.BlockSpec((tm, D), lambda i, e: (i, 0)),
                  pl.BlockSpec((tm, N_EXPERTS), lambda i, e: (i, 0)),
                  pl.BlockSpec((1, 1, D, D_EXPERT), lambda i, e: (l, e, 0, 0)),
                  pl.BlockSpec((1, 1, D, D_EXPERT), lambda i, e: (l, e, 0, 0)),
                  pl.BlockSpec((1, 1, D_EXPERT, D), lambda i, e: (l, e, 0, 0)),
                  pl.BlockSpec((tm, D), lambda i, e: (i, 0)),
                  pl.BlockSpec((1, 6, D), lambda i, e: (_mod_row(i, tm), 0, 0)),
                  pl.BlockSpec((1, D), lambda i, e: (0, 0))],
        out_specs=out_specs,
        out_shape=out_shape,
        scratch_shapes=[pltpu.VMEM((tm, D), F32)],
        compiler_params=_cparams(("arbitrary", "arbitrary")),
        name="moe_experts",
    )(h2, comb, wg, wu, wd, xn, mod, fg)


@functools.lru_cache(None)
def _pos_embed():
    rows = L_LAT // GRID_W
    r = np.repeat(np.arange(rows, dtype=np.float64), GRID_W)
    col = np.tile(np.arange(GRID_W, dtype=np.float64), rows)
    quarter = D // 4
    freq = np.exp(-math.log(POS_BASE) * np.arange(quarter, dtype=np.float64) / quarter)
    ar = r[:, None] * freq[None]
    ac = col[:, None] * freq[None]
    emb = np.concatenate([np.sin(ar), np.cos(ar), np.sin(ac), np.cos(ac)], axis=-1)
    return emb.astype(np.float32)


def _pad_w_in(w):
    segs = [w[:, OFF_HY:OFF_QK], w[:, OFF_QK:OFF_V], w[:, OFF_V:OFF_O], w[:, OFF_O:OFF_G],
            w[:, OFF_S5:IN_W], w[:, OFF_G:OFF_S5], jnp.zeros((D, GATE_PAD - 16), w.dtype)]
    return jnp.concatenate(segs, axis=1).astype(BF16)


def _to_time_major_ctx(a):
    c = a.shape[-1]
    a = a.reshape(2, S5_ROWS, L_CTX, c).transpose(0, 2, 1, 3)
    return a.reshape(2, L_CTX * S5_ROWS, c)


def _from_time_major_ctx(a):
    c = a.shape[-1]
    a = a.reshape(2, L_CTX, S5_ROWS, c).transpose(0, 2, 1, 3)
    return a.reshape(T_CTX, c)


def _to_time_major_lat(a):
    c = a.shape[-1]
    a = a.reshape(N_LAT_SEQ, 4, S5_STEPS, c).transpose(2, 1, 0, 3)
    return a.reshape(1, S5_STEPS * S5_ROWS, c)


def _from_time_major_lat(a):
    c = a.shape[-1]
    a = a.reshape(S5_STEPS, 4, N_LAT_SEQ, c).transpose(2, 1, 0, 3)
    return a.reshape(T_LAT, c)


def kernel(x_prompt, x_sample, c, state_mlstm_C, state_mlstm_n, state_mlstm_m, state_s5_re, state_s5_im, c_ctx, w_ada, b_ada, norm1_g, norm2_g, final_g, w_in, w_out, hy_short, hy_fw1, hy_fb1, hy_fw2, hy_fb2, hy_fw3, hy_log_decay, hy_bias, ml_short, ml_gate_bias, ml_norm_g, s5_a_re, s5_a_im, s5_log_dt, s5_b_re, s5_b_im, s5_c_re, s5_c_im, s5_d, s5_w_glu, router_w, router_b, moe_w_gate, moe_w_up, moe_w_down):
    x = jnp.concatenate([x_prompt.reshape(T_CTX, D), (x_sample + _pos_embed()[None]).reshape(T_LAT, D)], axis=0)
    cc = jnp.concatenate([c_ctx[None], c, jnp.zeros((8 - 1 - N_LAT_SEQ, D), F32)], axis=0)
    mod_all = _ada(cc, w_ada, b_ada).reshape(DEPTH, 8, 6, D)
    rw_t = router_w.T
    rb = router_b.reshape(N_EXPERTS, 1)
    fg = final_g.reshape(1, D)
    lat_blk = T_CTX // L_LAT

    new_c, new_n, new_m, new_re, new_im = [], [], [], [], []
    y_all = None
    for l in range(DEPTH):
        mod = mod_all[l]
        u_hy, qk, v, o, u_s5, gates = _inproj(x, norm1_g[l], mod, _pad_w_in(w_in[l]))

        w1p = jnp.zeros((LANES, HY_FILTER_W), F32).at[:HY_EMB].set(hy_fw1[l])
        b1 = hy_fb1[l].reshape(1, HY_FILTER_W)
        b2 = hy_fb2[l].reshape(1, HY_FILTER_W)
        ld = hy_log_decay[l].reshape(1, 4 * HY_CH)
        y_hy = []
        for L, n_seq, blk0 in ((L_CTX, N_CTX_SEQ, 0), (L_LAT, N_LAT_SEQ, lat_blk)):
            p, q, r = _hy_filter(L, w1p, b1, hy_fw2[l], b2, hy_fw3[l], ld)
            y_hy.append(_hyena(u_hy, L, n_seq, blk0, hy_short[l], hy_bias[l], p, q, r))
        y_hy = jnp.concatenate(y_hy, axis=0)

        gates_t = gates[:, :16].reshape(T_ALL // ML_CHUNK, ML_CHUNK, 16).transpose(0, 2, 1)
        gb = jnp.zeros((1, GATE_PAD), F32).at[0, :16].set(ml_gate_bias[l].reshape(16))
        gbt = ml_gate_bias[l].reshape(16, 1)
        ng = ml_norm_g[l].reshape(1, ML_W)
        m0 = jnp.broadcast_to(state_mlstm_m[:, l].reshape(N_LAT_SEQ, 2 * ML_HEADS, 1), (N_LAT_SEQ, 2 * ML_HEADS, LANES))
        yc, cc_, nc_, mc_ = _mlstm(qk, v, o, gates, gates_t, L_CTX, N_CTX_SEQ, 0, ml_short[l], gb, gbt, ng, None)
        yl, _, _, _ = _mlstm(qk, v, o, gates, gates_t, L_LAT, N_LAT_SEQ, lat_blk, ml_short[l], gb, gbt, ng,
                             (state_mlstm_C[:, l].reshape(N_LAT_SEQ, 2 * ML_HEADS, ML_DH, ML_DH),
                              state_mlstm_n[:, l].reshape(N_LAT_SEQ, 2 * ML_HEADS, ML_DH), m0))
        y_ml = jnp.concatenate([yc, yl], axis=0)
        new_c.append(cc_.reshape(N_CTX_SEQ, 2, ML_HEADS, ML_DH, ML_DH))
        new_n.append(nc_.reshape(N_CTX_SEQ, 2, ML_HEADS, ML_DH))
        new_m.append(mc_[:, :, 0].reshape(N_CTX_SEQ, 2, ML_HEADS))

        bmat, cmat, lam = _s5_params(s5_a_re[l], s5_a_im[l], s5_log_dt[l], s5_b_re[l], s5_b_im[l],
                                     s5_c_re[l], s5_c_im[l])
        dsk = s5_d[l].reshape(1, S5_CH)
        wglu = s5_w_glu[l].astype(BF16)
        ys_c, fin = _s5(_to_time_major_ctx(u_s5[:T_CTX]), bmat, cmat, lam, dsk, wglu, None)
        s0 = jnp.concatenate([state_s5_re[:, l].reshape(N_LAT_SEQ, 2, S5_STATE),
                              state_s5_im[:, l].reshape(N_LAT_SEQ, 2, S5_STATE)], axis=-1)
        s0 = jnp.tile(s0.transpose(1, 0, 2), (1, 4, 1))
        (ys_l,) = _s5(_to_time_major_lat(u_s5[T_CTX:]), bmat, cmat, lam, dsk, wglu, s0)
        y_s5 = jnp.concatenate([_from_time_major_ctx(ys_c), _from_time_major_lat(ys_l)], axis=0)
        fin = fin.transpose(0, 2, 1, 3).reshape(N_CTX_SEQ, 2, 2 * S5_STATE)
        new_re.append(fin[..., :S5_STATE].reshape(N_CTX_SEQ, 2, S5_G, S5_P))
        new_im.append(fin[..., S5_STATE:].reshape(N_CTX_SEQ, 2, S5_G, S5_P))

        xn, h2, comb_t = _outproj(x, y_hy, y_ml, y_s5, w_out[l].astype(BF16), mod,
                                  norm2_g[l].reshape(1, D), rw_t, rb)
        res = _moe(h2, comb_t.T, moe_w_gate, moe_w_up, moe_w_down, xn, mod, fg, l, l == DEPTH - 1)
        x = res[0]
        if l == DEPTH - 1:
            y_all = res[1]

    y_prompt = y_all[:T_CTX].reshape(N_CTX_SEQ, L_CTX, D)
    y_sample = y_all[T_CTX:].reshape(N_LAT_SEQ, L_LAT, D)
    return (y_prompt, y_sample, jnp.stack(new_c, axis=1), jnp.stack(new_n, axis=1), jnp.stack(new_m, axis=1),
            jnp.stack(new_re, axis=1), jnp.stack(new_im, axis=1))
```

```python
import functools
import math

import numpy as np
import jax
import jax.numpy as jnp
from jax import lax
from jax.experimental import pallas as pl
from jax.experimental.pallas import tpu as pltpu

F32 = jnp.float32
BF16 = jnp.bfloat16
HIGHEST = lax.Precision.HIGHEST

D = 1024
N_CTX_SEQ, L_CTX = 16, 256
N_LAT_SEQ, L_LAT = 2, 1024
T_CTX = N_CTX_SEQ * L_CTX
T_LAT = N_LAT_SEQ * L_LAT
T_ALL = T_CTX + T_LAT
DEPTH = 2
EPS = 1e-6
GRID_W = 64
POS_BASE = 10000.0
HY_CH = 256
HY_EMB = 33
HY_FILTER_W = 64
ML_HEADS = 4
ML_DH = 128
ML_W = ML_HEADS * ML_DH
S5_CH = 256
S5_G = 16
S5_GROUP = 16
S5_P = 64
S5_STATE = S5_G * S5_P
N_EXPERTS = 16
N_GROUPS = 4
GROUP_SIZE = N_EXPERTS // N_GROUPS
D_EXPERT = 512
OFF_HY = 0
OFF_QK = 3 * HY_CH
OFF_V = OFF_QK + 2 * ML_W
OFF_O = OFF_V + ML_W
OFF_G = OFF_O + ML_W
OFF_S5 = OFF_G + 16
IN_W = OFF_S5 + S5_CH
LANES = 128
GATE_PAD = LANES
IN_W_PAD = 3 * HY_CH + 2 * ML_W + ML_W + ML_W + S5_CH + GATE_PAD

TM = 512
TM_MOE = 512
MOE_SLOTS = T_ALL + N_GROUPS * TM_MOE
ML_CHUNK = 256
S5_ROWS = 8
S5_STEPS = 256
VMEM_LIMIT = 56 * 1024 * 1024


def _cparams(sem, vmem=VMEM_LIMIT):
    if sem is None:
        return pltpu.CompilerParams(vmem_limit_bytes=vmem)
    return pltpu.CompilerParams(dimension_semantics=sem, vmem_limit_bytes=vmem)


def _bdot(a, b):
    return jnp.dot(a.astype(BF16), b.astype(BF16), preferred_element_type=F32)


def _silu(x):
    return x * jax.nn.sigmoid(x)


def _rms(x, g):
    return x * lax.rsqrt(jnp.mean(x * x, axis=-1, keepdims=True) + EPS) * g


def _log_sigmoid(x):
    return jnp.minimum(x, 0.0) - jnp.log1p(jnp.exp(-jnp.abs(x)))


def _conv3(u, w, n_rows, seq_len):
    row = lax.broadcasted_iota(jnp.int32, (n_rows, 1), 0) % seq_len
    prev = jnp.where(row == 0, 0.0, pltpu.roll(u, 1, 0))
    nxt = jnp.where(row == seq_len - 1, 0.0, pltpu.roll(u, n_rows - 1, 0))
    return prev * w[0:1] + u * w[1:2] + nxt * w[2:3]


def _ada_kernel(c_ref, w_ref, b_ref, o_ref):
    o_ref[0] = _bdot(_silu(c_ref[...]), w_ref[0]) + b_ref[0]


def _ada(cc, w_ada, b_ada):
    tn = 1536
    return pl.pallas_call(
        _ada_kernel,
        grid=(DEPTH, 6 * D // tn),
        in_specs=[pl.BlockSpec((8, D), lambda l, j: (0, 0)),
                  pl.BlockSpec((1, D, tn), lambda l, j: (l, 0, j)),
                  pl.BlockSpec((1, 1, tn), lambda l, j: (l, 0, j))],
        out_specs=pl.BlockSpec((1, 8, tn), lambda l, j: (l, 0, j)),
        out_shape=jax.ShapeDtypeStruct((DEPTH, 8, 6 * D), F32),
        compiler_params=_cparams(("arbitrary", "arbitrary")),
        name="ada_mod",
    )(cc, w_ada, b_ada.reshape(DEPTH, 1, 6 * D))


def _mod_row(i, tm):
    n_ctx = T_CTX // tm
    return jnp.where(i < n_ctx, 0, 1 + (i - n_ctx) // (L_LAT // tm))


_SEG = ((0, 3 * HY_CH), (3 * HY_CH, 2 * ML_W), (3 * HY_CH + 2 * ML_W, ML_W),
        (3 * HY_CH + 3 * ML_W, ML_W), (3 * HY_CH + 4 * ML_W, S5_CH),
        (3 * HY_CH + 4 * ML_W + S5_CH, GATE_PAD))


def _inproj_kernel(x_ref, g_ref, mod_ref, w_ref, hy_ref, qk_ref, v_ref, o_ref, s5_ref, gt_ref):
    mod = mod_ref[0]
    h = _rms(x_ref[...], g_ref[...]) * (1.0 + mod[1:2]) + mod[0:1]
    hb = h.astype(BF16)
    for (a, w), ref in zip(_SEG, (hy_ref, qk_ref, v_ref, o_ref, s5_ref, gt_ref)):
        ref[...] = jnp.dot(hb, w_ref[:, a:a + w], preferred_element_type=F32)


def _inproj(x, g, mod, w_pad):
    widths = [w for _, w in _SEG]
    return pl.pallas_call(
        _inproj_kernel,
        grid=(T_ALL // TM,),
        in_specs=[pl.BlockSpec((TM, D), lambda i: (i, 0)),
                  pl.BlockSpec((1, D), lambda i: (0, 0)),
                  pl.BlockSpec((1, 6, D), lambda i: (_mod_row(i, TM), 0, 0)),
                  pl.BlockSpec((D, IN_W_PAD), lambda i: (0, 0))],
        out_specs=[pl.BlockSpec((TM, w), lambda i: (i, 0)) for w in widths],
        out_shape=[jax.ShapeDtypeStruct((T_ALL, w), F32) for w in widths],
        compiler_params=_cparams(("arbitrary",)),
        name="norm_inproj",
    )(x, g.reshape(1, D), mod, w_pad)


@functools.lru_cache(None)
def _dft_mats(L):
    n = 2 * L
    k = np.arange(L)[:, None]
    t = np.arange(L)[None, :]
    ang = 2.0 * np.pi * ((k * t) % n) / n
    top = np.cos(ang)
    bot = -np.sin(ang)
    bot[0] = np.cos(np.pi * np.arange(L))
    fwd = np.concatenate([top, bot], 0)
    s = np.full((n, 1), 2.0 / n)
    s[0] = s[L] = 1.0 / n
    inv = (fwd * s).T
    return fwd.astype(np.float32), inv.astype(np.float32)


@functools.lru_cache(None)
def _hy_positions(L):
    t = np.linspace(0.0, 1.0, L)
    bands = (HY_EMB - 1) // 2
    f = np.linspace(1e-4, bands - 1, bands)
    w = 2.0 * np.pi * np.arange(L) / L
    ang = w[:, None] * f[None, :]
    z = np.concatenate([t[:, None], np.cos(ang), -np.sin(ang)], -1)
    zp = np.zeros((L, LANES))
    zp[:, :HY_EMB] = z
    return zp.astype(np.float32), t[:, None].astype(np.float32)


def _hy_filter_kernel(z_ref, t_ref, w1_ref, b1_ref, w2_ref, b2_ref, w3_ref, ld_ref, f_ref,
                      p_ref, q_ref, r_ref, *, L):
    h = jnp.sin(jnp.dot(z_ref[...], w1_ref[...], precision=HIGHEST, preferred_element_type=F32) + b1_ref[...])
    h = jnp.sin(jnp.dot(h, w2_ref[...], precision=HIGHEST, preferred_element_type=F32) + b2_ref[...])
    filt = jnp.dot(h, w3_ref[...], precision=HIGHEST, preferred_element_type=F32)
    filt = filt * jnp.exp(-t_ref[...] * jnp.exp(ld_ref[...]))
    c = HY_CH
    h_fwd = jnp.concatenate([filt[:, 0:c], filt[:, 2 * c:3 * c]], axis=1)
    h_bwd = jnp.concatenate([filt[:, c:2 * c], filt[:, 3 * c:4 * c]], axis=1)
    row = lax.broadcasted_iota(jnp.int32, (L, 1), 0)
    h_bwd = jnp.where(row == 0, 0.0, h_bwd)
    a = jnp.dot(f_ref[...], h_fwd.astype(BF16), preferred_element_type=F32)
    b = jnp.dot(f_ref[...], h_bwd.astype(BF16), preferred_element_type=F32)
    re = a[:L] + b[:L]
    im = a[L:] - b[L:]
    nyq = a[L:L + 1] + b[L:L + 1]
    p_ref[...] = re
    q_ref[...] = jnp.where(row == 0, 0.0, im)
    r_ref[...] = jnp.where(row == 0, nyq, re)


def _hy_filter(L, w1p, b1, w2, b2, w3, ld):
    z, t = _hy_positions(L)
    fwd = jnp.asarray(_dft_mats(L)[0]).astype(BF16)
    out = jax.ShapeDtypeStruct((L, 2 * HY_CH), F32)
    return pl.pallas_call(
        functools.partial(_hy_filter_kernel, L=L),
        out_shape=[out, out, out],
        compiler_params=_cparams(None),
        name=f"hyena_filter_{L}",
    )(z, t, w1p, b1, w2, b2, w3, ld, fwd)


def _hyena_kernel(u_ref, sw_ref, bias_ref, p_ref, q_ref, r_ref, f_ref, g_ref, o_ref, *, L):
    c = HY_CH
    u = _conv3(u_ref[...], sw_ref[...], L, L)
    z = u[:, 0:c]
    for o in range(2):
        gate = u[:, (o + 1) * c:(o + 2) * c]
        zf = jnp.dot(f_ref[...], z.astype(BF16), preferred_element_type=F32)
        a, b = zf[:L], zf[L:]
        p = p_ref[:, o * c:(o + 1) * c]
        q = q_ref[:, o * c:(o + 1) * c]
        r = r_ref[:, o * c:(o + 1) * c]
        y_re = (a * p - b * q).astype(BF16)
        y_im = (a * q + b * r).astype(BF16)
        y = (jnp.dot(g_ref[:, :L], y_re, preferred_element_type=F32)
             + jnp.dot(g_ref[:, L:], y_im, preferred_element_type=F32))
        z = gate * (y + bias_ref[o:o + 1, :] * z)
    o_ref[...] = z


def _hyena(u_hy, L, n_seq, row_block0, sw, bias, p, q, r):
    fwd, inv = (jnp.asarray(m).astype(BF16) for m in _dft_mats(L))
    full = lambda a: pl.BlockSpec(a.shape, lambda b: (0,) * a.ndim)
    return pl.pallas_call(
        functools.partial(_hyena_kernel, L=L),
        grid=(n_seq,),
        in_specs=[pl.BlockSpec((L, 3 * HY_CH), lambda b: (row_block0 + b, 0)),
                  full(sw), full(bias), full(p), full(q), full(r), full(fwd), full(inv)],
        out_specs=pl.BlockSpec((L, HY_CH), lambda b: (b, 0)),
        out_shape=jax.ShapeDtypeStruct((n_seq * L, HY_CH), F32),
        compiler_params=_cparams(("arbitrary",)),
        name=f"hyena_{L}",
    )(u_hy, sw, bias, p, q, r, fwd, inv)


def _cumsum_rows(x, n, reverse):
    row = lax.broadcasted_iota(jnp.int32, (n, 1), 0)
    s = 1
    while s < n:
        if reverse:
            x = x + jnp.where(row < n - s, pltpu.roll(x, n - s, 0), 0.0)
        else:
            x = x + jnp.where(row >= s, pltpu.roll(x, s, 0), 0.0)
        s *= 2
    return x


def _cumsum_lanes(x, n, reverse):
    col = lax.broadcasted_iota(jnp.int32, (1, n), 1)
    s = 1
    while s < n:
        if reverse:
            x = x + jnp.where(col < n - s, pltpu.roll(x, n - s, 1), 0.0)
        else:
            x = x + jnp.where(col >= s, pltpu.roll(x, s, 1), 0.0)
        s *= 2
    return x


def _mlstm_kernel(*refs, L, has_state):
    if has_state:
        (qk_ref, v_ref, o_ref, g_ref, gt_ref, sw_ref, gb_ref, gbt_ref, ng_ref, c0_ref, n0_ref, m0_ref,
         y_ref, cout_ref, nout_ref, mout_ref, q_s, k_s, h_s, c_s, n_s, m_s) = refs
    else:
        (qk_ref, v_ref, o_ref, g_ref, gt_ref, sw_ref, gb_ref, gbt_ref, ng_ref,
         y_ref, cout_ref, nout_ref, mout_ref, q_s, k_s, h_s, c_s, n_s, m_s) = refs
    tc = ML_CHUNK
    nc = L // tc
    nh = ML_HEADS
    dh = ML_DH

    qk = _silu(_conv3(qk_ref[...], sw_ref[...], L, L))
    q_s[...] = qk[:, :ML_W].astype(BF16)
    k_s[...] = (qk[:, ML_W:] * (dh ** -0.5)).astype(BF16)

    if has_state:
        c_s[...] = c0_ref[0]
        n_s[...] = n0_ref[0]
        m_s[...] = m0_ref[0]
    else:
        c_s[...] = jnp.zeros_like(c_s)
        n_s[...] = jnp.zeros_like(n_s)
        m_s[...] = jnp.zeros_like(m_s)

    ri = lax.broadcasted_iota(jnp.int32, (tc, tc), 0)
    ci = lax.broadcasted_iota(jnp.int32, (tc, tc), 1)

    for d in range(2):
        rev = d == 1
        mask = (ci >= ri) if rev else (ci <= ri)
        edge = 0 if rev else tc - 1

        def chunk(j, carry, d=d, rev=rev, mask=mask, edge=edge):
            cidx = (nc - 1 - j) if rev else j
            r0 = pl.multiple_of(cidx * tc, tc)
            pre = g_ref[pl.ds(r0, tc), :] + gb_ref[...]
            pre_t = gt_ref[cidx] + gbt_ref[...]
            cum = _cumsum_rows(_log_sigmoid(pre), tc, rev)
            cum_t = _cumsum_lanes(_log_sigmoid(pre_t), tc, rev)
            for h in range(nh):
                col = d * nh + h
                s_idx = d * nh + h
                ig_col = pre[:, col:col + 1]
                ig_row = pre_t[col:col + 1, :]
                b_col = cum[:, 8 + col:9 + col]
                b_row = cum_t[8 + col:9 + col, :]
                b_end = b_col[edge:edge + 1, :]
                m_prev = m_s[s_idx:s_idx + 1, 0:1]
                dmat = jnp.where(mask, b_col - b_row + ig_row, -jnp.inf)
                inter = b_col + m_prev
                m_row = jnp.maximum(inter, jnp.max(dmat, axis=1, keepdims=True))
                w_intra = jnp.exp(dmat - m_row)
                w_state = jnp.exp(inter - m_row)
                qh = q_s[pl.ds(r0, tc), h * dh:(h + 1) * dh]
                kh = k_s[pl.ds(r0, tc), h * dh:(h + 1) * dh]
                vh = v_ref[pl.ds(r0, tc), h * dh:(h + 1) * dh].astype(BF16)
                c_prev = c_s[s_idx]
                n_prev = n_s[s_idx:s_idx + 1, :]
                s = lax.dot_general(qh, kh, (((1,), (1,)), ((), ())), preferred_element_type=F32) * w_intra
                num = (jnp.dot(s.astype(BF16), vh, preferred_element_type=F32)
                       + w_state * jnp.dot(qh, c_prev.astype(BF16), preferred_element_type=F32))
                den = (jnp.sum(s, axis=1, keepdims=True)
                       + w_state * jnp.sum(qh.astype(F32) * n_prev, axis=1, keepdims=True))
                hout = num / jnp.maximum(jnp.abs(den), jnp.exp(-m_row))
                if d == 0:
                    h_s[pl.ds(r0, tc), h * dh:(h + 1) * dh] = hout
                else:
                    h_s[pl.ds(r0, tc), h * dh:(h + 1) * dh] += hout
                g_col = b_end - b_col + ig_col
                m_new = jnp.maximum(b_end + m_prev, jnp.max(g_col, axis=0, keepdims=True))
                wg = jnp.exp(g_col - m_new)
                decay = jnp.exp(b_end + m_prev - m_new)
                kw = kh.astype(F32) * wg
                c_s[s_idx] = decay * c_prev + lax.dot_general(
                    kw.astype(BF16), vh, (((0,), (0,)), ((), ())), preferred_element_type=F32)
                n_s[s_idx:s_idx + 1, :] = decay * n_prev + jnp.sum(kw, axis=0, keepdims=True)
                m_s[s_idx:s_idx + 1, :] = jnp.broadcast_to(m_new, (1, LANES))
            return carry

        lax.fori_loop(0, nc, chunk, 0)

    for h in range(nh):
        hs = h_s[:, h * dh:(h + 1) * dh]
        hn = hs * lax.rsqrt(jnp.mean(hs * hs, axis=1, keepdims=True) + EPS)
        y_ref[:, h * dh:(h + 1) * dh] = (jax.nn.sigmoid(o_ref[:, h * dh:(h + 1) * dh])
                                          * (hn * ng_ref[:, h * dh:(h + 1) * dh]))
    cout_ref[0] = c_s[...]
    nout_ref[0] = n_s[...]
    mout_ref[0] = m_s[...]


def _mlstm(qk, v, o, gates, gates_t, L, n_seq, row_block0, sw, gb, gbt, ng, state):
    nc = L // ML_CHUNK
    has_state = state is not None
    full = lambda a: pl.BlockSpec(a.shape, lambda b: (0,) * a.ndim)
    in_specs = [pl.BlockSpec((L, 2 * ML_W), lambda b: (row_block0 + b, 0)),
                pl.BlockSpec((L, ML_W), lambda b: (row_block0 + b, 0)),
                pl.BlockSpec((L, ML_W), lambda b: (row_block0 + b, 0)),
                pl.BlockSpec((L, GATE_PAD), lambda b: (row_block0 + b, 0)),
                pl.BlockSpec((nc, 16, ML_CHUNK), lambda b: (row_block0 + b, 0, 0)),
                full(sw), full(gb), full(gbt), full(ng)]
    args = [qk, v, o, gates, gates_t, sw, gb, gbt, ng]
    if has_state:
        c0, n0, m0 = state
        in_specs += [pl.BlockSpec((1, 2 * ML_HEADS, ML_DH, ML_DH), lambda b: (b, 0, 0, 0)),
                     pl.BlockSpec((1, 2 * ML_HEADS, ML_DH), lambda b: (b, 0, 0)),
                     pl.BlockSpec((1, 2 * ML_HEADS, LANES), lambda b: (b, 0, 0))]
        args += [c0, n0, m0]
    return pl.pallas_call(
        functools.partial(_mlstm_kernel, L=L, has_state=has_state),
        grid=(n_seq,),
        in_specs=in_specs,
        out_specs=[pl.BlockSpec((L, ML_W), lambda b: (b, 0)),
                   pl.BlockSpec((1, 2 * ML_HEADS, ML_DH, ML_DH), lambda b: (b, 0, 0, 0)),
                   pl.BlockSpec((1, 2 * ML_HEADS, ML_DH), lambda b: (b, 0, 0)),
                   pl.BlockSpec((1, 2 * ML_HEADS, LANES), lambda b: (b, 0, 0))],
        out_shape=[jax.ShapeDtypeStruct((n_seq * L, ML_W), F32),
                   jax.ShapeDtypeStruct((n_seq, 2 * ML_HEADS, ML_DH, ML_DH), F32),
                   jax.ShapeDtypeStruct((n_seq, 2 * ML_HEADS, ML_DH), F32),
                   jax.ShapeDtypeStruct((n_seq, 2 * ML_HEADS, LANES), F32)],
        scratch_shapes=[pltpu.VMEM((L, ML_W), BF16), pltpu.VMEM((L, ML_W), BF16),
                        pltpu.VMEM((L, ML_W), F32),
                        pltpu.VMEM((2 * ML_HEADS, ML_DH, ML_DH), F32),
                        pltpu.VMEM((2 * ML_HEADS, ML_DH), F32),
                        pltpu.VMEM((2 * ML_HEADS, LANES), F32)],
        compiler_params=_cparams(("arbitrary",)),
        name=f"mlstm_{L}",
    )(*args)


def _cmul(ar, ai, br, bi):
    return ar * br - ai * bi, ar * bi + ai * br


def _s5_kernel(*refs, segmented):
    if segmented:
        (u_ref, b_ref, c_ref, lam_ref, dsk_ref, wglu_ref, s0_ref, y_ref, sbuf, yacc, pw) = refs
    else:
        (u_ref, b_ref, c_ref, lam_ref, dsk_ref, wglu_ref, y_ref, fin_ref, sbuf, yacc) = refs
    n = S5_STATE
    rows = S5_ROWS
    steps = S5_STEPS
    blk = 256
    n_blk = steps * rows // blk
    n_seg = 4

    yacc[...] = u_ref[0] * dsk_ref[...]
    ub = u_ref[0].astype(BF16)

    for d in range(2):
        rev = d == 1
        for i in range(n_blk):
            sbuf[i * blk:(i + 1) * blk, :] = jnp.dot(ub[i * blk:(i + 1) * blk], b_ref[d],
                                                     preferred_element_type=F32)
        lam = lam_ref[d]
        lr = jnp.broadcast_to(lam[:, :n], (rows, n))
        li = jnp.broadcast_to(lam[:, n:], (rows, n))

        def step(i, carry, rev=rev, lr=lr, li=li):
            sr, si = carry
            t = (steps - 1 - i) if rev else i
            off = pl.multiple_of(t * rows, rows)
            pr, pi = _cmul(lr, li, sr, si)
            nr = pr + sbuf[pl.ds(off, rows), 0:n]
            ni = pi + sbuf[pl.ds(off, rows), n:2 * n]
            sbuf[pl.ds(off, rows), 0:n] = nr
            sbuf[pl.ds(off, rows), n:2 * n] = ni
            return nr, ni

        zero = jnp.zeros((rows, n), F32)
        sr, si = lax.fori_loop(0, steps, step, (zero, zero), unroll=2)

        if not segmented:
            fin_ref[0, d, :, 0:n] = sr
            fin_ref[0, d, :, n:2 * n] = si
        else:
            lam_r, lam_i = lam[:, :n], lam[:, n:]
            row8 = lax.broadcasted_iota(jnp.int32, (rows, 1), 0)
            cr, ci = lam_r, lam_i
            acc_r = jnp.broadcast_to(cr, (rows, n))
            acc_i = jnp.broadcast_to(ci, (rows, n))
            for j in range(1, rows):
                cr, ci = _cmul(cr, ci, lam_r, lam_i)
                acc_r = jnp.where(row8 >= j, jnp.broadcast_to(cr, (rows, n)), acc_r)
                acc_i = jnp.where(row8 >= j, jnp.broadcast_to(ci, (rows, n)), acc_i)
            pw[0:rows, 0:n] = acc_r
            pw[0:rows, n:2 * n] = acc_i
            size = rows
            while size < steps:
                tr = pw[size - 1:size, 0:n]
                ti = pw[size - 1:size, n:2 * n]
                xr, xi = _cmul(pw[0:size, 0:n], pw[0:size, n:2 * n], tr, ti)
                pw[size:2 * size, 0:n] = xr
                pw[size:2 * size, n:2 * n] = xi
                size *= 2
            end_off = 0 if rev else (steps - 1) * rows
            loc_r = sbuf[end_off:end_off + rows, 0:n]
            loc_i = sbuf[end_off:end_off + rows, n:2 * n]
            pl_r = pw[steps - 1:steps, 0:n]
            pl_i = pw[steps - 1:steps, n:2 * n]
            s0r = s0_ref[d, :, 0:n]
            s0i = s0_ref[d, :, n:2 * n]
            seg = row8 // 2
            first = (seg == n_seg - 1) if rev else (seg == 0)
            shift = (rows - 2) if rev else 2
            cin_r, cin_i = s0r, s0i
            for _ in range(n_seg - 1):
                fr, fi = _cmul(jnp.broadcast_to(pl_r, (rows, n)), jnp.broadcast_to(pl_i, (rows, n)), cin_r, cin_i)
                tru_r = loc_r + fr
                tru_i = loc_i + fi
                cin_r = jnp.where(first, s0r, pltpu.roll(tru_r, shift, 0))
                cin_i = jnp.where(first, s0i, pltpu.roll(tru_i, shift, 0))

            def fix(tb, carry, rev=rev, cin_r=cin_r, cin_i=cin_i):
                pb = (steps // rows - 1 - tb) if rev else tb
                poff = pl.multiple_of(pb * rows, rows)
                p_r = pw[pl.ds(poff, rows), 0:n]
                p_i = pw[pl.ds(poff, rows), n:2 * n]
                for j in range(rows):
                    jj = rows - 1 - j if rev else j
                    off = pl.multiple_of((tb * rows + j) * rows, rows)
                    fr, fi = _cmul(jnp.broadcast_to(p_r[jj:jj + 1], (rows, n)),
                                   jnp.broadcast_to(p_i[jj:jj + 1], (rows, n)), cin_r, cin_i)
                    sbuf[pl.ds(off, rows), 0:n] += fr
                    sbuf[pl.ds(off, rows), n:2 * n] += fi
                return carry

            lax.fori_loop(0, steps // rows, fix, 0)

        for i in range(n_blk):
            yacc[i * blk:(i + 1) * blk, :] += jnp.dot(sbuf[i * blk:(i + 1) * blk, :].astype(BF16), c_ref[d],
                                                      preferred_element_type=F32)

    g = jax.nn.gelu(yacc[...], approximate=True)
    y_ref[0] = g * jax.nn.sigmoid(_bdot(g, wglu_ref[...]))


def _s5(u_tm, bmat, cmat, lam, dskip, wglu, s0):
    n_grp = u_tm.shape[0]
    n_rows = S5_STEPS * S5_ROWS
    segmented = s0 is not None
    full = lambda a: pl.BlockSpec(a.shape, lambda g: (0,) * a.ndim)
    in_specs = [pl.BlockSpec((1, n_rows, S5_CH), lambda g: (g, 0, 0)),
                full(bmat), full(cmat), full(lam), full(dskip), full(wglu)]
    args = [u_tm, bmat, cmat, lam, dskip, wglu]
    out_specs = [pl.BlockSpec((1, n_rows, S5_CH), lambda g: (g, 0, 0))]
    out_shape = [jax.ShapeDtypeStruct((n_grp, n_rows, S5_CH), F32)]
    scratch = [pltpu.VMEM((n_rows, 2 * S5_STATE), F32), pltpu.VMEM((n_rows, S5_CH), F32)]
    if segmented:
        in_specs.append(full(s0))
        args.append(s0)
        scratch.append(pltpu.VMEM((S5_STEPS, 2 * S5_STATE), F32))
    else:
        out_specs.append(pl.BlockSpec((1, 2, S5_ROWS, 2 * S5_STATE), lambda g: (g, 0, 0, 0)))
        out_shape.append(jax.ShapeDtypeStruct((n_grp, 2, S5_ROWS, 2 * S5_STATE), F32))
    return pl.pallas_call(
        functools.partial(_s5_kernel, segmented=segmented),
        grid=(n_grp,),
        in_specs=in_specs,
        out_specs=out_specs,
        out_shape=out_shape,
        scratch_shapes=scratch,
        compiler_params=_cparams(("arbitrary",)),
        name="s5_seg" if segmented else "s5_ctx",
    )(*args)


def _s5_params(a_re, a_im, log_dt, b_re, b_im, c_re, c_im):
    dt = jnp.exp(log_dt)[:, :, None]
    mag = jnp.exp(a_re * dt)
    lb_re = mag * jnp.cos(a_im * dt)
    lb_im = mag * jnp.sin(a_im * dt)
    den = a_re * a_re + a_im * a_im
    nr, ni = lb_re - 1.0, lb_im
    k_re = (nr * a_re + ni * a_im) / den
    k_im = (ni * a_re - nr * a_im) / den
    bb_re = k_re[..., None] * b_re - k_im[..., None] * b_im
    bb_im = k_re[..., None] * b_im + k_im[..., None] * b_re
    eye = jnp.eye(S5_G, dtype=F32)
    blk_b = lambda m: jnp.einsum('dgpc,gh->dgchp', m, eye).reshape(2, S5_CH, S5_STATE)
    blk_c = lambda m: jnp.einsum('dgcp,gh->dgphc', m, eye).reshape(2, S5_STATE, S5_CH)
    bmat = jnp.concatenate([blk_b(bb_re), blk_b(bb_im)], axis=2).astype(BF16)
    cmat = jnp.concatenate([blk_c(c_re), -blk_c(c_im)], axis=1).astype(BF16)
    lam = jnp.concatenate([lb_re.reshape(2, 1, S5_STATE), lb_im.reshape(2, 1, S5_STATE)], axis=2)
    return bmat, cmat, lam


def _outproj_kernel(x_ref, hy_ref, ml_ref, s5_ref, w_ref, mod_ref, g_ref, rw_ref, rb_ref,
                    xn_ref, h2e_ref, best_ref):
    mod = mod_ref[0]
    a, b = HY_CH, HY_CH + ML_W
    mix = (jnp.dot(hy_ref[...].astype(BF16), w_ref[0:a, :], preferred_element_type=F32)
           + jnp.dot(ml_ref[...].astype(BF16), w_ref[a:b, :], preferred_element_type=F32)
           + jnp.dot(s5_ref[...].astype(BF16), w_ref[b:, :], preferred_element_type=F32))
    xn = x_ref[...] + mod[2:3] * mix
    xn_ref[...] = xn
    h2 = _rms(xn, g_ref[...]) * (1.0 + mod[4:5]) + mod[3:4]
    h2e_ref[:, 0:D] = h2
    logits = lax.dot_general(rw_ref[...], h2, (((1,), (1,)), ((), ())),
                             precision=HIGHEST, preferred_element_type=F32)
    ex = jnp.exp(logits - jnp.max(logits, axis=0, keepdims=True))
    probs = ex / jnp.sum(ex, axis=0, keepdims=True)
    sel = probs + rb_ref[...]
    best = None
    best_score = None
    for g in range(N_GROUPS):
        r = [sel[g * GROUP_SIZE + i:g * GROUP_SIZE + i + 1, :] for i in range(GROUP_SIZE)]
        score = None
        for i in range(GROUP_SIZE):
            for j in range(i + 1, GROUP_SIZE):
                pair = r[i] + r[j]
                score = pair if score is None else jnp.maximum(score, pair)
        if g == 0:
            best, best_score = jnp.zeros_like(score, dtype=jnp.int32), score
        else:
            upd = score > best_score
            best = jnp.where(upd, g, best)
            best_score = jnp.where(upd, score, best_score)
    eid = lax.broadcasted_iota(jnp.int32, (N_EXPERTS, 1), 0)
    masked = jnp.where(eid // GROUP_SIZE == best, sel, -jnp.inf)
    m1 = jnp.max(masked, axis=0, keepdims=True)
    i1 = jnp.min(jnp.where(masked == m1, eid, N_EXPERTS), axis=0, keepdims=True)
    masked2 = jnp.where(eid == i1, -jnp.inf, masked)
    m2 = jnp.max(masked2, axis=0, keepdims=True)
    i2 = jnp.min(jnp.where(masked2 == m2, eid, N_EXPERTS), axis=0, keepdims=True)
    p1 = jnp.sum(jnp.where(eid == i1, probs, 0.0), axis=0, keepdims=True)
    p2 = jnp.sum(jnp.where(eid == i2, probs, 0.0), axis=0, keepdims=True)
    tot = p1 + p2
    comb = jnp.where(eid == i1, p1 / tot, 0.0) + jnp.where(eid == i2, p2 / tot, 0.0)
    comb = jnp.concatenate([comb, jnp.zeros((LANES - N_EXPERTS, comb.shape[1]), F32)], axis=0)
    h2e_ref[:, D:] = comb.T
    best_ref[...] = best


def _outproj(x, y_hy, y_ml, y_s5, w_out, mod, g2, rw_t, rb):
    full = lambda a: pl.BlockSpec(a.shape, lambda i: (0,) * a.ndim)
    return pl.pallas_call(
        _outproj_kernel,
        grid=(T_ALL // TM,),
        in_specs=[pl.BlockSpec((TM, D), lambda i: (i, 0)),
                  pl.BlockSpec((TM, HY_CH), lambda i: (i, 0)),
                  pl.BlockSpec((TM, ML_W), lambda i: (i, 0)),
                  pl.BlockSpec((TM, S5_CH), lambda i: (i, 0)),
                  full(w_out),
                  pl.BlockSpec((1, 6, D), lambda i: (_mod_row(i, TM), 0, 0)),
                  full(g2), full(rw_t), full(rb)],
        out_specs=[pl.BlockSpec((TM, D), lambda i: (i, 0)),
                   pl.BlockSpec((TM, D + LANES), lambda i: (i, 0)),
                   pl.BlockSpec((1, TM), lambda i: (0, i))],
        out_shape=[jax.ShapeDtypeStruct((T_ALL, D), F32),
                   jax.ShapeDtypeStruct((T_ALL, D + LANES), F32),
                   jax.ShapeDtypeStruct((1, T_ALL), jnp.int32)],
        compiler_params=_cparams(("arbitrary",)),
        name="outproj_router",
    )(x, y_hy, y_ml, y_s5, w_out, mod, g2, rw_t, rb)


def _gather_rows(idx_ref, idx_base, src_ref, dst_ref, n_rows):
    def body(r8, carry):
        base = pl.multiple_of(r8 * 8, 8)
        for k in range(8):
            idx = idx_ref[idx_base + base + k]
            dst_ref[pl.ds(base + k, 1), :] = src_ref[pl.ds(idx, 1), :]
        return carry

    lax.fori_loop(0, n_rows // 8, body, 0)


def _dispatch(best):
    tm = TM_MOE
    g = best.reshape(T_ALL)
    onehot = (g[:, None] == jnp.arange(N_GROUPS, dtype=jnp.int32)[None, :]).astype(jnp.int32)
    cnt = jnp.sum(onehot, axis=0)
    n_tile_g = (cnt + tm - 1) // tm
    tile_end = jnp.cumsum(n_tile_g)
    row_off = (tile_end - n_tile_g) * tm
    rank = jnp.cumsum(onehot, axis=0) - onehot
    pos = jnp.sum(onehot * (row_off[None, :] + rank), axis=1).astype(jnp.int32)
    src = jnp.zeros((MOE_SLOTS,), jnp.int32).at[pos].set(jnp.arange(T_ALL, dtype=jnp.int32))
    tiles = jnp.arange(MOE_SLOTS // tm, dtype=jnp.int32)
    tile_group = jnp.minimum(jnp.sum((tiles[:, None] >= tile_end[None, :]).astype(jnp.int32), axis=1), N_GROUPS - 1)
    return pos, src, tile_group.astype(jnp.int32), tile_end[N_GROUPS - 1:].astype(jnp.int32)


def _moe_kernel(src_ref, tg_ref, nt_ref, h_ref, wg_ref, wu_ref, wd_ref, ys_ref, gbuf, xs, acc):
    i = pl.program_id(0)
    j = pl.program_id(1)
    tm = TM_MOE

    @pl.when(i < nt_ref[0])
    def _():
        @pl.when(j == 0)
        def _():
            _gather_rows(src_ref, i * tm, h_ref, gbuf, tm)
            xs[...] = gbuf[:, 0:D].astype(BF16)

        e = tg_ref[i] * GROUP_SIZE + j
        lane = lax.broadcasted_iota(jnp.int32, (1, LANES), 1)
        wg = wg_ref[0, 0].astype(BF16)
        wu = wu_ref[0, 0].astype(BF16)
        wd = wd_ref[0, 0].astype(BF16)
        sub = 256
        for s in range(tm // sub):
            rs = slice(s * sub, (s + 1) * sub)
            hb = xs[rs, :]
            ce = jnp.sum(jnp.where(lane == e, gbuf[rs, D:], 0.0), axis=1, keepdims=True)
            hid = _silu(jnp.dot(hb, wg, preferred_element_type=F32)) * jnp.dot(hb, wu, preferred_element_type=F32)
            part = jnp.dot((hid * ce).astype(BF16), wd, preferred_element_type=F32)

            @pl.when(j == 0)
            def _():
                acc[rs, :] = part

            @pl.when(j > 0)
            def _():
                acc[rs, :] += part

        @pl.when(j == GROUP_SIZE - 1)
        def _():
            ys_ref[...] = acc[...]


def _moe(h2e, src, tile_group, n_tiles, wg, wu, wd, l):
    tm = TM_MOE

    def w_map(i, j, src, tg, nt):
        live = i < nt[0]
        ii = jnp.minimum(i, nt[0] - 1)
        return (l, tg[ii] * GROUP_SIZE + jnp.where(live, j, GROUP_SIZE - 1), 0, 0)

    return pl.pallas_call(
        _moe_kernel,
        grid_spec=pltpu.PrefetchScalarGridSpec(
            num_scalar_prefetch=3,
            grid=(MOE_SLOTS // tm, GROUP_SIZE),
            in_specs=[pl.BlockSpec(memory_space=pltpu.VMEM),
                      pl.BlockSpec((1, 1, D, D_EXPERT), w_map),
                      pl.BlockSpec((1, 1, D, D_EXPERT), w_map),
                      pl.BlockSpec((1, 1, D_EXPERT, D), w_map)],
            out_specs=pl.BlockSpec((tm, D), lambda i, j, src, tg, nt: (jnp.minimum(i, nt[0] - 1), 0)),
            scratch_shapes=[pltpu.VMEM((tm, D + LANES), F32), pltpu.VMEM((tm, D), BF16),
                            pltpu.VMEM((tm, D), F32)]),
        out_shape=jax.ShapeDtypeStruct((MOE_SLOTS, D), F32),
        compiler_params=_cparams(("arbitrary", "arbitrary")),
        name="moe_experts",
    )(src, tile_group, n_tiles, h2e, wg, wu, wd)


def _combine_kernel(pos_ref, ys_ref, xn_ref, mod_ref, fg_ref, *rest, final):
    if final:
        out_ref, y_ref, gbuf = rest
    else:
        out_ref, gbuf = rest
    _gather_rows(pos_ref, pl.program_id(0) * TM, ys_ref, gbuf, TM)
    out = xn_ref[...] + mod_ref[0][5:6] * gbuf[...]
    out_ref[...] = out
    if final:
        y_ref[...] = _rms(out, fg_ref[...])


def _combine(pos, ys, xn, mod, fg, final):
    spec = pl.BlockSpec((TM, D), lambda i, pos: (i, 0))
    n_out = 2 if final else 1
    return pl.pallas_call(
        functools.partial(_combine_kernel, final=final),
        grid_spec=pltpu.PrefetchScalarGridSpec(
            num_scalar_prefetch=1,
            grid=(T_ALL // TM,),
            in_specs=[pl.BlockSpec(memory_space=pltpu.VMEM),
                      spec,
                      pl.BlockSpec((1, 6, D), lambda i, pos: (_mod_row(i, TM), 0, 0)),
                      pl.BlockSpec((1, D), lambda i, pos: (0, 0))],
            out_specs=[spec] * n_out,
            scratch_shapes=[pltpu.VMEM((TM, D), F32)]),
        out_shape=[jax.ShapeDtypeStruct((T_ALL, D), F32)] * n_out,
        compiler_params=_cparams(("arbitrary",)),
        name="moe_combine",
    )(pos, ys, xn, mod, fg)


@functools.lru_cache(None)
def _pos_embed():
    rows = L_LAT // GRID_W
    r = np.repeat(np.arange(rows, dtype=np.float64), GRID_W)
    col = np.tile(np.arange(GRID_W, dtype=np.float64), rows)
    quarter = D // 4
    freq = np.exp(-math.log(POS_BASE) * np.arange(quarter, dtype=np.float64) / quarter)
    ar = r[:, None] * freq[None]
    ac = col[:, None] * freq[None]
    emb = np.concatenate([np.sin(ar), np.cos(ar), np.sin(ac), np.cos(ac)], axis=-1)
    return emb.astype(np.float32)


def _pad_w_in(w):
    segs = [w[:, OFF_HY:OFF_QK], w[:, OFF_QK:OFF_V], w[:, OFF_V:OFF_O], w[:, OFF_O:OFF_G],
            w[:, OFF_S5:IN_W], w[:, OFF_G:OFF_S5], jnp.zeros((D, GATE_PAD - 16), w.dtype)]
    return jnp.concatenate(segs, axis=1).astype(BF16)


def _to_time_major_ctx(a):
    c = a.shape[-1]
    a = a.reshape(2, S5_ROWS, L_CTX, c).transpose(0, 2, 1, 3)
    return a.reshape(2, L_CTX * S5_ROWS, c)


def _from_time_major_ctx(a):
    c = a.shape[-1]
    a = a.reshape(2, L_CTX, S5_ROWS, c).transpose(0, 2, 1, 3)
    return a.reshape(T_CTX, c)


def _to_time_major_lat(a):
    c = a.shape[-1]
    a = a.reshape(N_LAT_SEQ, 4, S5_STEPS, c).transpose(2, 1, 0, 3)
    return a.reshape(1, S5_STEPS * S5_ROWS, c)


def _from_time_major_lat(a):
    c = a.shape[-1]
    a = a.reshape(S5_STEPS, 4, N_LAT_SEQ, c).transpose(2, 1, 0, 3)
    return a.reshape(T_LAT, c)


def kernel(x_prompt, x_sample, c, state_mlstm_C, state_mlstm_n, state_mlstm_m, state_s5_re, state_s5_im, c_ctx, w_ada, b_ada, norm1_g, norm2_g, final_g, w_in, w_out, hy_short, hy_fw1, hy_fb1, hy_fw2, hy_fb2, hy_fw3, hy_log_decay, hy_bias, ml_short, ml_gate_bias, ml_norm_g, s5_a_re, s5_a_im, s5_log_dt, s5_b_re, s5_b_im, s5_c_re, s5_c_im, s5_d, s5_w_glu, router_w, router_b, moe_w_gate, moe_w_up, moe_w_down):
    x = jnp.concatenate([x_prompt.reshape(T_CTX, D), (x_sample + _pos_embed()[None]).reshape(T_LAT, D)], axis=0)
    cc = jnp.concatenate([c_ctx[None], c, jnp.zeros((8 - 1 - N_LAT_SEQ, D), F32)], axis=0)
    mod_all = _ada(cc, w_ada, b_ada).reshape(DEPTH, 8, 6, D)
    rw_t = router_w.T
    rb = router_b.reshape(N_EXPERTS, 1)
    fg = final_g.reshape(1, D)
    lat_blk = T_CTX // L_LAT

    new_c, new_n, new_m, new_re, new_im = [], [], [], [], []
    y_all = None
    for l in range(DEPTH):
        mod = mod_all[l]
        u_hy, qk, v, o, u_s5, gates = _inproj(x, norm1_g[l], mod, _pad_w_in(w_in[l]))

        w1p = jnp.zeros((LANES, HY_FILTER_W), F32).at[:HY_EMB].set(hy_fw1[l])
        b1 = hy_fb1[l].reshape(1, HY_FILTER_W)
        b2 = hy_fb2[l].reshape(1, HY_FILTER_W)
        ld = hy_log_decay[l].reshape(1, 4 * HY_CH)
        y_hy = []
        for L, n_seq, blk0 in ((L_CTX, N_CTX_SEQ, 0), (L_LAT, N_LAT_SEQ, lat_blk)):
            p, q, r = _hy_filter(L, w1p, b1, hy_fw2[l], b2, hy_fw3[l], ld)
            y_hy.append(_hyena(u_hy, L, n_seq, blk0, hy_short[l], hy_bias[l], p, q, r))
        y_hy = jnp.concatenate(y_hy, axis=0)

        gates_t = gates[:, :16].reshape(T_ALL // ML_CHUNK, ML_CHUNK, 16).transpose(0, 2, 1)
        gb = jnp.zeros((1, GATE_PAD), F32).at[0, :16].set(ml_gate_bias[l].reshape(16))
        gbt = ml_gate_bias[l].reshape(16, 1)
        ng = ml_norm_g[l].reshape(1, ML_W)
        m0 = jnp.broadcast_to(state_mlstm_m[:, l].reshape(N_LAT_SEQ, 2 * ML_HEADS, 1), (N_LAT_SEQ, 2 * ML_HEADS, LANES))
        yc, cc_, nc_, mc_ = _mlstm(qk, v, o, gates, gates_t, L_CTX, N_CTX_SEQ, 0, ml_short[l], gb, gbt, ng, None)
        yl, _, _, _ = _mlstm(qk, v, o, gates, gates_t, L_LAT, N_LAT_SEQ, lat_blk, ml_short[l], gb, gbt, ng,
                             (state_mlstm_C[:, l].reshape(N_LAT_SEQ, 2 * ML_HEADS, ML_DH, ML_DH),
                              state_mlstm_n[:, l].reshape(N_LAT_SEQ, 2 * ML_HEADS, ML_DH), m0))
        y_ml = jnp.concatenate([yc, yl], axis=0)
        new_c.append(cc_.reshape(N_CTX_SEQ, 2, ML_HEADS, ML_DH, ML_DH))
        new_n.append(nc_.reshape(N_CTX_SEQ, 2, ML_HEADS, ML_DH))
        new_m.append(mc_[:, :, 0].reshape(N_CTX_SEQ, 2, ML_HEADS))

        bmat, cmat, lam = _s5_params(s5_a_re[l], s5_a_im[l], s5_log_dt[l], s5_b_re[l], s5_b_im[l],
                                     s5_c_re[l], s5_c_im[l])
        dsk = s5_d[l].reshape(1, S5_CH)
        wglu = s5_w_glu[l].astype(BF16)
        ys_c, fin = _s5(_to_time_major_ctx(u_s5[:T_CTX]), bmat, cmat, lam, dsk, wglu, None)
        s0 = jnp.concatenate([state_s5_re[:, l].reshape(N_LAT_SEQ, 2, S5_STATE),
                              state_s5_im[:, l].reshape(N_LAT_SEQ, 2, S5_STATE)], axis=-1)
        s0 = jnp.tile(s0.transpose(1, 0, 2), (1, 4, 1))
        (ys_l,) = _s5(_to_time_major_lat(u_s5[T_CTX:]), bmat, cmat, lam, dsk, wglu, s0)
        y_s5 = jnp.concatenate([_from_time_major_ctx(ys_c), _from_time_major_lat(ys_l)], axis=0)
        fin = fin.transpose(0, 2, 1, 3).reshape(N_CTX_SEQ, 2, 2 * S5_STATE)
        new_re.append(fin[..., :S5_STATE].reshape(N_CTX_SEQ, 2, S5_G, S5_P))
        new_im.append(fin[..., S5_STATE:].reshape(N_CTX_SEQ, 2, S5_G, S5_P))

        xn, h2e, best = _outproj(x, y_hy, y_ml, y_s5, w_out[l].astype(BF16), mod,
                                 norm2_g[l].reshape(1, D), rw_t, rb)
        pos, src, tile_group, n_tiles = _dispatch(best)
        ys = _moe(h2e, src, tile_group, n_tiles, moe_w_gate, moe_w_up, moe_w_down, l)
        res = _combine(pos, ys, xn, mod, fg, l == DEPTH - 1)
        x = res[0]
        if l == DEPTH - 1:
            y_all = res[1]

    y_prompt = y_all[:T_CTX].reshape(N_CTX_SEQ, L_CTX, D)
    y_sample = y_all[T_CTX:].reshape(N_LAT_SEQ, L_LAT, D)
    return (y_prompt, y_sample, jnp.stack(new_c, axis=1), jnp.stack(new_n, axis=1), jnp.stack(new_m, axis=1),
            jnp.stack(new_re, axis=1), jnp.stack(new_im, axis=1))
```

```python
import functools
import math

import numpy as np
import jax
import jax.numpy as jnp
from jax import lax
from jax.experimental import pallas as pl
from jax.experimental.pallas import tpu as pltpu

F32 = jnp.float32
BF16 = jnp.bfloat16
HIGHEST = lax.Precision.HIGHEST

D = 1024
N_CTX_SEQ, L_CTX = 16, 256
N_LAT_SEQ, L_LAT = 2, 1024
T_CTX = N_CTX_SEQ * L_CTX
T_LAT = N_LAT_SEQ * L_LAT
T_ALL = T_CTX + T_LAT
DEPTH = 2
EPS = 1e-6
GRID_W = 64
POS_BASE = 10000.0
HY_CH = 256
HY_EMB = 33
HY_FILTER_W = 64
ML_HEADS = 4
ML_DH = 128
ML_W = ML_HEADS * ML_DH
S5_CH = 256
S5_G = 16
S5_GROUP = 16
S5_P = 64
S5_STATE = S5_G * S5_P
N_EXPERTS = 16
N_GROUPS = 4
GROUP_SIZE = N_EXPERTS // N_GROUPS
D_EXPERT = 512
OFF_HY = 0
OFF_QK = 3 * HY_CH
OFF_V = OFF_QK + 2 * ML_W
OFF_O = OFF_V + ML_W
OFF_G = OFF_O + ML_W
OFF_S5 = OFF_G + 16
IN_W = OFF_S5 + S5_CH
LANES = 128
GATE_PAD = LANES

TM = 512
TM_MOE = 512
MOE_SLOTS = T_ALL + N_GROUPS * TM_MOE
ML_CHUNK = 256
S5_ROWS = 8
S5_STEPS = 256
VMEM_LIMIT = 56 * 1024 * 1024


def _cparams(sem, vmem=VMEM_LIMIT):
    if sem is None:
        return pltpu.CompilerParams(vmem_limit_bytes=vmem)
    return pltpu.CompilerParams(dimension_semantics=sem, vmem_limit_bytes=vmem)


def _bdot(a, b):
    return jnp.dot(a.astype(BF16), b.astype(BF16), preferred_element_type=F32)


def _silu(x):
    return x * jax.nn.sigmoid(x)


def _rms(x, g):
    return x * lax.rsqrt(jnp.mean(x * x, axis=-1, keepdims=True) + EPS) * g


def _log_sigmoid(x):
    return jnp.minimum(x, 0.0) - jnp.log1p(jnp.exp(-jnp.abs(x)))


def _conv3(u, w, n_rows, seq_len):
    row = lax.broadcasted_iota(jnp.int32, (n_rows, 1), 0) % seq_len
    prev = jnp.where(row == 0, 0.0, pltpu.roll(u, 1, 0))
    nxt = jnp.where(row == seq_len - 1, 0.0, pltpu.roll(u, n_rows - 1, 0))
    return prev * w[0:1] + u * w[1:2] + nxt * w[2:3]


def _ada_kernel(c_ref, w_ref, b_ref, o_ref):
    o_ref[0] = _bdot(_silu(c_ref[...]), w_ref[0]) + b_ref[0]


def _ada(cc, w_ada, b_ada):
    tn = 1536
    return pl.pallas_call(
        _ada_kernel,
        grid=(DEPTH, 6 * D // tn),
        in_specs=[pl.BlockSpec((8, D), lambda l, j: (0, 0)),
                  pl.BlockSpec((1, D, tn), lambda l, j: (l, 0, j)),
                  pl.BlockSpec((1, 1, tn), lambda l, j: (l, 0, j))],
        out_specs=pl.BlockSpec((1, 8, tn), lambda l, j: (l, 0, j)),
        out_shape=jax.ShapeDtypeStruct((DEPTH, 8, 6 * D), F32),
        compiler_params=_cparams(("arbitrary", "arbitrary")),
        name="ada_mod",
    )(cc, w_ada, b_ada.reshape(DEPTH, 1, 6 * D))


def _mod_row(i, tm):
    n_ctx = T_CTX // tm
    return jnp.where(i < n_ctx, 0, 1 + (i - n_ctx) // (L_LAT // tm))


_SEG = ((OFF_HY, OFF_QK - OFF_HY), (OFF_QK, OFF_V - OFF_QK), (OFF_V, OFF_O - OFF_V), (OFF_O, OFF_G - OFF_O))
TAIL_W = IN_W - OFF_G


def _inproj_kernel(x_ref, g_ref, mod_ref, w_ref, hy_ref, qk_ref, v_ref, o_ref, s5_ref, gt_ref, wb):
    @pl.when(pl.program_id(0) == 0)
    def _():
        wb[...] = w_ref[0].astype(BF16)

    mod = mod_ref[0]
    h = _rms(x_ref[...], g_ref[...]) * (1.0 + mod[1:2]) + mod[0:1]
    hb = h.astype(BF16)
    for (a, w), ref in zip(_SEG, (hy_ref, qk_ref, v_ref, o_ref)):
        ref[...] = jnp.dot(hb, wb[:, a:a + w], preferred_element_type=F32)
    tail = jnp.dot(hb, wb[:, OFF_G:IN_W], preferred_element_type=F32)
    gt_ref[...] = tail[:, 0:GATE_PAD]
    s5_ref[...] = tail[:, OFF_S5 - OFF_G:TAIL_W]


def _inproj(x, g, mod, w_in, l):
    widths = [w for _, w in _SEG] + [S5_CH, GATE_PAD]
    return pl.pallas_call(
        _inproj_kernel,
        grid=(T_ALL // TM,),
        in_specs=[pl.BlockSpec((TM, D), lambda i: (i, 0)),
                  pl.BlockSpec((1, D), lambda i: (0, 0)),
                  pl.BlockSpec((1, 6, D), lambda i: (_mod_row(i, TM), 0, 0)),
                  pl.BlockSpec((1, D, IN_W), lambda i: (l, 0, 0), pipeline_mode=pl.Buffered(1))],
        out_specs=[pl.BlockSpec((TM, w), lambda i: (i, 0)) for w in widths],
        out_shape=[jax.ShapeDtypeStruct((T_ALL, w), F32) for w in widths],
        scratch_shapes=[pltpu.VMEM((D, IN_W), BF16)],
        compiler_params=_cparams(("arbitrary",)),
        name="norm_inproj",
    )(x, g.reshape(1, D), mod, w_in)


@functools.lru_cache(None)
def _dft_mats(L):
    n = 2 * L
    k = np.arange(L)[:, None]
    t = np.arange(L)[None, :]
    ang = 2.0 * np.pi * ((k * t) % n) / n
    top = np.cos(ang)
    bot = -np.sin(ang)
    bot[0] = np.cos(np.pi * np.arange(L))
    fwd = np.concatenate([top, bot], 0)
    s = np.full((n, 1), 2.0 / n)
    s[0] = s[L] = 1.0 / n
    inv = (fwd * s).T
    return fwd.astype(np.float32), inv.astype(np.float32)


@functools.lru_cache(None)
def _hy_positions(L):
    t = np.linspace(0.0, 1.0, L)
    bands = (HY_EMB - 1) // 2
    f = np.linspace(1e-4, bands - 1, bands)
    w = 2.0 * np.pi * np.arange(L) / L
    ang = w[:, None] * f[None, :]
    z = np.concatenate([t[:, None], np.cos(ang), -np.sin(ang)], -1)
    zp = np.zeros((L, LANES))
    zp[:, :HY_EMB] = z
    return zp.astype(np.float32), t[:, None].astype(np.float32)


def _hy_filter_kernel(z_ref, t_ref, w1_ref, b1_ref, w2_ref, b2_ref, w3_ref, ld_ref, f_ref,
                      p_ref, q_ref, r_ref, *, L):
    h = jnp.sin(jnp.dot(z_ref[...], w1_ref[...], precision=HIGHEST, preferred_element_type=F32) + b1_ref[...])
    h = jnp.sin(jnp.dot(h, w2_ref[...], precision=HIGHEST, preferred_element_type=F32) + b2_ref[...])
    filt = jnp.dot(h, w3_ref[...], precision=HIGHEST, preferred_element_type=F32)
    filt = filt * jnp.exp(-t_ref[...] * jnp.exp(ld_ref[...]))
    c = HY_CH
    h_fwd = jnp.concatenate([filt[:, 0:c], filt[:, 2 * c:3 * c]], axis=1)
    h_bwd = jnp.concatenate([filt[:, c:2 * c], filt[:, 3 * c:4 * c]], axis=1)
    row = lax.broadcasted_iota(jnp.int32, (L, 1), 0)
    h_bwd = jnp.where(row == 0, 0.0, h_bwd)
    a = jnp.dot(f_ref[...], h_fwd.astype(BF16), preferred_element_type=F32)
    b = jnp.dot(f_ref[...], h_bwd.astype(BF16), preferred_element_type=F32)
    re = a[:L] + b[:L]
    im = a[L:] - b[L:]
    nyq = a[L:L + 1] + b[L:L + 1]
    p_ref[...] = re
    q_ref[...] = jnp.where(row == 0, 0.0, im)
    r_ref[...] = jnp.where(row == 0, nyq, re)


def _hy_filter(L, w1p, b1, w2, b2, w3, ld):
    z, t = _hy_positions(L)
    fwd = jnp.asarray(_dft_mats(L)[0]).astype(BF16)
    out = jax.ShapeDtypeStruct((L, 2 * HY_CH), F32)
    return pl.pallas_call(
        functools.partial(_hy_filter_kernel, L=L),
        out_shape=[out, out, out],
        compiler_params=_cparams(None),
        name=f"hyena_filter_{L}",
    )(z, t, w1p, b1, w2, b2, w3, ld, fwd)


def _hyena_kernel(u_ref, sw_ref, bias_ref, p_ref, q_ref, r_ref, f_ref, g_ref, o_ref, *, L):
    c = HY_CH
    u = _conv3(u_ref[...], sw_ref[...], L, L)
    z = u[:, 0:c]
    for o in range(2):
        gate = u[:, (o + 1) * c:(o + 2) * c]
        zf = jnp.dot(f_ref[...], z.astype(BF16), preferred_element_type=F32)
        a, b = zf[:L], zf[L:]
        p = p_ref[:, o * c:(o + 1) * c]
        q = q_ref[:, o * c:(o + 1) * c]
        r = r_ref[:, o * c:(o + 1) * c]
        y_re = (a * p - b * q).astype(BF16)
        y_im = (a * q + b * r).astype(BF16)
        y = (jnp.dot(g_ref[:, :L], y_re, preferred_element_type=F32)
             + jnp.dot(g_ref[:, L:], y_im, preferred_element_type=F32))
        z = gate * (y + bias_ref[o:o + 1, :] * z)
    o_ref[...] = z


def _hyena(u_hy, L, n_seq, row_block0, sw, bias, p, q, r):
    fwd, inv = (jnp.asarray(m).astype(BF16) for m in _dft_mats(L))
    full = lambda a: pl.BlockSpec(a.shape, lambda b: (0,) * a.ndim)
    return pl.pallas_call(
        functools.partial(_hyena_kernel, L=L),
        grid=(n_seq,),
        in_specs=[pl.BlockSpec((L, 3 * HY_CH), lambda b: (row_block0 + b, 0)),
                  full(sw), full(bias), full(p), full(q), full(r), full(fwd), full(inv)],
        out_specs=pl.BlockSpec((L, HY_CH), lambda b: (b, 0)),
        out_shape=jax.ShapeDtypeStruct((n_seq * L, HY_CH), F32),
        compiler_params=_cparams(("arbitrary",)),
        name=f"hyena_{L}",
    )(u_hy, sw, bias, p, q, r, fwd, inv)


def _cumsum_rows(x, n, reverse):
    row = lax.broadcasted_iota(jnp.int32, (n, 1), 0)
    s = 1
    while s < n:
        if reverse:
            x = x + jnp.where(row < n - s, pltpu.roll(x, n - s, 0), 0.0)
        else:
            x = x + jnp.where(row >= s, pltpu.roll(x, s, 0), 0.0)
        s *= 2
    return x


def _cumsum_lanes(x, n, reverse):
    col = lax.broadcasted_iota(jnp.int32, (1, n), 1)
    s = 1
    while s < n:
        if reverse:
            x = x + jnp.where(col < n - s, pltpu.roll(x, n - s, 1), 0.0)
        else:
            x = x + jnp.where(col >= s, pltpu.roll(x, s, 1), 0.0)
        s *= 2
    return x


def _mlstm_kernel(*refs, L, has_state):
    if has_state:
        (qk_ref, v_ref, o_ref, g_ref, gt_ref, sw_ref, gb_ref, gbt_ref, ng_ref, c0_ref, n0_ref, m0_ref,
         y_ref, cout_ref, nout_ref, mout_ref, q_s, k_s, h_s, c_s, n_s, m_s) = refs
    else:
        (qk_ref, v_ref, o_ref, g_ref, gt_ref, sw_ref, gb_ref, gbt_ref, ng_ref,
         y_ref, cout_ref, nout_ref, mout_ref, q_s, k_s, h_s, c_s, n_s, m_s) = refs
    tc = ML_CHUNK
    nc = L // tc
    nh = ML_HEADS
    dh = ML_DH

    qk = _silu(_conv3(qk_ref[...], sw_ref[...], L, L))
    q_s[...] = qk[:, :ML_W].astype(BF16)
    k_s[...] = (qk[:, ML_W:] * (dh ** -0.5)).astype(BF16)

    if has_state:
        c_s[...] = c0_ref[0]
        n_s[...] = n0_ref[0]
        m_s[...] = m0_ref[0]
    else:
        c_s[...] = jnp.zeros_like(c_s)
        n_s[...] = jnp.zeros_like(n_s)
        m_s[...] = jnp.zeros_like(m_s)

    ri = lax.broadcasted_iota(jnp.int32, (tc, tc), 0)
    ci = lax.broadcasted_iota(jnp.int32, (tc, tc), 1)

    for d in range(2):
        rev = d == 1
        mask = (ci >= ri) if rev else (ci <= ri)
        edge = 0 if rev else tc - 1

        def chunk(j, carry, d=d, rev=rev, mask=mask, edge=edge):
            cidx = (nc - 1 - j) if rev else j
            r0 = pl.multiple_of(cidx * tc, tc)
            pre = g_ref[pl.ds(r0, tc), :] + gb_ref[...]
            pre_t = gt_ref[cidx] + gbt_ref[...]
            cum = _cumsum_rows(_log_sigmoid(pre), tc, rev)
            cum_t = _cumsum_lanes(_log_sigmoid(pre_t), tc, rev)
            for h in range(nh):
                col = d * nh + h
                s_idx = d * nh + h
                ig_col = pre[:, col:col + 1]
                ig_row = pre_t[col:col + 1, :]
                b_col = cum[:, 8 + col:9 + col]
                b_row = cum_t[8 + col:9 + col, :]
                b_end = b_col[edge:edge + 1, :]
                m_prev = m_s[s_idx:s_idx + 1, 0:1]
                dmat = jnp.where(mask, b_col - b_row + ig_row, -jnp.inf)
                inter = b_col + m_prev
                m_row = jnp.maximum(inter, jnp.max(dmat, axis=1, keepdims=True))
                w_intra = jnp.exp(dmat - m_row)
                w_state = jnp.exp(inter - m_row)
                qh = q_s[pl.ds(r0, tc), h * dh:(h + 1) * dh]
                kh = k_s[pl.ds(r0, tc), h * dh:(h + 1) * dh]
                vh = v_ref[pl.ds(r0, tc), h * dh:(h + 1) * dh].astype(BF16)
                c_prev = c_s[s_idx]
                n_prev = n_s[s_idx:s_idx + 1, :]
                s = lax.dot_general(qh, kh, (((1,), (1,)), ((), ())), preferred_element_type=F32) * w_intra
                num = (jnp.dot(s.astype(BF16), vh, preferred_element_type=F32)
                       + w_state * jnp.dot(qh, c_prev.astype(BF16), preferred_element_type=F32))
                den = (jnp.sum(s, axis=1, keepdims=True)
                       + w_state * jnp.sum(qh.astype(F32) * n_prev, axis=1, keepdims=True))
                hout = num / jnp.maximum(jnp.abs(den), jnp.exp(-m_row))
                if d == 0:
                    h_s[pl.ds(r0, tc), h * dh:(h + 1) * dh] = hout
                else:
                    h_s[pl.ds(r0, tc), h * dh:(h + 1) * dh] += hout
                g_col = b_end - b_col + ig_col
                m_new = jnp.maximum(b_end + m_prev, jnp.max(g_col, axis=0, keepdims=True))
                wg = jnp.exp(g_col - m_new)
                decay = jnp.exp(b_end + m_prev - m_new)
                kw = kh.astype(F32) * wg
                c_s[s_idx] = decay * c_prev + lax.dot_general(
                    kw.astype(BF16), vh, (((0,), (0,)), ((), ())), preferred_element_type=F32)
                n_s[s_idx:s_idx + 1, :] = decay * n_prev + jnp.sum(kw, axis=0, keepdims=True)
                m_s[s_idx:s_idx + 1, :] = jnp.broadcast_to(m_new, (1, LANES))
            return carry

        lax.fori_loop(0, nc, chunk, 0)

    for h in range(nh):
        hs = h_s[:, h * dh:(h + 1) * dh]
        hn = hs * lax.rsqrt(jnp.mean(hs * hs, axis=1, keepdims=True) + EPS)
        y_ref[:, h * dh:(h + 1) * dh] = (jax.nn.sigmoid(o_ref[:, h * dh:(h + 1) * dh])
                                          * (hn * ng_ref[:, h * dh:(h + 1) * dh]))
    cout_ref[0] = c_s[...]
    nout_ref[0] = n_s[...]
    mout_ref[0] = m_s[...]


def _mlstm(qk, v, o, gates, gates_t, L, n_seq, row_block0, sw, gb, gbt, ng, state):
    nc = L // ML_CHUNK
    has_state = state is not None
    full = lambda a: pl.BlockSpec(a.shape, lambda b: (0,) * a.ndim)
    in_specs = [pl.BlockSpec((L, 2 * ML_W), lambda b: (row_block0 + b, 0)),
                pl.BlockSpec((L, ML_W), lambda b: (row_block0 + b, 0)),
                pl.BlockSpec((L, ML_W), lambda b: (row_block0 + b, 0)),
                pl.BlockSpec((L, GATE_PAD), lambda b: (row_block0 + b, 0)),
                pl.BlockSpec((nc, 16, ML_CHUNK), lambda b: (row_block0 + b, 0, 0)),
                full(sw), full(gb), full(gbt), full(ng)]
    args = [qk, v, o, gates, gates_t, sw, gb, gbt, ng]
    if has_state:
        c0, n0, m0 = state
        in_specs += [pl.BlockSpec((1, 2 * ML_HEADS, ML_DH, ML_DH), lambda b: (b, 0, 0, 0)),
                     pl.BlockSpec((1, 2 * ML_HEADS, ML_DH), lambda b: (b, 0, 0)),
                     pl.BlockSpec((1, 2 * ML_HEADS, LANES), lambda b: (b, 0, 0))]
        args += [c0, n0, m0]
    return pl.pallas_call(
        functools.partial(_mlstm_kernel, L=L, has_state=has_state),
        grid=(n_seq,),
        in_specs=in_specs,
        out_specs=[pl.BlockSpec((L, ML_W), lambda b: (b, 0)),
                   pl.BlockSpec((1, 2 * ML_HEADS, ML_DH, ML_DH), lambda b: (b, 0, 0, 0)),
                   pl.BlockSpec((1, 2 * ML_HEADS, ML_DH), lambda b: (b, 0, 0)),
                   pl.BlockSpec((1, 2 * ML_HEADS, LANES), lambda b: (b, 0, 0))],
        out_shape=[jax.ShapeDtypeStruct((n_seq * L, ML_W), F32),
                   jax.ShapeDtypeStruct((n_seq, 2 * ML_HEADS, ML_DH, ML_DH), F32),
                   jax.ShapeDtypeStruct((n_seq, 2 * ML_HEADS, ML_DH), F32),
                   jax.ShapeDtypeStruct((n_seq, 2 * ML_HEADS, LANES), F32)],
        scratch_shapes=[pltpu.VMEM((L, ML_W), BF16), pltpu.VMEM((L, ML_W), BF16),
                        pltpu.VMEM((L, ML_W), F32),
                        pltpu.VMEM((2 * ML_HEADS, ML_DH, ML_DH), F32),
                        pltpu.VMEM((2 * ML_HEADS, ML_DH), F32),
                        pltpu.VMEM((2 * ML_HEADS, LANES), F32)],
        compiler_params=_cparams(("arbitrary",)),
        name=f"mlstm_{L}",
    )(*args)


def _cmul(ar, ai, br, bi):
    return ar * br - ai * bi, ar * bi + ai * br


def _s5_kernel(*refs, segmented):
    if segmented:
        (u_ref, bb_ref, cc_ref, eb_ref, ec_ref, lam_ref, dsk_ref, wglu_ref, s0_ref, y_ref,
         sbuf, yacc, bmat, cmat, pw) = refs
    else:
        (u_ref, bb_ref, cc_ref, eb_ref, ec_ref, lam_ref, dsk_ref, wglu_ref, y_ref, fin_ref,
         sbuf, yacc, bmat, cmat) = refs
    n = S5_STATE
    rows = S5_ROWS
    steps = S5_STEPS
    blk = 256
    n_blk = steps * rows // blk
    n_seg = 4

    yacc[...] = u_ref[0] * dsk_ref[...]
    ub = u_ref[0].astype(BF16)
    b_keep = (lax.broadcasted_iota(jnp.int32, (S5_CH, 2 * n), 0) // S5_GROUP
              == (lax.broadcasted_iota(jnp.int32, (S5_CH, 2 * n), 1) % n) // S5_P)
    c_keep = ((lax.broadcasted_iota(jnp.int32, (2 * n, S5_CH), 0) % n) // S5_P
              == lax.broadcasted_iota(jnp.int32, (2 * n, S5_CH), 1) // S5_GROUP)

    for d in range(2):
        rev = d == 1
        bmat[...] = jnp.where(b_keep, jnp.dot(bb_ref[d].astype(BF16), eb_ref[...], preferred_element_type=F32),
                              0.0).astype(BF16)
        cmat[...] = jnp.where(c_keep, jnp.dot(cc_ref[d].astype(BF16), ec_ref[...], preferred_element_type=F32),
                              0.0).astype(BF16)
        for i in range(n_blk):
            sbuf[i * blk:(i + 1) * blk, :] = jnp.dot(ub[i * blk:(i + 1) * blk], bmat[...],
                                                     preferred_element_type=F32)
        lam = lam_ref[d]
        lr = jnp.broadcast_to(lam[:, :n], (rows, n))
        li = jnp.broadcast_to(lam[:, n:], (rows, n))

        def step(i, carry, rev=rev, lr=lr, li=li):
            sr, si = carry
            t = (steps - 1 - i) if rev else i
            off = pl.multiple_of(t * rows, rows)
            pr, pi = _cmul(lr, li, sr, si)
            nr = pr + sbuf[pl.ds(off, rows), 0:n]
            ni = pi + sbuf[pl.ds(off, rows), n:2 * n]
            sbuf[pl.ds(off, rows), 0:n] = nr
            sbuf[pl.ds(off, rows), n:2 * n] = ni
            return nr, ni

        zero = jnp.zeros((rows, n), F32)
        sr, si = lax.fori_loop(0, steps, step, (zero, zero), unroll=2)

        if not segmented:
            fin_ref[0, d, :, 0:n] = sr
            fin_ref[0, d, :, n:2 * n] = si
        else:
            lam_r, lam_i = lam[:, :n], lam[:, n:]
            row8 = lax.broadcasted_iota(jnp.int32, (rows, 1), 0)
            cr, ci = lam_r, lam_i
            acc_r = jnp.broadcast_to(cr, (rows, n))
            acc_i = jnp.broadcast_to(ci, (rows, n))
            for j in range(1, rows):
                cr, ci = _cmul(cr, ci, lam_r, lam_i)
                acc_r = jnp.where(row8 >= j, jnp.broadcast_to(cr, (rows, n)), acc_r)
                acc_i = jnp.where(row8 >= j, jnp.broadcast_to(ci, (rows, n)), acc_i)
            pw[0:rows, 0:n] = acc_r
            pw[0:rows, n:2 * n] = acc_i
            size = rows
            while size < steps:
                tr = pw[size - 1:size, 0:n]
                ti = pw[size - 1:size, n:2 * n]
                xr, xi = _cmul(pw[0:size, 0:n], pw[0:size, n:2 * n], tr, ti)
                pw[size:2 * size, 0:n] = xr
                pw[size:2 * size, n:2 * n] = xi
                size *= 2
            end_off = 0 if rev else (steps - 1) * rows
            loc_r = sbuf[end_off:end_off + rows, 0:n]
            loc_i = sbuf[end_off:end_off + rows, n:2 * n]
            pl_r = pw[steps - 1:steps, 0:n]
            pl_i = pw[steps - 1:steps, n:2 * n]
            s0r = s0_ref[d, :, 0:n]
            s0i = s0_ref[d, :, n:2 * n]
            seg = row8 // 2
            first = (seg == n_seg - 1) if rev else (seg == 0)
            shift = (rows - 2) if rev else 2
            cin_r, cin_i = s0r, s0i
            for _ in range(n_seg - 1):
                fr, fi = _cmul(jnp.broadcast_to(pl_r, (rows, n)), jnp.broadcast_to(pl_i, (rows, n)), cin_r, cin_i)
                tru_r = loc_r + fr
                tru_i = loc_i + fi
                cin_r = jnp.where(first, s0r, pltpu.roll(tru_r, shift, 0))
                cin_i = jnp.where(first, s0i, pltpu.roll(tru_i, shift, 0))

            def fix(tb, carry, rev=rev, cin_r=cin_r, cin_i=cin_i):
                pb = (steps // rows - 1 - tb) if rev else tb
                poff = pl.multiple_of(pb * rows, rows)
                p_r = pw[pl.ds(poff, rows), 0:n]
                p_i = pw[pl.ds(poff, rows), n:2 * n]
                for j in range(rows):
                    jj = rows - 1 - j if rev else j
                    off = pl.multiple_of((tb * rows + j) * rows, rows)
                    fr, fi = _cmul(jnp.broadcast_to(p_r[jj:jj + 1], (rows, n)),
                                   jnp.broadcast_to(p_i[jj:jj + 1], (rows, n)), cin_r, cin_i)
                    sbuf[pl.ds(off, rows), 0:n] += fr
                    sbuf[pl.ds(off, rows), n:2 * n] += fi
                return carry

            lax.fori_loop(0, steps // rows, fix, 0)

        for i in range(n_blk):
            yacc[i * blk:(i + 1) * blk, :] += jnp.dot(sbuf[i * blk:(i + 1) * blk, :].astype(BF16), cmat[...],
                                                      preferred_element_type=F32)

    g = jax.nn.gelu(yacc[...], approximate=True)
    y_ref[0] = g * jax.nn.sigmoid(_bdot(g, wglu_ref[...]))


@functools.lru_cache(None)
def _s5_spread():
    eb = np.zeros((2 * S5_P, 2 * S5_STATE), np.float32)
    for half in range(2):
        for g in range(S5_G):
            c0 = half * S5_STATE + g * S5_P
            eb[half * S5_P:(half + 1) * S5_P, c0:c0 + S5_P] = np.eye(S5_P)
    ec = np.zeros((LANES, S5_CH), np.float32)
    for g in range(S5_G):
        ec[:S5_GROUP, g * S5_GROUP:(g + 1) * S5_GROUP] = np.eye(S5_GROUP)
    return eb, ec


def _s5(u_tm, bb, cc, lam, dskip, wglu, s0):
    n_grp = u_tm.shape[0]
    n_rows = S5_STEPS * S5_ROWS
    segmented = s0 is not None
    eb, ec = (jnp.asarray(m).astype(BF16) for m in _s5_spread())
    full = lambda a: pl.BlockSpec(a.shape, lambda g: (0,) * a.ndim)
    in_specs = [pl.BlockSpec((1, n_rows, S5_CH), lambda g: (g, 0, 0)),
                full(bb), full(cc), full(eb), full(ec), full(lam), full(dskip), full(wglu)]
    args = [u_tm, bb, cc, eb, ec, lam, dskip, wglu]
    out_specs = [pl.BlockSpec((1, n_rows, S5_CH), lambda g: (g, 0, 0))]
    out_shape = [jax.ShapeDtypeStruct((n_grp, n_rows, S5_CH), F32)]
    scratch = [pltpu.VMEM((n_rows, 2 * S5_STATE), F32), pltpu.VMEM((n_rows, S5_CH), F32),
               pltpu.VMEM((S5_CH, 2 * S5_STATE), BF16), pltpu.VMEM((2 * S5_STATE, S5_CH), BF16)]
    if segmented:
        in_specs.append(full(s0))
        args.append(s0)
        scratch.append(pltpu.VMEM((S5_STEPS, 2 * S5_STATE), F32))
    else:
        out_specs.append(pl.BlockSpec((1, 2, S5_ROWS, 2 * S5_STATE), lambda g: (g, 0, 0, 0)))
        out_shape.append(jax.ShapeDtypeStruct((n_grp, 2, S5_ROWS, 2 * S5_STATE), F32))
    return pl.pallas_call(
        functools.partial(_s5_kernel, segmented=segmented),
        grid=(n_grp,),
        in_specs=in_specs,
        out_specs=out_specs,
        out_shape=out_shape,
        scratch_shapes=scratch,
        compiler_params=_cparams(("arbitrary",)),
        name="s5_seg" if segmented else "s5_ctx",
    )(*args)


def _s5_params(a_re, a_im, log_dt, b_re, b_im, c_re, c_im):
    dt = jnp.exp(log_dt)[:, :, None]
    mag = jnp.exp(a_re * dt)
    lb_re = mag * jnp.cos(a_im * dt)
    lb_im = mag * jnp.sin(a_im * dt)
    den = a_re * a_re + a_im * a_im
    nr, ni = lb_re - 1.0, lb_im
    k_re = (nr * a_re + ni * a_im) / den
    k_im = (ni * a_re - nr * a_im) / den
    bb_re = k_re[..., None] * b_re - k_im[..., None] * b_im
    bb_im = k_re[..., None] * b_im + k_im[..., None] * b_re
    to_gc_p = lambda m: m.transpose(0, 1, 3, 2).reshape(2, S5_CH, S5_P)
    bb = jnp.concatenate([to_gc_p(bb_re), to_gc_p(bb_im)], axis=2)
    to_gp_c = lambda m: m.transpose(0, 1, 3, 2).reshape(2, S5_STATE, S5_GROUP)
    cc = jnp.concatenate([to_gp_c(c_re), -to_gp_c(c_im)], axis=1)
    cc = jnp.pad(cc, ((0, 0), (0, 0), (0, LANES - S5_GROUP)))
    lam = jnp.concatenate([lb_re.reshape(2, 1, S5_STATE), lb_im.reshape(2, 1, S5_STATE)], axis=2)
    return bb, cc, lam


def _outproj_kernel(x_ref, hyc_ref, hyl_ref, mlc_ref, mll_ref, s5c_ref, s5l_ref, w_ref, mod_ref, g_ref,
                    rw_ref, rb_ref, xn_ref, h2e_ref, best_ref, rank_ref, cnt_ref, wb, cnt_s):
    step = pl.program_id(0)

    @pl.when(step == 0)
    def _():
        wb[...] = w_ref[0].astype(BF16)
        cnt_s[...] = jnp.zeros_like(cnt_s)

    is_ctx = step < T_CTX // TM
    pick = lambda c_ref, l_ref: jnp.where(is_ctx, c_ref[...], l_ref[...]).astype(BF16)
    mod = mod_ref[0]
    a, b = HY_CH, HY_CH + ML_W
    mix = (jnp.dot(pick(hyc_ref, hyl_ref), wb[0:a, :], preferred_element_type=F32)
           + jnp.dot(pick(mlc_ref, mll_ref), wb[a:b, :], preferred_element_type=F32)
           + jnp.dot(pick(s5c_ref, s5l_ref), wb[b:, :], preferred_element_type=F32))
    xn = x_ref[...] + mod[2:3] * mix
    xn_ref[...] = xn
    h2 = _rms(xn, g_ref[...]) * (1.0 + mod[4:5]) + mod[3:4]
    h2e_ref[:, 0:D] = h2
    logits = lax.dot_general(rw_ref[...], h2, (((1,), (1,)), ((), ())),
                             precision=HIGHEST, preferred_element_type=F32)
    ex = jnp.exp(logits - jnp.max(logits, axis=0, keepdims=True))
    probs = ex / jnp.sum(ex, axis=0, keepdims=True)
    sel = probs + rb_ref[...]
    best = None
    best_score = None
    for g in range(N_GROUPS):
        r = [sel[g * GROUP_SIZE + i:g * GROUP_SIZE + i + 1, :] for i in range(GROUP_SIZE)]
        score = None
        for i in range(GROUP_SIZE):
            for j in range(i + 1, GROUP_SIZE):
                pair = r[i] + r[j]
                score = pair if score is None else jnp.maximum(score, pair)
        if g == 0:
            best, best_score = jnp.zeros_like(score, dtype=jnp.int32), score
        else:
            upd = score > best_score
            best = jnp.where(upd, g, best)
            best_score = jnp.where(upd, score, best_score)
    eid = lax.broadcasted_iota(jnp.int32, (N_EXPERTS, 1), 0)
    masked = jnp.where(eid // GROUP_SIZE == best, sel, -jnp.inf)
    m1 = jnp.max(masked, axis=0, keepdims=True)
    i1 = jnp.min(jnp.where(masked == m1, eid, N_EXPERTS), axis=0, keepdims=True)
    masked2 = jnp.where(eid == i1, -jnp.inf, masked)
    m2 = jnp.max(masked2, axis=0, keepdims=True)
    i2 = jnp.min(jnp.where(masked2 == m2, eid, N_EXPERTS), axis=0, keepdims=True)
    p1 = jnp.sum(jnp.where(eid == i1, probs, 0.0), axis=0, keepdims=True)
    p2 = jnp.sum(jnp.where(eid == i2, probs, 0.0), axis=0, keepdims=True)
    tot = p1 + p2
    comb = jnp.where(eid == i1, p1 / tot, 0.0) + jnp.where(eid == i2, p2 / tot, 0.0)
    comb = jnp.concatenate([comb, jnp.zeros((LANES - N_EXPERTS, comb.shape[1]), F32)], axis=0)
    h2e_ref[:, D:] = comb.T
    best_ref[...] = best
    gid = lax.broadcasted_iota(jnp.int32, (8, 1), 0)
    onehot = (gid == best).astype(F32)
    cum = _cumsum_lanes(onehot, TM, False)
    run = cnt_s[:, 0:1]
    rank_ref[...] = jnp.sum(onehot * (cum - onehot + run), axis=0, keepdims=True).astype(jnp.int32)
    cnt_s[...] = jnp.broadcast_to(run + cum[:, TM - 1:TM], cnt_s.shape)
    cnt_ref[...] = cnt_s[...]


def _outproj(x, y_hy, y_ml, y_s5, w_out, l, mod, g2, rw_t, rb):
    full = lambda a: pl.BlockSpec(a.shape, lambda i: (0,) * a.ndim)
    n_ctx = T_CTX // TM
    ctx = lambda w: pl.BlockSpec((TM, w), lambda i: (jnp.minimum(i, n_ctx - 1), 0))
    lat = lambda w: pl.BlockSpec((TM, w), lambda i: (jnp.maximum(i - n_ctx, 0), 0))
    tok = lambda w: pl.BlockSpec((TM, w), lambda i: (i, 0))
    row = pl.BlockSpec((1, TM), lambda i: (0, i))
    return pl.pallas_call(
        _outproj_kernel,
        grid=(T_ALL // TM,),
        in_specs=[tok(D), ctx(HY_CH), lat(HY_CH), ctx(ML_W), lat(ML_W), ctx(S5_CH), lat(S5_CH),
                  pl.BlockSpec((1, D, D), lambda i: (l, 0, 0), pipeline_mode=pl.Buffered(1)),
                  pl.BlockSpec((1, 6, D), lambda i: (_mod_row(i, TM), 0, 0)),
                  full(g2), full(rw_t), full(rb)],
        out_specs=[tok(D), tok(D + LANES), row, row, pl.BlockSpec((8, LANES), lambda i: (0, 0))],
        out_shape=[jax.ShapeDtypeStruct((T_ALL, D), F32),
                   jax.ShapeDtypeStruct((T_ALL, D + LANES), F32),
                   jax.ShapeDtypeStruct((1, T_ALL), jnp.int32),
                   jax.ShapeDtypeStruct((1, T_ALL), jnp.int32),
                   jax.ShapeDtypeStruct((8, LANES), F32)],
        scratch_shapes=[pltpu.VMEM((D, D), BF16), pltpu.VMEM((8, LANES), F32)],
        compiler_params=_cparams(("arbitrary",)),
        name="outproj_router",
    )(x, *y_hy, *y_ml, *y_s5, w_out, mod, g2, rw_t, rb)


def _gather_rows(idx_ref, idx_base, src_ref, dst_ref, n_rows):
    def body(r8, carry):
        base = pl.multiple_of(r8 * 8, 8)
        for k in range(8):
            idx = idx_ref[idx_base + base + k]
            dst_ref[pl.ds(base + k, 1), :] = src_ref[pl.ds(idx, 1), :]
        return carry

    lax.fori_loop(0, n_rows // 8, body, 0)


def _dispatch(best, rank, cnt):
    tm = TM_MOE
    g = best.reshape(T_ALL)
    cnt = cnt[:N_GROUPS, 0].astype(jnp.int32)
    n_tile_g = (cnt + tm - 1) // tm
    tile_end = jnp.cumsum(n_tile_g)
    row_off = (tile_end - n_tile_g) * tm
    pos = rank.reshape(T_ALL)
    for k in range(N_GROUPS):
        pos = pos + jnp.where(g == k, row_off[k], 0)
    src = jnp.zeros((MOE_SLOTS,), jnp.int32).at[pos].set(jnp.arange(T_ALL, dtype=jnp.int32))
    tiles = jnp.arange(MOE_SLOTS // tm, dtype=jnp.int32)
    tile_group = jnp.minimum(jnp.sum((tiles[:, None] >= tile_end[None, :]).astype(jnp.int32), axis=1), N_GROUPS - 1)
    return pos, src, tile_group.astype(jnp.int32), tile_end[N_GROUPS - 1:].astype(jnp.int32)


def _moe_kernel(src_ref, tg_ref, nt_ref, h_ref, wg_ref, wu_ref, wd_ref, ys_ref, gbuf, xs, acc):
    i = pl.program_id(0)
    j = pl.program_id(1)
    tm = TM_MOE

    @pl.when(i < nt_ref[0])
    def _():
        @pl.when(j == 0)
        def _():
            _gather_rows(src_ref, i * tm, h_ref, gbuf, tm)
            xs[...] = gbuf[:, 0:D].astype(BF16)

        e = tg_ref[i] * GROUP_SIZE + j
        lane = lax.broadcasted_iota(jnp.int32, (1, LANES), 1)
        wg = wg_ref[0, 0].astype(BF16)
        wu = wu_ref[0, 0].astype(BF16)
        wd = wd_ref[0, 0].astype(BF16)
        sub = 256
        for s in range(tm // sub):
            rs = slice(s * sub, (s + 1) * sub)
            hb = xs[rs, :]
            ce = jnp.sum(jnp.where(lane == e, gbuf[rs, D:], 0.0), axis=1, keepdims=True)
            hid = _silu(jnp.dot(hb, wg, preferred_element_type=F32)) * jnp.dot(hb, wu, preferred_element_type=F32)
            part = jnp.dot((hid * ce).astype(BF16), wd, preferred_element_type=F32)

            @pl.when(j == 0)
            def _():
                acc[rs, :] = part

            @pl.when(j > 0)
            def _():
                acc[rs, :] += part

        @pl.when(j == GROUP_SIZE - 1)
        def _():
            ys_ref[...] = acc[...]


def _moe(h2e, src, tile_group, n_tiles, wg, wu, wd, l):
    tm = TM_MOE

    def w_map(i, j, src, tg, nt):
        live = i < nt[0]
        ii = jnp.minimum(i, nt[0] - 1)
        return (l, tg[ii] * GROUP_SIZE + jnp.where(live, j, GROUP_SIZE - 1), 0, 0)

    return pl.pallas_call(
        _moe_kernel,
        grid_spec=pltpu.PrefetchScalarGridSpec(
            num_scalar_prefetch=3,
            grid=(MOE_SLOTS // tm, GROUP_SIZE),
            in_specs=[pl.BlockSpec(memory_space=pltpu.VMEM),
                      pl.BlockSpec((1, 1, D, D_EXPERT), w_map),
                      pl.BlockSpec((1, 1, D, D_EXPERT), w_map),
                      pl.BlockSpec((1, 1, D_EXPERT, D), w_map)],
            out_specs=pl.BlockSpec((tm, D), lambda i, j, src, tg, nt: (jnp.minimum(i, nt[0] - 1), 0)),
            scratch_shapes=[pltpu.VMEM((tm, D + LANES), F32), pltpu.VMEM((tm, D), BF16),
                            pltpu.VMEM((tm, D), F32)]),
        out_shape=jax.ShapeDtypeStruct((MOE_SLOTS, D), F32),
        compiler_params=_cparams(("arbitrary", "arbitrary")),
        name="moe_experts",
    )(src, tile_group, n_tiles, h2e, wg, wu, wd)


def _combine_kernel(pos_ref, ys_ref, xn_ref, mod_ref, fg_ref, *rest, final):
    step = pl.program_id(0)
    if final:
        yc_ref, yl_ref, gbuf = rest
    else:
        out_ref, gbuf = rest
    _gather_rows(pos_ref, step * TM, ys_ref, gbuf, TM)
    out = xn_ref[...] + mod_ref[0][5:6] * gbuf[...]
    if final:
        y = _rms(out, fg_ref[...])

        @pl.when(step < T_CTX // TM)
        def _():
            yc_ref[...] = y

        @pl.when(step >= T_CTX // TM)
        def _():
            yl_ref[...] = y
    else:
        out_ref[...] = out


def _combine(pos, ys, xn, mod, fg, final):
    spec = pl.BlockSpec((TM, D), lambda i, pos: (i, 0))
    n_ctx = T_CTX // TM
    if final:
        out_specs = [pl.BlockSpec((TM, D), lambda i, pos: (jnp.minimum(i, n_ctx - 1), 0)),
                     pl.BlockSpec((TM, D), lambda i, pos: (jnp.maximum(i - n_ctx, 0), 0))]
        out_shape = [jax.ShapeDtypeStruct((T_CTX, D), F32), jax.ShapeDtypeStruct((T_LAT, D), F32)]
    else:
        out_specs = [spec]
        out_shape = [jax.ShapeDtypeStruct((T_ALL, D), F32)]
    return pl.pallas_call(
        functools.partial(_combine_kernel, final=final),
        grid_spec=pltpu.PrefetchScalarGridSpec(
            num_scalar_prefetch=1,
            grid=(T_ALL // TM,),
            in_specs=[pl.BlockSpec(memory_space=pltpu.VMEM),
                      spec,
                      pl.BlockSpec((1, 6, D), lambda i, pos: (_mod_row(i, TM), 0, 0)),
                      pl.BlockSpec((1, D), lambda i, pos: (0, 0))],
            out_specs=out_specs,
            scratch_shapes=[pltpu.VMEM((TM, D), F32)]),
        out_shape=out_shape,
        compiler_params=_cparams(("arbitrary",)),
        name="moe_combine",
    )(pos, ys, xn, mod, fg)


@functools.lru_cache(None)
def _pos_embed():
    rows = L_LAT // GRID_W
    r = np.repeat(np.arange(rows, dtype=np.float64), GRID_W)
    col = np.tile(np.arange(GRID_W, dtype=np.float64), rows)
    quarter = D // 4
    freq = np.exp(-math.log(POS_BASE) * np.arange(quarter, dtype=np.float64) / quarter)
    ar = r[:, None] * freq[None]
    ac = col[:, None] * freq[None]
    emb = np.concatenate([np.sin(ar), np.cos(ar), np.sin(ac), np.cos(ac)], axis=-1)
    return emb.astype(np.float32)


def _to_time_major_ctx(a):
    c = a.shape[-1]
    a = a.reshape(2, S5_ROWS, L_CTX, c).transpose(0, 2, 1, 3)
    return a.reshape(2, L_CTX * S5_ROWS, c)


def _from_time_major_ctx(a):
    c = a.shape[-1]
    a = a.reshape(2, L_CTX, S5_ROWS, c).transpose(0, 2, 1, 3)
    return a.reshape(T_CTX, c)


def _to_time_major_lat(a):
    c = a.shape[-1]
    a = a.reshape(N_LAT_SEQ, 4, S5_STEPS, c).transpose(2, 1, 0, 3)
    return a.reshape(1, S5_STEPS * S5_ROWS, c)


def _from_time_major_lat(a):
    c = a.shape[-1]
    a = a.reshape(S5_STEPS, 4, N_LAT_SEQ, c).transpose(2, 1, 0, 3)
    return a.reshape(T_LAT, c)


def kernel(x_prompt, x_sample, c, state_mlstm_C, state_mlstm_n, state_mlstm_m, state_s5_re, state_s5_im, c_ctx, w_ada, b_ada, norm1_g, norm2_g, final_g, w_in, w_out, hy_short, hy_fw1, hy_fb1, hy_fw2, hy_fb2, hy_fw3, hy_log_decay, hy_bias, ml_short, ml_gate_bias, ml_norm_g, s5_a_re, s5_a_im, s5_log_dt, s5_b_re, s5_b_im, s5_c_re, s5_c_im, s5_d, s5_w_glu, router_w, router_b, moe_w_gate, moe_w_up, moe_w_down):
    x = jnp.concatenate([x_prompt.reshape(T_CTX, D), (x_sample + _pos_embed()[None]).reshape(T_LAT, D)], axis=0)
    cc = jnp.concatenate([c_ctx[None], c, jnp.zeros((8 - 1 - N_LAT_SEQ, D), F32)], axis=0)
    mod_all = _ada(cc, w_ada, b_ada).reshape(DEPTH, 8, 6, D)
    rw_t = router_w.T
    rb = router_b.reshape(N_EXPERTS, 1)
    fg = final_g.reshape(1, D)
    lat_blk = T_CTX // L_LAT

    new_c, new_n, new_m, new_re, new_im = [], [], [], [], []
    y_prompt = y_sample = None
    for l in range(DEPTH):
        mod = mod_all[l]
        u_hy, qk, v, o, u_s5, gates = _inproj(x, norm1_g[l], mod, w_in, l)

        w1p = jnp.zeros((LANES, HY_FILTER_W), F32).at[:HY_EMB].set(hy_fw1[l])
        b1 = hy_fb1[l].reshape(1, HY_FILTER_W)
        b2 = hy_fb2[l].reshape(1, HY_FILTER_W)
        ld = hy_log_decay[l].reshape(1, 4 * HY_CH)
        y_hy = []
        for L, n_seq, blk0 in ((L_CTX, N_CTX_SEQ, 0), (L_LAT, N_LAT_SEQ, lat_blk)):
            p, q, r = _hy_filter(L, w1p, b1, hy_fw2[l], b2, hy_fw3[l], ld)
            y_hy.append(_hyena(u_hy, L, n_seq, blk0, hy_short[l], hy_bias[l], p, q, r))

        gates_t = gates[:, :16].reshape(T_ALL // ML_CHUNK, ML_CHUNK, 16).transpose(0, 2, 1)
        gb = jnp.zeros((1, GATE_PAD), F32).at[0, :16].set(ml_gate_bias[l].reshape(16))
        gbt = ml_gate_bias[l].reshape(16, 1)
        ng = ml_norm_g[l].reshape(1, ML_W)
        m0 = jnp.broadcast_to(state_mlstm_m[:, l].reshape(N_LAT_SEQ, 2 * ML_HEADS, 1), (N_LAT_SEQ, 2 * ML_HEADS, LANES))
        yc, cc_, nc_, mc_ = _mlstm(qk, v, o, gates, gates_t, L_CTX, N_CTX_SEQ, 0, ml_short[l], gb, gbt, ng, None)
        yl, _, _, _ = _mlstm(qk, v, o, gates, gates_t, L_LAT, N_LAT_SEQ, lat_blk, ml_short[l], gb, gbt, ng,
                             (state_mlstm_C[:, l].reshape(N_LAT_SEQ, 2 * ML_HEADS, ML_DH, ML_DH),
                              state_mlstm_n[:, l].reshape(N_LAT_SEQ, 2 * ML_HEADS, ML_DH), m0))
        y_ml = (yc, yl)
        new_c.append(cc_.reshape(N_CTX_SEQ, 2, ML_HEADS, ML_DH, ML_DH))
        new_n.append(nc_.reshape(N_CTX_SEQ, 2, ML_HEADS, ML_DH))
        new_m.append(mc_[:, :, 0].reshape(N_CTX_SEQ, 2, ML_HEADS))

        bb, cc_s5, lam = _s5_params(s5_a_re[l], s5_a_im[l], s5_log_dt[l], s5_b_re[l], s5_b_im[l],
                                    s5_c_re[l], s5_c_im[l])
        dsk = s5_d[l].reshape(1, S5_CH)
        wglu = s5_w_glu[l].astype(BF16)
        ys_c, fin = _s5(_to_time_major_ctx(u_s5[:T_CTX]), bb, cc_s5, lam, dsk, wglu, None)
        s0 = jnp.concatenate([state_s5_re[:, l].reshape(N_LAT_SEQ, 2, S5_STATE),
                              state_s5_im[:, l].reshape(N_LAT_SEQ, 2, S5_STATE)], axis=-1)
        s0 = jnp.tile(s0.transpose(1, 0, 2), (1, 4, 1))
        (ys_l,) = _s5(_to_time_major_lat(u_s5[T_CTX:]), bb, cc_s5, lam, dsk, wglu, s0)
        y_s5 = (_from_time_major_ctx(ys_c), _from_time_major_lat(ys_l))
        fin = fin.transpose(0, 2, 1, 3).reshape(N_CTX_SEQ, 2, 2 * S5_STATE)
        new_re.append(fin[..., :S5_STATE].reshape(N_CTX_SEQ, 2, S5_G, S5_P))
        new_im.append(fin[..., S5_STATE:].reshape(N_CTX_SEQ, 2, S5_G, S5_P))

        xn, h2e, best, rank, cnt = _outproj(x, y_hy, y_ml, y_s5, w_out, l, mod,
                                            norm2_g[l].reshape(1, D), rw_t, rb)
        pos, src, tile_group, n_tiles = _dispatch(best, rank, cnt)
        ys = _moe(h2e, src, tile_group, n_tiles, moe_w_gate, moe_w_up, moe_w_down, l)
        res = _combine(pos, ys, xn, mod, fg, l == DEPTH - 1)
        if l == DEPTH - 1:
            y_prompt = res[0].reshape(N_CTX_SEQ, L_CTX, D)
            y_sample = res[1].reshape(N_LAT_SEQ, L_LAT, D)
        else:
            x = res[0]

    return (y_prompt, y_sample, jnp.stack(new_c, axis=1), jnp.stack(new_n, axis=1), jnp.stack(new_m, axis=1),
            jnp.stack(new_re, axis=1), jnp.stack(new_im, axis=1))
```

```python
import functools
import math

import numpy as np
import jax
import jax.numpy as jnp
from jax import lax
from jax.experimental import pallas as pl
from jax.experimental.pallas import tpu as pltpu

F32 = jnp.float32
BF16 = jnp.bfloat16
HIGHEST = lax.Precision.HIGHEST

D = 1024
N_CTX_SEQ, L_CTX = 16, 256
N_LAT_SEQ, L_LAT = 2, 1024
T_CTX = N_CTX_SEQ * L_CTX
T_LAT = N_LAT_SEQ * L_LAT
T_ALL = T_CTX + T_LAT
DEPTH = 2
EPS = 1e-6
GRID_W = 64
POS_BASE = 10000.0
HY_CH = 256
HY_EMB = 33
HY_FILTER_W = 64
ML_HEADS = 4
ML_DH = 128
ML_W = ML_HEADS * ML_DH
S5_CH = 256
S5_G = 16
S5_GROUP = 16
S5_P = 64
S5_STATE = S5_G * S5_P
N_EXPERTS = 16
N_GROUPS = 4
GROUP_SIZE = N_EXPERTS // N_GROUPS
D_EXPERT = 512
OFF_HY = 0
OFF_QK = 3 * HY_CH
OFF_V = OFF_QK + 2 * ML_W
OFF_O = OFF_V + ML_W
OFF_G = OFF_O + ML_W
OFF_S5 = OFF_G + 16
IN_W = OFF_S5 + S5_CH
LANES = 128
GATE_PAD = LANES

TM = 512
TM_MOE = 512
MOE_SLOTS = T_ALL + N_GROUPS * TM_MOE
ML_CHUNK = 256
S5_ROWS = 8
S5_STEPS = 256
VMEM_LIMIT = 56 * 1024 * 1024


def _cparams(sem, vmem=VMEM_LIMIT):
    if sem is None:
        return pltpu.CompilerParams(vmem_limit_bytes=vmem)
    return pltpu.CompilerParams(dimension_semantics=sem, vmem_limit_bytes=vmem)


def _bdot(a, b):
    return jnp.dot(a.astype(BF16), b.astype(BF16), preferred_element_type=F32)


def _silu(x):
    return x * jax.nn.sigmoid(x)


def _rms(x, g):
    return x * lax.rsqrt(jnp.mean(x * x, axis=-1, keepdims=True) + EPS) * g


def _log_sigmoid(x):
    return jnp.minimum(x, 0.0) - jnp.log1p(jnp.exp(-jnp.abs(x)))


def _conv3(u, w, n_rows, seq_len):
    row = lax.broadcasted_iota(jnp.int32, (n_rows, 1), 0) % seq_len
    prev = jnp.where(row == 0, 0.0, pltpu.roll(u, 1, 0))
    nxt = jnp.where(row == seq_len - 1, 0.0, pltpu.roll(u, n_rows - 1, 0))
    return prev * w[0:1] + u * w[1:2] + nxt * w[2:3]


def _ada_kernel(c_ref, w_ref, b_ref, o_ref):
    o_ref[0] = _bdot(_silu(c_ref[...]), w_ref[0]) + b_ref[0]


def _ada(cc, w_ada, b_ada):
    tn = 1536
    return pl.pallas_call(
        _ada_kernel,
        grid=(DEPTH, 6 * D // tn),
        in_specs=[pl.BlockSpec((8, D), lambda l, j: (0, 0)),
                  pl.BlockSpec((1, D, tn), lambda l, j: (l, 0, j)),
                  pl.BlockSpec((1, 1, tn), lambda l, j: (l, 0, j))],
        out_specs=pl.BlockSpec((1, 8, tn), lambda l, j: (l, 0, j)),
        out_shape=jax.ShapeDtypeStruct((DEPTH, 8, 6 * D), F32),
        compiler_params=_cparams(("arbitrary", "arbitrary")),
        name="ada_mod",
    )(cc, w_ada, b_ada.reshape(DEPTH, 1, 6 * D))


def _mod_row(i, tm):
    n_ctx = T_CTX // tm
    return jnp.where(i < n_ctx, 0, 1 + (i - n_ctx) // (L_LAT // tm))


_SEG = ((OFF_HY, OFF_QK - OFF_HY), (OFF_QK, OFF_V - OFF_QK), (OFF_V, OFF_O - OFF_V), (OFF_O, OFF_G - OFF_O))
TAIL_W = IN_W - OFF_G


N_CTX_TILES = T_CTX // TM
_NT = (((1,), (1,)), ((), ()))


def _x_specs(x):
    if len(x) == 1:
        return [pl.BlockSpec((TM, D), lambda i, *_: (i, 0))]
    per_seq = L_LAT // TM
    return [pl.BlockSpec((TM, D), lambda i, *_: (jnp.minimum(i, N_CTX_TILES - 1), 0)),
            pl.BlockSpec((TM, D), lambda i, *_: (jnp.maximum(i - N_CTX_TILES, 0), 0)),
            pl.BlockSpec((TM, D), lambda i, *_: (jnp.maximum(i - N_CTX_TILES, 0) % per_seq, 0))]


def _x_tile(step, x_refs):
    if len(x_refs) == 1:
        return x_refs[0][...]
    xc_ref, xl_ref, pos_ref = x_refs
    return jnp.where(step < N_CTX_TILES, xc_ref[...], xl_ref[...] + pos_ref[...])


def _inproj_kernel(*refs, n_x):
    x_refs = refs[:n_x]
    g_ref, mod_ref, w_ref, hy_ref, qk_ref, v_ref, o_ref, s5_ref, gt_ref, wb = refs[n_x:]
    step = pl.program_id(0)

    @pl.when(step == 0)
    def _():
        wb[...] = w_ref[0].astype(BF16)

    mod = mod_ref[0]
    h = _rms(_x_tile(step, x_refs), g_ref[...]) * (1.0 + mod[1:2]) + mod[0:1]
    hb = h.astype(BF16)
    for (a, w), ref in zip(_SEG, (hy_ref, qk_ref, v_ref, o_ref)):
        ref[...] = lax.dot_general(hb, wb[a:a + w, :], _NT, preferred_element_type=F32)
    tail = lax.dot_general(hb, wb[OFF_G:IN_W, :], _NT, preferred_element_type=F32)
    gt_ref[...] = tail[:, 0:GATE_PAD]
    s5_ref[...] = tail[:, OFF_S5 - OFF_G:TAIL_W]


def _inproj(x, g, mod, w_in_t, l):
    widths = [w for _, w in _SEG] + [S5_CH, GATE_PAD]
    return pl.pallas_call(
        functools.partial(_inproj_kernel, n_x=len(x)),
        grid=(T_ALL // TM,),
        in_specs=_x_specs(x) + [
            pl.BlockSpec((1, D), lambda i: (0, 0)),
            pl.BlockSpec((1, 6, D), lambda i: (_mod_row(i, TM), 0, 0)),
            pl.BlockSpec((1, IN_W, D), lambda i: (l, 0, 0), pipeline_mode=pl.Buffered(1))],
        out_specs=[pl.BlockSpec((TM, w), lambda i: (i, 0)) for w in widths],
        out_shape=[jax.ShapeDtypeStruct((T_ALL, w), F32) for w in widths],
        scratch_shapes=[pltpu.VMEM((IN_W, D), BF16)],
        compiler_params=_cparams(("arbitrary",)),
        name="norm_inproj",
    )(*x, g.reshape(1, D), mod, w_in_t)


@functools.lru_cache(None)
def _dft_mats(L):
    n = 2 * L
    k = np.arange(L)[:, None]
    t = np.arange(L)[None, :]
    ang = 2.0 * np.pi * ((k * t) % n) / n
    top = np.cos(ang)
    bot = -np.sin(ang)
    bot[0] = np.cos(np.pi * np.arange(L))
    fwd = np.concatenate([top, bot], 0)
    s = np.full((n, 1), 2.0 / n)
    s[0] = s[L] = 1.0 / n
    inv = (fwd * s).T
    return fwd.astype(np.float32), inv.astype(np.float32)


@functools.lru_cache(None)
def _hy_positions(L):
    t = np.linspace(0.0, 1.0, L)
    bands = (HY_EMB - 1) // 2
    f = np.linspace(1e-4, bands - 1, bands)
    w = 2.0 * np.pi * np.arange(L) / L
    ang = w[:, None] * f[None, :]
    z = np.concatenate([t[:, None], np.cos(ang), -np.sin(ang)], -1)
    zp = np.zeros((L, LANES))
    zp[:, :HY_EMB] = z
    return zp.astype(np.float32), t[:, None].astype(np.float32)


def _hy_filter_kernel(z_ref, t_ref, w1_ref, b1_ref, w2_ref, b2_ref, w3_ref, ld_ref, f_ref,
                      p_ref, q_ref, r_ref, *, L):
    h = jnp.sin(jnp.dot(z_ref[...], w1_ref[...], precision=HIGHEST, preferred_element_type=F32) + b1_ref[...])
    h = jnp.sin(jnp.dot(h, w2_ref[...], precision=HIGHEST, preferred_element_type=F32) + b2_ref[...])
    filt = jnp.dot(h, w3_ref[...], precision=HIGHEST, preferred_element_type=F32)
    filt = filt * jnp.exp(-t_ref[...] * jnp.exp(ld_ref[...]))
    c = HY_CH
    h_fwd = jnp.concatenate([filt[:, 0:c], filt[:, 2 * c:3 * c]], axis=1)
    h_bwd = jnp.concatenate([filt[:, c:2 * c], filt[:, 3 * c:4 * c]], axis=1)
    row = lax.broadcasted_iota(jnp.int32, (L, 1), 0)
    h_bwd = jnp.where(row == 0, 0.0, h_bwd)
    a = jnp.dot(f_ref[...], h_fwd.astype(BF16), preferred_element_type=F32)
    b = jnp.dot(f_ref[...], h_bwd.astype(BF16), preferred_element_type=F32)
    re = a[:L] + b[:L]
    im = a[L:] - b[L:]
    nyq = a[L:L + 1] + b[L:L + 1]
    p_ref[...] = re
    q_ref[...] = jnp.where(row == 0, 0.0, im)
    r_ref[...] = jnp.where(row == 0, nyq, re)


def _hy_filter(L, w1p, b1, w2, b2, w3, ld):
    z, t = _hy_positions(L)
    fwd = jnp.asarray(_dft_mats(L)[0]).astype(BF16)
    out = jax.ShapeDtypeStruct((L, 2 * HY_CH), F32)
    return pl.pallas_call(
        functools.partial(_hy_filter_kernel, L=L),
        out_shape=[out, out, out],
        compiler_params=_cparams(None),
        name=f"hyena_filter_{L}",
    )(z, t, w1p, b1, w2, b2, w3, ld, fwd)


def _hyena_kernel(u_ref, sw_ref, bias_ref, p_ref, q_ref, r_ref, f_ref, g_ref, o_ref, *, L):
    c = HY_CH
    u = _conv3(u_ref[...], sw_ref[...], L, L)
    z = u[:, 0:c]
    for o in range(2):
        gate = u[:, (o + 1) * c:(o + 2) * c]
        zf = jnp.dot(f_ref[...], z.astype(BF16), preferred_element_type=F32)
        a, b = zf[:L], zf[L:]
        p = p_ref[:, o * c:(o + 1) * c]
        q = q_ref[:, o * c:(o + 1) * c]
        r = r_ref[:, o * c:(o + 1) * c]
        y_re = (a * p - b * q).astype(BF16)
        y_im = (a * q + b * r).astype(BF16)
        y = (jnp.dot(g_ref[:, :L], y_re, preferred_element_type=F32)
             + jnp.dot(g_ref[:, L:], y_im, preferred_element_type=F32))
        z = gate * (y + bias_ref[o:o + 1, :] * z)
    o_ref[...] = z


def _hyena(u_hy, L, n_seq, row_block0, sw, bias, p, q, r):
    fwd, inv = (jnp.asarray(m).astype(BF16) for m in _dft_mats(L))
    full = lambda a: pl.BlockSpec(a.shape, lambda b: (0,) * a.ndim)
    return pl.pallas_call(
        functools.partial(_hyena_kernel, L=L),
        grid=(n_seq,),
        in_specs=[pl.BlockSpec((L, 3 * HY_CH), lambda b: (row_block0 + b, 0)),
                  full(sw), full(bias), full(p), full(q), full(r), full(fwd), full(inv)],
        out_specs=pl.BlockSpec((L, HY_CH), lambda b: (b, 0)),
        out_shape=jax.ShapeDtypeStruct((n_seq * L, HY_CH), F32),
        compiler_params=_cparams(("arbitrary",)),
        name=f"hyena_{L}",
    )(u_hy, sw, bias, p, q, r, fwd, inv)


def _cumsum_rows(x, n, reverse):
    row = lax.broadcasted_iota(jnp.int32, (n, 1), 0)
    s = 1
    while s < n:
        if reverse:
            x = x + jnp.where(row < n - s, pltpu.roll(x, n - s, 0), 0.0)
        else:
            x = x + jnp.where(row >= s, pltpu.roll(x, s, 0), 0.0)
        s *= 2
    return x


def _cumsum_lanes(x, n, reverse):
    col = lax.broadcasted_iota(jnp.int32, (1, n), 1)
    s = 1
    while s < n:
        if reverse:
            x = x + jnp.where(col < n - s, pltpu.roll(x, n - s, 1), 0.0)
        else:
            x = x + jnp.where(col >= s, pltpu.roll(x, s, 1), 0.0)
        s *= 2
    return x


def _mlstm_kernel(*refs, L, has_state):
    if has_state:
        (qk_ref, v_ref, o_ref, g_ref, gt_ref, sw_ref, gb_ref, gbt_ref, ng_ref, c0_ref, n0_ref, m0_ref,
         y_ref, cout_ref, nout_ref, mout_ref, q_s, k_s, h_s, c_s, n_s, m_s) = refs
    else:
        (qk_ref, v_ref, o_ref, g_ref, gt_ref, sw_ref, gb_ref, gbt_ref, ng_ref,
         y_ref, cout_ref, nout_ref, mout_ref, q_s, k_s, h_s, c_s, n_s, m_s) = refs
    tc = ML_CHUNK
    nc = L // tc
    nh = ML_HEADS
    dh = ML_DH

    qk = _silu(_conv3(qk_ref[...], sw_ref[...], L, L))
    q_s[...] = qk[:, :ML_W].astype(BF16)
    k_s[...] = (qk[:, ML_W:] * (dh ** -0.5)).astype(BF16)

    if has_state:
        c_s[...] = c0_ref[0]
        n_s[...] = n0_ref[0]
        m_s[...] = m0_ref[0]
    else:
        c_s[...] = jnp.zeros_like(c_s)
        n_s[...] = jnp.zeros_like(n_s)
        m_s[...] = jnp.zeros_like(m_s)

    ri = lax.broadcasted_iota(jnp.int32, (tc, tc), 0)
    ci = lax.broadcasted_iota(jnp.int32, (tc, tc), 1)

    for d in range(2):
        rev = d == 1
        mask = (ci >= ri) if rev else (ci <= ri)
        edge = 0 if rev else tc - 1

        def chunk(j, carry, d=d, rev=rev, mask=mask, edge=edge):
            cidx = (nc - 1 - j) if rev else j
            r0 = pl.multiple_of(cidx * tc, tc)
            pre = g_ref[pl.ds(r0, tc), :] + gb_ref[...]
            pre_t = gt_ref[cidx] + gbt_ref[...]
            cum = _cumsum_rows(_log_sigmoid(pre), tc, rev)
            cum_t = _cumsum_lanes(_log_sigmoid(pre_t), tc, rev)
            for h in range(nh):
                col = d * nh + h
                s_idx = d * nh + h
                ig_col = pre[:, col:col + 1]
                ig_row = pre_t[col:col + 1, :]
                b_col = cum[:, 8 + col:9 + col]
                b_row = cum_t[8 + col:9 + col, :]
                b_end = b_col[edge:edge + 1, :]
                m_prev = m_s[s_idx:s_idx + 1, 0:1]
                dmat = jnp.where(mask, b_col - b_row + ig_row, -jnp.inf)
                inter = b_col + m_prev
                m_row = jnp.maximum(inter, jnp.max(dmat, axis=1, keepdims=True))
                w_intra = jnp.exp(dmat - m_row)
                w_state = jnp.exp(inter - m_row)
                qh = q_s[pl.ds(r0, tc), h * dh:(h + 1) * dh]
                kh = k_s[pl.ds(r0, tc), h * dh:(h + 1) * dh]
                vh = v_ref[pl.ds(r0, tc), h * dh:(h + 1) * dh].astype(BF16)
                c_prev = c_s[s_idx]
                n_prev = n_s[s_idx:s_idx + 1, :]
                s = lax.dot_general(qh, kh, (((1,), (1,)), ((), ())), preferred_element_type=F32) * w_intra
                num = (jnp.dot(s.astype(BF16), vh, preferred_element_type=F32)
                       + w_state * jnp.dot(qh, c_prev.astype(BF16), preferred_element_type=F32))
                den = (jnp.sum(s, axis=1, keepdims=True)
                       + w_state * jnp.sum(qh.astype(F32) * n_prev, axis=1, keepdims=True))
                hout = num / jnp.maximum(jnp.abs(den), jnp.exp(-m_row))
                if d == 0:
                    h_s[pl.ds(r0, tc), h * dh:(h + 1) * dh] = hout
                else:
                    h_s[pl.ds(r0, tc), h * dh:(h + 1) * dh] += hout
                g_col = b_end - b_col + ig_col
                m_new = jnp.maximum(b_end + m_prev, jnp.max(g_col, axis=0, keepdims=True))
                wg = jnp.exp(g_col - m_new)
                decay = jnp.exp(b_end + m_prev - m_new)
                kw = kh.astype(F32) * wg
                c_s[s_idx] = decay * c_prev + lax.dot_general(
                    kw.astype(BF16), vh, (((0,), (0,)), ((), ())), preferred_element_type=F32)
                n_s[s_idx:s_idx + 1, :] = decay * n_prev + jnp.sum(kw, axis=0, keepdims=True)
                m_s[s_idx:s_idx + 1, :] = jnp.broadcast_to(m_new, (1, LANES))
            return carry

        lax.fori_loop(0, nc, chunk, 0)

    for h in range(nh):
        hs = h_s[:, h * dh:(h + 1) * dh]
        hn = hs * lax.rsqrt(jnp.mean(hs * hs, axis=1, keepdims=True) + EPS)
        y_ref[:, h * dh:(h + 1) * dh] = (jax.nn.sigmoid(o_ref[:, h * dh:(h + 1) * dh])
                                          * (hn * ng_ref[:, h * dh:(h + 1) * dh]))
    cout_ref[0] = c_s[...]
    nout_ref[0] = n_s[...]
    mout_ref[0] = m_s[...]


def _mlstm(qk, v, o, gates, gates_t, L, n_seq, row_block0, sw, gb, gbt, ng, state):
    nc = L // ML_CHUNK
    has_state = state is not None
    full = lambda a: pl.BlockSpec(a.shape, lambda b: (0,) * a.ndim)
    in_specs = [pl.BlockSpec((L, 2 * ML_W), lambda b: (row_block0 + b, 0)),
                pl.BlockSpec((L, ML_W), lambda b: (row_block0 + b, 0)),
                pl.BlockSpec((L, ML_W), lambda b: (row_block0 + b, 0)),
                pl.BlockSpec((L, GATE_PAD), lambda b: (row_block0 + b, 0)),
                pl.BlockSpec((nc, 16, ML_CHUNK), lambda b: (row_block0 + b, 0, 0)),
                full(sw), full(gb), full(gbt), full(ng)]
    args = [qk, v, o, gates, gates_t, sw, gb, gbt, ng]
    if has_state:
        c0, n0, m0 = state
        in_specs += [pl.BlockSpec((1, 2 * ML_HEADS, ML_DH, ML_DH), lambda b: (b, 0, 0, 0)),
                     pl.BlockSpec((1, 2 * ML_HEADS, ML_DH), lambda b: (b, 0, 0)),
                     pl.BlockSpec((1, 2 * ML_HEADS, LANES), lambda b: (b, 0, 0))]
        args += [c0, n0, m0]
    return pl.pallas_call(
        functools.partial(_mlstm_kernel, L=L, has_state=has_state),
        grid=(n_seq,),
        in_specs=in_specs,
        out_specs=[pl.BlockSpec((L, ML_W), lambda b: (b, 0)),
                   pl.BlockSpec((1, 2 * ML_HEADS, ML_DH, ML_DH), lambda b: (b, 0, 0, 0)),
                   pl.BlockSpec((1, 2 * ML_HEADS, ML_DH), lambda b: (b, 0, 0)),
                   pl.BlockSpec((1, 2 * ML_HEADS, LANES), lambda b: (b, 0, 0))],
        out_shape=[jax.ShapeDtypeStruct((n_seq * L, ML_W), F32),
                   jax.ShapeDtypeStruct((n_seq, 2 * ML_HEADS, ML_DH, ML_DH), F32),
                   jax.ShapeDtypeStruct((n_seq, 2 * ML_HEADS, ML_DH), F32),
                   jax.ShapeDtypeStruct((n_seq, 2 * ML_HEADS, LANES), F32)],
        scratch_shapes=[pltpu.VMEM((L, ML_W), BF16), pltpu.VMEM((L, ML_W), BF16),
                        pltpu.VMEM((L, ML_W), F32),
                        pltpu.VMEM((2 * ML_HEADS, ML_DH, ML_DH), F32),
                        pltpu.VMEM((2 * ML_HEADS, ML_DH), F32),
                        pltpu.VMEM((2 * ML_HEADS, LANES), F32)],
        compiler_params=_cparams(("arbitrary",)),
        name=f"mlstm_{L}",
    )(*args)


def _cmul(ar, ai, br, bi):
    return ar * br - ai * bi, ar * bi + ai * br


def _s5_kernel(*refs, segmented):
    if segmented:
        (u_ref, bb_ref, cc_ref, eb_ref, ec_ref, lam_ref, dsk_ref, wglu_ref, s0_ref, y_ref,
         sbuf, yacc, bmat, cmat, pw) = refs
    else:
        (u_ref, bb_ref, cc_ref, eb_ref, ec_ref, lam_ref, dsk_ref, wglu_ref, y_ref, fin_ref,
         sbuf, yacc, bmat, cmat) = refs
    n = S5_STATE
    rows = S5_ROWS
    steps = S5_STEPS
    blk = 256
    n_blk = steps * rows // blk
    n_seg = 4

    yacc[...] = u_ref[0] * dsk_ref[...]
    ub = u_ref[0].astype(BF16)
    b_keep = (lax.broadcasted_iota(jnp.int32, (S5_CH, 2 * n), 0) // S5_GROUP
              == (lax.broadcasted_iota(jnp.int32, (S5_CH, 2 * n), 1) % n) // S5_P)
    c_keep = ((lax.broadcasted_iota(jnp.int32, (2 * n, S5_CH), 0) % n) // S5_P
              == lax.broadcasted_iota(jnp.int32, (2 * n, S5_CH), 1) // S5_GROUP)

    for d in range(2):
        rev = d == 1
        bmat[...] = jnp.where(b_keep, jnp.dot(bb_ref[d].astype(BF16), eb_ref[...], preferred_element_type=F32),
                              0.0).astype(BF16)
        cmat[...] = jnp.where(c_keep, jnp.dot(cc_ref[d].astype(BF16), ec_ref[...], preferred_element_type=F32),
                              0.0).astype(BF16)
        for i in range(n_blk):
            sbuf[i * blk:(i + 1) * blk, :] = jnp.dot(ub[i * blk:(i + 1) * blk], bmat[...],
                                                     preferred_element_type=F32)
        lam = lam_ref[d]
        lr = jnp.broadcast_to(lam[:, :n], (rows, n))
        li = jnp.broadcast_to(lam[:, n:], (rows, n))

        def step(i, carry, rev=rev, lr=lr, li=li):
            sr, si = carry
            t = (steps - 1 - i) if rev else i
            off = pl.multiple_of(t * rows, rows)
            pr, pi = _cmul(lr, li, sr, si)
            nr = pr + sbuf[pl.ds(off, rows), 0:n]
            ni = pi + sbuf[pl.ds(off, rows), n:2 * n]
            sbuf[pl.ds(off, rows), 0:n] = nr
            sbuf[pl.ds(off, rows), n:2 * n] = ni
            return nr, ni

        zero = jnp.zeros((rows, n), F32)
        sr, si = lax.fori_loop(0, steps, step, (zero, zero), unroll=2)

        if not segmented:
            fin_ref[0, d, :, 0:n] = sr
            fin_ref[0, d, :, n:2 * n] = si
        else:
            lam_r, lam_i = lam[:, :n], lam[:, n:]
            row8 = lax.broadcasted_iota(jnp.int32, (rows, 1), 0)
            cr, ci = lam_r, lam_i
            acc_r = jnp.broadcast_to(cr, (rows, n))
            acc_i = jnp.broadcast_to(ci, (rows, n))
            for j in range(1, rows):
                cr, ci = _cmul(cr, ci, lam_r, lam_i)
                acc_r = jnp.where(row8 >= j, jnp.broadcast_to(cr, (rows, n)), acc_r)
                acc_i = jnp.where(row8 >= j, jnp.broadcast_to(ci, (rows, n)), acc_i)
            pw[0:rows, 0:n] = acc_r
            pw[0:rows, n:2 * n] = acc_i
            size = rows
            while size < steps:
                tr = pw[size - 1:size, 0:n]
                ti = pw[size - 1:size, n:2 * n]
                xr, xi = _cmul(pw[0:size, 0:n], pw[0:size, n:2 * n], tr, ti)
                pw[size:2 * size, 0:n] = xr
                pw[size:2 * size, n:2 * n] = xi
                size *= 2
            end_off = 0 if rev else (steps - 1) * rows
            loc_r = sbuf[end_off:end_off + rows, 0:n]
            loc_i = sbuf[end_off:end_off + rows, n:2 * n]
            pl_r = pw[steps - 1:steps, 0:n]
            pl_i = pw[steps - 1:steps, n:2 * n]
            s0r = s0_ref[d, :, 0:n]
            s0i = s0_ref[d, :, n:2 * n]
            seg = row8 // 2
            first = (seg == n_seg - 1) if rev else (seg == 0)
            shift = (rows - 2) if rev else 2
            cin_r, cin_i = s0r, s0i
            for _ in range(n_seg - 1):
                fr, fi = _cmul(jnp.broadcast_to(pl_r, (rows, n)), jnp.broadcast_to(pl_i, (rows, n)), cin_r, cin_i)
                tru_r = loc_r + fr
                tru_i = loc_i + fi
                cin_r = jnp.where(first, s0r, pltpu.roll(tru_r, shift, 0))
                cin_i = jnp.where(first, s0i, pltpu.roll(tru_i, shift, 0))

            def fix(tb, carry, rev=rev, cin_r=cin_r, cin_i=cin_i):
                pb = (steps // rows - 1 - tb) if rev else tb
                poff = pl.multiple_of(pb * rows, rows)
                p_r = pw[pl.ds(poff, rows), 0:n]
                p_i = pw[pl.ds(poff, rows), n:2 * n]
                for j in range(rows):
                    jj = rows - 1 - j if rev else j
                    off = pl.multiple_of((tb * rows + j) * rows, rows)
                    fr, fi = _cmul(jnp.broadcast_to(p_r[jj:jj + 1], (rows, n)),
                                   jnp.broadcast_to(p_i[jj:jj + 1], (rows, n)), cin_r, cin_i)
                    sbuf[pl.ds(off, rows), 0:n] += fr
                    sbuf[pl.ds(off, rows), n:2 * n] += fi
                return carry

            lax.fori_loop(0, steps // rows, fix, 0)

        for i in range(n_blk):
            yacc[i * blk:(i + 1) * blk, :] += jnp.dot(sbuf[i * blk:(i + 1) * blk, :].astype(BF16), cmat[...],
                                                      preferred_element_type=F32)

    g = jax.nn.gelu(yacc[...], approximate=True)
    y_ref[0] = g * jax.nn.sigmoid(_bdot(g, wglu_ref[...]))


@functools.lru_cache(None)
def _s5_spread():
    eb = np.zeros((2 * S5_P, 2 * S5_STATE), np.float32)
    for half in range(2):
        for g in range(S5_G):
            c0 = half * S5_STATE + g * S5_P
            eb[half * S5_P:(half + 1) * S5_P, c0:c0 + S5_P] = np.eye(S5_P)
    ec = np.zeros((LANES, S5_CH), np.float32)
    for g in range(S5_G):
        ec[:S5_GROUP, g * S5_GROUP:(g + 1) * S5_GROUP] = np.eye(S5_GROUP)
    return eb, ec


def _s5(u_tm, bb, cc, lam, dskip, wglu, s0):
    n_grp = u_tm.shape[0]
    n_rows = S5_STEPS * S5_ROWS
    segmented = s0 is not None
    eb, ec = (jnp.asarray(m).astype(BF16) for m in _s5_spread())
    full = lambda a: pl.BlockSpec(a.shape, lambda g: (0,) * a.ndim)
    in_specs = [pl.BlockSpec((1, n_rows, S5_CH), lambda g: (g, 0, 0)),
                full(bb), full(cc), full(eb), full(ec), full(lam), full(dskip), full(wglu)]
    args = [u_tm, bb, cc, eb, ec, lam, dskip, wglu]
    out_specs = [pl.BlockSpec((1, n_rows, S5_CH), lambda g: (g, 0, 0))]
    out_shape = [jax.ShapeDtypeStruct((n_grp, n_rows, S5_CH), F32)]
    scratch = [pltpu.VMEM((n_rows, 2 * S5_STATE), F32), pltpu.VMEM((n_rows, S5_CH), F32),
               pltpu.VMEM((S5_CH, 2 * S5_STATE), BF16), pltpu.VMEM((2 * S5_STATE, S5_CH), BF16)]
    if segmented:
        in_specs.append(full(s0))
        args.append(s0)
        scratch.append(pltpu.VMEM((S5_STEPS, 2 * S5_STATE), F32))
    else:
        out_specs.append(pl.BlockSpec((1, 2, S5_ROWS, 2 * S5_STATE), lambda g: (g, 0, 0, 0)))
        out_shape.append(jax.ShapeDtypeStruct((n_grp, 2, S5_ROWS, 2 * S5_STATE), F32))
    return pl.pallas_call(
        functools.partial(_s5_kernel, segmented=segmented),
        grid=(n_grp,),
        in_specs=in_specs,
        out_specs=out_specs,
        out_shape=out_shape,
        scratch_shapes=scratch,
        compiler_params=_cparams(("arbitrary",)),
        name="s5_seg" if segmented else "s5_ctx",
    )(*args)


def _s5_params(a_re, a_im, log_dt, b_re, b_im, c_re, c_im):
    dt = jnp.exp(log_dt)[:, :, None]
    mag = jnp.exp(a_re * dt)
    lb_re = mag * jnp.cos(a_im * dt)
    lb_im = mag * jnp.sin(a_im * dt)
    den = a_re * a_re + a_im * a_im
    nr, ni = lb_re - 1.0, lb_im
    k_re = (nr * a_re + ni * a_im) / den
    k_im = (ni * a_re - nr * a_im) / den
    bb_re = k_re[..., None] * b_re - k_im[..., None] * b_im
    bb_im = k_re[..., None] * b_im + k_im[..., None] * b_re
    to_gc_p = lambda m: m.transpose(0, 1, 3, 2).reshape(2, S5_CH, S5_P)
    bb = jnp.concatenate([to_gc_p(bb_re), to_gc_p(bb_im)], axis=2)
    to_gp_c = lambda m: m.transpose(0, 1, 3, 2).reshape(2, S5_STATE, S5_GROUP)
    cc = jnp.concatenate([to_gp_c(c_re), -to_gp_c(c_im)], axis=1)
    cc = jnp.pad(cc, ((0, 0), (0, 0), (0, LANES - S5_GROUP)))
    lam = jnp.concatenate([lb_re.reshape(2, 1, S5_STATE), lb_im.reshape(2, 1, S5_STATE)], axis=2)
    return bb, cc, lam


def _outproj_kernel(*refs, n_x):
    x_refs = refs[:n_x]
    (hyc_ref, hyl_ref, mlc_ref, mll_ref, s5c_ref, s5l_ref, w_ref, mod_ref, g_ref,
     rw_ref, rb_ref, xn_ref, h2e_ref, best_ref, rank_ref, cnt_ref, wb, cnt_s) = refs[n_x:]
    step = pl.program_id(0)

    @pl.when(step == 0)
    def _():
        wb[...] = w_ref[0].astype(BF16)
        cnt_s[...] = jnp.zeros_like(cnt_s)

    is_ctx = step < T_CTX // TM
    pick = lambda c_ref, l_ref: jnp.where(is_ctx, c_ref[...], l_ref[...]).astype(BF16)
    mod = mod_ref[0]
    a, b = HY_CH, HY_CH + ML_W
    mix = (jnp.dot(pick(hyc_ref, hyl_ref), wb[0:a, :], preferred_element_type=F32)
           + jnp.dot(pick(mlc_ref, mll_ref), wb[a:b, :], preferred_element_type=F32)
           + jnp.dot(pick(s5c_ref, s5l_ref), wb[b:, :], preferred_element_type=F32))
    xn = _x_tile(step, x_refs) + mod[2:3] * mix
    xn_ref[...] = xn
    h2 = _rms(xn, g_ref[...]) * (1.0 + mod[4:5]) + mod[3:4]
    h2e_ref[:, 0:D] = h2
    logits = lax.dot_general(rw_ref[...], h2, (((1,), (1,)), ((), ())),
                             precision=HIGHEST, preferred_element_type=F32)
    ex = jnp.exp(logits - jnp.max(logits, axis=0, keepdims=True))
    probs = ex / jnp.sum(ex, axis=0, keepdims=True)
    sel = probs + rb_ref[...]
    best = None
    best_score = None
    for g in range(N_GROUPS):
        r = [sel[g * GROUP_SIZE + i:g * GROUP_SIZE + i + 1, :] for i in range(GROUP_SIZE)]
        score = None
        for i in range(GROUP_SIZE):
            for j in range(i + 1, GROUP_SIZE):
                pair = r[i] + r[j]
                score = pair if score is None else jnp.maximum(score, pair)
        if g == 0:
            best, best_score = jnp.zeros_like(score, dtype=jnp.int32), score
        else:
            upd = score > best_score
            best = jnp.where(upd, g, best)
            best_score = jnp.where(upd, score, best_score)
    eid = lax.broadcasted_iota(jnp.int32, (N_EXPERTS, 1), 0)
    masked = jnp.where(eid // GROUP_SIZE == best, sel, -jnp.inf)
    m1 = jnp.max(masked, axis=0, keepdims=True)
    i1 = jnp.min(jnp.where(masked == m1, eid, N_EXPERTS), axis=0, keepdims=True)
    masked2 = jnp.where(eid == i1, -jnp.inf, masked)
    m2 = jnp.max(masked2, axis=0, keepdims=True)
    i2 = jnp.min(jnp.where(masked2 == m2, eid, N_EXPERTS), axis=0, keepdims=True)
    p1 = jnp.sum(jnp.where(eid == i1, probs, 0.0), axis=0, keepdims=True)
    p2 = jnp.sum(jnp.where(eid == i2, probs, 0.0), axis=0, keepdims=True)
    tot = p1 + p2
    comb = jnp.where(eid == i1, p1 / tot, 0.0) + jnp.where(eid == i2, p2 / tot, 0.0)
    comb = jnp.concatenate([comb, jnp.zeros((LANES - N_EXPERTS, comb.shape[1]), F32)], axis=0)
    h2e_ref[:, D:] = comb.T
    best_ref[...] = best
    gid = lax.broadcasted_iota(jnp.int32, (8, 1), 0)
    onehot = (gid == best).astype(F32)
    cum = _cumsum_lanes(onehot, TM, False)
    run = cnt_s[:, 0:1]
    rank_ref[...] = jnp.sum(onehot * (cum - onehot + run), axis=0, keepdims=True).astype(jnp.int32)
    cnt_s[...] = jnp.broadcast_to(run + cum[:, TM - 1:TM], cnt_s.shape)
    cnt_ref[...] = cnt_s[...]


def _outproj(x, y_hy, y_ml, y_s5, w_out, l, mod, g2, rw_t, rb):
    full = lambda a: pl.BlockSpec(a.shape, lambda i: (0,) * a.ndim)
    n_ctx = T_CTX // TM
    ctx = lambda w: pl.BlockSpec((TM, w), lambda i: (jnp.minimum(i, n_ctx - 1), 0))
    lat = lambda w: pl.BlockSpec((TM, w), lambda i: (jnp.maximum(i - n_ctx, 0), 0))
    tok = lambda w: pl.BlockSpec((TM, w), lambda i: (i, 0))
    row = pl.BlockSpec((1, TM), lambda i: (0, i))
    return pl.pallas_call(
        functools.partial(_outproj_kernel, n_x=len(x)),
        grid=(T_ALL // TM,),
        in_specs=_x_specs(x) + [ctx(HY_CH), lat(HY_CH), ctx(ML_W), lat(ML_W), ctx(S5_CH), lat(S5_CH),
                  pl.BlockSpec((1, D, D), lambda i: (l, 0, 0), pipeline_mode=pl.Buffered(1)),
                  pl.BlockSpec((1, 6, D), lambda i: (_mod_row(i, TM), 0, 0)),
                  full(g2), full(rw_t), full(rb)],
        out_specs=[tok(D), tok(D + LANES), row, row, pl.BlockSpec((8, LANES), lambda i: (0, 0))],
        out_shape=[jax.ShapeDtypeStruct((T_ALL, D), F32),
                   jax.ShapeDtypeStruct((T_ALL, D + LANES), F32),
                   jax.ShapeDtypeStruct((1, T_ALL), jnp.int32),
                   jax.ShapeDtypeStruct((1, T_ALL), jnp.int32),
                   jax.ShapeDtypeStruct((8, LANES), F32)],
        scratch_shapes=[pltpu.VMEM((D, D), BF16), pltpu.VMEM((8, LANES), F32)],
        compiler_params=_cparams(("arbitrary",)),
        name="outproj_router",
    )(*x, *y_hy, *y_ml, *y_s5, w_out, mod, g2, rw_t, rb)


def _gather_rows(idx_ref, idx_base, src_ref, dst_ref, n_rows):
    def body(r8, carry):
        base = pl.multiple_of(r8 * 8, 8)
        for k in range(8):
            idx = idx_ref[idx_base + base + k]
            dst_ref[pl.ds(base + k, 1), :] = src_ref[pl.ds(idx, 1), :]
        return carry

    lax.fori_loop(0, n_rows // 8, body, 0)


def _dispatch(best, rank, cnt):
    tm = TM_MOE
    g = best.reshape(T_ALL)
    cnt = cnt[:N_GROUPS, 0].astype(jnp.int32)
    n_tile_g = (cnt + tm - 1) // tm
    tile_end = jnp.cumsum(n_tile_g)
    row_off = (tile_end - n_tile_g) * tm
    pos = rank.reshape(T_ALL)
    for k in range(N_GROUPS):
        pos = pos + jnp.where(g == k, row_off[k], 0)
    src = jnp.zeros((MOE_SLOTS,), jnp.int32).at[pos].set(jnp.arange(T_ALL, dtype=jnp.int32))
    tiles = jnp.arange(MOE_SLOTS // tm, dtype=jnp.int32)
    tile_group = jnp.minimum(jnp.sum((tiles[:, None] >= tile_end[None, :]).astype(jnp.int32), axis=1), N_GROUPS - 1)
    return pos, src, tile_group.astype(jnp.int32), tile_end[N_GROUPS - 1:].astype(jnp.int32)


def _moe_kernel(src_ref, tg_ref, nt_ref, h_ref, wg_ref, wu_ref, wd_ref, ys_ref, gbuf, xs, acc):
    i = pl.program_id(0)
    j = pl.program_id(1)
    tm = TM_MOE

    @pl.when(i < nt_ref[0])
    def _():
        @pl.when(j == 0)
        def _():
            _gather_rows(src_ref, i * tm, h_ref, gbuf, tm)
            xs[...] = gbuf[:, 0:D].astype(BF16)

        e = tg_ref[i] * GROUP_SIZE + j
        lane = lax.broadcasted_iota(jnp.int32, (1, LANES), 1)
        wg = wg_ref[0, 0].astype(BF16)
        wu = wu_ref[0, 0].astype(BF16)
        wd = wd_ref[0, 0].astype(BF16)
        sub = 256
        for s in range(tm // sub):
            rs = slice(s * sub, (s + 1) * sub)
            hb = xs[rs, :]
            ce = jnp.sum(jnp.where(lane == e, gbuf[rs, D:], 0.0), axis=1, keepdims=True)
            hid = _silu(jnp.dot(hb, wg, preferred_element_type=F32)) * jnp.dot(hb, wu, preferred_element_type=F32)
            part = jnp.dot((hid * ce).astype(BF16), wd, preferred_element_type=F32)

            @pl.when(j == 0)
            def _():
                acc[rs, :] = part

            @pl.when(j > 0)
            def _():
                acc[rs, :] += part

        @pl.when(j == GROUP_SIZE - 1)
        def _():
            ys_ref[...] = acc[...]


def _moe(h2e, src, tile_group, n_tiles, wg, wu, wd, l):
    tm = TM_MOE

    def w_map(i, j, src, tg, nt):
        live = i < nt[0]
        ii = jnp.minimum(i, nt[0] - 1)
        return (l, tg[ii] * GROUP_SIZE + jnp.where(live, j, GROUP_SIZE - 1), 0, 0)

    return pl.pallas_call(
        _moe_kernel,
        grid_spec=pltpu.PrefetchScalarGridSpec(
            num_scalar_prefetch=3,
            grid=(MOE_SLOTS // tm, GROUP_SIZE),
            in_specs=[pl.BlockSpec(memory_space=pltpu.VMEM),
                      pl.BlockSpec((1, 1, D, D_EXPERT), w_map),
                      pl.BlockSpec((1, 1, D, D_EXPERT), w_map),
                      pl.BlockSpec((1, 1, D_EXPERT, D), w_map)],
            out_specs=pl.BlockSpec((tm, D), lambda i, j, src, tg, nt: (jnp.minimum(i, nt[0] - 1), 0)),
            scratch_shapes=[pltpu.VMEM((tm, D + LANES), F32), pltpu.VMEM((tm, D), BF16),
                            pltpu.VMEM((tm, D), F32)]),
        out_shape=jax.ShapeDtypeStruct((MOE_SLOTS, D), F32),
        compiler_params=_cparams(("arbitrary", "arbitrary")),
        name="moe_experts",
    )(src, tile_group, n_tiles, h2e, wg, wu, wd)


def _combine_kernel(pos_ref, ys_ref, xn_ref, mod_ref, fg_ref, *rest, final):
    step = pl.program_id(0)
    if final:
        yc_ref, yl_ref, gbuf = rest
    else:
        out_ref, gbuf = rest
    _gather_rows(pos_ref, step * TM, ys_ref, gbuf, TM)
    out = xn_ref[...] + mod_ref[0][5:6] * gbuf[...]
    if final:
        y = _rms(out, fg_ref[...])

        @pl.when(step < T_CTX // TM)
        def _():
            yc_ref[...] = y

        @pl.when(step >= T_CTX // TM)
        def _():
            yl_ref[...] = y
    else:
        out_ref[...] = out


def _combine(pos, ys, xn, mod, fg, final):
    spec = pl.BlockSpec((TM, D), lambda i, pos: (i, 0))
    n_ctx = T_CTX // TM
    if final:
        out_specs = [pl.BlockSpec((TM, D), lambda i, pos: (jnp.minimum(i, n_ctx - 1), 0)),
                     pl.BlockSpec((TM, D), lambda i, pos: (jnp.maximum(i - n_ctx, 0), 0))]
        out_shape = [jax.ShapeDtypeStruct((T_CTX, D), F32), jax.ShapeDtypeStruct((T_LAT, D), F32)]
    else:
        out_specs = [spec]
        out_shape = [jax.ShapeDtypeStruct((T_ALL, D), F32)]
    return pl.pallas_call(
        functools.partial(_combine_kernel, final=final),
        grid_spec=pltpu.PrefetchScalarGridSpec(
            num_scalar_prefetch=1,
            grid=(T_ALL // TM,),
            in_specs=[pl.BlockSpec(memory_space=pltpu.VMEM),
                      spec,
                      pl.BlockSpec((1, 6, D), lambda i, pos: (_mod_row(i, TM), 0, 0)),
                      pl.BlockSpec((1, D), lambda i, pos: (0, 0))],
            out_specs=out_specs,
            scratch_shapes=[pltpu.VMEM((TM, D), F32)]),
        out_shape=out_shape,
        compiler_params=_cparams(("arbitrary",)),
        name="moe_combine",
    )(pos, ys, xn, mod, fg)


@functools.lru_cache(None)
def _pos_embed():
    rows = L_LAT // GRID_W
    r = np.repeat(np.arange(rows, dtype=np.float64), GRID_W)
    col = np.tile(np.arange(GRID_W, dtype=np.float64), rows)
    quarter = D // 4
    freq = np.exp(-math.log(POS_BASE) * np.arange(quarter, dtype=np.float64) / quarter)
    ar = r[:, None] * freq[None]
    ac = col[:, None] * freq[None]
    emb = np.concatenate([np.sin(ar), np.cos(ar), np.sin(ac), np.cos(ac)], axis=-1)
    return emb.astype(np.float32)


def _to_time_major_ctx(a):
    c = a.shape[-1]
    a = a.reshape(2, S5_ROWS, L_CTX, c).transpose(0, 2, 1, 3)
    return a.reshape(2, L_CTX * S5_ROWS, c)


def _from_time_major_ctx(a):
    c = a.shape[-1]
    a = a.reshape(2, L_CTX, S5_ROWS, c).transpose(0, 2, 1, 3)
    return a.reshape(T_CTX, c)


def _to_time_major_lat(a):
    c = a.shape[-1]
    a = a.reshape(N_LAT_SEQ, 4, S5_STEPS, c).transpose(2, 1, 0, 3)
    return a.reshape(1, S5_STEPS * S5_ROWS, c)


def _from_time_major_lat(a):
    c = a.shape[-1]
    a = a.reshape(S5_STEPS, 4, N_LAT_SEQ, c).transpose(2, 1, 0, 3)
    return a.reshape(T_LAT, c)


def kernel(x_prompt, x_sample, c, state_mlstm_C, state_mlstm_n, state_mlstm_m, state_s5_re, state_s5_im, c_ctx, w_ada, b_ada, norm1_g, norm2_g, final_g, w_in, w_out, hy_short, hy_fw1, hy_fb1, hy_fw2, hy_fb2, hy_fw3, hy_log_decay, hy_bias, ml_short, ml_gate_bias, ml_norm_g, s5_a_re, s5_a_im, s5_log_dt, s5_b_re, s5_b_im, s5_c_re, s5_c_im, s5_d, s5_w_glu, router_w, router_b, moe_w_gate, moe_w_up, moe_w_down):
    x = (x_prompt.reshape(T_CTX, D), x_sample.reshape(T_LAT, D), jnp.asarray(_pos_embed()))
    w_in_t = jnp.swapaxes(w_in, 1, 2)
    cc =jnp.concatenate([c_ctx[None], c, jnp.zeros((8 - 1 - N_LAT_SEQ, D), F32)], axis=0)
    mod_all = _ada(cc, w_ada, b_ada).reshape(DEPTH, 8, 6, D)
    rw_t = router_w.T
    rb = router_b.reshape(N_EXPERTS, 1)
    fg = final_g.reshape(1, D)
    lat_blk = T_CTX // L_LAT

    new_c, new_n, new_m, new_re, new_im = [], [], [], [], []
    y_prompt = y_sample = None
    for l in range(DEPTH):
        mod = mod_all[l]
        u_hy, qk, v, o, u_s5, gates = _inproj(x, norm1_g[l], mod, w_in_t, l)

        w1p = jnp.zeros((LANES, HY_FILTER_W), F32).at[:HY_EMB].set(hy_fw1[l])
        b1 = hy_fb1[l].reshape(1, HY_FILTER_W)
        b2 = hy_fb2[l].reshape(1, HY_FILTER_W)
        ld = hy_log_decay[l].reshape(1, 4 * HY_CH)
        y_hy = []
        for L, n_seq, blk0 in ((L_CTX, N_CTX_SEQ, 0), (L_LAT, N_LAT_SEQ, lat_blk)):
            p, q, r = _hy_filter(L, w1p, b1, hy_fw2[l], b2, hy_fw3[l], ld)
            y_hy.append(_hyena(u_hy, L, n_seq, blk0, hy_short[l], hy_bias[l], p, q, r))

        gates_t = gates[:, :16].reshape(T_ALL // ML_CHUNK, ML_CHUNK, 16).transpose(0, 2, 1)
        gb = jnp.zeros((1, GATE_PAD), F32).at[0, :16].set(ml_gate_bias[l].reshape(16))
        gbt = ml_gate_bias[l].reshape(16, 1)
        ng = ml_norm_g[l].reshape(1, ML_W)
        m0 = jnp.broadcast_to(state_mlstm_m[:, l].reshape(N_LAT_SEQ, 2 * ML_HEADS, 1), (N_LAT_SEQ, 2 * ML_HEADS, LANES))
        yc, cc_, nc_, mc_ = _mlstm(qk, v, o, gates, gates_t, L_CTX, N_CTX_SEQ, 0, ml_short[l], gb, gbt, ng, None)
        yl, _, _, _ = _mlstm(qk, v, o, gates, gates_t, L_LAT, N_LAT_SEQ, lat_blk, ml_short[l], gb, gbt, ng,
                             (state_mlstm_C[:, l].reshape(N_LAT_SEQ, 2 * ML_HEADS, ML_DH, ML_DH),
                              state_mlstm_n[:, l].reshape(N_LAT_SEQ, 2 * ML_HEADS, ML_DH), m0))
        y_ml = (yc, yl)
        new_c.append(cc_.reshape(N_CTX_SEQ, 2, ML_HEADS, ML_DH, ML_DH))
        new_n.append(nc_.reshape(N_CTX_SEQ, 2, ML_HEADS, ML_DH))
        new_m.append(mc_[:, :, 0].reshape(N_CTX_SEQ, 2, ML_HEADS))

        bb, cc_s5, lam = _s5_params(s5_a_re[l], s5_a_im[l], s5_log_dt[l], s5_b_re[l], s5_b_im[l],
                                    s5_c_re[l], s5_c_im[l])
        dsk = s5_d[l].reshape(1, S5_CH)
        wglu = s5_w_glu[l].astype(BF16)
        ys_c, fin = _s5(_to_time_major_ctx(u_s5[:T_CTX]), bb, cc_s5, lam, dsk, wglu, None)
        s0 = jnp.concatenate([state_s5_re[:, l].reshape(N_LAT_SEQ, 2, S5_STATE),
                              state_s5_im[:, l].reshape(N_LAT_SEQ, 2, S5_STATE)], axis=-1)
        s0 = jnp.tile(s0.transpose(1, 0, 2), (1, 4, 1))
        (ys_l,) = _s5(_to_time_major_lat(u_s5[T_CTX:]), bb, cc_s5, lam, dsk, wglu, s0)
        y_s5 = (_from_time_major_ctx(ys_c), _from_time_major_lat(ys_l))
        fin = fin.transpose(0, 2, 1, 3).reshape(N_CTX_SEQ, 2, 2 * S5_STATE)
        new_re.append(fin[..., :S5_STATE].reshape(N_CTX_SEQ, 2, S5_G, S5_P))
        new_im.append(fin[..., S5_STATE:].reshape(N_CTX_SEQ, 2, S5_G, S5_P))

        xn, h2e, best, rank, cnt = _outproj(x, y_hy, y_ml, y_s5, w_out, l, mod,
                                            norm2_g[l].reshape(1, D), rw_t, rb)
        pos, src, tile_group, n_tiles = _dispatch(best, rank, cnt)
        ys = _moe(h2e, src, tile_group, n_tiles, moe_w_gate, moe_w_up, moe_w_down, l)
        res = _combine(pos, ys, xn, mod, fg, l == DEPTH - 1)
        if l == DEPTH - 1:
            y_prompt = res[0].reshape(N_CTX_SEQ, L_CTX, D)
            y_sample = res[1].reshape(N_LAT_SEQ, L_LAT, D)
        else:
            x = (res[0],)

    return (y_prompt, y_sample, jnp.stack(new_c, axis=1), jnp.stack(new_n, axis=1), jnp.stack(new_m, axis=1),
            jnp.stack(new_re, axis=1), jnp.stack(new_im, axis=1))
```

```python
import functools
import math

import numpy as np
import jax
import jax.numpy as jnp
from jax import lax
from jax.experimental import pallas as pl
from jax.experimental.pallas import tpu as pltpu

F32 = jnp.float32
BF16 = jnp.bfloat16
HIGHEST = lax.Precision.HIGHEST

D = 1024
N_CTX_SEQ, L_CTX = 16, 256
N_LAT_SEQ, L_LAT = 2, 1024
T_CTX = N_CTX_SEQ * L_CTX
T_LAT = N_LAT_SEQ * L_LAT
T_ALL = T_CTX + T_LAT
DEPTH = 2
EPS = 1e-6
GRID_W = 64
POS_BASE = 10000.0
HY_CH = 256
HY_EMB = 33
HY_FILTER_W = 64
ML_HEADS = 4
ML_DH = 128
ML_W = ML_HEADS * ML_DH
S5_CH = 256
S5_G = 16
S5_GROUP = 16
S5_P = 64
S5_STATE = S5_G * S5_P
N_EXPERTS = 16
N_GROUPS = 4
GROUP_SIZE = N_EXPERTS // N_GROUPS
D_EXPERT = 512
OFF_HY = 0
OFF_QK = 3 * HY_CH
OFF_V = OFF_QK + 2 * ML_W
OFF_O = OFF_V + ML_W
OFF_G = OFF_O + ML_W
OFF_S5 = OFF_G + 16
IN_W = OFF_S5 + S5_CH
LANES = 128
GATE_PAD = LANES

TM = 512
TM_MOE = 512
MOE_SLOTS = T_ALL + N_GROUPS * TM_MOE
ML_CHUNK = 256
S5_ROWS = 8
S5_STEPS = 256
VMEM_LIMIT = 56 * 1024 * 1024


def _cparams(sem, vmem=VMEM_LIMIT):
    if sem is None:
        return pltpu.CompilerParams(vmem_limit_bytes=vmem)
    return pltpu.CompilerParams(dimension_semantics=sem, vmem_limit_bytes=vmem)


def _bdot(a, b):
    return jnp.dot(a.astype(BF16), b.astype(BF16), preferred_element_type=F32)


def _silu(x):
    return x * jax.nn.sigmoid(x)


def _rms(x, g):
    return x * lax.rsqrt(jnp.mean(x * x, axis=-1, keepdims=True) + EPS) * g


def _log_sigmoid(x):
    return jnp.minimum(x, 0.0) - jnp.log1p(jnp.exp(-jnp.abs(x)))


def _conv3(u, w, n_rows, seq_len):
    row = lax.broadcasted_iota(jnp.int32, (n_rows, 1), 0) % seq_len
    prev = jnp.where(row == 0, 0.0, pltpu.roll(u, 1, 0))
    nxt = jnp.where(row == seq_len - 1, 0.0, pltpu.roll(u, n_rows - 1, 0))
    return prev * w[0:1] + u * w[1:2] + nxt * w[2:3]


def _ada_kernel(c_ref, w_ref, b_ref, o_ref):
    o_ref[0] = _bdot(_silu(c_ref[...]), w_ref[0]) + b_ref[0]


def _ada(cc, w_ada, b_ada):
    tn = 1536
    return pl.pallas_call(
        _ada_kernel,
        grid=(DEPTH, 6 * D // tn),
        in_specs=[pl.BlockSpec((8, D), lambda l, j: (0, 0)),
                  pl.BlockSpec((1, D, tn), lambda l, j: (l, 0, j)),
                  pl.BlockSpec((1, 1, tn), lambda l, j: (l, 0, j))],
        out_specs=pl.BlockSpec((1, 8, tn), lambda l, j: (l, 0, j)),
        out_shape=jax.ShapeDtypeStruct((DEPTH, 8, 6 * D), F32),
        compiler_params=_cparams(("arbitrary", "arbitrary")),
        name="ada_mod",
    )(cc, w_ada, b_ada.reshape(DEPTH, 1, 6 * D))


def _mod_row(i, tm):
    n_ctx = T_CTX // tm
    return jnp.where(i < n_ctx, 0, 1 + (i - n_ctx) // (L_LAT // tm))


_SEG = ((OFF_HY, OFF_QK - OFF_HY), (OFF_QK, OFF_V - OFF_QK), (OFF_V, OFF_O - OFF_V), (OFF_O, OFF_G - OFF_O))
TAIL_W = IN_W - OFF_G


N_CTX_TILES = T_CTX // TM
_NT = (((1,), (1,)), ((), ()))


def _x_specs(x):
    if len(x) == 1:
        return [pl.BlockSpec((TM, D), lambda i, *_: (i, 0))]
    per_seq = L_LAT // TM
    return [pl.BlockSpec((TM, D), lambda i, *_: (jnp.minimum(i, N_CTX_TILES - 1), 0)),
            pl.BlockSpec((TM, D), lambda i, *_: (jnp.maximum(i - N_CTX_TILES, 0), 0)),
            pl.BlockSpec((TM, D), lambda i, *_: (jnp.maximum(i - N_CTX_TILES, 0) % per_seq, 0))]


def _x_tile(step, x_refs):
    if len(x_refs) == 1:
        return x_refs[0][...]
    xc_ref, xl_ref, pos_ref = x_refs
    return jnp.where(step < N_CTX_TILES, xc_ref[...], xl_ref[...] + pos_ref[...])


def _inproj_kernel(*refs, n_x):
    x_refs = refs[:n_x]
    g_ref, mod_ref, w_ref, hy_ref, qk_ref, v_ref, o_ref, s5_ref, gt_ref, wb = refs[n_x:]
    step = pl.program_id(0)

    @pl.when(step == 0)
    def _():
        wb[...] = w_ref[0].astype(BF16)

    mod = mod_ref[0]
    h = _rms(_x_tile(step, x_refs), g_ref[...]) * (1.0 + mod[1:2]) + mod[0:1]
    hb = h.astype(BF16)
    for (a, w), ref in zip(_SEG, (hy_ref, qk_ref, v_ref, o_ref)):
        ref[...] = lax.dot_general(hb, wb[a:a + w, :], _NT, preferred_element_type=F32)
    tail = lax.dot_general(hb, wb[OFF_G:IN_W, :], _NT, preferred_element_type=F32)
    gt_ref[...] = tail[:, 0:GATE_PAD]
    s5_ref[...] = tail[:, OFF_S5 - OFF_G:TAIL_W]


def _inproj(x, g, mod, w_in_t, l):
    widths = [w for _, w in _SEG] + [S5_CH, GATE_PAD]
    return pl.pallas_call(
        functools.partial(_inproj_kernel, n_x=len(x)),
        grid=(T_ALL // TM,),
        in_specs=_x_specs(x) + [
            pl.BlockSpec((1, D), lambda i: (0, 0)),
            pl.BlockSpec((1, 6, D), lambda i: (_mod_row(i, TM), 0, 0)),
            pl.BlockSpec((1, IN_W, D), lambda i: (l, 0, 0), pipeline_mode=pl.Buffered(1))],
        out_specs=[pl.BlockSpec((TM, w), lambda i: (i, 0)) for w in widths],
        out_shape=[jax.ShapeDtypeStruct((T_ALL, w), F32) for w in widths],
        scratch_shapes=[pltpu.VMEM((IN_W, D), BF16)],
        compiler_params=_cparams(("arbitrary",)),
        name="norm_inproj",
    )(*x, g.reshape(1, D), mod, w_in_t)


@functools.lru_cache(None)
def _dft_mats(L):
    n = 2 * L
    k = np.arange(L)[:, None]
    t = np.arange(L)[None, :]
    ang = 2.0 * np.pi * ((k * t) % n) / n
    top = np.cos(ang)
    bot = -np.sin(ang)
    bot[0] = np.cos(np.pi * np.arange(L))
    fwd = np.concatenate([top, bot], 0)
    s = np.full((n, 1), 2.0 / n)
    s[0] = s[L] = 1.0 / n
    inv = (fwd * s).T
    return fwd.astype(np.float32), inv.astype(np.float32)


@functools.lru_cache(None)
def _hy_positions(L):
    t = np.linspace(0.0, 1.0, L)
    bands = (HY_EMB - 1) // 2
    f = np.linspace(1e-4, bands - 1, bands)
    w = 2.0 * np.pi * np.arange(L) / L
    ang = w[:, None] * f[None, :]
    z = np.concatenate([t[:, None], np.cos(ang), -np.sin(ang)], -1)
    zp = np.zeros((L, LANES))
    zp[:, :HY_EMB] = z
    return zp.astype(np.float32), t[:, None].astype(np.float32)


def _hy_filter_kernel(z_ref, t_ref, w1_ref, b1_ref, w2_ref, b2_ref, w3_ref, ld_ref, f_ref,
                      p_ref, q_ref, r_ref, *, L):
    h = jnp.sin(jnp.dot(z_ref[...], w1_ref[...], precision=HIGHEST, preferred_element_type=F32) + b1_ref[...])
    h = jnp.sin(jnp.dot(h, w2_ref[...], precision=HIGHEST, preferred_element_type=F32) + b2_ref[...])
    filt = jnp.dot(h, w3_ref[...], precision=HIGHEST, preferred_element_type=F32)
    filt = filt * jnp.exp(-t_ref[...] * jnp.exp(ld_ref[...]))
    c = HY_CH
    h_fwd = jnp.concatenate([filt[:, 0:c], filt[:, 2 * c:3 * c]], axis=1)
    h_bwd = jnp.concatenate([filt[:, c:2 * c], filt[:, 3 * c:4 * c]], axis=1)
    row = lax.broadcasted_iota(jnp.int32, (L, 1), 0)
    h_bwd = jnp.where(row == 0, 0.0, h_bwd)
    a = jnp.dot(f_ref[...], h_fwd.astype(BF16), preferred_element_type=F32)
    b = jnp.dot(f_ref[...], h_bwd.astype(BF16), preferred_element_type=F32)
    re = a[:L] + b[:L]
    im = a[L:] - b[L:]
    nyq = a[L:L + 1] + b[L:L + 1]
    p_ref[...] = re
    q_ref[...] = jnp.where(row == 0, 0.0, im)
    r_ref[...] = jnp.where(row == 0, nyq, re)


def _hy_filter(L, w1p, b1, w2, b2, w3, ld):
    z, t = _hy_positions(L)
    fwd = jnp.asarray(_dft_mats(L)[0]).astype(BF16)
    out = jax.ShapeDtypeStruct((L, 2 * HY_CH), F32)
    return pl.pallas_call(
        functools.partial(_hy_filter_kernel, L=L),
        out_shape=[out, out, out],
        compiler_params=_cparams(None),
        name=f"hyena_filter_{L}",
    )(z, t, w1p, b1, w2, b2, w3, ld, fwd)


def _hyena_kernel(u_ref, sw_ref, bias_ref, p_ref, q_ref, r_ref, f_ref, g_ref, o_ref, *, L):
    c = HY_CH
    u = _conv3(u_ref[...], sw_ref[...], L, L)
    z = u[:, 0:c]
    for o in range(2):
        gate = u[:, (o + 1) * c:(o + 2) * c]
        zf = jnp.dot(f_ref[...], z.astype(BF16), preferred_element_type=F32)
        a, b = zf[:L], zf[L:]
        p = p_ref[:, o * c:(o + 1) * c]
        q = q_ref[:, o * c:(o + 1) * c]
        r = r_ref[:, o * c:(o + 1) * c]
        y_re = (a * p - b * q).astype(BF16)
        y_im = (a * q + b * r).astype(BF16)
        y = (jnp.dot(g_ref[:, :L], y_re, preferred_element_type=F32)
             + jnp.dot(g_ref[:, L:], y_im, preferred_element_type=F32))
        z = gate * (y + bias_ref[o:o + 1, :] * z)
    o_ref[...] = z


def _hyena(u_hy, L, n_seq, row_block0, sw, bias, p, q, r):
    fwd, inv = (jnp.asarray(m).astype(BF16) for m in _dft_mats(L))
    full = lambda a: pl.BlockSpec(a.shape, lambda b: (0,) * a.ndim)
    return pl.pallas_call(
        functools.partial(_hyena_kernel, L=L),
        grid=(n_seq,),
        in_specs=[pl.BlockSpec((L, 3 * HY_CH), lambda b: (row_block0 + b, 0)),
                  full(sw), full(bias), full(p), full(q), full(r), full(fwd), full(inv)],
        out_specs=pl.BlockSpec((L, HY_CH), lambda b: (b, 0)),
        out_shape=jax.ShapeDtypeStruct((n_seq * L, HY_CH), F32),
        compiler_params=_cparams(("arbitrary",)),
        name=f"hyena_{L}",
    )(u_hy, sw, bias, p, q, r, fwd, inv)


def _cumsum_rows(x, n, reverse):
    row = lax.broadcasted_iota(jnp.int32, (n, 1), 0)
    s = 1
    while s < n:
        if reverse:
            x = x + jnp.where(row < n - s, pltpu.roll(x, n - s, 0), 0.0)
        else:
            x = x + jnp.where(row >= s, pltpu.roll(x, s, 0), 0.0)
        s *= 2
    return x


def _cumsum_lanes(x, n, reverse):
    col = lax.broadcasted_iota(jnp.int32, (1, n), 1)
    s = 1
    while s < n:
        if reverse:
            x = x + jnp.where(col < n - s, pltpu.roll(x, n - s, 1), 0.0)
        else:
            x = x + jnp.where(col >= s, pltpu.roll(x, s, 1), 0.0)
        s *= 2
    return x


def _mlstm_kernel(*refs, L, has_state):
    if has_state:
        (qk_ref, v_ref, o_ref, g_ref, gt_ref, sw_ref, gb_ref, gbt_ref, ng_ref, c0_ref, n0_ref, m0_ref,
         y_ref, cout_ref, nout_ref, mout_ref, q_s, k_s, vt_s, ht_s, ct_s, n_s, m_s) = refs
    else:
        (qk_ref, v_ref, o_ref, g_ref, gt_ref, sw_ref, gb_ref, gbt_ref, ng_ref,
         y_ref, cout_ref, nout_ref, mout_ref, q_s, k_s, vt_s, ht_s, ct_s, n_s, m_s) = refs
    tc = ML_CHUNK
    nc = L // tc
    nh = ML_HEADS
    dh = ML_DH

    qk = _silu(_conv3(qk_ref[...], sw_ref[...], L, L))
    q_s[...] = qk[:, :ML_W].astype(BF16)
    k_s[...] = (qk[:, ML_W:] * (dh ** -0.5)).astype(BF16)
    for c in range(nc):
        vt_s[c] = v_ref[c * tc:(c + 1) * tc, :].T.astype(BF16)

    for i in range(2 * nh):
        ct_s[i] = c0_ref[0, i].T if has_state else jnp.zeros((dh, dh), F32)
    n_s[...] = n0_ref[0] if has_state else jnp.zeros_like(n_s)
    m_s[...] = m0_ref[0] if has_state else jnp.zeros_like(m_s)

    si = lax.broadcasted_iota(jnp.int32, (tc, tc), 0)
    ti = lax.broadcasted_iota(jnp.int32, (tc, tc), 1)

    for d in range(2):
        rev = d == 1
        mask = (si >= ti) if rev else (si <= ti)
        edge = 0 if rev else tc - 1

        def chunk(j, carry, d=d, rev=rev, mask=mask, edge=edge):
            cidx = (nc - 1 - j) if rev else j
            r0 = pl.multiple_of(cidx * tc, tc)
            pre = g_ref[pl.ds(r0, tc), :] + gb_ref[...]
            pre_t = gt_ref[cidx] + gbt_ref[...]
            cum = _cumsum_rows(_log_sigmoid(pre), tc, rev)
            cum_t = _cumsum_lanes(_log_sigmoid(pre_t), tc, rev)
            key_all = cum - pltpu.roll(pre, 8, 1)
            for h in range(nh):
                col = d * nh + h
                hs = slice(h * dh, (h + 1) * dh)
                key = key_all[:, 8 + col:9 + col]
                b_row = cum_t[8 + col:9 + col, :]
                b_end = b_row[:, edge:edge + 1]
                m_prev = m_s[col:col + 1, 0:1]
                dmat = jnp.where(mask, b_row - key, -jnp.inf)
                inter = b_row + m_prev
                m_row = jnp.maximum(inter, jnp.max(dmat, axis=0, keepdims=True))
                w_intra = jnp.exp(dmat - m_row)
                w_state = jnp.exp(inter - m_row)
                qh = q_s[pl.ds(r0, tc), hs]
                kh = k_s[pl.ds(r0, tc), hs]
                vt = vt_s[cidx, hs, :]
                ct_prev = ct_s[col]
                n_prev = n_s[col:col + 1, :]
                s = lax.dot_general(kh, qh, _NT, preferred_element_type=F32) * w_intra
                num = (jnp.dot(vt, s.astype(BF16), preferred_element_type=F32)
                       + w_state * lax.dot_general(ct_prev.astype(BF16), qh, _NT, preferred_element_type=F32))
                qn = lax.dot_general(jnp.broadcast_to(n_prev, (8, dh)).astype(BF16), qh, _NT,
                                     preferred_element_type=F32)[0:1]
                den = jnp.sum(s, axis=0, keepdims=True) + w_state * qn
                hout = num * (1.0 / jnp.maximum(jnp.abs(den), jnp.exp(-m_row)))
                if d == 0:
                    ht_s[cidx, hs, :] = hout
                else:
                    ht_s[cidx, hs, :] += hout
                m_new = jnp.maximum(b_end + m_prev, b_end - jnp.min(key, axis=0, keepdims=True))
                wg = jnp.exp(b_end - key - m_new)
                decay = jnp.exp(b_end + m_prev - m_new)
                kw = kh.astype(F32) * wg
                ct_s[col] = decay * ct_prev + jnp.dot(vt, kw.astype(BF16), preferred_element_type=F32)
                n_s[col:col + 1, :] = decay * n_prev + jnp.sum(kw, axis=0, keepdims=True)
                m_s[col:col + 1, :] = jnp.broadcast_to(m_new, (1, LANES))
            return carry

        lax.fori_loop(0, nc, chunk, 0)

    for c in range(nc):
        for h in range(nh):
            hs = slice(h * dh, (h + 1) * dh)
            rs = slice(c * tc, (c + 1) * tc)
            ht = ht_s[c, hs, :]
            hn = ht * lax.rsqrt(jnp.mean(ht * ht, axis=0, keepdims=True) + EPS)
            y_ref[rs, hs] = jax.nn.sigmoid(o_ref[rs, hs]) * (hn.T * ng_ref[:, hs])
    for i in range(2 * nh):
        cout_ref[0, i] = ct_s[i].T
    nout_ref[0] = n_s[...]
    mout_ref[0] = m_s[...]


def _mlstm(qk, v, o, gates, gates_t, L, n_seq, row_block0, sw, gb, gbt, ng, state):
    nc = L // ML_CHUNK
    has_state = state is not None
    full = lambda a: pl.BlockSpec(a.shape, lambda b: (0,) * a.ndim)
    in_specs = [pl.BlockSpec((L, 2 * ML_W), lambda b: (row_block0 + b, 0)),
                pl.BlockSpec((L, ML_W), lambda b: (row_block0 + b, 0)),
                pl.BlockSpec((L, ML_W), lambda b: (row_block0 + b, 0)),
                pl.BlockSpec((L, GATE_PAD), lambda b: (row_block0 + b, 0)),
                pl.BlockSpec((nc, 16, ML_CHUNK), lambda b: (row_block0 + b, 0, 0)),
                full(sw), full(gb), full(gbt), full(ng)]
    args = [qk, v, o, gates, gates_t, sw, gb, gbt, ng]
    if has_state:
        c0, n0, m0 = state
        in_specs += [pl.BlockSpec((1, 2 * ML_HEADS, ML_DH, ML_DH), lambda b: (b, 0, 0, 0)),
                     pl.BlockSpec((1, 2 * ML_HEADS, ML_DH), lambda b: (b, 0, 0)),
                     pl.BlockSpec((1, 2 * ML_HEADS, LANES), lambda b: (b, 0, 0))]
        args += [c0, n0, m0]
    return pl.pallas_call(
        functools.partial(_mlstm_kernel, L=L, has_state=has_state),
        grid=(n_seq,),
        in_specs=in_specs,
        out_specs=[pl.BlockSpec((L, ML_W), lambda b: (b, 0)),
                   pl.BlockSpec((1, 2 * ML_HEADS, ML_DH, ML_DH), lambda b: (b, 0, 0, 0)),
                   pl.BlockSpec((1, 2 * ML_HEADS, ML_DH), lambda b: (b, 0, 0)),
                   pl.BlockSpec((1, 2 * ML_HEADS, LANES), lambda b: (b, 0, 0))],
        out_shape=[jax.ShapeDtypeStruct((n_seq * L, ML_W), F32),
                   jax.ShapeDtypeStruct((n_seq, 2 * ML_HEADS, ML_DH, ML_DH), F32),
                   jax.ShapeDtypeStruct((n_seq, 2 * ML_HEADS, ML_DH), F32),
                   jax.ShapeDtypeStruct((n_seq, 2 * ML_HEADS, LANES), F32)],
        scratch_shapes=[pltpu.VMEM((L, ML_W), BF16), pltpu.VMEM((L, ML_W), BF16),
                        pltpu.VMEM((nc, ML_W, ML_CHUNK), BF16),
                        pltpu.VMEM((nc, ML_W, ML_CHUNK), F32),
                        pltpu.VMEM((2 * ML_HEADS, ML_DH, ML_DH), F32),
                        pltpu.VMEM((2 * ML_HEADS, ML_DH), F32),
                        pltpu.VMEM((2 * ML_HEADS, LANES), F32)],
        compiler_params=_cparams(("arbitrary",)),
        name=f"mlstm_{L}",
    )(*args)


def _cmul(ar, ai, br, bi):
    return ar * br - ai * bi, ar * bi + ai * br


def _s5_kernel(*refs, segmented):
    if segmented:
        (u_ref, bb_ref, cc_ref, eb_ref, ec_ref, lam_ref, dsk_ref, wglu_ref, s0_ref, y_ref,
         sbuf, yacc, bmat, cmat, pw) = refs
    else:
        (u_ref, bb_ref, cc_ref, eb_ref, ec_ref, lam_ref, dsk_ref, wglu_ref, y_ref, fin_ref,
         sbuf, yacc, bmat, cmat) = refs
    n = S5_STATE
    rows = S5_ROWS
    steps = S5_STEPS
    blk = 256
    n_blk = steps * rows // blk
    n_seg = 4

    yacc[...] = u_ref[0] * dsk_ref[...]
    ub = u_ref[0].astype(BF16)
    b_keep = (lax.broadcasted_iota(jnp.int32, (S5_CH, 2 * n), 0) // S5_GROUP
              == (lax.broadcasted_iota(jnp.int32, (S5_CH, 2 * n), 1) % n) // S5_P)
    c_keep = ((lax.broadcasted_iota(jnp.int32, (2 * n, S5_CH), 0) % n) // S5_P
              == lax.broadcasted_iota(jnp.int32, (2 * n, S5_CH), 1) // S5_GROUP)

    for d in range(2):
        rev = d == 1
        bmat[...] = jnp.where(b_keep, jnp.dot(bb_ref[d].astype(BF16), eb_ref[...], preferred_element_type=F32),
                              0.0).astype(BF16)
        cmat[...] = jnp.where(c_keep, jnp.dot(cc_ref[d].astype(BF16), ec_ref[...], preferred_element_type=F32),
                              0.0).astype(BF16)
        for i in range(n_blk):
            sbuf[i * blk:(i + 1) * blk, :] = jnp.dot(ub[i * blk:(i + 1) * blk], bmat[...],
                                                     preferred_element_type=F32)
        lam = lam_ref[d]
        lr = jnp.broadcast_to(lam[:, :n], (rows, n))
        li = jnp.broadcast_to(lam[:, n:], (rows, n))

        def step(i, carry, rev=rev, lr=lr, li=li):
            sr, si = carry
            t = (steps - 1 - i) if rev else i
            off = pl.multiple_of(t * rows, rows)
            pr, pi = _cmul(lr, li, sr, si)
            nr = pr + sbuf[pl.ds(off, rows), 0:n]
            ni = pi + sbuf[pl.ds(off, rows), n:2 * n]
            sbuf[pl.ds(off, rows), 0:n] = nr
            sbuf[pl.ds(off, rows), n:2 * n] = ni
            return nr, ni

        zero = jnp.zeros((rows, n), F32)
        sr, si = lax.fori_loop(0, steps, step, (zero, zero), unroll=2)

        if not segmented:
            fin_ref[0, d, :, 0:n] = sr
            fin_ref[0, d, :, n:2 * n] = si
        else:
            lam_r, lam_i = lam[:, :n], lam[:, n:]
            row8 = lax.broadcasted_iota(jnp.int32, (rows, 1), 0)
            cr, ci = lam_r, lam_i
            acc_r = jnp.broadcast_to(cr, (rows, n))
            acc_i = jnp.broadcast_to(ci, (rows, n))
            for j in range(1, rows):
                cr, ci = _cmul(cr, ci, lam_r, lam_i)
                acc_r = jnp.where(row8 >= j, jnp.broadcast_to(cr, (rows, n)), acc_r)
                acc_i = jnp.where(row8 >= j, jnp.broadcast_to(ci, (rows, n)), acc_i)
            pw[0:rows, 0:n] = acc_r
            pw[0:rows, n:2 * n] = acc_i
            size = rows
            while size < steps:
                tr = pw[size - 1:size, 0:n]
                ti = pw[size - 1:size, n:2 * n]
                xr, xi = _cmul(pw[0:size, 0:n], pw[0:size, n:2 * n], tr, ti)
                pw[size:2 * size, 0:n] = xr
                pw[size:2 * size, n:2 * n] = xi
                size *= 2
            end_off = 0 if rev else (steps - 1) * rows
            loc_r = sbuf[end_off:end_off + rows, 0:n]
            loc_i = sbuf[end_off:end_off + rows, n:2 * n]
            pl_r = pw[steps - 1:steps, 0:n]
            pl_i = pw[steps - 1:steps, n:2 * n]
            s0r = s0_ref[d, :, 0:n]
            s0i = s0_ref[d, :, n:2 * n]
            seg = row8 // 2
            first = (seg == n_seg - 1) if rev else (seg == 0)
            shift = (rows - 2) if rev else 2
            cin_r, cin_i = s0r, s0i
            for _ in range(n_seg - 1):
                fr, fi = _cmul(jnp.broadcast_to(pl_r, (rows, n)), jnp.broadcast_to(pl_i, (rows, n)), cin_r, cin_i)
                tru_r = loc_r + fr
                tru_i = loc_i + fi
                cin_r = jnp.where(first, s0r, pltpu.roll(tru_r, shift, 0))
                cin_i = jnp.where(first, s0i, pltpu.roll(tru_i, shift, 0))

            def fix(tb, carry, rev=rev, cin_r=cin_r, cin_i=cin_i):
                pb = (steps // rows - 1 - tb) if rev else tb
                poff = pl.multiple_of(pb * rows, rows)
                p_r = pw[pl.ds(poff, rows), 0:n]
                p_i = pw[pl.ds(poff, rows), n:2 * n]
                for j in range(rows):
                    jj = rows - 1 - j if rev else j
                    off = pl.multiple_of((tb * rows + j) * rows, rows)
                    fr, fi = _cmul(jnp.broadcast_to(p_r[jj:jj + 1], (rows, n)),
                                   jnp.broadcast_to(p_i[jj:jj + 1], (rows, n)), cin_r, cin_i)
                    sbuf[pl.ds(off, rows), 0:n] += fr
                    sbuf[pl.ds(off, rows), n:2 * n] += fi
                return carry

            lax.fori_loop(0, steps // rows, fix, 0)

        for i in range(n_blk):
            yacc[i * blk:(i + 1) * blk, :] += jnp.dot(sbuf[i * blk:(i + 1) * blk, :].astype(BF16), cmat[...],
                                                      preferred_element_type=F32)

    g = jax.nn.gelu(yacc[...], approximate=True)
    y_ref[0] = g * jax.nn.sigmoid(_bdot(g, wglu_ref[...]))


@functools.lru_cache(None)
def _s5_spread():
    eb = np.zeros((2 * S5_P, 2 * S5_STATE), np.float32)
    for half in range(2):
        for g in range(S5_G):
            c0 = half * S5_STATE + g * S5_P
            eb[half * S5_P:(half + 1) * S5_P, c0:c0 + S5_P] = np.eye(S5_P)
    ec = np.zeros((LANES, S5_CH), np.float32)
    for g in range(S5_G):
        ec[:S5_GROUP, g * S5_GROUP:(g + 1) * S5_GROUP] = np.eye(S5_GROUP)
    return eb, ec


def _s5(u_tm, bb, cc, lam, dskip, wglu, s0):
    n_grp = u_tm.shape[0]
    n_rows = S5_STEPS * S5_ROWS
    segmented = s0 is not None
    eb, ec = (jnp.asarray(m).astype(BF16) for m in _s5_spread())
    full = lambda a: pl.BlockSpec(a.shape, lambda g: (0,) * a.ndim)
    in_specs = [pl.BlockSpec((1, n_rows, S5_CH), lambda g: (g, 0, 0)),
                full(bb), full(cc), full(eb), full(ec), full(lam), full(dskip), full(wglu)]
    args = [u_tm, bb, cc, eb, ec, lam, dskip, wglu]
    out_specs = [pl.BlockSpec((1, n_rows, S5_CH), lambda g: (g, 0, 0))]
    out_shape = [jax.ShapeDtypeStruct((n_grp, n_rows, S5_CH), F32)]
    scratch = [pltpu.VMEM((n_rows, 2 * S5_STATE), F32), pltpu.VMEM((n_rows, S5_CH), F32),
               pltpu.VMEM((S5_CH, 2 * S5_STATE), BF16), pltpu.VMEM((2 * S5_STATE, S5_CH), BF16)]
    if segmented:
        in_specs.append(full(s0))
        args.append(s0)
        scratch.append(pltpu.VMEM((S5_STEPS, 2 * S5_STATE), F32))
    else:
        out_specs.append(pl.BlockSpec((1, 2, S5_ROWS, 2 * S5_STATE), lambda g: (g, 0, 0, 0)))
        out_shape.append(jax.ShapeDtypeStruct((n_grp, 2, S5_ROWS, 2 * S5_STATE), F32))
    return pl.pallas_call(
        functools.partial(_s5_kernel, segmented=segmented),
        grid=(n_grp,),
        in_specs=in_specs,
        out_specs=out_specs,
        out_shape=out_shape,
        scratch_shapes=scratch,
        compiler_params=_cparams(("arbitrary",)),
        name="s5_seg" if segmented else "s5_ctx",
    )(*args)


def _s5_params(a_re, a_im, log_dt, b_re, b_im, c_re, c_im):
    dt = jnp.exp(log_dt)[:, :, None]
    mag = jnp.exp(a_re * dt)
    lb_re = mag * jnp.cos(a_im * dt)
    lb_im = mag * jnp.sin(a_im * dt)
    den = a_re * a_re + a_im * a_im
    nr, ni = lb_re - 1.0, lb_im
    k_re = (nr * a_re + ni * a_im) / den
    k_im = (ni * a_re - nr * a_im) / den
    bb_re = k_re[..., None] * b_re - k_im[..., None] * b_im
    bb_im = k_re[..., None] * b_im + k_im[..., None] * b_re
    to_gc_p = lambda m: m.transpose(0, 1, 3, 2).reshape(2, S5_CH, S5_P)
    bb = jnp.concatenate([to_gc_p(bb_re), to_gc_p(bb_im)], axis=2)
    to_gp_c = lambda m: m.transpose(0, 1, 3, 2).reshape(2, S5_STATE, S5_GROUP)
    cc = jnp.concatenate([to_gp_c(c_re), -to_gp_c(c_im)], axis=1)
    cc = jnp.pad(cc, ((0, 0), (0, 0), (0, LANES - S5_GROUP)))
    lam = jnp.concatenate([lb_re.reshape(2, 1, S5_STATE), lb_im.reshape(2, 1, S5_STATE)], axis=2)
    return bb, cc, lam


def _outproj_kernel(*refs, n_x):
    x_refs = refs[:n_x]
    (hyc_ref, hyl_ref, mlc_ref, mll_ref, s5c_ref, s5l_ref, w_ref, mod_ref, g_ref,
     rw_ref, rb_ref, xn_ref, h2e_ref, best_ref, rank_ref, cnt_ref, wb, cnt_s) = refs[n_x:]
    step = pl.program_id(0)

    @pl.when(step == 0)
    def _():
        wb[...] = w_ref[0].astype(BF16)
        cnt_s[...] = jnp.zeros_like(cnt_s)

    is_ctx = step < T_CTX // TM
    pick = lambda c_ref, l_ref: jnp.where(is_ctx, c_ref[...], l_ref[...]).astype(BF16)
    mod = mod_ref[0]
    a, b = HY_CH, HY_CH + ML_W
    mix = (jnp.dot(pick(hyc_ref, hyl_ref), wb[0:a, :], preferred_element_type=F32)
           + jnp.dot(pick(mlc_ref, mll_ref), wb[a:b, :], preferred_element_type=F32)
           + jnp.dot(pick(s5c_ref, s5l_ref), wb[b:, :], preferred_element_type=F32))
    xn = _x_tile(step, x_refs) + mod[2:3] * mix
    xn_ref[...] = xn
    h2 = _rms(xn, g_ref[...]) * (1.0 + mod[4:5]) + mod[3:4]
    h2e_ref[:, 0:D] = h2
    logits = lax.dot_general(rw_ref[...], h2, (((1,), (1,)), ((), ())),
                             precision=HIGHEST, preferred_element_type=F32)
    ex = jnp.exp(logits - jnp.max(logits, axis=0, keepdims=True))
    probs = ex / jnp.sum(ex, axis=0, keepdims=True)
    sel = probs + rb_ref[...]
    best = None
    best_score = None
    for g in range(N_GROUPS):
        r = [sel[g * GROUP_SIZE + i:g * GROUP_SIZE + i + 1, :] for i in range(GROUP_SIZE)]
        score = None
        for i in range(GROUP_SIZE):
            for j in range(i + 1, GROUP_SIZE):
                pair = r[i] + r[j]
                score = pair if score is None else jnp.maximum(score, pair)
        if g == 0:
            best, best_score = jnp.zeros_like(score, dtype=jnp.int32), score
        else:
            upd = score > best_score
            best = jnp.where(upd, g, best)
            best_score = jnp.where(upd, score, best_score)
    eid = lax.broadcasted_iota(jnp.int32, (N_EXPERTS, 1), 0)
    masked = jnp.where(eid // GROUP_SIZE == best, sel, -jnp.inf)
    m1 = jnp.max(masked, axis=0, keepdims=True)
    i1 = jnp.min(jnp.where(masked == m1, eid, N_EXPERTS), axis=0, keepdims=True)
    masked2 = jnp.where(eid == i1, -jnp.inf, masked)
    m2 = jnp.max(masked2, axis=0, keepdims=True)
    i2 = jnp.min(jnp.where(masked2 == m2, eid, N_EXPERTS), axis=0, keepdims=True)
    p1 = jnp.sum(jnp.where(eid == i1, probs, 0.0), axis=0, keepdims=True)
    p2 = jnp.sum(jnp.where(eid == i2, probs, 0.0), axis=0, keepdims=True)
    tot = p1 + p2
    comb = jnp.where(eid == i1, p1 / tot, 0.0) + jnp.where(eid == i2, p2 / tot, 0.0)
    comb = jnp.concatenate([comb, jnp.zeros((LANES - N_EXPERTS, comb.shape[1]), F32)], axis=0)
    h2e_ref[:, D:] = comb.T
    best_ref[...] = best
    gid = lax.broadcasted_iota(jnp.int32, (8, 1), 0)
    onehot = (gid == best).astype(F32)
    cum = _cumsum_lanes(onehot, TM, False)
    run = cnt_s[:, 0:1]
    rank_ref[...] = jnp.sum(onehot * (cum - onehot + run), axis=0, keepdims=True).astype(jnp.int32)
    cnt_s[...] = jnp.broadcast_to(run + cum[:, TM - 1:TM], cnt_s.shape)
    cnt_ref[...] = cnt_s[...]


def _outproj(x, y_hy, y_ml, y_s5, w_out, l, mod, g2, rw_t, rb):
    full = lambda a: pl.BlockSpec(a.shape, lambda i: (0,) * a.ndim)
    n_ctx = T_CTX // TM
    ctx = lambda w: pl.BlockSpec((TM, w), lambda i: (jnp.minimum(i, n_ctx - 1), 0))
    lat = lambda w: pl.BlockSpec((TM, w), lambda i: (jnp.maximum(i - n_ctx, 0), 0))
    tok = lambda w: pl.BlockSpec((TM, w), lambda i: (i, 0))
    row = pl.BlockSpec((1, TM), lambda i: (0, i))
    return pl.pallas_call(
        functools.partial(_outproj_kernel, n_x=len(x)),
        grid=(T_ALL // TM,),
        in_specs=_x_specs(x) + [ctx(HY_CH), lat(HY_CH), ctx(ML_W), lat(ML_W), ctx(S5_CH), lat(S5_CH),
                  pl.BlockSpec((1, D, D), lambda i: (l, 0, 0), pipeline_mode=pl.Buffered(1)),
                  pl.BlockSpec((1, 6, D), lambda i: (_mod_row(i, TM), 0, 0)),
                  full(g2), full(rw_t), full(rb)],
        out_specs=[tok(D), tok(D + LANES), row, row, pl.BlockSpec((8, LANES), lambda i: (0, 0))],
        out_shape=[jax.ShapeDtypeStruct((T_ALL, D), F32),
                   jax.ShapeDtypeStruct((T_ALL, D + LANES), F32),
                   jax.ShapeDtypeStruct((1, T_ALL), jnp.int32),
                   jax.ShapeDtypeStruct((1, T_ALL), jnp.int32),
                   jax.ShapeDtypeStruct((8, LANES), F32)],
        scratch_shapes=[pltpu.VMEM((D, D), BF16), pltpu.VMEM((8, LANES), F32)],
        compiler_params=_cparams(("arbitrary",)),
        name="outproj_router",
    )(*x, *y_hy, *y_ml, *y_s5, w_out, mod, g2, rw_t, rb)


def _gather_rows(idx_ref, idx_base, src_ref, dst_ref, n_rows):
    def body(r8, carry):
        base = pl.multiple_of(r8 * 8, 8)
        for k in range(8):
            idx = idx_ref[idx_base + base + k]
            dst_ref[pl.ds(base + k, 1), :] = src_ref[pl.ds(idx, 1), :]
        return carry

    lax.fori_loop(0, n_rows // 8, body, 0)


def _dispatch(best, rank, cnt):
    tm = TM_MOE
    g = best.reshape(T_ALL)
    cnt = cnt[:N_GROUPS, 0].astype(jnp.int32)
    n_tile_g = (cnt + tm - 1) // tm
    tile_end = jnp.cumsum(n_tile_g)
    row_off = (tile_end - n_tile_g) * tm
    pos = rank.reshape(T_ALL)
    for k in range(N_GROUPS):
        pos = pos + jnp.where(g == k, row_off[k], 0)
    src = jnp.zeros((MOE_SLOTS,), jnp.int32).at[pos].set(jnp.arange(T_ALL, dtype=jnp.int32))
    tiles = jnp.arange(MOE_SLOTS // tm, dtype=jnp.int32)
    tile_group = jnp.minimum(jnp.sum((tiles[:, None] >= tile_end[None, :]).astype(jnp.int32), axis=1), N_GROUPS - 1)
    return pos, src, tile_group.astype(jnp.int32), tile_end[N_GROUPS - 1:].astype(jnp.int32)


def _moe_kernel(src_ref, tg_ref, nt_ref, h_ref, wg_ref, wu_ref, wd_ref, ys_ref, gbuf, xs, acc):
    i = pl.program_id(0)
    j = pl.program_id(1)
    tm = TM_MOE

    @pl.when(i < nt_ref[0])
    def _():
        @pl.when(j == 0)
        def _():
            _gather_rows(src_ref, i * tm, h_ref, gbuf, tm)
            xs[...] = gbuf[:, 0:D].astype(BF16)

        e = tg_ref[i] * GROUP_SIZE + j
        lane = lax.broadcasted_iota(jnp.int32, (1, LANES), 1)
        wg = wg_ref[0, 0].astype(BF16)
        wu = wu_ref[0, 0].astype(BF16)
        wd = wd_ref[0, 0].astype(BF16)
        sub = 256
        for s in range(tm // sub):
            rs = slice(s * sub, (s + 1) * sub)
            hb = xs[rs, :]
            ce = jnp.sum(jnp.where(lane == e, gbuf[rs, D:], 0.0), axis=1, keepdims=True)
            hid = _silu(jnp.dot(hb, wg, preferred_element_type=F32)) * jnp.dot(hb, wu, preferred_element_type=F32)
            part = jnp.dot((hid * ce).astype(BF16), wd, preferred_element_type=F32)

            @pl.when(j == 0)
            def _():
                acc[rs, :] = part

            @pl.when(j > 0)
            def _():
                acc[rs, :] += part

        @pl.when(j == GROUP_SIZE - 1)
        def _():
            ys_ref[...] = acc[...]


def _moe(h2e, src, tile_group, n_tiles, wg, wu, wd, l):
    tm = TM_MOE

    def w_map(i, j, src, tg, nt):
        live = i < nt[0]
        ii = jnp.minimum(i, nt[0] - 1)
        return (l, tg[ii] * GROUP_SIZE + jnp.where(live, j, GROUP_SIZE - 1), 0, 0)

    return pl.pallas_call(
        _moe_kernel,
        grid_spec=pltpu.PrefetchScalarGridSpec(
            num_scalar_prefetch=3,
            grid=(MOE_SLOTS // tm, GROUP_SIZE),
            in_specs=[pl.BlockSpec(memory_space=pltpu.VMEM),
                      pl.BlockSpec((1, 1, D, D_EXPERT), w_map),
                      pl.BlockSpec((1, 1, D, D_EXPERT), w_map),
                      pl.BlockSpec((1, 1, D_EXPERT, D), w_map)],
            out_specs=pl.BlockSpec((tm, D), lambda i, j, src, tg, nt: (jnp.minimum(i, nt[0] - 1), 0)),
            scratch_shapes=[pltpu.VMEM((tm, D + LANES), F32), pltpu.VMEM((tm, D), BF16),
                            pltpu.VMEM((tm, D), F32)]),
        out_shape=jax.ShapeDtypeStruct((MOE_SLOTS, D), F32),
        compiler_params=_cparams(("arbitrary", "arbitrary")),
        name="moe_experts",
    )(src, tile_group, n_tiles, h2e, wg, wu, wd)


def _combine_kernel(pos_ref, ys_ref, xn_ref, mod_ref, fg_ref, *rest, final):
    step = pl.program_id(0)
    if final:
        yc_ref, yl_ref, gbuf = rest
    else:
        out_ref, gbuf = rest
    _gather_rows(pos_ref, step * TM, ys_ref, gbuf, TM)
    out = xn_ref[...] + mod_ref[0][5:6] * gbuf[...]
    if final:
        y = _rms(out, fg_ref[...])

        @pl.when(step < T_CTX // TM)
        def _():
            yc_ref[...] = y

        @pl.when(step >= T_CTX // TM)
        def _():
            yl_ref[...] = y
    else:
        out_ref[...] = out


def _combine(pos, ys, xn, mod, fg, final):
    spec = pl.BlockSpec((TM, D), lambda i, pos: (i, 0))
    n_ctx = T_CTX // TM
    if final:
        out_specs = [pl.BlockSpec((TM, D), lambda i, pos: (jnp.minimum(i, n_ctx - 1), 0)),
                     pl.BlockSpec((TM, D), lambda i, pos: (jnp.maximum(i - n_ctx, 0), 0))]
        out_shape = [jax.ShapeDtypeStruct((T_CTX, D), F32), jax.ShapeDtypeStruct((T_LAT, D), F32)]
    else:
        out_specs = [spec]
        out_shape = [jax.ShapeDtypeStruct((T_ALL, D), F32)]
    return pl.pallas_call(
        functools.partial(_combine_kernel, final=final),
        grid_spec=pltpu.PrefetchScalarGridSpec(
            num_scalar_prefetch=1,
            grid=(T_ALL // TM,),
            in_specs=[pl.BlockSpec(memory_space=pltpu.VMEM),
                      spec,
                      pl.BlockSpec((1, 6, D), lambda i, pos: (_mod_row(i, TM), 0, 0)),
                      pl.BlockSpec((1, D), lambda i, pos: (0, 0))],
            out_specs=out_specs,
            scratch_shapes=[pltpu.VMEM((TM, D), F32)]),
        out_shape=out_shape,
        compiler_params=_cparams(("arbitrary",)),
        name="moe_combine",
    )(pos, ys, xn, mod, fg)


@functools.lru_cache(None)
def _pos_embed():
    rows = L_LAT // GRID_W
    r = np.repeat(np.arange(rows, dtype=np.float64), GRID_W)
    col = np.tile(np.arange(GRID_W, dtype=np.float64), rows)
    quarter = D // 4
    freq = np.exp(-math.log(POS_BASE) * np.arange(quarter, dtype=np.float64) / quarter)
    ar = r[:, None] * freq[None]
    ac = col[:, None] * freq[None]
    emb = np.concatenate([np.sin(ar), np.cos(ar), np.sin(ac), np.cos(ac)], axis=-1)
    return emb.astype(np.float32)


def _to_time_major_ctx(a):
    c = a.shape[-1]
    a = a.reshape(2, S5_ROWS, L_CTX, c).transpose(0, 2, 1, 3)
    return a.reshape(2, L_CTX * S5_ROWS, c)


def _from_time_major_ctx(a):
    c = a.shape[-1]
    a = a.reshape(2, L_CTX, S5_ROWS, c).transpose(0, 2, 1, 3)
    return a.reshape(T_CTX, c)


def _to_time_major_lat(a):
    c = a.shape[-1]
    a = a.reshape(N_LAT_SEQ, 4, S5_STEPS, c).transpose(2, 1, 0, 3)
    return a.reshape(1, S5_STEPS * S5_ROWS, c)


def _from_time_major_lat(a):
    c = a.shape[-1]
    a = a.reshape(S5_STEPS, 4, N_LAT_SEQ, c).transpose(2, 1, 0, 3)
    return a.reshape(T_LAT, c)


def kernel(x_prompt, x_sample, c, state_mlstm_C, state_mlstm_n, state_mlstm_m, state_s5_re, state_s5_im, c_ctx, w_ada, b_ada, norm1_g, norm2_g, final_g, w_in, w_out, hy_short, hy_fw1, hy_fb1, hy_fw2, hy_fb2, hy_fw3, hy_log_decay, hy_bias, ml_short, ml_gate_bias, ml_norm_g, s5_a_re, s5_a_im, s5_log_dt, s5_b_re, s5_b_im, s5_c_re, s5_c_im, s5_d, s5_w_glu, router_w, router_b, moe_w_gate, moe_w_up, moe_w_down):
    x = (x_prompt.reshape(T_CTX, D), x_sample.reshape(T_LAT, D), jnp.asarray(_pos_embed()))
    w_in_t = jnp.swapaxes(w_in, 1, 2)
    cc =jnp.concatenate([c_ctx[None], c, jnp.zeros((8 - 1 - N_LAT_SEQ, D), F32)], axis=0)
    mod_all = _ada(cc, w_ada, b_ada).reshape(DEPTH, 8, 6, D)
    rw_t = router_w.T
    rb = router_b.reshape(N_EXPERTS, 1)
    fg = final_g.reshape(1, D)
    lat_blk = T_CTX // L_LAT

    new_c, new_n, new_m, new_re, new_im = [], [], [], [], []
    y_prompt = y_sample = None
    for l in range(DEPTH):
        mod = mod_all[l]
        u_hy, qk, v, o, u_s5, gates = _inproj(x, norm1_g[l], mod, w_in_t, l)

        w1p = jnp.zeros((LANES, HY_FILTER_W), F32).at[:HY_EMB].set(hy_fw1[l])
        b1 = hy_fb1[l].reshape(1, HY_FILTER_W)
        b2 = hy_fb2[l].reshape(1, HY_FILTER_W)
        ld = hy_log_decay[l].reshape(1, 4 * HY_CH)
        y_hy = []
        for L, n_seq, blk0 in ((L_CTX, N_CTX_SEQ, 0), (L_LAT, N_LAT_SEQ, lat_blk)):
            p, q, r = _hy_filter(L, w1p, b1, hy_fw2[l], b2, hy_fw3[l], ld)
            y_hy.append(_hyena(u_hy, L, n_seq, blk0, hy_short[l], hy_bias[l], p, q, r))

        gates_t = gates[:, :16].reshape(T_ALL // ML_CHUNK, ML_CHUNK, 16).transpose(0, 2, 1)
        gb = jnp.zeros((1, GATE_PAD), F32).at[0, :16].set(ml_gate_bias[l].reshape(16))
        gbt = ml_gate_bias[l].reshape(16, 1)
        ng = ml_norm_g[l].reshape(1, ML_W)
        m0 = jnp.broadcast_to(state_mlstm_m[:, l].reshape(N_LAT_SEQ, 2 * ML_HEADS, 1), (N_LAT_SEQ, 2 * ML_HEADS, LANES))
        yc, cc_, nc_, mc_ = _mlstm(qk, v, o, gates, gates_t, L_CTX, N_CTX_SEQ, 0, ml_short[l], gb, gbt, ng, None)
        yl, _, _, _ = _mlstm(qk, v, o, gates, gates_t, L_LAT, N_LAT_SEQ, lat_blk, ml_short[l], gb, gbt, ng,
                             (state_mlstm_C[:, l].reshape(N_LAT_SEQ, 2 * ML_HEADS, ML_DH, ML_DH),
                              state_mlstm_n[:, l].reshape(N_LAT_SEQ, 2 * ML_HEADS, ML_DH), m0))
        y_ml = (yc, yl)
        new_c.append(cc_.reshape(N_CTX_SEQ, 2, ML_HEADS, ML_DH, ML_DH))
        new_n.append(nc_.reshape(N_CTX_SEQ, 2, ML_HEADS, ML_DH))
        new_m.append(mc_[:, :, 0].reshape(N_CTX_SEQ, 2, ML_HEADS))

        bb, cc_s5, lam = _s5_params(s5_a_re[l], s5_a_im[l], s5_log_dt[l], s5_b_re[l], s5_b_im[l],
                                    s5_c_re[l], s5_c_im[l])
        dsk = s5_d[l].reshape(1, S5_CH)
        wglu = s5_w_glu[l].astype(BF16)
        ys_c, fin = _s5(_to_time_major_ctx(u_s5[:T_CTX]), bb, cc_s5, lam, dsk, wglu, None)
        s0 = jnp.concatenate([state_s5_re[:, l].reshape(N_LAT_SEQ, 2, S5_STATE),
                              state_s5_im[:, l].reshape(N_LAT_SEQ, 2, S5_STATE)], axis=-1)
        s0 = jnp.tile(s0.transpose(1, 0, 2), (1, 4, 1))
        (ys_l,) = _s5(_to_time_major_lat(u_s5[T_CTX:]), bb, cc_s5, lam, dsk, wglu, s0)
        y_s5 = (_from_time_major_ctx(ys_c), _from_time_major_lat(ys_l))
        fin = fin.transpose(0, 2, 1, 3).reshape(N_CTX_SEQ, 2, 2 * S5_STATE)
        new_re.append(fin[..., :S5_STATE].reshape(N_CTX_SEQ, 2, S5_G, S5_P))
        new_im.append(fin[..., S5_STATE:].reshape(N_CTX_SEQ, 2, S5_G, S5_P))

        xn, h2e, best, rank, cnt = _outproj(x, y_hy, y_ml, y_s5, w_out, l, mod,
                                            norm2_g[l].reshape(1, D), rw_t, rb)
        pos, src, tile_group, n_tiles = _dispatch(best, rank, cnt)
        ys = _moe(h2e, src, tile_group, n_tiles, moe_w_gate, moe_w_up, moe_w_down, l)
        res = _combine(pos, ys, xn, mod, fg, l == DEPTH - 1)
        if l == DEPTH - 1:
            y_prompt = res[0].reshape(N_CTX_SEQ, L_CTX, D)
            y_sample = res[1].reshape(N_LAT_SEQ, L_LAT, D)
        else:
            x = (res[0],)

    return (y_prompt, y_sample, jnp.stack(new_c, axis=1), jnp.stack(new_n, axis=1), jnp.stack(new_m, axis=1),
            jnp.stack(new_re, axis=1), jnp.stack(new_im, axis=1))
```

```python
import functools
import math

import numpy as np
import jax
import jax.numpy as jnp
from jax import lax
from jax.experimental import pallas as pl
from jax.experimental.pallas import tpu as pltpu

F32 = jnp.float32
BF16 = jnp.bfloat16
HIGHEST = lax.Precision.HIGHEST

D = 1024
N_CTX_SEQ, L_CTX = 16, 256
N_LAT_SEQ, L_LAT = 2, 1024
T_CTX = N_CTX_SEQ * L_CTX
T_LAT = N_LAT_SEQ * L_LAT
T_ALL = T_CTX + T_LAT
DEPTH = 2
EPS = 1e-6
GRID_W = 64
POS_BASE = 10000.0
HY_CH = 256
HY_EMB = 33
HY_FILTER_W = 64
ML_HEADS = 4
ML_DH = 128
ML_W = ML_HEADS * ML_DH
S5_CH = 256
S5_G = 16
S5_GROUP = 16
S5_P = 64
S5_STATE = S5_G * S5_P
N_EXPERTS = 16
N_GROUPS = 4
GROUP_SIZE = N_EXPERTS // N_GROUPS
D_EXPERT = 512
OFF_HY = 0
OFF_QK = 3 * HY_CH
OFF_V = OFF_QK + 2 * ML_W
OFF_O = OFF_V + ML_W
OFF_G = OFF_O + ML_W
OFF_S5 = OFF_G + 16
IN_W = OFF_S5 + S5_CH
LANES = 128
GATE_PAD = LANES

TM = 512
TM_MOE = 512
MOE_SLOTS = T_ALL + N_GROUPS * TM_MOE
MOE_SUB = 512
ML_CHUNK = 256
S5_ROWS = 8
S5_STEPS = 256
VMEM_LIMIT = 56 * 1024 * 1024


def _cparams(sem, vmem=VMEM_LIMIT):
    if sem is None:
        return pltpu.CompilerParams(vmem_limit_bytes=vmem)
    return pltpu.CompilerParams(dimension_semantics=sem, vmem_limit_bytes=vmem)


def _bdot(a, b):
    return jnp.dot(a.astype(BF16), b.astype(BF16), preferred_element_type=F32)


def _split_bf16(x):
    hi = x.astype(BF16)
    return hi, (x - hi.astype(F32)).astype(BF16)


def _silu(x):
    return x * jax.nn.sigmoid(x)


def _rms(x, g):
    return x * lax.rsqrt(jnp.mean(x * x, axis=-1, keepdims=True) + EPS) * g


def _log_sigmoid(x):
    return jnp.minimum(x, 0.0) - jnp.log1p(jnp.exp(-jnp.abs(x)))


def _conv3(u, w, n_rows, seq_len):
    row = lax.broadcasted_iota(jnp.int32, (n_rows, 1), 0) % seq_len
    prev = jnp.where(row == 0, 0.0, pltpu.roll(u, 1, 0))
    nxt = jnp.where(row == seq_len - 1, 0.0, pltpu.roll(u, n_rows - 1, 0))
    return prev * w[0:1] + u * w[1:2] + nxt * w[2:3]


def _ada_kernel(c_ref, w_ref, b_ref, o_ref):
    o_ref[0] = _bdot(_silu(c_ref[...]), w_ref[0]) + b_ref[0]


def _ada(cc, w_ada, b_ada):
    tn = 1536
    return pl.pallas_call(
        _ada_kernel,
        grid=(DEPTH, 6 * D // tn),
        in_specs=[pl.BlockSpec((8, D), lambda l, j: (0, 0)),
                  pl.BlockSpec((1, D, tn), lambda l, j: (l, 0, j)),
                  pl.BlockSpec((1, 1, tn), lambda l, j: (l, 0, j))],
        out_specs=pl.BlockSpec((1, 8, tn), lambda l, j: (l, 0, j)),
        out_shape=jax.ShapeDtypeStruct((DEPTH, 8, 6 * D), F32),
        compiler_params=_cparams(("arbitrary", "arbitrary")),
        name="ada_mod",
    )(cc, w_ada, b_ada.reshape(DEPTH, 1, 6 * D))


def _mod_row(i, tm):
    n_ctx = T_CTX // tm
    return jnp.where(i < n_ctx, 0, 1 + (i - n_ctx) // (L_LAT // tm))


_SEG = ((OFF_HY, OFF_QK - OFF_HY), (OFF_QK, OFF_V - OFF_QK), (OFF_V, OFF_O - OFF_V), (OFF_O, OFF_G - OFF_O))
TAIL_W = IN_W - OFF_G


N_CTX_TILES = T_CTX // TM
_NT = (((1,), (1,)), ((), ()))


def _x_specs(x):
    if len(x) == 1:
        return [pl.BlockSpec((TM, D), lambda i, *_: (i, 0))]
    per_seq = L_LAT // TM
    return [pl.BlockSpec((TM, D), lambda i, *_: (jnp.minimum(i, N_CTX_TILES - 1), 0)),
            pl.BlockSpec((TM, D), lambda i, *_: (jnp.maximum(i - N_CTX_TILES, 0), 0)),
            pl.BlockSpec((TM, D), lambda i, *_: (jnp.maximum(i - N_CTX_TILES, 0) % per_seq, 0))]


def _x_tile(step, x_refs):
    if len(x_refs) == 1:
        return x_refs[0][...]
    xc_ref, xl_ref, pos_ref = x_refs
    return jnp.where(step < N_CTX_TILES, xc_ref[...], xl_ref[...] + pos_ref[...])


def _inproj_kernel(*refs, n_x):
    x_refs = refs[:n_x]
    g_ref, mod_ref, w_ref, hy_ref, qk_ref, v_ref, o_ref, s5_ref, gt_ref, wb = refs[n_x:]
    step = pl.program_id(0)

    @pl.when(step == 0)
    def _():
        wb[...] = w_ref[0].astype(BF16)

    mod = mod_ref[0]
    h = _rms(_x_tile(step, x_refs), g_ref[...]) * (1.0 + mod[1:2]) + mod[0:1]
    hb = h.astype(BF16)
    for (a, w), ref in zip(_SEG, (hy_ref, qk_ref, v_ref, o_ref)):
        ref[...] = lax.dot_general(hb, wb[a:a + w, :], _NT, preferred_element_type=F32)
    tail = lax.dot_general(hb, wb[OFF_G:IN_W, :], _NT, preferred_element_type=F32)
    gt_ref[...] = tail[:, 0:GATE_PAD]
    s5_ref[...] = tail[:, OFF_S5 - OFF_G:TAIL_W]


def _inproj(x, g, mod, w_in_t, l):
    widths = [w for _, w in _SEG] + [S5_CH, GATE_PAD]
    return pl.pallas_call(
        functools.partial(_inproj_kernel, n_x=len(x)),
        grid=(T_ALL // TM,),
        in_specs=_x_specs(x) + [
            pl.BlockSpec((1, D), lambda i: (0, 0)),
            pl.BlockSpec((1, 6, D), lambda i: (_mod_row(i, TM), 0, 0)),
            pl.BlockSpec((1, IN_W, D), lambda i: (l, 0, 0), pipeline_mode=pl.Buffered(1))],
        out_specs=[pl.BlockSpec((TM, w), lambda i: (i, 0)) for w in widths],
        out_shape=[jax.ShapeDtypeStruct((T_ALL, w), F32) for w in widths],
        scratch_shapes=[pltpu.VMEM((IN_W, D), BF16)],
        compiler_params=_cparams(("arbitrary",)),
        name="norm_inproj",
    )(*x, g.reshape(1, D), mod, w_in_t)


@functools.lru_cache(None)
def _dft_mats(L):
    n = 2 * L
    k = np.arange(L)[:, None]
    t = np.arange(L)[None, :]
    ang = 2.0 * np.pi * ((k * t) % n) / n
    top = np.cos(ang)
    bot = -np.sin(ang)
    bot[0] = np.cos(np.pi * np.arange(L))
    fwd = np.concatenate([top, bot], 0)
    s = np.full((n, 1), 2.0 / n)
    s[0] = s[L] = 1.0 / n
    inv = (fwd * s).T
    return fwd.astype(np.float32), inv.astype(np.float32)


@functools.lru_cache(None)
def _hy_positions(L):
    t = np.linspace(0.0, 1.0, L)
    bands = (HY_EMB - 1) // 2
    f = np.linspace(1e-4, bands - 1, bands)
    w = 2.0 * np.pi * np.arange(L) / L
    ang = w[:, None] * f[None, :]
    z = np.concatenate([t[:, None], np.cos(ang), -np.sin(ang)], -1)
    zp = np.zeros((L, LANES))
    zp[:, :HY_EMB] = z
    return zp.astype(np.float32), t[:, None].astype(np.float32)


def _hy_filter_kernel(z_ref, t_ref, w1_ref, b1_ref, w2_ref, b2_ref, w3_ref, ld_ref, f_ref,
                      p_ref, q_ref, r_ref, *, L):
    h = jnp.sin(jnp.dot(z_ref[...], w1_ref[...], precision=HIGHEST, preferred_element_type=F32) + b1_ref[...])
    h = jnp.sin(jnp.dot(h, w2_ref[...], precision=HIGHEST, preferred_element_type=F32) + b2_ref[...])
    filt = jnp.dot(h, w3_ref[...], precision=HIGHEST, preferred_element_type=F32)
    filt = filt * jnp.exp(-t_ref[...] * jnp.exp(ld_ref[...]))
    c = HY_CH
    h_fwd = jnp.concatenate([filt[:, 0:c], filt[:, 2 * c:3 * c]], axis=1)
    h_bwd = jnp.concatenate([filt[:, c:2 * c], filt[:, 3 * c:4 * c]], axis=1)
    row = lax.broadcasted_iota(jnp.int32, (L, 1), 0)
    h_bwd = jnp.where(row == 0, 0.0, h_bwd)
    a = jnp.dot(f_ref[...], h_fwd.astype(BF16), preferred_element_type=F32)
    b = jnp.dot(f_ref[...], h_bwd.astype(BF16), preferred_element_type=F32)
    re = a[:L] + b[:L]
    im = a[L:] - b[L:]
    nyq = a[L:L + 1] + b[L:L + 1]
    p_ref[...] = re
    q_ref[...] = jnp.where(row == 0, 0.0, im)
    r_ref[...] = jnp.where(row == 0, nyq, re)


def _hy_filter(L, w1p, b1, w2, b2, w3, ld):
    z, t = _hy_positions(L)
    fwd = jnp.asarray(_dft_mats(L)[0]).astype(BF16)
    out = jax.ShapeDtypeStruct((L, 2 * HY_CH), F32)
    return pl.pallas_call(
        functools.partial(_hy_filter_kernel, L=L),
        out_shape=[out, out, out],
        compiler_params=_cparams(None),
        name=f"hyena_filter_{L}",
    )(z, t, w1p, b1, w2, b2, w3, ld, fwd)


def _hyena_kernel(u_ref, sw_ref, bias_ref, p_ref, q_ref, r_ref, f_ref, g_ref, o_ref, *, L):
    c = HY_CH
    u = _conv3(u_ref[...], sw_ref[...], L, L)
    z = u[:, 0:c]
    for o in range(2):
        gate = u[:, (o + 1) * c:(o + 2) * c]
        zf = jnp.dot(f_ref[...], z.astype(BF16), preferred_element_type=F32)
        a, b = zf[:L], zf[L:]
        p = p_ref[:, o * c:(o + 1) * c]
        q = q_ref[:, o * c:(o + 1) * c]
        r = r_ref[:, o * c:(o + 1) * c]
        y_re = (a * p - b * q).astype(BF16)
        y_im = (a * q + b * r).astype(BF16)
        y = (jnp.dot(g_ref[:, :L], y_re, preferred_element_type=F32)
             + jnp.dot(g_ref[:, L:], y_im, preferred_element_type=F32))
        z = gate * (y + bias_ref[o:o + 1, :] * z)
    o_ref[...] = z


def _hyena(u_hy, L, n_seq, row_block0, sw, bias, p, q, r):
    fwd, inv = (jnp.asarray(m).astype(BF16) for m in _dft_mats(L))
    full = lambda a: pl.BlockSpec(a.shape, lambda b: (0,) * a.ndim)
    return pl.pallas_call(
        functools.partial(_hyena_kernel, L=L),
        grid=(n_seq,),
        in_specs=[pl.BlockSpec((L, 3 * HY_CH), lambda b: (row_block0 + b, 0)),
                  full(sw), full(bias), full(p), full(q), full(r), full(fwd), full(inv)],
        out_specs=pl.BlockSpec((L, HY_CH), lambda b: (b, 0)),
        out_shape=jax.ShapeDtypeStruct((n_seq * L, HY_CH), F32),
        compiler_params=_cparams(("arbitrary",)),
        name=f"hyena_{L}",
    )(u_hy, sw, bias, p, q, r, fwd, inv)


def _cumsum_rows(x, n, reverse):
    row = lax.broadcasted_iota(jnp.int32, (n, 1), 0)
    s = 1
    while s < n:
        if reverse:
            x = x + jnp.where(row < n - s, pltpu.roll(x, n - s, 0), 0.0)
        else:
            x = x + jnp.where(row >= s, pltpu.roll(x, s, 0), 0.0)
        s *= 2
    return x


def _cumsum_lanes(x, n, reverse):
    col = lax.broadcasted_iota(jnp.int32, (1, n), 1)
    s = 1
    while s < n:
        if reverse:
            x = x + jnp.where(col < n - s, pltpu.roll(x, n - s, 1), 0.0)
        else:
            x = x + jnp.where(col >= s, pltpu.roll(x, s, 1), 0.0)
        s *= 2
    return x


def _mlstm_kernel(*refs, L, has_state):
    if has_state:
        (qk_ref, v_ref, o_ref, g_ref, gt_ref, sw_ref, gb_ref, gbt_ref, ng_ref, c0_ref, n0_ref, m0_ref,
         y_ref, cout_ref, nout_ref, mout_ref, q_s, k_s, vt_s, ht_s, ct_s, n_s, m_s) = refs
    else:
        (qk_ref, v_ref, o_ref, g_ref, gt_ref, sw_ref, gb_ref, gbt_ref, ng_ref,
         y_ref, cout_ref, nout_ref, mout_ref, q_s, k_s, vt_s, ht_s, ct_s, n_s, m_s) = refs
    tc = ML_CHUNK
    nc = L // tc
    nh = ML_HEADS
    dh = ML_DH

    qk = _silu(_conv3(qk_ref[...], sw_ref[...], L, L))
    q_s[...] = qk[:, :ML_W].astype(BF16)
    k_s[...] = (qk[:, ML_W:] * (dh ** -0.5)).astype(BF16)
    for c in range(nc):
        vt_s[c] = v_ref[c * tc:(c + 1) * tc, :].T.astype(BF16)

    for i in range(2 * nh):
        ct_s[i] = c0_ref[0, i].T if has_state else jnp.zeros((dh, dh), F32)
    n_s[...] = n0_ref[0] if has_state else jnp.zeros_like(n_s)
    m_s[...] = m0_ref[0] if has_state else jnp.zeros_like(m_s)

    si = lax.broadcasted_iota(jnp.int32, (tc, tc), 0)
    ti = lax.broadcasted_iota(jnp.int32, (tc, tc), 1)

    for d in range(2):
        rev = d == 1
        mask = (si >= ti) if rev else (si <= ti)
        edge = 0 if rev else tc - 1

        def chunk(j, carry, d=d, rev=rev, mask=mask, edge=edge):
            cidx = (nc - 1 - j) if rev else j
            r0 = pl.multiple_of(cidx * tc, tc)
            pre = g_ref[pl.ds(r0, tc), :] + gb_ref[...]
            pre_t = gt_ref[cidx] + gbt_ref[...]
            cum = _cumsum_rows(_log_sigmoid(pre), tc, rev)
            cum_t = _cumsum_lanes(_log_sigmoid(pre_t), tc, rev)
            key_all = cum - pltpu.roll(pre, 8, 1)
            for h in range(nh):
                col = d * nh + h
                hs = slice(h * dh, (h + 1) * dh)
                key = key_all[:, 8 + col:9 + col]
                b_row = cum_t[8 + col:9 + col, :]
                b_end = b_row[:, edge:edge + 1]
                m_prev = m_s[col:col + 1, 0:1]
                dmat = jnp.where(mask, b_row - key, -jnp.inf)
                inter = b_row + m_prev
                m_row = jnp.maximum(inter, jnp.max(dmat, axis=0, keepdims=True))
                w_intra = jnp.exp(dmat - m_row)
                w_state = jnp.exp(inter - m_row)
                qh = q_s[pl.ds(r0, tc), hs]
                kh = k_s[pl.ds(r0, tc), hs]
                vt = vt_s[cidx, hs, :]
                ct_prev = ct_s[col]
                n_prev = n_s[col:col + 1, :]
                s = lax.dot_general(kh, qh, _NT, preferred_element_type=F32) * w_intra
                num = (jnp.dot(vt, s.astype(BF16), preferred_element_type=F32)
                       + w_state * lax.dot_general(ct_prev.astype(BF16), qh, _NT, preferred_element_type=F32))
                qn = lax.dot_general(jnp.broadcast_to(n_prev, (8, dh)).astype(BF16), qh, _NT,
                                     preferred_element_type=F32)[0:1]
                den = jnp.sum(s, axis=0, keepdims=True) + w_state * qn
                hout = num * (1.0 / jnp.maximum(jnp.abs(den), jnp.exp(-m_row)))
                if d == 0:
                    ht_s[cidx, hs, :] = hout
                else:
                    ht_s[cidx, hs, :] += hout
                m_new = jnp.maximum(b_end + m_prev, b_end - jnp.min(key, axis=0, keepdims=True))
                wg = jnp.exp(b_end - key - m_new)
                decay = jnp.exp(b_end + m_prev - m_new)
                kw = kh.astype(F32) * wg
                ct_s[col] = decay * ct_prev + jnp.dot(vt, kw.astype(BF16), preferred_element_type=F32)
                n_s[col:col + 1, :] = decay * n_prev + jnp.sum(kw, axis=0, keepdims=True)
                m_s[col:col + 1, :] = jnp.broadcast_to(m_new, (1, LANES))
            return carry

        lax.fori_loop(0, nc, chunk, 0)

    for c in range(nc):
        for h in range(nh):
            hs = slice(h * dh, (h + 1) * dh)
            rs = slice(c * tc, (c + 1) * tc)
            ht = ht_s[c, hs, :]
            hn = ht * lax.rsqrt(jnp.mean(ht * ht, axis=0, keepdims=True) + EPS)
            y_ref[rs, hs] = jax.nn.sigmoid(o_ref[rs, hs]) * (hn.T * ng_ref[:, hs])
    for i in range(2 * nh):
        cout_ref[0, i] = ct_s[i].T
    nout_ref[0] = n_s[...]
    mout_ref[0] = m_s[...]


def _mlstm(qk, v, o, gates, gates_t, L, n_seq, row_block0, sw, gb, gbt, ng, state):
    nc = L // ML_CHUNK
    has_state = state is not None
    full = lambda a: pl.BlockSpec(a.shape, lambda b: (0,) * a.ndim)
    in_specs = [pl.BlockSpec((L, 2 * ML_W), lambda b: (row_block0 + b, 0)),
                pl.BlockSpec((L, ML_W), lambda b: (row_block0 + b, 0)),
                pl.BlockSpec((L, ML_W), lambda b: (row_block0 + b, 0)),
                pl.BlockSpec((L, GATE_PAD), lambda b: (row_block0 + b, 0)),
                pl.BlockSpec((nc, 16, ML_CHUNK), lambda b: (row_block0 + b, 0, 0)),
                full(sw), full(gb), full(gbt), full(ng)]
    args = [qk, v, o, gates, gates_t, sw, gb, gbt, ng]
    if has_state:
        c0, n0, m0 = state
        in_specs += [pl.BlockSpec((1, 2 * ML_HEADS, ML_DH, ML_DH), lambda b: (b, 0, 0, 0)),
                     pl.BlockSpec((1, 2 * ML_HEADS, ML_DH), lambda b: (b, 0, 0)),
                     pl.BlockSpec((1, 2 * ML_HEADS, LANES), lambda b: (b, 0, 0))]
        args += [c0, n0, m0]
    return pl.pallas_call(
        functools.partial(_mlstm_kernel, L=L, has_state=has_state),
        grid=(n_seq,),
        in_specs=in_specs,
        out_specs=[pl.BlockSpec((L, ML_W), lambda b: (b, 0)),
                   pl.BlockSpec((1, 2 * ML_HEADS, ML_DH, ML_DH), lambda b: (b, 0, 0, 0)),
                   pl.BlockSpec((1, 2 * ML_HEADS, ML_DH), lambda b: (b, 0, 0)),
                   pl.BlockSpec((1, 2 * ML_HEADS, LANES), lambda b: (b, 0, 0))],
        out_shape=[jax.ShapeDtypeStruct((n_seq * L, ML_W), F32),
                   jax.ShapeDtypeStruct((n_seq, 2 * ML_HEADS, ML_DH, ML_DH), F32),
                   jax.ShapeDtypeStruct((n_seq, 2 * ML_HEADS, ML_DH), F32),
                   jax.ShapeDtypeStruct((n_seq, 2 * ML_HEADS, LANES), F32)],
        scratch_shapes=[pltpu.VMEM((L, ML_W), BF16), pltpu.VMEM((L, ML_W), BF16),
                        pltpu.VMEM((nc, ML_W, ML_CHUNK), BF16),
                        pltpu.VMEM((nc, ML_W, ML_CHUNK), F32),
                        pltpu.VMEM((2 * ML_HEADS, ML_DH, ML_DH), F32),
                        pltpu.VMEM((2 * ML_HEADS, ML_DH), F32),
                        pltpu.VMEM((2 * ML_HEADS, LANES), F32)],
        compiler_params=_cparams(("arbitrary",)),
        name=f"mlstm_{L}",
    )(*args)


def _cmul(ar, ai, br, bi):
    return ar * br - ai * bi, ar * bi + ai * br


def _s5_kernel(*refs, segmented):
    if segmented:
        (u_ref, bb_ref, cc_ref, eb_ref, ec_ref, lam_ref, dsk_ref, wglu_ref, s0_ref, y_ref,
         sbuf, yacc, bmat, cmat, pw) = refs
    else:
        (u_ref, bb_ref, cc_ref, eb_ref, ec_ref, lam_ref, dsk_ref, wglu_ref, y_ref, fin_ref,
         sbuf, yacc, bmat, cmat) = refs
    n = S5_STATE
    rows = S5_ROWS
    steps = S5_STEPS
    blk = 256
    n_blk = steps * rows // blk
    n_seg = 4

    yacc[...] = u_ref[0] * dsk_ref[...]
    ub = u_ref[0].astype(BF16)
    b_keep = (lax.broadcasted_iota(jnp.int32, (S5_CH, 2 * n), 0) // S5_GROUP
              == (lax.broadcasted_iota(jnp.int32, (S5_CH, 2 * n), 1) % n) // S5_P)
    c_keep = ((lax.broadcasted_iota(jnp.int32, (2 * n, S5_CH), 0) % n) // S5_P
              == lax.broadcasted_iota(jnp.int32, (2 * n, S5_CH), 1) // S5_GROUP)

    for d in range(2):
        rev = d == 1
        bmat[...] = jnp.where(b_keep, jnp.dot(bb_ref[d].astype(BF16), eb_ref[...], preferred_element_type=F32),
                              0.0).astype(BF16)
        cmat[...] = jnp.where(c_keep, jnp.dot(cc_ref[d].astype(BF16), ec_ref[...], preferred_element_type=F32),
                              0.0).astype(BF16)
        for i in range(n_blk):
            sbuf[i * blk:(i + 1) * blk, :] = jnp.dot(ub[i * blk:(i + 1) * blk], bmat[...],
                                                     preferred_element_type=F32)
        lam = lam_ref[d]
        lr = jnp.broadcast_to(lam[:, :n], (rows, n))
        li = jnp.broadcast_to(lam[:, n:], (rows, n))

        def step(i, carry, rev=rev, lr=lr, li=li):
            sr, si = carry
            t = (steps - 1 - i) if rev else i
            off = pl.multiple_of(t * rows, rows)
            pr, pi = _cmul(lr, li, sr, si)
            nr = pr + sbuf[pl.ds(off, rows), 0:n]
            ni = pi + sbuf[pl.ds(off, rows), n:2 * n]
            sbuf[pl.ds(off, rows), 0:n] = nr
            sbuf[pl.ds(off, rows), n:2 * n] = ni
            return nr, ni

        zero = jnp.zeros((rows, n), F32)
        sr, si = lax.fori_loop(0, steps, step, (zero, zero), unroll=2)

        if not segmented:
            fin_ref[0, d, :, 0:n] = sr
            fin_ref[0, d, :, n:2 * n] = si
        else:
            lam_r, lam_i = lam[:, :n], lam[:, n:]
            row8 = lax.broadcasted_iota(jnp.int32, (rows, 1), 0)
            cr, ci = lam_r, lam_i
            acc_r = jnp.broadcast_to(cr, (rows, n))
            acc_i = jnp.broadcast_to(ci, (rows, n))
            for j in range(1, rows):
                cr, ci = _cmul(cr, ci, lam_r, lam_i)
                acc_r = jnp.where(row8 >= j, jnp.broadcast_to(cr, (rows, n)), acc_r)
                acc_i = jnp.where(row8 >= j, jnp.broadcast_to(ci, (rows, n)), acc_i)
            pw[0:rows, 0:n] = acc_r
            pw[0:rows, n:2 * n] = acc_i
            size = rows
            while size < steps:
                tr = pw[size - 1:size, 0:n]
                ti = pw[size - 1:size, n:2 * n]
                xr, xi = _cmul(pw[0:size, 0:n], pw[0:size, n:2 * n], tr, ti)
                pw[size:2 * size, 0:n] = xr
                pw[size:2 * size, n:2 * n] = xi
                size *= 2
            end_off = 0 if rev else (steps - 1) * rows
            loc_r = sbuf[end_off:end_off + rows, 0:n]
            loc_i = sbuf[end_off:end_off + rows, n:2 * n]
            pl_r = pw[steps - 1:steps, 0:n]
            pl_i = pw[steps - 1:steps, n:2 * n]
            s0r = s0_ref[d, :, 0:n]
            s0i = s0_ref[d, :, n:2 * n]
            seg = row8 // 2
            first = (seg == n_seg - 1) if rev else (seg == 0)
            shift = (rows - 2) if rev else 2
            cin_r, cin_i = s0r, s0i
            for _ in range(n_seg - 1):
                fr, fi = _cmul(jnp.broadcast_to(pl_r, (rows, n)), jnp.broadcast_to(pl_i, (rows, n)), cin_r, cin_i)
                tru_r = loc_r + fr
                tru_i = loc_i + fi
                cin_r = jnp.where(first, s0r, pltpu.roll(tru_r, shift, 0))
                cin_i = jnp.where(first, s0i, pltpu.roll(tru_i, shift, 0))

            def fix(tb, carry, rev=rev, cin_r=cin_r, cin_i=cin_i):
                pb = (steps // rows - 1 - tb) if rev else tb
                poff = pl.multiple_of(pb * rows, rows)
                p_r = pw[pl.ds(poff, rows), 0:n]
                p_i = pw[pl.ds(poff, rows), n:2 * n]
                for j in range(rows):
                    jj = rows - 1 - j if rev else j
                    off = pl.multiple_of((tb * rows + j) * rows, rows)
                    fr, fi = _cmul(jnp.broadcast_to(p_r[jj:jj + 1], (rows, n)),
                                   jnp.broadcast_to(p_i[jj:jj + 1], (rows, n)), cin_r, cin_i)
                    sbuf[pl.ds(off, rows), 0:n] += fr
                    sbuf[pl.ds(off, rows), n:2 * n] += fi
                return carry

            lax.fori_loop(0, steps // rows, fix, 0)

        for i in range(n_blk):
            yacc[i * blk:(i + 1) * blk, :] += jnp.dot(sbuf[i * blk:(i + 1) * blk, :].astype(BF16), cmat[...],
                                                      preferred_element_type=F32)

    g = jax.nn.gelu(yacc[...], approximate=True)
    y_ref[0] = g * jax.nn.sigmoid(_bdot(g, wglu_ref[...]))


@functools.lru_cache(None)
def _s5_spread():
    eb = np.zeros((2 * S5_P, 2 * S5_STATE), np.float32)
    for half in range(2):
        for g in range(S5_G):
            c0 = half * S5_STATE + g * S5_P
            eb[half * S5_P:(half + 1) * S5_P, c0:c0 + S5_P] = np.eye(S5_P)
    ec = np.zeros((LANES, S5_CH), np.float32)
    for g in range(S5_G):
        ec[:S5_GROUP, g * S5_GROUP:(g + 1) * S5_GROUP] = np.eye(S5_GROUP)
    return eb, ec


def _s5(u_tm, bb, cc, lam, dskip, wglu, s0):
    n_grp = u_tm.shape[0]
    n_rows = S5_STEPS * S5_ROWS
    segmented = s0 is not None
    eb, ec = (jnp.asarray(m).astype(BF16) for m in _s5_spread())
    full = lambda a: pl.BlockSpec(a.shape, lambda g: (0,) * a.ndim)
    in_specs = [pl.BlockSpec((1, n_rows, S5_CH), lambda g: (g, 0, 0)),
                full(bb), full(cc), full(eb), full(ec), full(lam), full(dskip), full(wglu)]
    args = [u_tm, bb, cc, eb, ec, lam, dskip, wglu]
    out_specs = [pl.BlockSpec((1, n_rows, S5_CH), lambda g: (g, 0, 0))]
    out_shape = [jax.ShapeDtypeStruct((n_grp, n_rows, S5_CH), F32)]
    scratch = [pltpu.VMEM((n_rows, 2 * S5_STATE), F32), pltpu.VMEM((n_rows, S5_CH), F32),
               pltpu.VMEM((S5_CH, 2 * S5_STATE), BF16), pltpu.VMEM((2 * S5_STATE, S5_CH), BF16)]
    if segmented:
        in_specs.append(full(s0))
        args.append(s0)
        scratch.append(pltpu.VMEM((S5_STEPS, 2 * S5_STATE), F32))
    else:
        out_specs.append(pl.BlockSpec((1, 2, S5_ROWS, 2 * S5_STATE), lambda g: (g, 0, 0, 0)))
        out_shape.append(jax.ShapeDtypeStruct((n_grp, 2, S5_ROWS, 2 * S5_STATE), F32))
    return pl.pallas_call(
        functools.partial(_s5_kernel, segmented=segmented),
        grid=(n_grp,),
        in_specs=in_specs,
        out_specs=out_specs,
        out_shape=out_shape,
        scratch_shapes=scratch,
        compiler_params=_cparams(("arbitrary",)),
        name="s5_seg" if segmented else "s5_ctx",
    )(*args)


def _s5_params(a_re, a_im, log_dt, b_re, b_im, c_re, c_im):
    dt = jnp.exp(log_dt)[:, :, None]
    mag = jnp.exp(a_re * dt)
    lb_re = mag * jnp.cos(a_im * dt)
    lb_im = mag * jnp.sin(a_im * dt)
    den = a_re * a_re + a_im * a_im
    nr, ni = lb_re - 1.0, lb_im
    k_re = (nr * a_re + ni * a_im) / den
    k_im = (ni * a_re - nr * a_im) / den
    bb_re = k_re[..., None] * b_re - k_im[..., None] * b_im
    bb_im = k_re[..., None] * b_im + k_im[..., None] * b_re
    to_gc_p = lambda m: m.transpose(0, 1, 3, 2).reshape(2, S5_CH, S5_P)
    bb = jnp.concatenate([to_gc_p(bb_re), to_gc_p(bb_im)], axis=2)
    to_gp_c = lambda m: m.transpose(0, 1, 3, 2).reshape(2, S5_STATE, S5_GROUP)
    cc = jnp.concatenate([to_gp_c(c_re), -to_gp_c(c_im)], axis=1)
    cc = jnp.pad(cc, ((0, 0), (0, 0), (0, LANES - S5_GROUP)))
    lam = jnp.concatenate([lb_re.reshape(2, 1, S5_STATE), lb_im.reshape(2, 1, S5_STATE)], axis=2)
    return bb, cc, lam


def _outproj_kernel(*refs, n_x):
    x_refs = refs[:n_x]
    (hyc_ref, hyl_ref, mlc_ref, mll_ref, s5c_ref, s5l_ref, w_ref, mod_ref, g_ref,
     rw_ref, rb_ref, xn_ref, h2e_ref, best_ref, rank_ref, cnt_ref, wb, cnt_s) = refs[n_x:]
    step = pl.program_id(0)

    @pl.when(step == 0)
    def _():
        wb[...] = w_ref[0].astype(BF16)
        cnt_s[...] = jnp.zeros_like(cnt_s)

    is_ctx = step < T_CTX // TM
    pick = lambda c_ref, l_ref: jnp.where(is_ctx, c_ref[...], l_ref[...]).astype(BF16)
    mod = mod_ref[0]
    a, b = HY_CH, HY_CH + ML_W
    mix = (jnp.dot(pick(hyc_ref, hyl_ref), wb[0:a, :], preferred_element_type=F32)
           + jnp.dot(pick(mlc_ref, mll_ref), wb[a:b, :], preferred_element_type=F32)
           + jnp.dot(pick(s5c_ref, s5l_ref), wb[b:, :], preferred_element_type=F32))
    xn = _x_tile(step, x_refs) + mod[2:3] * mix
    xn_ref[...] = xn
    h2 = _rms(xn, g_ref[...]) * (1.0 + mod[4:5]) + mod[3:4]
    h2e_ref[:, 0:D] = h2
    h_hi, h_lo = _split_bf16(h2)
    r_hi, r_lo = _split_bf16(rw_ref[...])
    logits = (lax.dot_general(r_hi, h_hi, _NT, preferred_element_type=F32)
              + (lax.dot_general(r_hi, h_lo, _NT, preferred_element_type=F32)
                 + lax.dot_general(r_lo, h_hi, _NT, preferred_element_type=F32)))
    ex = jnp.exp(logits - jnp.max(logits, axis=0, keepdims=True))
    probs = ex / jnp.sum(ex, axis=0, keepdims=True)
    sel = probs + rb_ref[...]
    best = None
    best_score = None
    for g in range(N_GROUPS):
        r = [sel[g * GROUP_SIZE + i:g * GROUP_SIZE + i + 1, :] for i in range(GROUP_SIZE)]
        score = None
        for i in range(GROUP_SIZE):
            for j in range(i + 1, GROUP_SIZE):
                pair = r[i] + r[j]
                score = pair if score is None else jnp.maximum(score, pair)
        if g == 0:
            best, best_score = jnp.zeros_like(score, dtype=jnp.int32), score
        else:
            upd = score > best_score
            best = jnp.where(upd, g, best)
            best_score = jnp.where(upd, score, best_score)
    eid = lax.broadcasted_iota(jnp.int32, (N_EXPERTS, 1), 0)
    masked = jnp.where(eid // GROUP_SIZE == best, sel, -jnp.inf)
    m1 = jnp.max(masked, axis=0, keepdims=True)
    i1 = jnp.min(jnp.where(masked == m1, eid, N_EXPERTS), axis=0, keepdims=True)
    masked2 = jnp.where(eid == i1, -jnp.inf, masked)
    m2 = jnp.max(masked2, axis=0, keepdims=True)
    i2 = jnp.min(jnp.where(masked2 == m2, eid, N_EXPERTS), axis=0, keepdims=True)
    p1 = jnp.sum(jnp.where(eid == i1, probs, 0.0), axis=0, keepdims=True)
    p2 = jnp.sum(jnp.where(eid == i2, probs, 0.0), axis=0, keepdims=True)
    tot = p1 + p2
    comb = jnp.where(eid == i1, p1 / tot, 0.0) + jnp.where(eid == i2, p2 / tot, 0.0)
    comb = jnp.concatenate([comb, jnp.zeros((LANES - N_EXPERTS, comb.shape[1]), F32)], axis=0)
    h2e_ref[:, D:] = comb.T
    best_ref[...] = best
    gid = lax.broadcasted_iota(jnp.int32, (8, 1), 0)
    onehot = (gid == best).astype(F32)
    cum = _cumsum_lanes(onehot, TM, False)
    run = cnt_s[:, 0:1]
    rank_ref[...] = jnp.sum(onehot * (cum - onehot + run), axis=0, keepdims=True).astype(jnp.int32)
    cnt_s[...] = jnp.broadcast_to(run + cum[:, TM - 1:TM], cnt_s.shape)
    cnt_ref[...] = cnt_s[...]


def _outproj(x, y_hy, y_ml, y_s5, w_out, l, mod, g2, rw_t, rb):
    full = lambda a: pl.BlockSpec(a.shape, lambda i: (0,) * a.ndim)
    n_ctx = T_CTX // TM
    ctx = lambda w: pl.BlockSpec((TM, w), lambda i: (jnp.minimum(i, n_ctx - 1), 0))
    lat = lambda w: pl.BlockSpec((TM, w), lambda i: (jnp.maximum(i - n_ctx, 0), 0))
    tok = lambda w: pl.BlockSpec((TM, w), lambda i: (i, 0))
    row = pl.BlockSpec((1, TM), lambda i: (0, i))
    return pl.pallas_call(
        functools.partial(_outproj_kernel, n_x=len(x)),
        grid=(T_ALL // TM,),
        in_specs=_x_specs(x) + [ctx(HY_CH), lat(HY_CH), ctx(ML_W), lat(ML_W), ctx(S5_CH), lat(S5_CH),
                  pl.BlockSpec((1, D, D), lambda i: (l, 0, 0), pipeline_mode=pl.Buffered(1)),
                  pl.BlockSpec((1, 6, D), lambda i: (_mod_row(i, TM), 0, 0)),
                  full(g2), full(rw_t), full(rb)],
        out_specs=[tok(D), tok(D + LANES), row, row, pl.BlockSpec((8, LANES), lambda i: (0, 0))],
        out_shape=[jax.ShapeDtypeStruct((T_ALL, D), F32),
                   jax.ShapeDtypeStruct((T_ALL, D + LANES), F32),
                   jax.ShapeDtypeStruct((1, T_ALL), jnp.int32),
                   jax.ShapeDtypeStruct((1, T_ALL), jnp.int32),
                   jax.ShapeDtypeStruct((8, LANES), F32)],
        scratch_shapes=[pltpu.VMEM((D, D), BF16), pltpu.VMEM((8, LANES), F32)],
        compiler_params=_cparams(("arbitrary",)),
        name="outproj_router",
    )(*x, *y_hy, *y_ml, *y_s5, w_out, mod, g2, rw_t, rb)


def _gather_rows(idx_ref, idx_base, src_ref, dst_ref, n_rows):
    def body(r8, carry):
        base = pl.multiple_of(r8 * 8, 8)
        for k in range(8):
            idx = idx_ref[idx_base + base + k]
            dst_ref[pl.ds(base + k, 1), :] = src_ref[pl.ds(idx, 1), :]
        return carry

    lax.fori_loop(0, n_rows // 8, body, 0)


def _dispatch(best, rank, cnt):
    tm = TM_MOE
    g = best.reshape(T_ALL)
    cnt = cnt[:N_GROUPS, 0].astype(jnp.int32)
    n_tile_g = (cnt + tm - 1) // tm
    tile_end = jnp.cumsum(n_tile_g)
    row_off = (tile_end - n_tile_g) * tm
    pos = rank.reshape(T_ALL)
    for k in range(N_GROUPS):
        pos = pos + jnp.where(g == k, row_off[k], 0)
    src = jnp.zeros((MOE_SLOTS,), jnp.int32).at[pos].set(jnp.arange(T_ALL, dtype=jnp.int32))
    tiles = jnp.arange(MOE_SLOTS // tm, dtype=jnp.int32)
    tile_group = jnp.minimum(jnp.sum((tiles[:, None] >= tile_end[None, :]).astype(jnp.int32), axis=1), N_GROUPS - 1)
    return pos, src, tile_group.astype(jnp.int32), tile_end[N_GROUPS - 1:].astype(jnp.int32)


def _moe_kernel(src_ref, tg_ref, nt_ref, h_ref, wg_ref, wu_ref, wd_ref, ys_ref, g_a, x_a, g_b, x_b, acc):
    i = pl.program_id(0)
    j = pl.program_id(1)
    tm = TM_MOE
    quarter = tm // GROUP_SIZE
    n_live = nt_ref[0]

    @pl.when(jnp.logical_and(i == 0, j == 0))
    def _():
        _gather_rows(src_ref, 0, h_ref, g_a, tm)
        x_a[...] = g_a[:, 0:D].astype(BF16)

    def step(cur_g, cur_x, nxt_g, nxt_x):
        @pl.when(j == 0)
        def _():
            acc[...] = jnp.zeros_like(acc)

        e = tg_ref[i] * GROUP_SIZE + j
        lane = lax.broadcasted_iota(jnp.int32, (1, LANES), 1)
        wg = wg_ref[0, 0].astype(BF16)
        wu = wu_ref[0, 0].astype(BF16)
        wd = wd_ref[0, 0].astype(BF16)
        sub = MOE_SUB
        for s in range(tm // sub):
            rs = slice(s * sub, (s + 1) * sub)
            hb = cur_x[rs, :]
            ce = jnp.sum(jnp.where(lane == e, cur_g[rs, D:], 0.0), axis=1, keepdims=True)
            hid = _silu(jnp.dot(hb, wg, preferred_element_type=F32)) * jnp.dot(hb, wu, preferred_element_type=F32)
            acc[rs, :] += jnp.dot((hid * ce).astype(BF16), wd, preferred_element_type=F32)

        r0 = pl.multiple_of(j * quarter, quarter)
        base = jnp.minimum(i + 1, n_live - 1) * tm + r0
        for k in range(quarter):
            nxt_g[pl.ds(r0 + k, 1), :] = h_ref[pl.ds(src_ref[base + k], 1), :]
        nxt_x[pl.ds(r0, quarter), :] = nxt_g[pl.ds(r0, quarter), 0:D].astype(BF16)

        @pl.when(j == GROUP_SIZE - 1)
        def _():
            ys_ref[...] = acc[...]

    live = i < n_live

    @pl.when(jnp.logical_and(live, i % 2 == 0))
    def _():
        step(g_a, x_a, g_b, x_b)

    @pl.when(jnp.logical_and(live, i % 2 == 1))
    def _():
        step(g_b, x_b, g_a, x_a)


def _moe(h2e, src, tile_group, n_tiles, wg, wu, wd, l):
    tm = TM_MOE

    def w_map(i, j, src, tg, nt):
        live = i < nt[0]
        ii = jnp.minimum(i, nt[0] - 1)
        return (l, tg[ii] * GROUP_SIZE + jnp.where(live, j, GROUP_SIZE - 1), 0, 0)

    return pl.pallas_call(
        _moe_kernel,
        grid_spec=pltpu.PrefetchScalarGridSpec(
            num_scalar_prefetch=3,
            grid=(MOE_SLOTS // tm, GROUP_SIZE),
            in_specs=[pl.BlockSpec(memory_space=pltpu.VMEM),
                      pl.BlockSpec((1, 1, D, D_EXPERT), w_map),
                      pl.BlockSpec((1, 1, D, D_EXPERT), w_map),
                      pl.BlockSpec((1, 1, D_EXPERT, D), w_map)],
            out_specs=pl.BlockSpec((tm, D), lambda i, j, src, tg, nt: (jnp.minimum(i, nt[0] - 1), 0)),
            scratch_shapes=[pltpu.VMEM((tm, D + LANES), F32), pltpu.VMEM((tm, D), BF16),
                            pltpu.VMEM((tm, D + LANES), F32), pltpu.VMEM((tm, D), BF16),
                            pltpu.VMEM((tm, D), F32)]),
        out_shape=jax.ShapeDtypeStruct((MOE_SLOTS, D), F32),
        compiler_params=_cparams(("arbitrary", "arbitrary")),
        name="moe_experts",
    )(src, tile_group, n_tiles, h2e, wg, wu, wd)


def _combine_kernel(pos_ref, ys_ref, xn_ref, mod_ref, fg_ref, *rest, final):
    step = pl.program_id(0)
    if final:
        yc_ref, yl_ref, gbuf = rest
    else:
        out_ref, gbuf = rest
    _gather_rows(pos_ref, step * TM, ys_ref, gbuf, TM)
    out = xn_ref[...] + mod_ref[0][5:6] * gbuf[...]
    if final:
        y = _rms(out, fg_ref[...])

        @pl.when(step < T_CTX // TM)
        def _():
            yc_ref[...] = y

        @pl.when(step >= T_CTX // TM)
        def _():
            yl_ref[...] = y
    else:
        out_ref[...] = out


def _combine(pos, ys, xn, mod, fg, final):
    spec = pl.BlockSpec((TM, D), lambda i, pos: (i, 0))
    n_ctx = T_CTX // TM
    if final:
        out_specs = [pl.BlockSpec((TM, D), lambda i, pos: (jnp.minimum(i, n_ctx - 1), 0)),
                     pl.BlockSpec((TM, D), lambda i, pos: (jnp.maximum(i - n_ctx, 0), 0))]
        out_shape = [jax.ShapeDtypeStruct((T_CTX, D), F32), jax.ShapeDtypeStruct((T_LAT, D), F32)]
    else:
        out_specs = [spec]
        out_shape = [jax.ShapeDtypeStruct((T_ALL, D), F32)]
    return pl.pallas_call(
        functools.partial(_combine_kernel, final=final),
        grid_spec=pltpu.PrefetchScalarGridSpec(
            num_scalar_prefetch=1,
            grid=(T_ALL // TM,),
            in_specs=[pl.BlockSpec(memory_space=pltpu.VMEM),
                      spec,
                      pl.BlockSpec((1, 6, D), lambda i, pos: (_mod_row(i, TM), 0, 0)),
                      pl.BlockSpec((1, D), lambda i, pos: (0, 0))],
            out_specs=out_specs,
            scratch_shapes=[pltpu.VMEM((TM, D), F32)]),
        out_shape=out_shape,
        compiler_params=_cparams(("arbitrary",)),
        name="moe_combine",
    )(pos, ys, xn, mod, fg)


@functools.lru_cache(None)
def _pos_embed():
    rows = L_LAT // GRID_W
    r = np.repeat(np.arange(rows, dtype=np.float64), GRID_W)
    col = np.tile(np.arange(GRID_W, dtype=np.float64), rows)
    quarter = D // 4
    freq = np.exp(-math.log(POS_BASE) * np.arange(quarter, dtype=np.float64) / quarter)
    ar = r[:, None] * freq[None]
    ac = col[:, None] * freq[None]
    emb = np.concatenate([np.sin(ar), np.cos(ar), np.sin(ac), np.cos(ac)], axis=-1)
    return emb.astype(np.float32)


def _to_time_major_ctx(a):
    c = a.shape[-1]
    a = a.reshape(2, S5_ROWS, L_CTX, c).transpose(0, 2, 1, 3)
    return a.reshape(2, L_CTX * S5_ROWS, c)


def _from_time_major_ctx(a):
    c = a.shape[-1]
    a = a.reshape(2, L_CTX, S5_ROWS, c).transpose(0, 2, 1, 3)
    return a.reshape(T_CTX, c)


def _to_time_major_lat(a):
    c = a.shape[-1]
    a = a.reshape(N_LAT_SEQ, 4, S5_STEPS, c).transpose(2, 1, 0, 3)
    return a.reshape(1, S5_STEPS * S5_ROWS, c)


def _from_time_major_lat(a):
    c = a.shape[-1]
    a = a.reshape(S5_STEPS, 4, N_LAT_SEQ, c).transpose(2, 1, 0, 3)
    return a.reshape(T_LAT, c)


def kernel(x_prompt, x_sample, c, state_mlstm_C, state_mlstm_n, state_mlstm_m, state_s5_re, state_s5_im, c_ctx, w_ada, b_ada, norm1_g, norm2_g, final_g, w_in, w_out, hy_short, hy_fw1, hy_fb1, hy_fw2, hy_fb2, hy_fw3, hy_log_decay, hy_bias, ml_short, ml_gate_bias, ml_norm_g, s5_a_re, s5_a_im, s5_log_dt, s5_b_re, s5_b_im, s5_c_re, s5_c_im, s5_d, s5_w_glu, router_w, router_b, moe_w_gate, moe_w_up, moe_w_down):
    x = (x_prompt.reshape(T_CTX, D), x_sample.reshape(T_LAT, D), jnp.asarray(_pos_embed()))
    w_in_t = jnp.swapaxes(w_in, 1, 2)
    cc =jnp.concatenate([c_ctx[None], c, jnp.zeros((8 - 1 - N_LAT_SEQ, D), F32)], axis=0)
    mod_all = _ada(cc, w_ada, b_ada).reshape(DEPTH, 8, 6, D)
    rw_t = router_w.T
    rb = router_b.reshape(N_EXPERTS, 1)
    fg = final_g.reshape(1, D)
    lat_blk = T_CTX // L_LAT

    new_c, new_n, new_m, new_re, new_im = [], [], [], [], []
    y_prompt = y_sample = None
    for l in range(DEPTH):
        mod = mod_all[l]
        u_hy, qk, v, o, u_s5, gates = _inproj(x, norm1_g[l], mod, w_in_t, l)

        w1p = jnp.zeros((LANES, HY_FILTER_W), F32).at[:HY_EMB].set(hy_fw1[l])
        b1 = hy_fb1[l].reshape(1, HY_FILTER_W)
        b2 = hy_fb2[l].reshape(1, HY_FILTER_W)
        ld = hy_log_decay[l].reshape(1, 4 * HY_CH)
        y_hy = []
        for L, n_seq, blk0 in ((L_CTX, N_CTX_SEQ, 0), (L_LAT, N_LAT_SEQ, lat_blk)):
            p, q, r = _hy_filter(L, w1p, b1, hy_fw2[l], b2, hy_fw3[l], ld)
            y_hy.append(_hyena(u_hy, L, n_seq, blk0, hy_short[l], hy_bias[l], p, q, r))

        gates_t = gates[:, :16].reshape(T_ALL // ML_CHUNK, ML_CHUNK, 16).transpose(0, 2, 1)
        gb = jnp.zeros((1, GATE_PAD), F32).at[0, :16].set(ml_gate_bias[l].reshape(16))
        gbt = ml_gate_bias[l].reshape(16, 1)
        ng = ml_norm_g[l].reshape(1, ML_W)
        m0 = jnp.broadcast_to(state_mlstm_m[:, l].reshape(N_LAT_SEQ, 2 * ML_HEADS, 1), (N_LAT_SEQ, 2 * ML_HEADS, LANES))
        yc, cc_, nc_, mc_ = _mlstm(qk, v, o, gates, gates_t, L_CTX, N_CTX_SEQ, 0, ml_short[l], gb, gbt, ng, None)
        yl, _, _, _ = _mlstm(qk, v, o, gates, gates_t, L_LAT, N_LAT_SEQ, lat_blk, ml_short[l], gb, gbt, ng,
                             (state_mlstm_C[:, l].reshape(N_LAT_SEQ, 2 * ML_HEADS, ML_DH, ML_DH),
                              state_mlstm_n[:, l].reshape(N_LAT_SEQ, 2 * ML_HEADS, ML_DH), m0))
        y_ml = (yc, yl)
        new_c.append(cc_.reshape(N_CTX_SEQ, 2, ML_HEADS, ML_DH, ML_DH))
        new_n.append(nc_.reshape(N_CTX_SEQ, 2, ML_HEADS, ML_DH))
        new_m.append(mc_[:, :, 0].reshape(N_CTX_SEQ, 2, ML_HEADS))

        bb, cc_s5, lam = _s5_params(s5_a_re[l], s5_a_im[l], s5_log_dt[l], s5_b_re[l], s5_b_im[l],
                                    s5_c_re[l], s5_c_im[l])
        dsk = s5_d[l].reshape(1, S5_CH)
        wglu = s5_w_glu[l].astype(BF16)
        ys_c, fin = _s5(_to_time_major_ctx(u_s5[:T_CTX]), bb, cc_s5, lam, dsk, wglu, None)
        s0 = jnp.concatenate([state_s5_re[:, l].reshape(N_LAT_SEQ, 2, S5_STATE),
                              state_s5_im[:, l].reshape(N_LAT_SEQ, 2, S5_STATE)], axis=-1)
        s0 = jnp.tile(s0.transpose(1, 0, 2), (1, 4, 1))
        (ys_l,) = _s5(_to_time_major_lat(u_s5[T_CTX:]), bb, cc_s5, lam, dsk, wglu, s0)
        y_s5 = (_from_time_major_ctx(ys_c), _from_time_major_lat(ys_l))
        fin = fin.transpose(0, 2, 1, 3).reshape(N_CTX_SEQ, 2, 2 * S5_STATE)
        new_re.append(fin[..., :S5_STATE].reshape(N_CTX_SEQ, 2, S5_G, S5_P))
        new_im.append(fin[..., S5_STATE:].reshape(N_CTX_SEQ, 2, S5_G, S5_P))

        xn, h2e, best, rank, cnt = _outproj(x, y_hy, y_ml, y_s5, w_out, l, mod,
                                            norm2_g[l].reshape(1, D), rw_t, rb)
        pos, src, tile_group, n_tiles = _dispatch(best, rank, cnt)
        ys = _moe(h2e, src, tile_group, n_tiles, moe_w_gate, moe_w_up, moe_w_down, l)
        res = _combine(pos, ys, xn, mod, fg, l == DEPTH - 1)
        if l == DEPTH - 1:
            y_prompt = res[0].reshape(N_CTX_SEQ, L_CTX, D)
            y_sample = res[1].reshape(N_LAT_SEQ, L_LAT, D)
        else:
            x = (res[0],)

    return (y_prompt, y_sample, jnp.stack(new_c, axis=1), jnp.stack(new_n, axis=1), jnp.stack(new_m, axis=1),
            jnp.stack(new_re, axis=1), jnp.stack(new_im, axis=1))
```

```python
import functools
import math

import numpy as np
import jax
import jax.numpy as jnp
from jax import lax
from jax.experimental import pallas as pl
from jax.experimental.pallas import tpu as pltpu

F32 = jnp.float32
BF16 = jnp.bfloat16
HIGHEST = lax.Precision.HIGHEST

D = 1024
N_CTX_SEQ, L_CTX = 16, 256
N_LAT_SEQ, L_LAT = 2, 1024
T_CTX = N_CTX_SEQ * L_CTX
T_LAT = N_LAT_SEQ * L_LAT
T_ALL = T_CTX + T_LAT
DEPTH = 2
EPS = 1e-6
GRID_W = 64
POS_BASE = 10000.0
HY_CH = 256
HY_EMB = 33
HY_FILTER_W = 64
ML_HEADS = 4
ML_DH = 128
ML_W = ML_HEADS * ML_DH
S5_CH = 256
S5_G = 16
S5_GROUP = 16
S5_P = 64
S5_STATE = S5_G * S5_P
N_EXPERTS = 16
N_GROUPS = 4
GROUP_SIZE = N_EXPERTS // N_GROUPS
D_EXPERT = 512
OFF_HY = 0
OFF_QK = 3 * HY_CH
OFF_V = OFF_QK + 2 * ML_W
OFF_O = OFF_V + ML_W
OFF_G = OFF_O + ML_W
OFF_S5 = OFF_G + 16
IN_W = OFF_S5 + S5_CH
LANES = 128
GATE_PAD = LANES

TM = 512
TM_MOE = 512
MOE_SLOTS = T_ALL + N_GROUPS * TM_MOE
MOE_SUB = 512
ML_CHUNK = 256
S5_ROWS = 8
S5_STEPS = 256
VMEM_LIMIT = 56 * 1024 * 1024


def _cparams(sem, vmem=VMEM_LIMIT):
    if sem is None:
        return pltpu.CompilerParams(vmem_limit_bytes=vmem)
    return pltpu.CompilerParams(dimension_semantics=sem, vmem_limit_bytes=vmem)


def _bdot(a, b):
    return jnp.dot(a.astype(BF16), b.astype(BF16), preferred_element_type=F32)


def _split_bf16(x):
    hi = x.astype(BF16)
    return hi, (x - hi.astype(F32)).astype(BF16)


def _silu(x):
    return x * jax.nn.sigmoid(x)


def _rms(x, g):
    return x * lax.rsqrt(jnp.mean(x * x, axis=-1, keepdims=True) + EPS) * g


def _log_sigmoid(x):
    return jnp.minimum(x, 0.0) - jnp.log1p(jnp.exp(-jnp.abs(x)))


def _conv3(u, w, n_rows, seq_len):
    row = lax.broadcasted_iota(jnp.int32, (n_rows, 1), 0) % seq_len
    prev = jnp.where(row == 0, 0.0, pltpu.roll(u, 1, 0))
    nxt = jnp.where(row == seq_len - 1, 0.0, pltpu.roll(u, n_rows - 1, 0))
    return prev * w[0:1] + u * w[1:2] + nxt * w[2:3]


def _ada_kernel(c_ref, w_ref, b_ref, o_ref):
    o_ref[0] = _bdot(_silu(c_ref[...]), w_ref[0]) + b_ref[0]


def _ada(cc, w_ada, b_ada):
    tn = 1536
    return pl.pallas_call(
        _ada_kernel,
        grid=(DEPTH, 6 * D // tn),
        in_specs=[pl.BlockSpec((8, D), lambda l, j: (0, 0)),
                  pl.BlockSpec((1, D, tn), lambda l, j: (l, 0, j)),
                  pl.BlockSpec((1, 1, tn), lambda l, j: (l, 0, j))],
        out_specs=pl.BlockSpec((1, 8, tn), lambda l, j: (l, 0, j)),
        out_shape=jax.ShapeDtypeStruct((DEPTH, 8, 6 * D), F32),
        compiler_params=_cparams(("arbitrary", "arbitrary")),
        name="ada_mod",
    )(cc, w_ada, b_ada.reshape(DEPTH, 1, 6 * D))


def _mod_row(i, tm):
    n_ctx = T_CTX // tm
    return jnp.where(i < n_ctx, 0, 1 + (i - n_ctx) // (L_LAT // tm))


_SEG = ((OFF_HY, OFF_QK - OFF_HY), (OFF_QK, OFF_V - OFF_QK), (OFF_V, OFF_O - OFF_V), (OFF_O, OFF_G - OFF_O))
TAIL_W = IN_W - OFF_G


N_CTX_TILES = T_CTX // TM
_NT = (((1,), (1,)), ((), ()))


def _x_specs(x):
    if len(x) == 1:
        return [pl.BlockSpec((TM, D), lambda i, *_: (i, 0))]
    per_seq = L_LAT // TM
    return [pl.BlockSpec((TM, D), lambda i, *_: (jnp.minimum(i, N_CTX_TILES - 1), 0)),
            pl.BlockSpec((TM, D), lambda i, *_: (jnp.maximum(i - N_CTX_TILES, 0), 0)),
            pl.BlockSpec((TM, D), lambda i, *_: (jnp.maximum(i - N_CTX_TILES, 0) % per_seq, 0))]


def _x_tile(step, x_refs):
    if len(x_refs) == 1:
        return x_refs[0][...]
    xc_ref, xl_ref, pos_ref = x_refs
    return jnp.where(step < N_CTX_TILES, xc_ref[...], xl_ref[...] + pos_ref[...])


def _inproj_kernel(*refs, n_x):
    x_refs = refs[:n_x]
    g_ref, mod_ref, w_ref, hy_ref, qk_ref, v_ref, o_ref, s5_ref, gt_ref, gtt_ref, wb = refs[n_x:]
    step = pl.program_id(0)

    @pl.when(step == 0)
    def _():
        wb[...] = w_ref[0].astype(BF16)

    mod = mod_ref[0, 0]
    h = _rms(_x_tile(step, x_refs), g_ref[0]) * (1.0 + mod[1:2]) + mod[0:1]
    hb = h.astype(BF16)
    for (a, w), ref in zip(_SEG, (hy_ref, qk_ref, v_ref, o_ref)):
        ref[...] = lax.dot_general(hb, wb[a:a + w, :], _NT, preferred_element_type=F32)
    tail = lax.dot_general(hb, wb[OFF_G:IN_W, :], _NT, preferred_element_type=F32)
    gates = tail[:, 0:GATE_PAD]
    gt_ref[...] = gates
    for c in range(TM // ML_CHUNK):
        gtt_ref[c] = gates[c * ML_CHUNK:(c + 1) * ML_CHUNK, :].T[0:16, :]
    s5_ref[...] = tail[:, OFF_S5 - OFF_G:TAIL_W]


def _layer_spec(shape, l):
    return pl.BlockSpec((1,) + tuple(shape), lambda *_: (l,) + (0,) * len(shape))


def _mod_spec(l, tm):
    return pl.BlockSpec((1, 1, 6, D), lambda i, *_: (l, _mod_row(i, tm), 0, 0))


def _inproj(x, g, mod, w_in_t, l):
    widths = [w for _, w in _SEG] + [S5_CH, GATE_PAD]
    cpt = TM // ML_CHUNK
    return pl.pallas_call(
        functools.partial(_inproj_kernel, n_x=len(x)),
        grid=(T_ALL // TM,),
        in_specs=_x_specs(x) + [
            _layer_spec((1, D), l), _mod_spec(l, TM),
            pl.BlockSpec((1, IN_W, D), lambda i: (l, 0, 0), pipeline_mode=pl.Buffered(1))],
        out_specs=[pl.BlockSpec((TM, w), lambda i: (i, 0)) for w in widths]
        + [pl.BlockSpec((cpt, 16, ML_CHUNK), lambda i: (i, 0, 0))],
        out_shape=[jax.ShapeDtypeStruct((T_ALL, w), F32) for w in widths]
        + [jax.ShapeDtypeStruct((T_ALL // ML_CHUNK, 16, ML_CHUNK), F32)],
        scratch_shapes=[pltpu.VMEM((IN_W, D), BF16)],
        compiler_params=_cparams(("arbitrary",)),
        name="norm_inproj",
    )(*x, g, mod, w_in_t)


@functools.lru_cache(None)
def _dft_mats(L):
    n = 2 * L
    k = np.arange(L)[:, None]
    t = np.arange(L)[None, :]
    ang = 2.0 * np.pi * ((k * t) % n) / n
    top = np.cos(ang)
    bot = -np.sin(ang)
    bot[0] = np.cos(np.pi * np.arange(L))
    fwd = np.concatenate([top, bot], 0)
    s = np.full((n, 1), 2.0 / n)
    s[0] = s[L] = 1.0 / n
    inv = (fwd * s).T
    return fwd.astype(np.float32), inv.astype(np.float32)


@functools.lru_cache(None)
def _hy_positions(L):
    t = np.linspace(0.0, 1.0, L)
    bands = (HY_EMB - 1) // 2
    f = np.linspace(1e-4, bands - 1, bands)
    w = 2.0 * np.pi * np.arange(L) / L
    ang = w[:, None] * f[None, :]
    z = np.concatenate([t[:, None], np.cos(ang), -np.sin(ang)], -1)
    zp = np.zeros((L, LANES))
    zp[:, :HY_EMB] = z
    return zp.astype(np.float32), t[:, None].astype(np.float32)


def _hy_filter_kernel(z_ref, t_ref, w1_ref, b1_ref, w2_ref, b2_ref, w3_ref, ld_ref, f_ref,
                      p_ref, q_ref, r_ref, *, L):
    h = jnp.sin(jnp.dot(z_ref[...], w1_ref[0], precision=HIGHEST, preferred_element_type=F32) + b1_ref[0])
    h = jnp.sin(jnp.dot(h, w2_ref[0], precision=HIGHEST, preferred_element_type=F32) + b2_ref[0])
    filt = jnp.dot(h, w3_ref[0], precision=HIGHEST, preferred_element_type=F32)
    filt = filt * jnp.exp(-t_ref[...] * jnp.exp(ld_ref[0]))
    c = HY_CH
    h_fwd = jnp.concatenate([filt[:, 0:c], filt[:, 2 * c:3 * c]], axis=1)
    h_bwd = jnp.concatenate([filt[:, c:2 * c], filt[:, 3 * c:4 * c]], axis=1)
    row = lax.broadcasted_iota(jnp.int32, (L, 1), 0)
    h_bwd = jnp.where(row == 0, 0.0, h_bwd)
    a = jnp.dot(f_ref[...], h_fwd.astype(BF16), preferred_element_type=F32)
    b = jnp.dot(f_ref[...], h_bwd.astype(BF16), preferred_element_type=F32)
    re = a[:L] + b[:L]
    im = a[L:] - b[L:]
    nyq = a[L:L + 1] + b[L:L + 1]
    p_ref[0] = re
    q_ref[0] = jnp.where(row == 0, 0.0, im)
    r_ref[0] = jnp.where(row == 0, nyq, re)


def _hy_filter(L, w1p, b1, w2, b2, w3, ld):
    z, t = _hy_positions(L)
    fwd = jnp.asarray(_dft_mats(L)[0]).astype(BF16)
    out = jax.ShapeDtypeStruct((DEPTH, L, 2 * HY_CH), F32)
    full = lambda a: pl.BlockSpec(a.shape, lambda l: (0,) * a.ndim)
    layer = lambda a: pl.BlockSpec((1,) + a.shape[1:], lambda l: (l,) + (0,) * (a.ndim - 1))
    return pl.pallas_call(
        functools.partial(_hy_filter_kernel, L=L),
        grid=(DEPTH,),
        in_specs=[full(z), full(t), layer(w1p), layer(b1), layer(w2), layer(b2), layer(w3), layer(ld), full(fwd)],
        out_specs=[pl.BlockSpec((1, L, 2 * HY_CH), lambda l: (l, 0, 0))] * 3,
        out_shape=[out, out, out],
        compiler_params=_cparams(("arbitrary",)),
        name=f"hyena_filter_{L}",
    )(z, t, w1p, b1, w2, b2, w3, ld, fwd)


def _hyena_kernel(u_ref, sw_ref, bias_ref, p_ref, q_ref, r_ref, f_ref, g_ref, o_ref, *, L):
    c = HY_CH
    u = _conv3(u_ref[...], sw_ref[0], L, L)
    z = u[:, 0:c]
    for o in range(2):
        gate = u[:, (o + 1) * c:(o + 2) * c]
        zf = jnp.dot(f_ref[...], z.astype(BF16), preferred_element_type=F32)
        a, b = zf[:L], zf[L:]
        p = p_ref[0, :, o * c:(o + 1) * c]
        q = q_ref[0, :, o * c:(o + 1) * c]
        r = r_ref[0, :, o * c:(o + 1) * c]
        y_re = (a * p - b * q).astype(BF16)
        y_im = (a * q + b * r).astype(BF16)
        y = (jnp.dot(g_ref[:, :L], y_re, preferred_element_type=F32)
             + jnp.dot(g_ref[:, L:], y_im, preferred_element_type=F32))
        z = gate * (y + bias_ref[0, o:o + 1, :] * z)
    o_ref[...] = z


def _hyena(u_hy, L, n_seq, row_block0, sw, bias, p, q, r, l):
    fwd, inv = (jnp.asarray(m).astype(BF16) for m in _dft_mats(L))
    full = lambda a: pl.BlockSpec(a.shape, lambda b: (0,) * a.ndim)
    layer = lambda a: _layer_spec(a.shape[1:], l)
    return pl.pallas_call(
        functools.partial(_hyena_kernel, L=L),
        grid=(n_seq,),
        in_specs=[pl.BlockSpec((L, 3 * HY_CH), lambda b: (row_block0 + b, 0)),
                  layer(sw), layer(bias), layer(p), layer(q), layer(r), full(fwd), full(inv)],
        out_specs=pl.BlockSpec((L, HY_CH), lambda b: (b, 0)),
        out_shape=jax.ShapeDtypeStruct((n_seq * L, HY_CH), F32),
        compiler_params=_cparams(("arbitrary",)),
        name=f"hyena_{L}",
    )(u_hy, sw, bias, p, q, r, fwd, inv)


def _cumsum_rows(x, n, reverse):
    row = lax.broadcasted_iota(jnp.int32, (n, 1), 0)
    s = 1
    while s < n:
        if reverse:
            x = x + jnp.where(row < n - s, pltpu.roll(x, n - s, 0), 0.0)
        else:
            x = x + jnp.where(row >= s, pltpu.roll(x, s, 0), 0.0)
        s *= 2
    return x


def _cumsum_lanes(x, n, reverse):
    col = lax.broadcasted_iota(jnp.int32, (1, n), 1)
    s = 1
    while s < n:
        if reverse:
            x = x + jnp.where(col < n - s, pltpu.roll(x, n - s, 1), 0.0)
        else:
            x = x + jnp.where(col >= s, pltpu.roll(x, s, 1), 0.0)
        s *= 2
    return x


def _mlstm_kernel(*refs, L, has_state):
    if has_state:
        (qk_ref, v_ref, o_ref, g_ref, gt_ref, sw_ref, gb_ref, gbt_ref, ng_ref, c0_ref, n0_ref, m0_ref,
         y_ref, cout_ref, nout_ref, mout_ref, q_s, k_s, vt_s, ht_s, ct_s, n_s, m_s) = refs
    else:
        (qk_ref, v_ref, o_ref, g_ref, gt_ref, sw_ref, gb_ref, gbt_ref, ng_ref,
         y_ref, cout_ref, nout_ref, mout_ref, q_s, k_s, vt_s, ht_s, ct_s, n_s, m_s) = refs
    tc = ML_CHUNK
    nc = L // tc
    nh = ML_HEADS
    dh = ML_DH

    qk = _silu(_conv3(qk_ref[...], sw_ref[0], L, L))
    q_s[...] = qk[:, :ML_W].astype(BF16)
    k_s[...] = (qk[:, ML_W:] * (dh ** -0.5)).astype(BF16)
    for c in range(nc):
        vt_s[c] = v_ref[c * tc:(c + 1) * tc, :].T.astype(BF16)

    for i in range(2 * nh):
        ct_s[i] = c0_ref[0, 0, i].T if has_state else jnp.zeros((dh, dh), F32)
    n_s[...] = n0_ref[0, 0] if has_state else jnp.zeros_like(n_s)
    m_s[...] = m0_ref[0, 0] if has_state else jnp.zeros_like(m_s)

    si = lax.broadcasted_iota(jnp.int32, (tc, tc), 0)
    ti = lax.broadcasted_iota(jnp.int32, (tc, tc), 1)

    for d in range(2):
        rev = d == 1
        mask = (si >= ti) if rev else (si <= ti)
        edge = 0 if rev else tc - 1

        def chunk(j, carry, d=d, rev=rev, mask=mask, edge=edge):
            cidx = (nc - 1 - j) if rev else j
            r0 = pl.multiple_of(cidx * tc, tc)
            pre = g_ref[pl.ds(r0, tc), :] + gb_ref[0]
            pre_t = gt_ref[cidx] + gbt_ref[0]
            cum = _cumsum_rows(_log_sigmoid(pre), tc, rev)
            cum_t = _cumsum_lanes(_log_sigmoid(pre_t), tc, rev)
            key_all = cum - pltpu.roll(pre, 8, 1)
            for h in range(nh):
                col = d * nh + h
                hs = slice(h * dh, (h + 1) * dh)
                key = key_all[:, 8 + col:9 + col]
                b_row = cum_t[8 + col:9 + col, :]
                b_end = b_row[:, edge:edge + 1]
                m_prev = m_s[col:col + 1, 0:1]
                dmat = jnp.where(mask, b_row - key, -jnp.inf)
                inter = b_row + m_prev
                m_row = jnp.maximum(inter, jnp.max(dmat, axis=0, keepdims=True))
                w_intra = jnp.exp(dmat - m_row)
                w_state = jnp.exp(inter - m_row)
                qh = q_s[pl.ds(r0, tc), hs]
                kh = k_s[pl.ds(r0, tc), hs]
                vt = vt_s[cidx, hs, :]
                ct_prev = ct_s[col]
                n_prev = n_s[col:col + 1, :]
                s = lax.dot_general(kh, qh, _NT, preferred_element_type=F32) * w_intra
                num = (jnp.dot(vt, s.astype(BF16), preferred_element_type=F32)
                       + w_state * lax.dot_general(ct_prev.astype(BF16), qh, _NT, preferred_element_type=F32))
                qn = lax.dot_general(jnp.broadcast_to(n_prev, (8, dh)).astype(BF16), qh, _NT,
                                     preferred_element_type=F32)[0:1]
                den = jnp.sum(s, axis=0, keepdims=True) + w_state * qn
                hout = num * (1.0 / jnp.maximum(jnp.abs(den), jnp.exp(-m_row)))
                if d == 0:
                    ht_s[cidx, hs, :] = hout
                else:
                    ht_s[cidx, hs, :] += hout
                m_new = jnp.maximum(b_end + m_prev, b_end - jnp.min(key, axis=0, keepdims=True))
                wg = jnp.exp(b_end - key - m_new)
                decay = jnp.exp(b_end + m_prev - m_new)
                kw = kh.astype(F32) * wg
                ct_s[col] = decay * ct_prev + jnp.dot(vt, kw.astype(BF16), preferred_element_type=F32)
                n_s[col:col + 1, :] = decay * n_prev + jnp.sum(kw, axis=0, keepdims=True)
                m_s[col:col + 1, :] = jnp.broadcast_to(m_new, (1, LANES))
            return carry

        lax.fori_loop(0, nc, chunk, 0)

    for c in range(nc):
        for h in range(nh):
            hs = slice(h * dh, (h + 1) * dh)
            rs = slice(c * tc, (c + 1) * tc)
            ht = ht_s[c, hs, :]
            hn = ht * lax.rsqrt(jnp.mean(ht * ht, axis=0, keepdims=True) + EPS)
            y_ref[rs, hs] = jax.nn.sigmoid(o_ref[rs, hs]) * (hn.T * ng_ref[0, :, hs])
    for i in range(2 * nh):
        cout_ref[0, i] = ct_s[i].T
    nout_ref[0] = n_s[...]
    mout_ref[0] = m_s[...]


def _mlstm(qk, v, o, gates, gates_t, L, n_seq, row_block0, sw, gb, gbt, ng, state, l):
    nc = L // ML_CHUNK
    has_state = state is not None
    layer = lambda a: _layer_spec(a.shape[1:], l)
    in_specs = [pl.BlockSpec((L, 2 * ML_W), lambda b: (row_block0 + b, 0)),
                pl.BlockSpec((L, ML_W), lambda b: (row_block0 + b, 0)),
                pl.BlockSpec((L, ML_W), lambda b: (row_block0 + b, 0)),
                pl.BlockSpec((L, GATE_PAD), lambda b: (row_block0 + b, 0)),
                pl.BlockSpec((nc, 16, ML_CHUNK), lambda b: (row_block0 + b, 0, 0)),
                layer(sw), layer(gb), layer(gbt), layer(ng)]
    args = [qk, v, o, gates, gates_t, sw, gb, gbt, ng]
    if has_state:
        c0, n0, m0 = state
        in_specs += [pl.BlockSpec((1, 1, 2 * ML_HEADS, ML_DH, ML_DH), lambda b: (b, l, 0, 0, 0)),
                     pl.BlockSpec((1, 1, 2 * ML_HEADS, ML_DH), lambda b: (b, l, 0, 0)),
                     pl.BlockSpec((1, 1, 2 * ML_HEADS, LANES), lambda b: (b, l, 0, 0))]
        args += [c0, n0, m0]
    return pl.pallas_call(
        functools.partial(_mlstm_kernel, L=L, has_state=has_state),
        grid=(n_seq,),
        in_specs=in_specs,
        out_specs=[pl.BlockSpec((L, ML_W), lambda b: (b, 0)),
                   pl.BlockSpec((1, 2 * ML_HEADS, ML_DH, ML_DH), lambda b: (b, 0, 0, 0)),
                   pl.BlockSpec((1, 2 * ML_HEADS, ML_DH), lambda b: (b, 0, 0)),
                   pl.BlockSpec((1, 2 * ML_HEADS, LANES), lambda b: (b, 0, 0))],
        out_shape=[jax.ShapeDtypeStruct((n_seq * L, ML_W), F32),
                   jax.ShapeDtypeStruct((n_seq, 2 * ML_HEADS, ML_DH, ML_DH), F32),
                   jax.ShapeDtypeStruct((n_seq, 2 * ML_HEADS, ML_DH), F32),
                   jax.ShapeDtypeStruct((n_seq, 2 * ML_HEADS, LANES), F32)],
        scratch_shapes=[pltpu.VMEM((L, ML_W), BF16), pltpu.VMEM((L, ML_W), BF16),
                        pltpu.VMEM((nc, ML_W, ML_CHUNK), BF16),
                        pltpu.VMEM((nc, ML_W, ML_CHUNK), F32),
                        pltpu.VMEM((2 * ML_HEADS, ML_DH, ML_DH), F32),
                        pltpu.VMEM((2 * ML_HEADS, ML_DH), F32),
                        pltpu.VMEM((2 * ML_HEADS, LANES), F32)],
        compiler_params=_cparams(("arbitrary",)),
        name=f"mlstm_{L}",
    )(*args)


def _cmul(ar, ai, br, bi):
    return ar * br - ai * bi, ar * bi + ai * br


def _s5_kernel(*refs, segmented):
    if segmented:
        (u_ref, bb_ref, cc_ref, eb_ref, ec_ref, lam_ref, dsk_ref, wglu_ref, s0_ref, y_ref,
         sbuf, yacc, bmat, cmat, pw) = refs
    else:
        (u_ref, bb_ref, cc_ref, eb_ref, ec_ref, lam_ref, dsk_ref, wglu_ref, y_ref, fin_ref,
         sbuf, yacc, bmat, cmat) = refs
    n = S5_STATE
    rows = S5_ROWS
    steps = S5_STEPS
    blk = 256
    n_blk = steps * rows // blk
    n_seg = 4

    yacc[...] = u_ref[0] * dsk_ref[0]
    ub = u_ref[0].astype(BF16)
    b_keep = (lax.broadcasted_iota(jnp.int32, (S5_CH, 2 * n), 0) // S5_GROUP
              == (lax.broadcasted_iota(jnp.int32, (S5_CH, 2 * n), 1) % n) // S5_P)
    c_keep = ((lax.broadcasted_iota(jnp.int32, (2 * n, S5_CH), 0) % n) // S5_P
              == lax.broadcasted_iota(jnp.int32, (2 * n, S5_CH), 1) // S5_GROUP)

    for d in range(2):
        rev = d == 1
        bmat[...] = jnp.where(b_keep, jnp.dot(bb_ref[0, d].astype(BF16), eb_ref[...], preferred_element_type=F32),
                              0.0).astype(BF16)
        cmat[...] = jnp.where(c_keep, jnp.dot(cc_ref[0, d].astype(BF16), ec_ref[...], preferred_element_type=F32),
                              0.0).astype(BF16)
        for i in range(n_blk):
            sbuf[i * blk:(i + 1) * blk, :] = jnp.dot(ub[i * blk:(i + 1) * blk], bmat[...],
                                                     preferred_element_type=F32)
        lam = lam_ref[0, d]
        lr = jnp.broadcast_to(lam[:, :n], (rows, n))
        li = jnp.broadcast_to(lam[:, n:], (rows, n))

        def step(i, carry, rev=rev, lr=lr, li=li):
            sr, si = carry
            t = (steps - 1 - i) if rev else i
            off = pl.multiple_of(t * rows, rows)
            pr, pi = _cmul(lr, li, sr, si)
            nr = pr + sbuf[pl.ds(off, rows), 0:n]
            ni = pi + sbuf[pl.ds(off, rows), n:2 * n]
            sbuf[pl.ds(off, rows), 0:n] = nr
            sbuf[pl.ds(off, rows), n:2 * n] = ni
            return nr, ni

        zero = jnp.zeros((rows, n), F32)
        sr, si = lax.fori_loop(0, steps, step, (zero, zero), unroll=2)

        if not segmented:
            fin_ref[0, d, :, 0:n] = sr
            fin_ref[0, d, :, n:2 * n] = si
        else:
            lam_r, lam_i = lam[:, :n], lam[:, n:]
            row8 = lax.broadcasted_iota(jnp.int32, (rows, 1), 0)
            cr, ci = lam_r, lam_i
            acc_r = jnp.broadcast_to(cr, (rows, n))
            acc_i = jnp.broadcast_to(ci, (rows, n))
            for j in range(1, rows):
                cr, ci = _cmul(cr, ci, lam_r, lam_i)
                acc_r = jnp.where(row8 >= j, jnp.broadcast_to(cr, (rows, n)), acc_r)
                acc_i = jnp.where(row8 >= j, jnp.broadcast_to(ci, (rows, n)), acc_i)
            pw[0:rows, 0:n] = acc_r
            pw[0:rows, n:2 * n] = acc_i
            size = rows
            while size < steps:
                tr = pw[size - 1:size, 0:n]
                ti = pw[size - 1:size, n:2 * n]
                xr, xi = _cmul(pw[0:size, 0:n], pw[0:size, n:2 * n], tr, ti)
                pw[size:2 * size, 0:n] = xr
                pw[size:2 * size, n:2 * n] = xi
                size *= 2
            end_off = 0 if rev else (steps - 1) * rows
            loc_r = sbuf[end_off:end_off + rows, 0:n]
            loc_i = sbuf[end_off:end_off + rows, n:2 * n]
            pl_r = pw[steps - 1:steps, 0:n]
            pl_i = pw[steps - 1:steps, n:2 * n]
            s0r = s0_ref[0, d, :, 0:n]
            s0i = s0_ref[0, d, :, n:2 * n]
            seg = row8 // 2
            first = (seg == n_seg - 1) if rev else (seg == 0)
            shift = (rows - 2) if rev else 2
            cin_r, cin_i = s0r, s0i
            for _ in range(n_seg - 1):
                fr, fi = _cmul(jnp.broadcast_to(pl_r, (rows, n)), jnp.broadcast_to(pl_i, (rows, n)), cin_r, cin_i)
                tru_r = loc_r + fr
                tru_i = loc_i + fi
                cin_r = jnp.where(first, s0r, pltpu.roll(tru_r, shift, 0))
                cin_i = jnp.where(first, s0i, pltpu.roll(tru_i, shift, 0))

            def fix(tb, carry, rev=rev, cin_r=cin_r, cin_i=cin_i):
                pb = (steps // rows - 1 - tb) if rev else tb
                poff = pl.multiple_of(pb * rows, rows)
                p_r = pw[pl.ds(poff, rows), 0:n]
                p_i = pw[pl.ds(poff, rows), n:2 * n]
                for j in range(rows):
                    jj = rows - 1 - j if rev else j
                    off = pl.multiple_of((tb * rows + j) * rows, rows)
                    fr, fi = _cmul(jnp.broadcast_to(p_r[jj:jj + 1], (rows, n)),
                                   jnp.broadcast_to(p_i[jj:jj + 1], (rows, n)), cin_r, cin_i)
                    sbuf[pl.ds(off, rows), 0:n] += fr
                    sbuf[pl.ds(off, rows), n:2 * n] += fi
                return carry

            lax.fori_loop(0, steps // rows, fix, 0)

        for i in range(n_blk):
            yacc[i * blk:(i + 1) * blk, :] += jnp.dot(sbuf[i * blk:(i + 1) * blk, :].astype(BF16), cmat[...],
                                                      preferred_element_type=F32)

    g = jax.nn.gelu(yacc[...], approximate=True)
    y_ref[0] = g * jax.nn.sigmoid(_bdot(g, wglu_ref[0]))


@functools.lru_cache(None)
def _s5_spread():
    eb = np.zeros((2 * S5_P, 2 * S5_STATE), np.float32)
    for half in range(2):
        for g in range(S5_G):
            c0 = half * S5_STATE + g * S5_P
            eb[half * S5_P:(half + 1) * S5_P, c0:c0 + S5_P] = np.eye(S5_P)
    ec = np.zeros((LANES, S5_CH), np.float32)
    for g in range(S5_G):
        ec[:S5_GROUP, g * S5_GROUP:(g + 1) * S5_GROUP] = np.eye(S5_GROUP)
    return eb, ec


def _s5(u_tm, bb, cc, lam, dskip, wglu, s0, l):
    n_grp = u_tm.shape[0]
    n_rows = S5_STEPS * S5_ROWS
    segmented = s0 is not None
    eb, ec = (jnp.asarray(m).astype(BF16) for m in _s5_spread())
    full = lambda a: pl.BlockSpec(a.shape, lambda g: (0,) * a.ndim)
    layer = lambda a: _layer_spec(a.shape[1:], l)
    in_specs = [pl.BlockSpec((1, n_rows, S5_CH), lambda g: (g, 0, 0)),
                layer(bb), layer(cc), full(eb), full(ec), layer(lam), layer(dskip), layer(wglu)]
    args = [u_tm, bb, cc, eb, ec, lam, dskip, wglu]
    out_specs = [pl.BlockSpec((1, n_rows, S5_CH), lambda g: (g, 0, 0))]
    out_shape = [jax.ShapeDtypeStruct((n_grp, n_rows, S5_CH), F32)]
    scratch = [pltpu.VMEM((n_rows, 2 * S5_STATE), F32), pltpu.VMEM((n_rows, S5_CH), F32),
               pltpu.VMEM((S5_CH, 2 * S5_STATE), BF16), pltpu.VMEM((2 * S5_STATE, S5_CH), BF16)]
    if segmented:
        in_specs.append(layer(s0))
        args.append(s0)
        scratch.append(pltpu.VMEM((S5_STEPS, 2 * S5_STATE), F32))
    else:
        out_specs.append(pl.BlockSpec((1, 2, S5_ROWS, 2 * S5_STATE), lambda g: (g, 0, 0, 0)))
        out_shape.append(jax.ShapeDtypeStruct((n_grp, 2, S5_ROWS, 2 * S5_STATE), F32))
    return pl.pallas_call(
        functools.partial(_s5_kernel, segmented=segmented),
        grid=(n_grp,),
        in_specs=in_specs,
        out_specs=out_specs,
        out_shape=out_shape,
        scratch_shapes=scratch,
        compiler_params=_cparams(("arbitrary",)),
        name="s5_seg" if segmented else "s5_ctx",
    )(*args)


def _s5_params(a_re, a_im, log_dt, b_re, b_im, c_re, c_im):
    dt = jnp.exp(log_dt)[..., None]
    mag = jnp.exp(a_re * dt)
    lb_re = mag * jnp.cos(a_im * dt)
    lb_im = mag * jnp.sin(a_im * dt)
    den = a_re * a_re + a_im * a_im
    nr, ni = lb_re - 1.0, lb_im
    k_re = (nr * a_re + ni * a_im) / den
    k_im = (ni * a_re - nr * a_im) / den
    bb_re = k_re[..., None] * b_re - k_im[..., None] * b_im
    bb_im = k_re[..., None] * b_im + k_im[..., None] * b_re
    to_gc_p = lambda m: m.transpose(0, 1, 2, 4, 3).reshape(DEPTH, 2, S5_CH, S5_P)
    bb = jnp.concatenate([to_gc_p(bb_re), to_gc_p(bb_im)], axis=3)
    to_gp_c = lambda m: m.transpose(0, 1, 2, 4, 3).reshape(DEPTH, 2, S5_STATE, S5_GROUP)
    cc = jnp.concatenate([to_gp_c(c_re), -to_gp_c(c_im)], axis=2)
    cc = jnp.pad(cc, ((0, 0), (0, 0), (0, 0), (0, LANES - S5_GROUP)))
    lam = jnp.concatenate([lb_re.reshape(DEPTH, 2, 1, S5_STATE), lb_im.reshape(DEPTH, 2, 1, S5_STATE)], axis=3)
    return bb, cc, lam


def _outproj_kernel(*refs, n_x):
    x_refs = refs[:n_x]
    (hyc_ref, hyl_ref, mlc_ref, mll_ref, s5c_ref, s5l_ref, w_ref, mod_ref, g_ref,
     rw_ref, rb_ref, xn_ref, h2e_ref, best_ref, rank_ref, cnt_ref, wb, cnt_s) = refs[n_x:]
    step = pl.program_id(0)

    @pl.when(step == 0)
    def _():
        wb[...] = w_ref[0].astype(BF16)
        cnt_s[...] = jnp.zeros_like(cnt_s)

    is_ctx = step < T_CTX // TM
    pick = lambda c_ref, l_ref: jnp.where(is_ctx, c_ref[...], l_ref[...]).astype(BF16)
    mod = mod_ref[0, 0]
    a, b = HY_CH, HY_CH + ML_W
    mix = (jnp.dot(pick(hyc_ref, hyl_ref), wb[0:a, :], preferred_element_type=F32)
           + jnp.dot(pick(mlc_ref, mll_ref), wb[a:b, :], preferred_element_type=F32)
           + jnp.dot(pick(s5c_ref, s5l_ref), wb[b:, :], preferred_element_type=F32))
    xn = _x_tile(step, x_refs) + mod[2:3] * mix
    xn_ref[...] = xn
    h2 = _rms(xn, g_ref[0]) * (1.0 + mod[4:5]) + mod[3:4]
    h2e_ref[:, 0:D] = h2
    h_hi, h_lo = _split_bf16(h2)
    r_hi, r_lo = _split_bf16(rw_ref[...])
    logits = (lax.dot_general(r_hi, h_hi, _NT, preferred_element_type=F32)
              + (lax.dot_general(r_hi, h_lo, _NT, preferred_element_type=F32)
                 + lax.dot_general(r_lo, h_hi, _NT, preferred_element_type=F32)))
    ex = jnp.exp(logits - jnp.max(logits, axis=0, keepdims=True))
    probs = ex / jnp.sum(ex, axis=0, keepdims=True)
    sel = probs + rb_ref[...]
    best = None
    best_score = None
    for g in range(N_GROUPS):
        r = [sel[g * GROUP_SIZE + i:g * GROUP_SIZE + i + 1, :] for i in range(GROUP_SIZE)]
        score = None
        for i in range(GROUP_SIZE):
            for j in range(i + 1, GROUP_SIZE):
                pair = r[i] + r[j]
                score = pair if score is None else jnp.maximum(score, pair)
        if g == 0:
            best, best_score = jnp.zeros_like(score, dtype=jnp.int32), score
        else:
            upd = score > best_score
            best = jnp.where(upd, g, best)
            best_score = jnp.where(upd, score, best_score)
    eid = lax.broadcasted_iota(jnp.int32, (N_EXPERTS, 1), 0)
    masked = jnp.where(eid // GROUP_SIZE == best, sel, -jnp.inf)
    m1 = jnp.max(masked, axis=0, keepdims=True)
    i1 = jnp.min(jnp.where(masked == m1, eid, N_EXPERTS), axis=0, keepdims=True)
    masked2 = jnp.where(eid == i1, -jnp.inf, masked)
    m2 = jnp.max(masked2, axis=0, keepdims=True)
    i2 = jnp.min(jnp.where(masked2 == m2, eid, N_EXPERTS), axis=0, keepdims=True)
    p1 = jnp.sum(jnp.where(eid == i1, probs, 0.0), axis=0, keepdims=True)
    p2 = jnp.sum(jnp.where(eid == i2, probs, 0.0), axis=0, keepdims=True)
    tot = p1 + p2
    comb = jnp.where(eid == i1, p1 / tot, 0.0) + jnp.where(eid == i2, p2 / tot, 0.0)
    comb = jnp.concatenate([comb, jnp.zeros((LANES - N_EXPERTS, comb.shape[1]), F32)], axis=0)
    h2e_ref[:, D:] = comb.T
    best_ref[...] = best
    gid = lax.broadcasted_iota(jnp.int32, (8, 1), 0)
    onehot = (gid == best).astype(F32)
    cum = _cumsum_lanes(onehot, TM, False)
    run = cnt_s[:, 0:1]
    rank_ref[...] = jnp.sum(onehot * (cum - onehot + run), axis=0, keepdims=True).astype(jnp.int32)
    cnt_s[...] = jnp.broadcast_to(run + cum[:, TM - 1:TM], cnt_s.shape)
    cnt_ref[...] = cnt_s[...]


def _outproj(x, y_hy, y_ml, y_s5, w_out, l, mod, g2, rw_t, rb):
    full = lambda a: pl.BlockSpec(a.shape, lambda i: (0,) * a.ndim)
    n_ctx = T_CTX // TM
    ctx = lambda w: pl.BlockSpec((TM, w), lambda i: (jnp.minimum(i, n_ctx - 1), 0))
    lat = lambda w: pl.BlockSpec((TM, w), lambda i: (jnp.maximum(i - n_ctx, 0), 0))
    tok = lambda w: pl.BlockSpec((TM, w), lambda i: (i, 0))
    row = pl.BlockSpec((1, TM), lambda i: (0, i))
    return pl.pallas_call(
        functools.partial(_outproj_kernel, n_x=len(x)),
        grid=(T_ALL // TM,),
        in_specs=_x_specs(x) + [ctx(HY_CH), lat(HY_CH), ctx(ML_W), lat(ML_W), ctx(S5_CH), lat(S5_CH),
                  pl.BlockSpec((1, D, D), lambda i: (l, 0, 0), pipeline_mode=pl.Buffered(1)),
                  _mod_spec(l, TM), _layer_spec((1, D), l), full(rw_t), full(rb)],
        out_specs=[tok(D), tok(D + LANES), row, row, pl.BlockSpec((8, LANES), lambda i: (0, 0))],
        out_shape=[jax.ShapeDtypeStruct((T_ALL, D), F32),
                   jax.ShapeDtypeStruct((T_ALL, D + LANES), F32),
                   jax.ShapeDtypeStruct((1, T_ALL), jnp.int32),
                   jax.ShapeDtypeStruct((1, T_ALL), jnp.int32),
                   jax.ShapeDtypeStruct((8, LANES), F32)],
        scratch_shapes=[pltpu.VMEM((D, D), BF16), pltpu.VMEM((8, LANES), F32)],
        compiler_params=_cparams(("arbitrary",)),
        name="outproj_router",
    )(*x, *y_hy, *y_ml, *y_s5, w_out, mod, g2, rw_t, rb)


def _gather_rows(idx_ref, idx_base, src_ref, dst_ref, n_rows):
    def body(r8, carry):
        base = pl.multiple_of(r8 * 8, 8)
        for k in range(8):
            idx = idx_ref[idx_base + base + k]
            dst_ref[pl.ds(base + k, 1), :] = src_ref[pl.ds(idx, 1), :]
        return carry

    lax.fori_loop(0, n_rows // 8, body, 0)


def _dispatch(best, rank, cnt):
    tm = TM_MOE
    g = best.reshape(T_ALL)
    cnt = cnt[:N_GROUPS, 0].astype(jnp.int32)
    n_tile_g = (cnt + tm - 1) // tm
    tile_end = jnp.cumsum(n_tile_g)
    row_off = (tile_end - n_tile_g) * tm
    pos = rank.reshape(T_ALL)
    for k in range(N_GROUPS):
        pos = pos + jnp.where(g == k, row_off[k], 0)
    src = jnp.zeros((MOE_SLOTS,), jnp.int32).at[pos].set(jnp.arange(T_ALL, dtype=jnp.int32))
    tiles = jnp.arange(MOE_SLOTS // tm, dtype=jnp.int32)
    tile_group = jnp.minimum(jnp.sum((tiles[:, None] >= tile_end[None, :]).astype(jnp.int32), axis=1), N_GROUPS - 1)
    return pos, src, tile_group.astype(jnp.int32), tile_end[N_GROUPS - 1:].astype(jnp.int32)


def _moe_kernel(src_ref, tg_ref, nt_ref, h_ref, wg_ref, wu_ref, wd_ref, ys_ref, g_a, x_a, g_b, x_b, acc):
    i = pl.program_id(0)
    j = pl.program_id(1)
    tm = TM_MOE
    quarter = tm // GROUP_SIZE
    n_live = nt_ref[0]

    @pl.when(jnp.logical_and(i == 0, j == 0))
    def _():
        _gather_rows(src_ref, 0, h_ref, g_a, tm)
        x_a[...] = g_a[:, 0:D].astype(BF16)

    def step(cur_g, cur_x, nxt_g, nxt_x):
        @pl.when(j == 0)
        def _():
            acc[...] = jnp.zeros_like(acc)

        e = tg_ref[i] * GROUP_SIZE + j
        lane = lax.broadcasted_iota(jnp.int32, (1, LANES), 1)
        wg = wg_ref[0, 0].astype(BF16)
        wu = wu_ref[0, 0].astype(BF16)
        wd = wd_ref[0, 0].astype(BF16)
        sub = MOE_SUB
        for s in range(tm // sub):
            rs = slice(s * sub, (s + 1) * sub)
            hb = cur_x[rs, :]
            ce = jnp.sum(jnp.where(lane == e, cur_g[rs, D:], 0.0), axis=1, keepdims=True)
            hid = _silu(jnp.dot(hb, wg, preferred_element_type=F32)) * jnp.dot(hb, wu, preferred_element_type=F32)
            acc[rs, :] += jnp.dot((hid * ce).astype(BF16), wd, preferred_element_type=F32)

        r0 = pl.multiple_of(j * quarter, quarter)
        base = jnp.minimum(i + 1, n_live - 1) * tm + r0
        for k in range(quarter):
            nxt_g[pl.ds(r0 + k, 1), :] = h_ref[pl.ds(src_ref[base + k], 1), :]
        nxt_x[pl.ds(r0, quarter), :] = nxt_g[pl.ds(r0, quarter), 0:D].astype(BF16)

        @pl.when(j == GROUP_SIZE - 1)
        def _():
            ys_ref[...] = acc[...]

    live = i < n_live

    @pl.when(jnp.logical_and(live, i % 2 == 0))
    def _():
        step(g_a, x_a, g_b, x_b)

    @pl.when(jnp.logical_and(live, i % 2 == 1))
    def _():
        step(g_b, x_b, g_a, x_a)


def _moe(h2e, src, tile_group, n_tiles, wg, wu, wd, l):
    tm = TM_MOE

    def w_map(i, j, src, tg, nt):
        live = i < nt[0]
        ii = jnp.minimum(i, nt[0] - 1)
        return (l, tg[ii] * GROUP_SIZE + jnp.where(live, j, GROUP_SIZE - 1), 0, 0)

    return pl.pallas_call(
        _moe_kernel,
        grid_spec=pltpu.PrefetchScalarGridSpec(
            num_scalar_prefetch=3,
            grid=(MOE_SLOTS // tm, GROUP_SIZE),
            in_specs=[pl.BlockSpec(memory_space=pltpu.VMEM),
                      pl.BlockSpec((1, 1, D, D_EXPERT), w_map),
                      pl.BlockSpec((1, 1, D, D_EXPERT), w_map),
                      pl.BlockSpec((1, 1, D_EXPERT, D), w_map)],
            out_specs=pl.BlockSpec((tm, D), lambda i, j, src, tg, nt: (jnp.minimum(i, nt[0] - 1), 0)),
            scratch_shapes=[pltpu.VMEM((tm, D + LANES), F32), pltpu.VMEM((tm, D), BF16),
                            pltpu.VMEM((tm, D + LANES), F32), pltpu.VMEM((tm, D), BF16),
                            pltpu.VMEM((tm, D), F32)]),
        out_shape=jax.ShapeDtypeStruct((MOE_SLOTS, D), F32),
        compiler_params=_cparams(("arbitrary", "arbitrary")),
        name="moe_experts",
    )(src, tile_group, n_tiles, h2e, wg, wu, wd)


def _combine_kernel(pos_ref, ys_ref, xn_ref, mod_ref, fg_ref, *rest, final):
    step = pl.program_id(0)
    if final:
        yc_ref, yl_ref, gbuf = rest
    else:
        out_ref, gbuf = rest
    _gather_rows(pos_ref, step * TM, ys_ref, gbuf, TM)
    out = xn_ref[...] + mod_ref[0, 0][5:6] * gbuf[...]
    if final:
        y = _rms(out, fg_ref[...])

        @pl.when(step < T_CTX // TM)
        def _():
            yc_ref[...] = y

        @pl.when(step >= T_CTX // TM)
        def _():
            yl_ref[...] = y
    else:
        out_ref[...] = out


def _combine(pos, ys, xn, mod, fg, l, final):
    spec = pl.BlockSpec((TM, D), lambda i, pos: (i, 0))
    n_ctx = T_CTX // TM
    if final:
        out_specs = [pl.BlockSpec((TM, D), lambda i, pos: (jnp.minimum(i, n_ctx - 1), 0)),
                     pl.BlockSpec((TM, D), lambda i, pos: (jnp.maximum(i - n_ctx, 0), 0))]
        out_shape = [jax.ShapeDtypeStruct((T_CTX, D), F32), jax.ShapeDtypeStruct((T_LAT, D), F32)]
    else:
        out_specs = [spec]
        out_shape = [jax.ShapeDtypeStruct((T_ALL, D), F32)]
    return pl.pallas_call(
        functools.partial(_combine_kernel, final=final),
        grid_spec=pltpu.PrefetchScalarGridSpec(
            num_scalar_prefetch=1,
            grid=(T_ALL // TM,),
            in_specs=[pl.BlockSpec(memory_space=pltpu.VMEM),
                      spec,
                      _mod_spec(l, TM),
                      pl.BlockSpec((1, D), lambda i, pos: (0, 0))],
            out_specs=out_specs,
            scratch_shapes=[pltpu.VMEM((TM, D), F32)]),
        out_shape=out_shape,
        compiler_params=_cparams(("arbitrary",)),
        name="moe_combine",
    )(pos, ys, xn, mod, fg)


@functools.lru_cache(None)
def _pos_embed():
    rows = L_LAT // GRID_W
    r = np.repeat(np.arange(rows, dtype=np.float64), GRID_W)
    col = np.tile(np.arange(GRID_W, dtype=np.float64), rows)
    quarter = D // 4
    freq = np.exp(-math.log(POS_BASE) * np.arange(quarter, dtype=np.float64) / quarter)
    ar = r[:, None] * freq[None]
    ac = col[:, None] * freq[None]
    emb = np.concatenate([np.sin(ar), np.cos(ar), np.sin(ac), np.cos(ac)], axis=-1)
    return emb.astype(np.float32)


def _to_time_major_ctx(a):
    c = a.shape[-1]
    a = a.reshape(2, S5_ROWS, L_CTX, c).transpose(0, 2, 1, 3)
    return a.reshape(2, L_CTX * S5_ROWS, c)


def _from_time_major_ctx(a):
    c = a.shape[-1]
    a = a.reshape(2, L_CTX, S5_ROWS, c).transpose(0, 2, 1, 3)
    return a.reshape(T_CTX, c)


def _to_time_major_lat(a):
    c = a.shape[-1]
    a = a.reshape(N_LAT_SEQ, 4, S5_STEPS, c).transpose(2, 1, 0, 3)
    return a.reshape(1, S5_STEPS * S5_ROWS, c)


def _from_time_major_lat(a):
    c = a.shape[-1]
    a = a.reshape(S5_STEPS, 4, N_LAT_SEQ, c).transpose(2, 1, 0, 3)
    return a.reshape(T_LAT, c)


def kernel(x_prompt, x_sample, c, state_mlstm_C, state_mlstm_n, state_mlstm_m, state_s5_re, state_s5_im, c_ctx, w_ada, b_ada, norm1_g, norm2_g, final_g, w_in, w_out, hy_short, hy_fw1, hy_fb1, hy_fw2, hy_fb2, hy_fw3, hy_log_decay, hy_bias, ml_short, ml_gate_bias, ml_norm_g, s5_a_re, s5_a_im, s5_log_dt, s5_b_re, s5_b_im, s5_c_re, s5_c_im, s5_d, s5_w_glu, router_w, router_b, moe_w_gate, moe_w_up, moe_w_down):
    x = (x_prompt.reshape(T_CTX, D), x_sample.reshape(T_LAT, D), jnp.asarray(_pos_embed()))
    w_in_t = jnp.swapaxes(w_in, 1, 2)
    cc =jnp.concatenate([c_ctx[None], c, jnp.zeros((8 - 1 - N_LAT_SEQ, D), F32)], axis=0)
    mod = _ada(cc, w_ada, b_ada).reshape(DEPTH, 8, 6, D)
    rw_t = router_w.T
    rb = router_b.reshape(N_EXPERTS, 1)
    fg = final_g.reshape(1, D)
    lat_blk = T_CTX // L_LAT
    g1 = norm1_g.reshape(DEPTH, 1, D)
    g2 = norm2_g.reshape(DEPTH, 1, D)

    w1p = jnp.pad(hy_fw1, ((0, 0), (0, LANES - HY_EMB), (0, 0)))
    b1 = hy_fb1.reshape(DEPTH, 1, HY_FILTER_W)
    b2 = hy_fb2.reshape(DEPTH, 1, HY_FILTER_W)
    ld = hy_log_decay.reshape(DEPTH, 1, 4 * HY_CH)
    hy_spec = {L: _hy_filter(L, w1p, b1, hy_fw2, b2, hy_fw3, ld) for L in (L_CTX, L_LAT)}
    gb = jnp.pad(ml_gate_bias.reshape(DEPTH, 1, 16), ((0, 0), (0, 0), (0, GATE_PAD - 16)))
    gbt = ml_gate_bias.reshape(DEPTH, 16, 1)
    ng = ml_norm_g.reshape(DEPTH, 1, ML_W)
    ml_state = (state_mlstm_C.reshape(N_LAT_SEQ, DEPTH, 2 * ML_HEADS, ML_DH, ML_DH),
                state_mlstm_n.reshape(N_LAT_SEQ, DEPTH, 2 * ML_HEADS, ML_DH),
                jnp.broadcast_to(state_mlstm_m.reshape(N_LAT_SEQ, DEPTH, 2 * ML_HEADS, 1),
                                 (N_LAT_SEQ, DEPTH, 2 * ML_HEADS, LANES)))
    bb, cc_s5, lam = _s5_params(s5_a_re, s5_a_im, s5_log_dt, s5_b_re, s5_b_im, s5_c_re, s5_c_im)
    dsk = s5_d.reshape(DEPTH, 1, S5_CH)
    wglu = s5_w_glu.astype(BF16)
    s0 = jnp.concatenate([state_s5_re.reshape(N_LAT_SEQ, DEPTH, 2, S5_STATE),
                          state_s5_im.reshape(N_LAT_SEQ, DEPTH, 2, S5_STATE)], axis=-1)
    s0 = jnp.tile(s0.transpose(1, 2, 0, 3), (1, 1, 4, 1))

    new_c, new_n, new_m, new_re, new_im = [], [], [], [], []
    y_prompt = y_sample = None
    for l in range(DEPTH):
        u_hy, qk, v, o, u_s5, gates, gates_t = _inproj(x, g1, mod, w_in_t, l)

        y_hy = [_hyena(u_hy, L, n_seq, blk0, hy_short, hy_bias, *hy_spec[L], l)
                for L, n_seq, blk0 in ((L_CTX, N_CTX_SEQ, 0), (L_LAT, N_LAT_SEQ, lat_blk))]

        yc, cc_, nc_, mc_ = _mlstm(qk, v, o, gates, gates_t, L_CTX, N_CTX_SEQ, 0, ml_short, gb, gbt, ng, None, l)
        yl, _, _, _ = _mlstm(qk, v, o, gates, gates_t, L_LAT, N_LAT_SEQ, lat_blk, ml_short, gb, gbt, ng,
                             ml_state, l)
        y_ml = (yc, yl)
        new_c.append(cc_.reshape(N_CTX_SEQ, 2, ML_HEADS, ML_DH, ML_DH))
        new_n.append(nc_.reshape(N_CTX_SEQ, 2, ML_HEADS, ML_DH))
        new_m.append(mc_[:, :, 0].reshape(N_CTX_SEQ, 2, ML_HEADS))

        ys_c, fin = _s5(_to_time_major_ctx(u_s5[:T_CTX]), bb, cc_s5, lam, dsk, wglu, None, l)
        (ys_l,) = _s5(_to_time_major_lat(u_s5[T_CTX:]), bb, cc_s5, lam, dsk, wglu, s0, l)
        y_s5 = (_from_time_major_ctx(ys_c), _from_time_major_lat(ys_l))
        fin = fin.transpose(0, 2, 1, 3).reshape(N_CTX_SEQ, 2, 2 * S5_STATE)
        new_re.append(fin[..., :S5_STATE].reshape(N_CTX_SEQ, 2, S5_G, S5_P))
        new_im.append(fin[..., S5_STATE:].reshape(N_CTX_SEQ, 2, S5_G, S5_P))

        xn, h2e, best, rank, cnt = _outproj(x, y_hy, y_ml, y_s5, w_out, l, mod, g2, rw_t, rb)
        pos, src, tile_group, n_tiles = _dispatch(best, rank, cnt)
        ys = _moe(h2e, src, tile_group, n_tiles, moe_w_gate, moe_w_up, moe_w_down, l)
        res = _combine(pos, ys, xn, mod, fg, l, l == DEPTH - 1)
        if l == DEPTH - 1:
            y_prompt = res[0].reshape(N_CTX_SEQ, L_CTX, D)
            y_sample = res[1].reshape(N_LAT_SEQ, L_LAT, D)
        else:
            x = (res[0],)

    return (y_prompt, y_sample, jnp.stack(new_c, axis=1), jnp.stack(new_n, axis=1), jnp.stack(new_m, axis=1),
            jnp.stack(new_re, axis=1), jnp.stack(new_im, axis=1))
```

```python
import functools
import math

import numpy as np
import jax
import jax.numpy as jnp
from jax import lax
from jax.experimental import pallas as pl
from jax.experimental.pallas import tpu as pltpu

F32 = jnp.float32
BF16 = jnp.bfloat16
HIGHEST = lax.Precision.HIGHEST

D = 1024
N_CTX_SEQ, L_CTX = 16, 256
N_LAT_SEQ, L_LAT = 2, 1024
T_CTX = N_CTX_SEQ * L_CTX
T_LAT = N_LAT_SEQ * L_LAT
T_ALL = T_CTX + T_LAT
DEPTH = 2
EPS = 1e-6
GRID_W = 64
POS_BASE = 10000.0
HY_CH = 256
HY_EMB = 33
HY_FILTER_W = 64
ML_HEADS = 4
ML_DH = 128
ML_W = ML_HEADS * ML_DH
S5_CH = 256
S5_G = 16
S5_GROUP = 16
S5_P = 64
S5_STATE = S5_G * S5_P
N_EXPERTS = 16
N_GROUPS = 4
GROUP_SIZE = N_EXPERTS // N_GROUPS
D_EXPERT = 512
OFF_HY = 0
OFF_QK = 3 * HY_CH
OFF_V = OFF_QK + 2 * ML_W
OFF_O = OFF_V + ML_W
OFF_G = OFF_O + ML_W
OFF_S5 = OFF_G + 16
IN_W = OFF_S5 + S5_CH
LANES = 128
GATE_PAD = LANES

TM = 512
TM_MOE = 512
MOE_SLOTS = T_ALL + N_GROUPS * TM_MOE
MOE_SUB = 512
HY_ROWS = 1024
ML_CHUNK = 256
S5_ROWS = 8
S5_STEPS = 256
VMEM_LIMIT = 56 * 1024 * 1024


def _cparams(sem, vmem=VMEM_LIMIT):
    if sem is None:
        return pltpu.CompilerParams(vmem_limit_bytes=vmem)
    return pltpu.CompilerParams(dimension_semantics=sem, vmem_limit_bytes=vmem)


def _bdot(a, b):
    return jnp.dot(a.astype(BF16), b.astype(BF16), preferred_element_type=F32)


def _split_bf16(x):
    hi = x.astype(BF16)
    return hi, (x - hi.astype(F32)).astype(BF16)


def _silu(x):
    return x * jax.nn.sigmoid(x)


def _rms(x, g):
    return x * lax.rsqrt(jnp.mean(x * x, axis=-1, keepdims=True) + EPS) * g


def _log_sigmoid(x):
    return jnp.minimum(x, 0.0) - jnp.log1p(jnp.exp(-jnp.abs(x)))


def _conv3(u, w, n_rows, seq_len):
    row = lax.broadcasted_iota(jnp.int32, (n_rows, 1), 0) % seq_len
    prev = jnp.where(row == 0, 0.0, pltpu.roll(u, 1, 0))
    nxt = jnp.where(row == seq_len - 1, 0.0, pltpu.roll(u, n_rows - 1, 0))
    return prev * w[0:1] + u * w[1:2] + nxt * w[2:3]


def _ada_kernel(c_ref, w_ref, b_ref, o_ref):
    o_ref[0] = _bdot(_silu(c_ref[...]), w_ref[0]) + b_ref[0]


def _ada(cc, w_ada, b_ada):
    tn = 1536
    return pl.pallas_call(
        _ada_kernel,
        grid=(DEPTH, 6 * D // tn),
        in_specs=[pl.BlockSpec((8, D), lambda l, j: (0, 0)),
                  pl.BlockSpec((1, D, tn), lambda l, j: (l, 0, j)),
                  pl.BlockSpec((1, 1, tn), lambda l, j: (l, 0, j))],
        out_specs=pl.BlockSpec((1, 8, tn), lambda l, j: (l, 0, j)),
        out_shape=jax.ShapeDtypeStruct((DEPTH, 8, 6 * D), F32),
        compiler_params=_cparams(("arbitrary", "arbitrary")),
        name="ada_mod",
    )(cc, w_ada, b_ada.reshape(DEPTH, 1, 6 * D))


def _mod_row(i, tm):
    n_ctx = T_CTX // tm
    return jnp.where(i < n_ctx, 0, 1 + (i - n_ctx) // (L_LAT // tm))


_SEG = ((OFF_HY, OFF_QK - OFF_HY), (OFF_QK, OFF_V - OFF_QK), (OFF_V, OFF_O - OFF_V), (OFF_O, OFF_G - OFF_O))
TAIL_W = IN_W - OFF_G


N_CTX_TILES = T_CTX // TM
_NT = (((1,), (1,)), ((), ()))


def _x_specs(x):
    if len(x) == 1:
        return [pl.BlockSpec((TM, D), lambda i, *_: (i, 0))]
    per_seq = L_LAT // TM
    return [pl.BlockSpec((TM, D), lambda i, *_: (jnp.minimum(i, N_CTX_TILES - 1), 0)),
            pl.BlockSpec((TM, D), lambda i, *_: (jnp.maximum(i - N_CTX_TILES, 0), 0)),
            pl.BlockSpec((TM, D), lambda i, *_: (jnp.maximum(i - N_CTX_TILES, 0) % per_seq, 0))]


def _x_tile(step, x_refs):
    if len(x_refs) == 1:
        return x_refs[0][...]
    xc_ref, xl_ref, pos_ref = x_refs
    return jnp.where(step < N_CTX_TILES, xc_ref[...], xl_ref[...] + pos_ref[...])


def _inproj_kernel(*refs, n_x):
    x_refs = refs[:n_x]
    g_ref, mod_ref, w_ref, hy_ref, qk_ref, v_ref, o_ref, s5_ref, gt_ref, gtt_ref, wb = refs[n_x:]
    step = pl.program_id(0)

    @pl.when(step == 0)
    def _():
        wb[...] = w_ref[0].astype(BF16)

    mod = mod_ref[0, 0]
    h = _rms(_x_tile(step, x_refs), g_ref[0]) * (1.0 + mod[1:2]) + mod[0:1]
    hb = h.astype(BF16)
    for (a, w), ref in zip(_SEG, (hy_ref, qk_ref, v_ref, o_ref)):
        ref[...] = lax.dot_general(hb, wb[a:a + w, :], _NT, preferred_element_type=F32)
    tail = lax.dot_general(hb, wb[OFF_G:IN_W, :], _NT, preferred_element_type=F32)
    gates = tail[:, 0:GATE_PAD]
    gt_ref[...] = gates
    for c in range(TM // ML_CHUNK):
        gtt_ref[c] = gates[c * ML_CHUNK:(c + 1) * ML_CHUNK, :].T[0:16, :]
    s5_ref[...] = tail[:, OFF_S5 - OFF_G:TAIL_W]


def _layer_spec(shape, l):
    return pl.BlockSpec((1,) + tuple(shape), lambda *_: (l,) + (0,) * len(shape))


def _mod_spec(l, tm):
    return pl.BlockSpec((1, 1, 6, D), lambda i, *_: (l, _mod_row(i, tm), 0, 0))


def _inproj(x, g, mod, w_in_t, l):
    widths = [w for _, w in _SEG] + [S5_CH, GATE_PAD]
    cpt = TM // ML_CHUNK
    return pl.pallas_call(
        functools.partial(_inproj_kernel, n_x=len(x)),
        grid=(T_ALL // TM,),
        in_specs=_x_specs(x) + [
            _layer_spec((1, D), l), _mod_spec(l, TM),
            pl.BlockSpec((1, IN_W, D), lambda i: (l, 0, 0), pipeline_mode=pl.Buffered(1))],
        out_specs=[pl.BlockSpec((TM, w), lambda i: (i, 0)) for w in widths]
        + [pl.BlockSpec((cpt, 16, ML_CHUNK), lambda i: (i, 0, 0))],
        out_shape=[jax.ShapeDtypeStruct((T_ALL, w), F32) for w in widths]
        + [jax.ShapeDtypeStruct((T_ALL // ML_CHUNK, 16, ML_CHUNK), F32)],
        scratch_shapes=[pltpu.VMEM((IN_W, D), BF16)],
        compiler_params=_cparams(("arbitrary",)),
        name="norm_inproj",
    )(*x, g, mod, w_in_t)


@functools.lru_cache(None)
def _dft_mats(L):
    n = 2 * L
    k = np.arange(L)[:, None]
    t = np.arange(L)[None, :]
    ang = 2.0 * np.pi * ((k * t) % n) / n
    top = np.cos(ang)
    bot = -np.sin(ang)
    bot[0] = np.cos(np.pi * np.arange(L))
    fwd = np.concatenate([top, bot], 0)
    s = np.full((n, 1), 2.0 / n)
    s[0] = s[L] = 1.0 / n
    inv = (fwd * s).T
    return fwd.astype(np.float32), inv.astype(np.float32)


@functools.lru_cache(None)
def _hy_positions(L):
    t = np.linspace(0.0, 1.0, L)
    bands = (HY_EMB - 1) // 2
    f = np.linspace(1e-4, bands - 1, bands)
    w = 2.0 * np.pi * np.arange(L) / L
    ang = w[:, None] * f[None, :]
    z = np.concatenate([t[:, None], np.cos(ang), -np.sin(ang)], -1)
    zp = np.zeros((L, LANES))
    zp[:, :HY_EMB] = z
    return zp.astype(np.float32), t[:, None].astype(np.float32)


def _hy_filter_kernel(z_ref, t_ref, w1_ref, b1_ref, w2_ref, b2_ref, w3_ref, ld_ref, f_ref,
                      p_ref, q_ref, r_ref, *, L):
    h = jnp.sin(jnp.dot(z_ref[...], w1_ref[0], precision=HIGHEST, preferred_element_type=F32) + b1_ref[0])
    h = jnp.sin(jnp.dot(h, w2_ref[0], precision=HIGHEST, preferred_element_type=F32) + b2_ref[0])
    filt = jnp.dot(h, w3_ref[0], precision=HIGHEST, preferred_element_type=F32)
    filt = filt * jnp.exp(-t_ref[...] * jnp.exp(ld_ref[0]))
    c = HY_CH
    h_fwd = jnp.concatenate([filt[:, 0:c], filt[:, 2 * c:3 * c]], axis=1)
    h_bwd = jnp.concatenate([filt[:, c:2 * c], filt[:, 3 * c:4 * c]], axis=1)
    row = lax.broadcasted_iota(jnp.int32, (L, 1), 0)
    h_bwd = jnp.where(row == 0, 0.0, h_bwd)
    a = jnp.dot(f_ref[...], h_fwd.astype(BF16), preferred_element_type=F32)
    b = jnp.dot(f_ref[...], h_bwd.astype(BF16), preferred_element_type=F32)
    re = a[:L] + b[:L]
    im = a[L:] - b[L:]
    nyq = a[L:L + 1] + b[L:L + 1]
    p_ref[0] = re
    q_ref[0] = jnp.where(row == 0, 0.0, im)
    r_ref[0] = jnp.where(row == 0, nyq, re)


def _hy_filter(L, w1p, b1, w2, b2, w3, ld):
    z, t = _hy_positions(L)
    fwd = jnp.asarray(_dft_mats(L)[0]).astype(BF16)
    out = jax.ShapeDtypeStruct((DEPTH, L, 2 * HY_CH), F32)
    full = lambda a: pl.BlockSpec(a.shape, lambda l: (0,) * a.ndim)
    layer = lambda a: pl.BlockSpec((1,) + a.shape[1:], lambda l: (l,) + (0,) * (a.ndim - 1))
    return pl.pallas_call(
        functools.partial(_hy_filter_kernel, L=L),
        grid=(DEPTH,),
        in_specs=[full(z), full(t), layer(w1p), layer(b1), layer(w2), layer(b2), layer(w3), layer(ld), full(fwd)],
        out_specs=[pl.BlockSpec((1, L, 2 * HY_CH), lambda l: (l, 0, 0))] * 3,
        out_shape=[out, out, out],
        compiler_params=_cparams(("arbitrary",)),
        name=f"hyena_filter_{L}",
    )(z, t, w1p, b1, w2, b2, w3, ld, fwd)


def _hyena_kernel(u_ref, sw_ref, bias_ref, p_ref, q_ref, r_ref, f_ref, g_ref, o_ref, *, L, n_sub):
    c = HY_CH
    u = _conv3(u_ref[...], sw_ref[0], n_sub * L, L)
    for s in range(n_sub):
        rs = slice(s * L, (s + 1) * L)
        z = u[rs, 0:c]
        for o in range(2):
            gate = u[rs, (o + 1) * c:(o + 2) * c]
            zf = jnp.dot(f_ref[...], z.astype(BF16), preferred_element_type=F32)
            a, b = zf[:L], zf[L:]
            p = p_ref[0, :, o * c:(o + 1) * c]
            q = q_ref[0, :, o * c:(o + 1) * c]
            r = r_ref[0, :, o * c:(o + 1) * c]
            y_re = (a * p - b * q).astype(BF16)
            y_im = (a * q + b * r).astype(BF16)
            y = (jnp.dot(g_ref[:, :L], y_re, preferred_element_type=F32)
                 + jnp.dot(g_ref[:, L:], y_im, preferred_element_type=F32))
            z = gate * (y + bias_ref[0, o:o + 1, :] * z)
        o_ref[rs, :] = z


def _hyena(u_hy, L, n_seq, row_block0, sw, bias, p, q, r, l):
    n_sub = max(1, HY_ROWS // L)
    fwd, inv = (jnp.asarray(m).astype(BF16) for m in _dft_mats(L))
    full = lambda a: pl.BlockSpec(a.shape, lambda b: (0,) * a.ndim)
    layer = lambda a: _layer_spec(a.shape[1:], l)
    blk0 = row_block0 // n_sub
    return pl.pallas_call(
        functools.partial(_hyena_kernel, L=L, n_sub=n_sub),
        grid=(n_seq // n_sub,),
        in_specs=[pl.BlockSpec((n_sub * L, 3 * HY_CH), lambda b: (blk0 + b, 0)),
                  layer(sw), layer(bias), layer(p), layer(q), layer(r), full(fwd), full(inv)],
        out_specs=pl.BlockSpec((n_sub * L, HY_CH), lambda b: (b, 0)),
        out_shape=jax.ShapeDtypeStruct((n_seq * L, HY_CH), F32),
        compiler_params=_cparams(("arbitrary",)),
        name=f"hyena_{L}",
    )(u_hy, sw, bias, p, q, r, fwd, inv)


def _cumsum_rows(x, n, reverse):
    row = lax.broadcasted_iota(jnp.int32, (n, 1), 0)
    s = 1
    while s < n:
        if reverse:
            x = x + jnp.where(row < n - s, pltpu.roll(x, n - s, 0), 0.0)
        else:
            x = x + jnp.where(row >= s, pltpu.roll(x, s, 0), 0.0)
        s *= 2
    return x


def _cumsum_lanes(x, n, reverse):
    col = lax.broadcasted_iota(jnp.int32, (1, n), 1)
    s = 1
    while s < n:
        if reverse:
            x = x + jnp.where(col < n - s, pltpu.roll(x, n - s, 1), 0.0)
        else:
            x = x + jnp.where(col >= s, pltpu.roll(x, s, 1), 0.0)
        s *= 2
    return x


def _mlstm_kernel(*refs, L, has_state, emit_state):
    qk_ref, v_ref, o_ref, g_ref, gt_ref, sw_ref, gb_ref, gbt_ref, ng_ref = refs[:9]
    refs = refs[9:]
    if has_state:
        c0_ref, n0_ref, m0_ref = refs[:3]
        refs = refs[3:]
    y_ref = refs[0]
    if emit_state:
        cout_ref, nout_ref, mout_ref = refs[1:4]
        refs = refs[4:]
    else:
        refs = refs[1:]
    q_s, k_s, vt_s, ht_s, ct_s, n_s, m_s = refs
    tc = ML_CHUNK
    nc = L // tc
    nh = ML_HEADS
    dh = ML_DH

    qk = _silu(_conv3(qk_ref[...], sw_ref[0], L, L))
    q_s[...] = qk[:, :ML_W].astype(BF16)
    k_s[...] = (qk[:, ML_W:] * (dh ** -0.5)).astype(BF16)
    for c in range(nc):
        vt_s[c] = v_ref[c * tc:(c + 1) * tc, :].T.astype(BF16)

    for i in range(2 * nh):
        ct_s[i] = c0_ref[0, 0, i].T if has_state else jnp.zeros((dh, dh), F32)
    n_s[...] = n0_ref[0, 0] if has_state else jnp.zeros_like(n_s)
    m_s[...] = m0_ref[0, 0] if has_state else jnp.zeros_like(m_s)

    si = lax.broadcasted_iota(jnp.int32, (tc, tc), 0)
    ti = lax.broadcasted_iota(jnp.int32, (tc, tc), 1)

    for d in range(2):
        rev = d == 1
        mask = (si >= ti) if rev else (si <= ti)
        edge = 0 if rev else tc - 1

        def chunk(j, carry, d=d, rev=rev, mask=mask, edge=edge):
            cidx = (nc - 1 - j) if rev else j
            r0 = pl.multiple_of(cidx * tc, tc)
            pre = g_ref[pl.ds(r0, tc), :] + gb_ref[0]
            pre_t = gt_ref[cidx] + gbt_ref[0]
            cum = _cumsum_rows(_log_sigmoid(pre), tc, rev)
            cum_t = _cumsum_lanes(_log_sigmoid(pre_t), tc, rev)
            key_all = cum - pltpu.roll(pre, 8, 1)
            for h in range(nh):
                col = d * nh + h
                hs = slice(h * dh, (h + 1) * dh)
                key = key_all[:, 8 + col:9 + col]
                b_row = cum_t[8 + col:9 + col, :]
                b_end = b_row[:, edge:edge + 1]
                m_prev = m_s[col:col + 1, 0:1]
                dmat = jnp.where(mask, b_row - key, -jnp.inf)
                inter = b_row + m_prev
                m_row = jnp.maximum(inter, jnp.max(dmat, axis=0, keepdims=True))
                w_intra = jnp.exp(dmat - m_row)
                w_state = jnp.exp(inter - m_row)
                qh = q_s[pl.ds(r0, tc), hs]
                kh = k_s[pl.ds(r0, tc), hs]
                vt = vt_s[cidx, hs, :]
                ct_prev = ct_s[col]
                n_prev = n_s[col:col + 1, :]
                s = lax.dot_general(kh, qh, _NT, preferred_element_type=F32) * w_intra
                num = (jnp.dot(vt, s.astype(BF16), preferred_element_type=F32)
                       + w_state * lax.dot_general(ct_prev.astype(BF16), qh, _NT, preferred_element_type=F32))
                qn = lax.dot_general(jnp.broadcast_to(n_prev, (8, dh)).astype(BF16), qh, _NT,
                                     preferred_element_type=F32)[0:1]
                den = jnp.sum(s, axis=0, keepdims=True) + w_state * qn
                hout = num * (1.0 / jnp.maximum(jnp.abs(den), jnp.exp(-m_row)))
                if d == 0:
                    ht_s[cidx, hs, :] = hout
                else:
                    ht_s[cidx, hs, :] += hout
                m_new = jnp.maximum(b_end + m_prev, b_end - jnp.min(key, axis=0, keepdims=True))
                wg = jnp.exp(b_end - key - m_new)
                decay = jnp.exp(b_end + m_prev - m_new)
                kw = kh.astype(F32) * wg
                ct_s[col] = decay * ct_prev + jnp.dot(vt, kw.astype(BF16), preferred_element_type=F32)
                n_s[col:col + 1, :] = decay * n_prev + jnp.sum(kw, axis=0, keepdims=True)
                m_s[col:col + 1, :] = jnp.broadcast_to(m_new, (1, LANES))
            return carry

        lax.fori_loop(0, nc, chunk, 0)

    for c in range(nc):
        for h in range(nh):
            hs = slice(h * dh, (h + 1) * dh)
            rs = slice(c * tc, (c + 1) * tc)
            ht = ht_s[c, hs, :]
            hn = ht * lax.rsqrt(jnp.mean(ht * ht, axis=0, keepdims=True) + EPS)
            y_ref[rs, hs] = jax.nn.sigmoid(o_ref[rs, hs]) * (hn.T * ng_ref[0, :, hs])
    if emit_state:
        for i in range(2 * nh):
            cout_ref[0, i] = ct_s[i].T
        nout_ref[0] = n_s[...]
        mout_ref[0] = m_s[...]


def _mlstm(qk, v, o, gates, gates_t, L, n_seq, row_block0, sw, gb, gbt, ng, state, l, emit_state=False):
    nc = L // ML_CHUNK
    has_state = state is not None
    layer = lambda a: _layer_spec(a.shape[1:], l)
    in_specs = [pl.BlockSpec((L, 2 * ML_W), lambda b: (row_block0 + b, 0)),
                pl.BlockSpec((L, ML_W), lambda b: (row_block0 + b, 0)),
                pl.BlockSpec((L, ML_W), lambda b: (row_block0 + b, 0)),
                pl.BlockSpec((L, GATE_PAD), lambda b: (row_block0 + b, 0)),
                pl.BlockSpec((nc, 16, ML_CHUNK), lambda b: (row_block0 + b, 0, 0)),
                layer(sw), layer(gb), layer(gbt), layer(ng)]
    args = [qk, v, o, gates, gates_t, sw, gb, gbt, ng]
    if has_state:
        c0, n0, m0 = state
        in_specs += [pl.BlockSpec((1, 1, 2 * ML_HEADS, ML_DH, ML_DH), lambda b: (b, l, 0, 0, 0)),
                     pl.BlockSpec((1, 1, 2 * ML_HEADS, ML_DH), lambda b: (b, l, 0, 0)),
                     pl.BlockSpec((1, 1, 2 * ML_HEADS, LANES), lambda b: (b, l, 0, 0))]
        args += [c0, n0, m0]
    out_specs = [pl.BlockSpec((L, ML_W), lambda b: (b, 0))]
    out_shape = [jax.ShapeDtypeStruct((n_seq * L, ML_W), F32)]
    if emit_state:
        tails = ((2 * ML_HEADS, ML_DH, ML_DH), (2 * ML_HEADS, ML_DH), (2 * ML_HEADS, LANES))
        for t in tails:
            out_specs.append(pl.BlockSpec((1,) + t, lambda b, n=len(t): (b,) + (0,) * n))
            out_shape.append(jax.ShapeDtypeStruct((n_seq,) + t, F32))
    return pl.pallas_call(
        functools.partial(_mlstm_kernel, L=L, has_state=has_state, emit_state=emit_state),
        grid=(n_seq,),
        in_specs=in_specs,
        out_specs=out_specs,
        out_shape=out_shape,
        scratch_shapes=[pltpu.VMEM((L, ML_W), BF16), pltpu.VMEM((L, ML_W), BF16),
                        pltpu.VMEM((nc, ML_W, ML_CHUNK), BF16),
                        pltpu.VMEM((nc, ML_W, ML_CHUNK), F32),
                        pltpu.VMEM((2 * ML_HEADS, ML_DH, ML_DH), F32),
                        pltpu.VMEM((2 * ML_HEADS, ML_DH), F32),
                        pltpu.VMEM((2 * ML_HEADS, LANES), F32)],
        compiler_params=_cparams(("arbitrary",)),
        name=f"mlstm_{L}",
    )(*args)


def _cmul(ar, ai, br, bi):
    return ar * br - ai * bi, ar * bi + ai * br


def _s5_kernel(*refs, segmented):
    if segmented:
        (u_ref, bb_ref, cc_ref, eb_ref, ec_ref, lam_ref, dsk_ref, wglu_ref, s0_ref, y_ref,
         sbuf, yacc, bmat, cmat, pw) = refs
    else:
        (u_ref, bb_ref, cc_ref, eb_ref, ec_ref, lam_ref, dsk_ref, wglu_ref, y_ref, fin_ref,
         sbuf, yacc, bmat, cmat) = refs
    n = S5_STATE
    rows = S5_ROWS
    steps = S5_STEPS
    blk = 256
    n_blk = steps * rows // blk
    n_seg = 4

    yacc[...] = u_ref[0] * dsk_ref[0]
    ub = u_ref[0].astype(BF16)
    b_keep = (lax.broadcasted_iota(jnp.int32, (S5_CH, 2 * n), 0) // S5_GROUP
              == (lax.broadcasted_iota(jnp.int32, (S5_CH, 2 * n), 1) % n) // S5_P)
    c_keep = ((lax.broadcasted_iota(jnp.int32, (2 * n, S5_CH), 0) % n) // S5_P
              == lax.broadcasted_iota(jnp.int32, (2 * n, S5_CH), 1) // S5_GROUP)

    for d in range(2):
        rev = d == 1
        bmat[...] = jnp.where(b_keep, jnp.dot(bb_ref[0, d].astype(BF16), eb_ref[...], preferred_element_type=F32),
                              0.0).astype(BF16)
        cmat[...] = jnp.where(c_keep, jnp.dot(cc_ref[0, d].astype(BF16), ec_ref[...], preferred_element_type=F32),
                              0.0).astype(BF16)
        for i in range(n_blk):
            sbuf[i * blk:(i + 1) * blk, :] = jnp.dot(ub[i * blk:(i + 1) * blk], bmat[...],
                                                     preferred_element_type=F32)
        lam = lam_ref[0, d]
        lr = jnp.broadcast_to(lam[:, :n], (rows, n))
        li = jnp.broadcast_to(lam[:, n:], (rows, n))

        def step(i, carry, rev=rev, lr=lr, li=li):
            sr, si = carry
            t = (steps - 1 - i) if rev else i
            off = pl.multiple_of(t * rows, rows)
            pr, pi = _cmul(lr, li, sr, si)
            nr = pr + sbuf[pl.ds(off, rows), 0:n]
            ni = pi + sbuf[pl.ds(off, rows), n:2 * n]
            sbuf[pl.ds(off, rows), 0:n] = nr
            sbuf[pl.ds(off, rows), n:2 * n] = ni
            return nr, ni

        zero = jnp.zeros((rows, n), F32)
        sr, si = lax.fori_loop(0, steps, step, (zero, zero), unroll=2)

        if not segmented:
            fin_ref[0, d, :, 0:n] = sr
            fin_ref[0, d, :, n:2 * n] = si
        else:
            lam_r, lam_i = lam[:, :n], lam[:, n:]
            row8 = lax.broadcasted_iota(jnp.int32, (rows, 1), 0)
            cr, ci = lam_r, lam_i
            acc_r = jnp.broadcast_to(cr, (rows, n))
            acc_i = jnp.broadcast_to(ci, (rows, n))
            for j in range(1, rows):
                cr, ci = _cmul(cr, ci, lam_r, lam_i)
                acc_r = jnp.where(row8 >= j, jnp.broadcast_to(cr, (rows, n)), acc_r)
                acc_i = jnp.where(row8 >= j, jnp.broadcast_to(ci, (rows, n)), acc_i)
            pw[0:rows, 0:n] = acc_r
            pw[0:rows, n:2 * n] = acc_i
            size = rows
            while size < steps:
                tr = pw[size - 1:size, 0:n]
                ti = pw[size - 1:size, n:2 * n]
                xr, xi = _cmul(pw[0:size, 0:n], pw[0:size, n:2 * n], tr, ti)
                pw[size:2 * size, 0:n] = xr
                pw[size:2 * size, n:2 * n] = xi
                size *= 2
            end_off = 0 if rev else (steps - 1) * rows
            loc_r = sbuf[end_off:end_off + rows, 0:n]
            loc_i = sbuf[end_off:end_off + rows, n:2 * n]
            pl_r = pw[steps - 1:steps, 0:n]
            pl_i = pw[steps - 1:steps, n:2 * n]
            s0r = s0_ref[0, d, :, 0:n]
            s0i = s0_ref[0, d, :, n:2 * n]
            seg = row8 // 2
            first = (seg == n_seg - 1) if rev else (seg == 0)
            shift = (rows - 2) if rev else 2
            cin_r, cin_i = s0r, s0i
            for _ in range(n_seg - 1):
                fr, fi = _cmul(jnp.broadcast_to(pl_r, (rows, n)), jnp.broadcast_to(pl_i, (rows, n)), cin_r, cin_i)
                tru_r = loc_r + fr
                tru_i = loc_i + fi
                cin_r = jnp.where(first, s0r, pltpu.roll(tru_r, shift, 0))
                cin_i = jnp.where(first, s0i, pltpu.roll(tru_i, shift, 0))

            def fix(tb, carry, rev=rev, cin_r=cin_r, cin_i=cin_i):
                pb = (steps // rows - 1 - tb) if rev else tb
                poff = pl.multiple_of(pb * rows, rows)
                p_r = pw[pl.ds(poff, rows), 0:n]
                p_i = pw[pl.ds(poff, rows), n:2 * n]
                for j in range(rows):
                    jj = rows - 1 - j if rev else j
                    off = pl.multiple_of((tb * rows + j) * rows, rows)
                    fr, fi = _cmul(jnp.broadcast_to(p_r[jj:jj + 1], (rows, n)),
                                   jnp.broadcast_to(p_i[jj:jj + 1], (rows, n)), cin_r, cin_i)
                    sbuf[pl.ds(off, rows), 0:n] += fr
                    sbuf[pl.ds(off, rows), n:2 * n] += fi
                return carry

            lax.fori_loop(0, steps // rows, fix, 0)

        for i in range(n_blk):
            yacc[i * blk:(i + 1) * blk, :] += jnp.dot(sbuf[i * blk:(i + 1) * blk, :].astype(BF16), cmat[...],
                                                      preferred_element_type=F32)

    g = jax.nn.gelu(yacc[...], approximate=True)
    y_ref[0] = g * jax.nn.sigmoid(_bdot(g, wglu_ref[0]))


@functools.lru_cache(None)
def _s5_spread():
    eb = np.zeros((2 * S5_P, 2 * S5_STATE), np.float32)
    for half in range(2):
        for g in range(S5_G):
            c0 = half * S5_STATE + g * S5_P
            eb[half * S5_P:(half + 1) * S5_P, c0:c0 + S5_P] = np.eye(S5_P)
    ec = np.zeros((LANES, S5_CH), np.float32)
    for g in range(S5_G):
        ec[:S5_GROUP, g * S5_GROUP:(g + 1) * S5_GROUP] = np.eye(S5_GROUP)
    return eb, ec


def _s5(u_tm, bb, cc, lam, dskip, wglu, s0, l):
    n_grp = u_tm.shape[0]
    n_rows = S5_STEPS * S5_ROWS
    segmented = s0 is not None
    eb, ec = (jnp.asarray(m).astype(BF16) for m in _s5_spread())
    full = lambda a: pl.BlockSpec(a.shape, lambda g: (0,) * a.ndim)
    layer = lambda a: _layer_spec(a.shape[1:], l)
    in_specs = [pl.BlockSpec((1, n_rows, S5_CH), lambda g: (g, 0, 0)),
                layer(bb), layer(cc), full(eb), full(ec), layer(lam), layer(dskip), layer(wglu)]
    args = [u_tm, bb, cc, eb, ec, lam, dskip, wglu]
    out_specs = [pl.BlockSpec((1, n_rows, S5_CH), lambda g: (g, 0, 0))]
    out_shape = [jax.ShapeDtypeStruct((n_grp, n_rows, S5_CH), F32)]
    scratch = [pltpu.VMEM((n_rows, 2 * S5_STATE), F32), pltpu.VMEM((n_rows, S5_CH), F32),
               pltpu.VMEM((S5_CH, 2 * S5_STATE), BF16), pltpu.VMEM((2 * S5_STATE, S5_CH), BF16)]
    if segmented:
        in_specs.append(layer(s0))
        args.append(s0)
        scratch.append(pltpu.VMEM((S5_STEPS, 2 * S5_STATE), F32))
    else:
        out_specs.append(pl.BlockSpec((1, 2, S5_ROWS, 2 * S5_STATE), lambda g: (g, 0, 0, 0)))
        out_shape.append(jax.ShapeDtypeStruct((n_grp, 2, S5_ROWS, 2 * S5_STATE), F32))
    return pl.pallas_call(
        functools.partial(_s5_kernel, segmented=segmented),
        grid=(n_grp,),
        in_specs=in_specs,
        out_specs=out_specs,
        out_shape=out_shape,
        scratch_shapes=scratch,
        compiler_params=_cparams(("arbitrary",)),
        name="s5_seg" if segmented else "s5_ctx",
    )(*args)


def _s5_params(a_re, a_im, log_dt, b_re, b_im, c_re, c_im):
    dt = jnp.exp(log_dt)[..., None]
    mag = jnp.exp(a_re * dt)
    lb_re = mag * jnp.cos(a_im * dt)
    lb_im = mag * jnp.sin(a_im * dt)
    den = a_re * a_re + a_im * a_im
    nr, ni = lb_re - 1.0, lb_im
    k_re = (nr * a_re + ni * a_im) / den
    k_im = (ni * a_re - nr * a_im) / den
    bb_re = k_re[..., None] * b_re - k_im[..., None] * b_im
    bb_im = k_re[..., None] * b_im + k_im[..., None] * b_re
    to_gc_p = lambda m: m.transpose(0, 1, 2, 4, 3).reshape(DEPTH, 2, S5_CH, S5_P)
    bb = jnp.concatenate([to_gc_p(bb_re), to_gc_p(bb_im)], axis=3)
    to_gp_c = lambda m: m.transpose(0, 1, 2, 4, 3).reshape(DEPTH, 2, S5_STATE, S5_GROUP)
    cc = jnp.concatenate([to_gp_c(c_re), -to_gp_c(c_im)], axis=2)
    cc = jnp.pad(cc, ((0, 0), (0, 0), (0, 0), (0, LANES - S5_GROUP)))
    lam = jnp.concatenate([lb_re.reshape(DEPTH, 2, 1, S5_STATE), lb_im.reshape(DEPTH, 2, 1, S5_STATE)], axis=3)
    return bb, cc, lam


def _outproj_kernel(*refs, n_x):
    x_refs = refs[:n_x]
    (hyc_ref, hyl_ref, mlc_ref, mll_ref, s5c_ref, s5l_ref, w_ref, mod_ref, g_ref,
     rw_ref, rb_ref, xn_ref, h2e_ref, best_ref, rank_ref, cnt_ref, wb, cnt_s) = refs[n_x:]
    step = pl.program_id(0)

    @pl.when(step == 0)
    def _():
        wb[...] = w_ref[0].astype(BF16)
        cnt_s[...] = jnp.zeros_like(cnt_s)

    is_ctx = step < T_CTX // TM
    pick = lambda c_ref, l_ref: jnp.where(is_ctx, c_ref[...], l_ref[...]).astype(BF16)
    mod = mod_ref[0, 0]
    a, b = HY_CH, HY_CH + ML_W
    mix = (jnp.dot(pick(hyc_ref, hyl_ref), wb[0:a, :], preferred_element_type=F32)
           + jnp.dot(pick(mlc_ref, mll_ref), wb[a:b, :], preferred_element_type=F32)
           + jnp.dot(pick(s5c_ref, s5l_ref), wb[b:, :], preferred_element_type=F32))
    xn = _x_tile(step, x_refs) + mod[2:3] * mix
    xn_ref[...] = xn
    h2 = _rms(xn, g_ref[0]) * (1.0 + mod[4:5]) + mod[3:4]
    h2e_ref[:, 0:D] = h2
    h_hi, h_lo = _split_bf16(h2)
    r_hi, r_lo = _split_bf16(rw_ref[...])
    logits = (lax.dot_general(r_hi, h_hi, _NT, preferred_element_type=F32)
              + (lax.dot_general(r_hi, h_lo, _NT, preferred_element_type=F32)
                 + lax.dot_general(r_lo, h_hi, _NT, preferred_element_type=F32)))
    ex = jnp.exp(logits - jnp.max(logits, axis=0, keepdims=True))
    probs = ex / jnp.sum(ex, axis=0, keepdims=True)
    sel = probs + rb_ref[...]
    best = None
    best_score = None
    for g in range(N_GROUPS):
        r = [sel[g * GROUP_SIZE + i:g * GROUP_SIZE + i + 1, :] for i in range(GROUP_SIZE)]
        score = None
        for i in range(GROUP_SIZE):
            for j in range(i + 1, GROUP_SIZE):
                pair = r[i] + r[j]
                score = pair if score is None else jnp.maximum(score, pair)
        if g == 0:
            best, best_score = jnp.zeros_like(score, dtype=jnp.int32), score
        else:
            upd = score > best_score
            best = jnp.where(upd, g, best)
            best_score = jnp.where(upd, score, best_score)
    eid = lax.broadcasted_iota(jnp.int32, (N_EXPERTS, 1), 0)
    masked = jnp.where(eid // GROUP_SIZE == best, sel, -jnp.inf)
    m1 = jnp.max(masked, axis=0, keepdims=True)
    i1 = jnp.min(jnp.where(masked == m1, eid, N_EXPERTS), axis=0, keepdims=True)
    masked2 = jnp.where(eid == i1, -jnp.inf, masked)
    m2 = jnp.max(masked2, axis=0, keepdims=True)
    i2 = jnp.min(jnp.where(masked2 == m2, eid, N_EXPERTS), axis=0, keepdims=True)
    p1 = jnp.sum(jnp.where(eid == i1, probs, 0.0), axis=0, keepdims=True)
    p2 = jnp.sum(jnp.where(eid == i2, probs, 0.0), axis=0, keepdims=True)
    tot = p1 + p2
    comb = jnp.where(eid == i1, p1 / tot, 0.0) + jnp.where(eid == i2, p2 / tot, 0.0)
    comb = jnp.concatenate([comb, jnp.zeros((LANES - N_EXPERTS, comb.shape[1]), F32)], axis=0)
    h2e_ref[:, D:] = comb.T
    best_ref[...] = best
    gid = lax.broadcasted_iota(jnp.int32, (8, 1), 0)
    onehot = (gid == best).astype(F32)
    cum = _cumsum_lanes(onehot, TM, False)
    run = cnt_s[:, 0:1]
    rank_ref[...] = jnp.sum(onehot * (cum - onehot + run), axis=0, keepdims=True).astype(jnp.int32)
    cnt_s[...] = jnp.broadcast_to(run + cum[:, TM - 1:TM], cnt_s.shape)
    cnt_ref[...] = cnt_s[...]


def _outproj(x, y_hy, y_ml, y_s5, w_out, l, mod, g2, rw_t, rb):
    full = lambda a: pl.BlockSpec(a.shape, lambda i: (0,) * a.ndim)
    n_ctx = T_CTX // TM
    ctx = lambda w: pl.BlockSpec((TM, w), lambda i: (jnp.minimum(i, n_ctx - 1), 0))
    lat = lambda w: pl.BlockSpec((TM, w), lambda i: (jnp.maximum(i - n_ctx, 0), 0))
    tok = lambda w: pl.BlockSpec((TM, w), lambda i: (i, 0))
    row = pl.BlockSpec((1, TM), lambda i: (0, i))
    return pl.pallas_call(
        functools.partial(_outproj_kernel, n_x=len(x)),
        grid=(T_ALL // TM,),
        in_specs=_x_specs(x) + [ctx(HY_CH), lat(HY_CH), ctx(ML_W), lat(ML_W), ctx(S5_CH), lat(S5_CH),
                  pl.BlockSpec((1, D, D), lambda i: (l, 0, 0), pipeline_mode=pl.Buffered(1)),
                  _mod_spec(l, TM), _layer_spec((1, D), l), full(rw_t), full(rb)],
        out_specs=[tok(D), tok(D + LANES), row, row, pl.BlockSpec((8, LANES), lambda i: (0, 0))],
        out_shape=[jax.ShapeDtypeStruct((T_ALL, D), F32),
                   jax.ShapeDtypeStruct((T_ALL, D + LANES), F32),
                   jax.ShapeDtypeStruct((1, T_ALL), jnp.int32),
                   jax.ShapeDtypeStruct((1, T_ALL), jnp.int32),
                   jax.ShapeDtypeStruct((8, LANES), F32)],
        scratch_shapes=[pltpu.VMEM((D, D), BF16), pltpu.VMEM((8, LANES), F32)],
        compiler_params=_cparams(("arbitrary",)),
        name="outproj_router",
    )(*x, *y_hy, *y_ml, *y_s5, w_out, mod, g2, rw_t, rb)


def _gather_rows(idx_ref, idx_base, src_ref, dst_ref, n_rows):
    def body(r8, carry):
        base = pl.multiple_of(r8 * 8, 8)
        for k in range(8):
            idx = idx_ref[idx_base + base + k]
            dst_ref[pl.ds(base + k, 1), :] = src_ref[pl.ds(idx, 1), :]
        return carry

    lax.fori_loop(0, n_rows // 8, body, 0)


def _dispatch(best, rank, cnt):
    tm = TM_MOE
    g = best.reshape(T_ALL)
    cnt = cnt[:N_GROUPS, 0].astype(jnp.int32)
    n_tile_g = (cnt + tm - 1) // tm
    tile_end = jnp.cumsum(n_tile_g)
    row_off = (tile_end - n_tile_g) * tm
    pos = rank.reshape(T_ALL)
    for k in range(N_GROUPS):
        pos = pos + jnp.where(g == k, row_off[k], 0)
    src = jnp.zeros((MOE_SLOTS,), jnp.int32).at[pos].set(jnp.arange(T_ALL, dtype=jnp.int32))
    tiles = jnp.arange(MOE_SLOTS // tm, dtype=jnp.int32)
    tile_group = jnp.minimum(jnp.sum((tiles[:, None] >= tile_end[None, :]).astype(jnp.int32), axis=1), N_GROUPS - 1)
    return pos, src, tile_group.astype(jnp.int32), tile_end[N_GROUPS - 1:].astype(jnp.int32)


def _moe_kernel(src_ref, tg_ref, nt_ref, h_ref, wg_ref, wu_ref, wd_ref, ys_ref, g_a, x_a, g_b, x_b, acc):
    i = pl.program_id(0)
    j = pl.program_id(1)
    tm = TM_MOE
    quarter = tm // GROUP_SIZE
    n_live = nt_ref[0]

    @pl.when(jnp.logical_and(i == 0, j == 0))
    def _():
        _gather_rows(src_ref, 0, h_ref, g_a, tm)
        x_a[...] = g_a[:, 0:D].astype(BF16)

    def step(cur_g, cur_x, nxt_g, nxt_x):
        @pl.when(j == 0)
        def _():
            acc[...] = jnp.zeros_like(acc)

        e = tg_ref[i] * GROUP_SIZE + j
        lane = lax.broadcasted_iota(jnp.int32, (1, LANES), 1)
        wg = wg_ref[0, 0].astype(BF16)
        wu = wu_ref[0, 0].astype(BF16)
        wd = wd_ref[0, 0].astype(BF16)
        sub = MOE_SUB
        for s in range(tm // sub):
            rs = slice(s * sub, (s + 1) * sub)
            hb = cur_x[rs, :]
            ce = jnp.sum(jnp.where(lane == e, cur_g[rs, D:], 0.0), axis=1, keepdims=True)
            hid = _silu(jnp.dot(hb, wg, preferred_element_type=F32)) * jnp.dot(hb, wu, preferred_element_type=F32)
            acc[rs, :] += jnp.dot((hid * ce).astype(BF16), wd, preferred_element_type=F32)

        r0 = pl.multiple_of(j * quarter, quarter)
        base = jnp.minimum(i + 1, n_live - 1) * tm + r0
        for k in range(quarter):
            nxt_g[pl.ds(r0 + k, 1), :] = h_ref[pl.ds(src_ref[base + k], 1), :]
        nxt_x[pl.ds(r0, quarter), :] = nxt_g[pl.ds(r0, quarter), 0:D].astype(BF16)

        @pl.when(j == GROUP_SIZE - 1)
        def _():
            ys_ref[...] = acc[...]

    live = i < n_live

    @pl.when(jnp.logical_and(live, i % 2 == 0))
    def _():
        step(g_a, x_a, g_b, x_b)

    @pl.when(jnp.logical_and(live, i % 2 == 1))
    def _():
        step(g_b, x_b, g_a, x_a)

    @pl.when(jnp.logical_and(jnp.logical_not(live), j == GROUP_SIZE - 1))
    def _():
        ys_ref[...] = jnp.zeros_like(ys_ref)


def _moe(h2e, src, tile_group, n_tiles, wg, wu, wd, l):
    tm = TM_MOE

    def w_map(i, j, src, tg, nt):
        live = i < nt[0]
        ii = jnp.minimum(i, nt[0] - 1)
        return (l, tg[ii] * GROUP_SIZE + jnp.where(live, j, GROUP_SIZE - 1), 0, 0)

    return pl.pallas_call(
        _moe_kernel,
        grid_spec=pltpu.PrefetchScalarGridSpec(
            num_scalar_prefetch=3,
            grid=(MOE_SLOTS // tm, GROUP_SIZE),
            in_specs=[pl.BlockSpec(memory_space=pltpu.VMEM),
                      pl.BlockSpec((1, 1, D, D_EXPERT), w_map),
                      pl.BlockSpec((1, 1, D, D_EXPERT), w_map),
                      pl.BlockSpec((1, 1, D_EXPERT, D), w_map)],
            out_specs=pl.BlockSpec((tm, D), lambda i, j, src, tg, nt: (i, 0)),
            scratch_shapes=[pltpu.VMEM((tm, D + LANES), F32), pltpu.VMEM((tm, D), BF16),
                            pltpu.VMEM((tm, D + LANES), F32), pltpu.VMEM((tm, D), BF16),
                            pltpu.VMEM((tm, D), F32)]),
        out_shape=jax.ShapeDtypeStruct((MOE_SLOTS, D), F32),
        compiler_params=_cparams(("arbitrary", "arbitrary")),
        name="moe_experts",
    )(src, tile_group, n_tiles, h2e, wg, wu, wd)


def _combine_kernel(pos_ref, ys_ref, xn_ref, mod_ref, fg_ref, *rest, final):
    step = pl.program_id(0)
    if final:
        yc_ref, yl_ref, gbuf = rest
    else:
        out_ref, gbuf = rest
    _gather_rows(pos_ref, step * TM, ys_ref, gbuf, TM)
    out = xn_ref[...] + mod_ref[0, 0][5:6] * gbuf[...]
    if final:
        y = _rms(out, fg_ref[...])

        @pl.when(step < T_CTX // TM)
        def _():
            yc_ref[...] = y

        @pl.when(step >= T_CTX // TM)
        def _():
            yl_ref[...] = y
    else:
        out_ref[...] = out


def _combine(pos, ys, xn, mod, fg, l, final):
    spec = pl.BlockSpec((TM, D), lambda i, pos: (i, 0))
    n_ctx = T_CTX // TM
    if final:
        out_specs = [pl.BlockSpec((TM, D), lambda i, pos: (jnp.minimum(i, n_ctx - 1), 0)),
                     pl.BlockSpec((TM, D), lambda i, pos: (jnp.maximum(i - n_ctx, 0), 0))]
        out_shape = [jax.ShapeDtypeStruct((T_CTX, D), F32), jax.ShapeDtypeStruct((T_LAT, D), F32)]
    else:
        out_specs = [spec]
        out_shape = [jax.ShapeDtypeStruct((T_ALL, D), F32)]
    return pl.pallas_call(
        functools.partial(_combine_kernel, final=final),
        grid_spec=pltpu.PrefetchScalarGridSpec(
            num_scalar_prefetch=1,
            grid=(T_ALL // TM,),
            in_specs=[pl.BlockSpec(memory_space=pltpu.VMEM),
                      spec,
                      _mod_spec(l, TM),
                      pl.BlockSpec((1, D), lambda i, pos: (0, 0))],
            out_specs=out_specs,
            scratch_shapes=[pltpu.VMEM((TM, D), F32)]),
        out_shape=out_shape,
        compiler_params=_cparams(("arbitrary",)),
        name="moe_combine",
    )(pos, ys, xn, mod, fg)


@functools.lru_cache(None)
def _pos_embed():
    rows = L_LAT // GRID_W
    r = np.repeat(np.arange(rows, dtype=np.float64), GRID_W)
    col = np.tile(np.arange(GRID_W, dtype=np.float64), rows)
    quarter = D // 4
    freq = np.exp(-math.log(POS_BASE) * np.arange(quarter, dtype=np.float64) / quarter)
    ar = r[:, None] * freq[None]
    ac = col[:, None] * freq[None]
    emb = np.concatenate([np.sin(ar), np.cos(ar), np.sin(ac), np.cos(ac)], axis=-1)
    return emb.astype(np.float32)


def _to_time_major_ctx(a):
    c = a.shape[-1]
    a = a.reshape(2, S5_ROWS, L_CTX, c).transpose(0, 2, 1, 3)
    return a.reshape(2, L_CTX * S5_ROWS, c)


def _from_time_major_ctx(a):
    c = a.shape[-1]
    a = a.reshape(2, L_CTX, S5_ROWS, c).transpose(0, 2, 1, 3)
    return a.reshape(T_CTX, c)


def _to_time_major_lat(a):
    c = a.shape[-1]
    a = a.reshape(N_LAT_SEQ, 4, S5_STEPS, c).transpose(2, 1, 0, 3)
    return a.reshape(1, S5_STEPS * S5_ROWS, c)


def _from_time_major_lat(a):
    c = a.shape[-1]
    a = a.reshape(S5_STEPS, 4, N_LAT_SEQ, c).transpose(2, 1, 0, 3)
    return a.reshape(T_LAT, c)


def kernel(x_prompt, x_sample, c, state_mlstm_C, state_mlstm_n, state_mlstm_m, state_s5_re, state_s5_im, c_ctx, w_ada, b_ada, norm1_g, norm2_g, final_g, w_in, w_out, hy_short, hy_fw1, hy_fb1, hy_fw2, hy_fb2, hy_fw3, hy_log_decay, hy_bias, ml_short, ml_gate_bias, ml_norm_g, s5_a_re, s5_a_im, s5_log_dt, s5_b_re, s5_b_im, s5_c_re, s5_c_im, s5_d, s5_w_glu, router_w, router_b, moe_w_gate, moe_w_up, moe_w_down):
    x = (x_prompt.reshape(T_CTX, D), x_sample.reshape(T_LAT, D), jnp.asarray(_pos_embed()))
    w_in_t = jnp.swapaxes(w_in, 1, 2)
    cc =jnp.concatenate([c_ctx[None], c, jnp.zeros((8 - 1 - N_LAT_SEQ, D), F32)], axis=0)
    mod = _ada(cc, w_ada, b_ada).reshape(DEPTH, 8, 6, D)
    rw_t = router_w.T
    rb = router_b.reshape(N_EXPERTS, 1)
    fg = final_g.reshape(1, D)
    lat_blk = T_CTX // L_LAT
    g1 = norm1_g.reshape(DEPTH, 1, D)
    g2 = norm2_g.reshape(DEPTH, 1, D)

    w1p = jnp.pad(hy_fw1, ((0, 0), (0, LANES - HY_EMB), (0, 0)))
    b1 = hy_fb1.reshape(DEPTH, 1, HY_FILTER_W)
    b2 = hy_fb2.reshape(DEPTH, 1, HY_FILTER_W)
    ld = hy_log_decay.reshape(DEPTH, 1, 4 * HY_CH)
    hy_spec = {L: _hy_filter(L, w1p, b1, hy_fw2, b2, hy_fw3, ld) for L in (L_CTX, L_LAT)}
    gb = jnp.pad(ml_gate_bias.reshape(DEPTH, 1, 16), ((0, 0), (0, 0), (0, GATE_PAD - 16)))
    gbt = ml_gate_bias.reshape(DEPTH, 16, 1)
    ng = ml_norm_g.reshape(DEPTH, 1, ML_W)
    ml_state = (state_mlstm_C.reshape(N_LAT_SEQ, DEPTH, 2 * ML_HEADS, ML_DH, ML_DH),
                state_mlstm_n.reshape(N_LAT_SEQ, DEPTH, 2 * ML_HEADS, ML_DH),
                jnp.broadcast_to(state_mlstm_m.reshape(N_LAT_SEQ, DEPTH, 2 * ML_HEADS, 1),
                                 (N_LAT_SEQ, DEPTH, 2 * ML_HEADS, LANES)))
    bb, cc_s5, lam = _s5_params(s5_a_re, s5_a_im, s5_log_dt, s5_b_re, s5_b_im, s5_c_re, s5_c_im)
    dsk = s5_d.reshape(DEPTH, 1, S5_CH)
    wglu = s5_w_glu.astype(BF16)
    s0 = jnp.concatenate([state_s5_re.reshape(N_LAT_SEQ, DEPTH, 2, S5_STATE),
                          state_s5_im.reshape(N_LAT_SEQ, DEPTH, 2, S5_STATE)], axis=-1)
    s0 = jnp.tile(s0.transpose(1, 2, 0, 3), (1, 1, 4, 1))

    new_c, new_n, new_m, new_re, new_im = [], [], [], [], []
    y_prompt = y_sample = None
    for l in range(DEPTH):
        u_hy, qk, v, o, u_s5, gates, gates_t = _inproj(x, g1, mod, w_in_t, l)

        y_hy = [_hyena(u_hy, L, n_seq, blk0, hy_short, hy_bias, *hy_spec[L], l)
                for L, n_seq, blk0 in ((L_CTX, N_CTX_SEQ, 0), (L_LAT, N_LAT_SEQ, lat_blk))]

        yc, cc_, nc_, mc_ = _mlstm(qk, v, o, gates, gates_t, L_CTX, N_CTX_SEQ, 0, ml_short, gb, gbt, ng, None, l,
                                   emit_state=True)
        (yl,) = _mlstm(qk, v, o, gates, gates_t, L_LAT, N_LAT_SEQ, lat_blk, ml_short, gb, gbt, ng, ml_state, l)
        y_ml = (yc, yl)
        new_c.append(cc_.reshape(N_CTX_SEQ, 2, ML_HEADS, ML_DH, ML_DH))
        new_n.append(nc_.reshape(N_CTX_SEQ, 2, ML_HEADS, ML_DH))
        new_m.append(mc_[:, :, 0].reshape(N_CTX_SEQ, 2, ML_HEADS))

        ys_c, fin = _s5(_to_time_major_ctx(u_s5[:T_CTX]), bb, cc_s5, lam, dsk, wglu, None, l)
        (ys_l,) = _s5(_to_time_major_lat(u_s5[T_CTX:]), bb, cc_s5, lam, dsk, wglu, s0, l)
        y_s5 = (_from_time_major_ctx(ys_c), _from_time_major_lat(ys_l))
        fin = fin.transpose(0, 2, 1, 3).reshape(N_CTX_SEQ, 2, 2 * S5_STATE)
        new_re.append(fin[..., :S5_STATE].reshape(N_CTX_SEQ, 2, S5_G, S5_P))
        new_im.append(fin[..., S5_STATE:].reshape(N_CTX_SEQ, 2, S5_G, S5_P))

        xn, h2e, best, rank, cnt = _outproj(x, y_hy, y_ml, y_s5, w_out, l, mod, g2, rw_t, rb)
        pos, src, tile_group, n_tiles = _dispatch(best, rank, cnt)
        ys = _moe(h2e, src, tile_group, n_tiles, moe_w_gate, moe_w_up, moe_w_down, l)
        res = _combine(pos, ys, xn, mod, fg, l, l == DEPTH - 1)
        if l == DEPTH - 1:
            y_prompt = res[0].reshape(N_CTX_SEQ, L_CTX, D)
            y_sample = res[1].reshape(N_LAT_SEQ, L_LAT, D)
        else:
            x = (res[0],)

    return (y_prompt, y_sample, jnp.stack(new_c, axis=1), jnp.stack(new_n, axis=1), jnp.stack(new_m, axis=1),
            jnp.stack(new_re, axis=1), jnp.stack(new_im, axis=1))
```

```python
import functools
import math

import numpy as np
import jax
import jax.numpy as jnp
from jax import lax
from jax.experimental import pallas as pl
from jax.experimental.pallas import tpu as pltpu

F32 = jnp.float32
BF16 = jnp.bfloat16
HIGHEST = lax.Precision.HIGHEST

D = 1024
N_CTX_SEQ, L_CTX = 16, 256
N_LAT_SEQ, L_LAT = 2, 1024
T_CTX = N_CTX_SEQ * L_CTX
T_LAT = N_LAT_SEQ * L_LAT
T_ALL = T_CTX + T_LAT
DEPTH = 2
EPS = 1e-6
GRID_W = 64
POS_BASE = 10000.0
HY_CH = 256
HY_EMB = 33
HY_FILTER_W = 64
ML_HEADS = 4
ML_DH = 128
ML_W = ML_HEADS * ML_DH
S5_CH = 256
S5_G = 16
S5_GROUP = 16
S5_P = 64
S5_STATE = S5_G * S5_P
N_EXPERTS = 16
N_GROUPS = 4
GROUP_SIZE = N_EXPERTS // N_GROUPS
D_EXPERT = 512
OFF_HY = 0
OFF_QK = 3 * HY_CH
OFF_V = OFF_QK + 2 * ML_W
OFF_O = OFF_V + ML_W
OFF_G = OFF_O + ML_W
OFF_S5 = OFF_G + 16
IN_W = OFF_S5 + S5_CH
LANES = 128
GATE_PAD = LANES

TM = 512
TM_MOE = 512
MOE_SLOTS = T_ALL + N_GROUPS * TM_MOE
MOE_SUB = 512
HY_ROWS = 1024
ML_CHUNK = 256
S5_ROWS = 8
S5_STEPS = 256
VMEM_LIMIT = 56 * 1024 * 1024


def _cparams(sem, vmem=VMEM_LIMIT):
    if sem is None:
        return pltpu.CompilerParams(vmem_limit_bytes=vmem)
    return pltpu.CompilerParams(dimension_semantics=sem, vmem_limit_bytes=vmem)


def _bdot(a, b):
    return jnp.dot(a.astype(BF16), b.astype(BF16), preferred_element_type=F32)


def _split_bf16(x):
    hi = x.astype(BF16)
    return hi, (x - hi.astype(F32)).astype(BF16)


def _silu(x):
    return x * jax.nn.sigmoid(x)


def _rms(x, g):
    return x * lax.rsqrt(jnp.mean(x * x, axis=-1, keepdims=True) + EPS) * g


def _log_sigmoid(x):
    return jnp.minimum(x, 0.0) - jnp.log1p(jnp.exp(-jnp.abs(x)))


def _conv3(u, w, n_rows, seq_len):
    row = lax.broadcasted_iota(jnp.int32, (n_rows, 1), 0) % seq_len
    prev = jnp.where(row == 0, 0.0, pltpu.roll(u, 1, 0))
    nxt = jnp.where(row == seq_len - 1, 0.0, pltpu.roll(u, n_rows - 1, 0))
    return prev * w[0:1] + u * w[1:2] + nxt * w[2:3]


def _ada_kernel(c_ref, w_ref, b_ref, o_ref):
    o_ref[0] = _bdot(_silu(c_ref[...]), w_ref[0]) + b_ref[0]


def _ada(cc, w_ada, b_ada):
    tn = 1536
    return pl.pallas_call(
        _ada_kernel,
        grid=(DEPTH, 6 * D // tn),
        in_specs=[pl.BlockSpec((8, D), lambda l, j: (0, 0)),
                  pl.BlockSpec((1, D, tn), lambda l, j: (l, 0, j)),
                  pl.BlockSpec((1, 1, tn), lambda l, j: (l, 0, j))],
        out_specs=pl.BlockSpec((1, 8, tn), lambda l, j: (l, 0, j)),
        out_shape=jax.ShapeDtypeStruct((DEPTH, 8, 6 * D), F32),
        compiler_params=_cparams(("arbitrary", "arbitrary")),
        name="ada_mod",
    )(cc, w_ada, b_ada.reshape(DEPTH, 1, 6 * D))


def _mod_row(i, tm):
    n_ctx = T_CTX // tm
    return jnp.where(i < n_ctx, 0, 1 + (i - n_ctx) // (L_LAT // tm))


_SEG = ((OFF_HY, OFF_QK - OFF_HY), (OFF_QK, OFF_V - OFF_QK), (OFF_V, OFF_O - OFF_V), (OFF_O, OFF_G - OFF_O))
TAIL_W = IN_W - OFF_G


N_CTX_TILES = T_CTX // TM
_NT = (((1,), (1,)), ((), ()))


def _x_specs(x):
    if len(x) == 1:
        return [pl.BlockSpec((TM, D), lambda i, *_: (i, 0))]
    per_seq = L_LAT // TM
    return [pl.BlockSpec((TM, D), lambda i, *_: (jnp.minimum(i, N_CTX_TILES - 1), 0)),
            pl.BlockSpec((TM, D), lambda i, *_: (jnp.maximum(i - N_CTX_TILES, 0), 0)),
            pl.BlockSpec((TM, D), lambda i, *_: (jnp.maximum(i - N_CTX_TILES, 0) % per_seq, 0))]


def _x_tile(step, x_refs):
    if len(x_refs) == 1:
        return x_refs[0][...]
    xc_ref, xl_ref, pos_ref = x_refs
    return jnp.where(step < N_CTX_TILES, xc_ref[...], xl_ref[...] + pos_ref[...])


def _inproj_kernel(*refs, n_x):
    x_refs = refs[:n_x]
    g_ref, mod_ref, w_ref, hy_ref, qk_ref, v_ref, o_ref, s5_ref, gt_ref, gtt_ref, wb = refs[n_x:]
    step = pl.program_id(0)

    @pl.when(step == 0)
    def _():
        wb[...] = w_ref[0].astype(BF16)

    mod = mod_ref[0, 0]
    h = _rms(_x_tile(step, x_refs), g_ref[0]) * (1.0 + mod[1:2]) + mod[0:1]
    hb = h.astype(BF16)
    for (a, w), ref in zip(_SEG, (hy_ref, qk_ref, v_ref, o_ref)):
        ref[...] = lax.dot_general(hb, wb[a:a + w, :], _NT, preferred_element_type=F32)
    tail = lax.dot_general(hb, wb[OFF_G:IN_W, :], _NT, preferred_element_type=F32)
    gates = tail[:, 0:GATE_PAD]
    gt_ref[...] = gates
    for c in range(TM // ML_CHUNK):
        gtt_ref[c] = gates[c * ML_CHUNK:(c + 1) * ML_CHUNK, :].T[0:16, :]
    s5_ref[...] = tail[:, OFF_S5 - OFF_G:TAIL_W]


def _layer_spec(shape, l):
    return pl.BlockSpec((1,) + tuple(shape), lambda *_: (l,) + (0,) * len(shape))


def _mod_spec(l, tm):
    return pl.BlockSpec((1, 1, 6, D), lambda i, *_: (l, _mod_row(i, tm), 0, 0))


def _inproj(x, g, mod, w_in_t, l):
    widths = [w for _, w in _SEG] + [S5_CH, GATE_PAD]
    cpt = TM // ML_CHUNK
    return pl.pallas_call(
        functools.partial(_inproj_kernel, n_x=len(x)),
        grid=(T_ALL // TM,),
        in_specs=_x_specs(x) + [
            _layer_spec((1, D), l), _mod_spec(l, TM),
            pl.BlockSpec((1, IN_W, D), lambda i: (l, 0, 0), pipeline_mode=pl.Buffered(1))],
        out_specs=[pl.BlockSpec((TM, w), lambda i: (i, 0)) for w in widths]
        + [pl.BlockSpec((cpt, 16, ML_CHUNK), lambda i: (i, 0, 0))],
        out_shape=[jax.ShapeDtypeStruct((T_ALL, w), F32) for w in widths]
        + [jax.ShapeDtypeStruct((T_ALL // ML_CHUNK, 16, ML_CHUNK), F32)],
        scratch_shapes=[pltpu.VMEM((IN_W, D), BF16)],
        compiler_params=_cparams(("arbitrary",)),
        name="norm_inproj",
    )(*x, g, mod, w_in_t)


@functools.lru_cache(None)
def _dft_mats(L):
    n = 2 * L
    k = np.arange(L)[:, None]
    t = np.arange(L)[None, :]
    ang = 2.0 * np.pi * ((k * t) % n) / n
    top = np.cos(ang)
    bot = -np.sin(ang)
    bot[0] = np.cos(np.pi * np.arange(L))
    fwd = np.concatenate([top, bot], 0)
    s = np.full((n, 1), 2.0 / n)
    s[0] = s[L] = 1.0 / n
    inv = (fwd * s).T
    return fwd.astype(np.float32), inv.astype(np.float32)


@functools.lru_cache(None)
def _hy_positions(L):
    t = np.linspace(0.0, 1.0, L)
    bands = (HY_EMB - 1) // 2
    f = np.linspace(1e-4, bands - 1, bands)
    w = 2.0 * np.pi * np.arange(L) / L
    ang = w[:, None] * f[None, :]
    z = np.concatenate([t[:, None], np.cos(ang), -np.sin(ang)], -1)
    zp = np.zeros((L, LANES))
    zp[:, :HY_EMB] = z
    return zp.astype(np.float32), t[:, None].astype(np.float32)


def _hy_filter_kernel(z_ref, t_ref, w1_ref, b1_ref, w2_ref, b2_ref, w3_ref, ld_ref, f_ref,
                      p_ref, q_ref, r_ref, *, L):
    h = jnp.sin(jnp.dot(z_ref[...], w1_ref[0], precision=HIGHEST, preferred_element_type=F32) + b1_ref[0])
    h = jnp.sin(jnp.dot(h, w2_ref[0], precision=HIGHEST, preferred_element_type=F32) + b2_ref[0])
    filt = jnp.dot(h, w3_ref[0], precision=HIGHEST, preferred_element_type=F32)
    filt = filt * jnp.exp(-t_ref[...] * jnp.exp(ld_ref[0]))
    c = HY_CH
    h_fwd = jnp.concatenate([filt[:, 0:c], filt[:, 2 * c:3 * c]], axis=1)
    h_bwd = jnp.concatenate([filt[:, c:2 * c], filt[:, 3 * c:4 * c]], axis=1)
    row = lax.broadcasted_iota(jnp.int32, (L, 1), 0)
    h_bwd = jnp.where(row == 0, 0.0, h_bwd)
    a = jnp.dot(f_ref[...], h_fwd.astype(BF16), preferred_element_type=F32)
    b = jnp.dot(f_ref[...], h_bwd.astype(BF16), preferred_element_type=F32)
    re = a[:L] + b[:L]
    im = a[L:] - b[L:]
    nyq = a[L:L + 1] + b[L:L + 1]
    p_ref[0] = re
    q_ref[0] = jnp.where(row == 0, 0.0, im)
    r_ref[0] = jnp.where(row == 0, nyq, re)


def _hy_filter(L, w1p, b1, w2, b2, w3, ld):
    z, t = _hy_positions(L)
    fwd = jnp.asarray(_dft_mats(L)[0]).astype(BF16)
    out = jax.ShapeDtypeStruct((DEPTH, L, 2 * HY_CH), F32)
    full = lambda a: pl.BlockSpec(a.shape, lambda l: (0,) * a.ndim)
    layer = lambda a: pl.BlockSpec((1,) + a.shape[1:], lambda l: (l,) + (0,) * (a.ndim - 1))
    return pl.pallas_call(
        functools.partial(_hy_filter_kernel, L=L),
        grid=(DEPTH,),
        in_specs=[full(z), full(t), layer(w1p), layer(b1), layer(w2), layer(b2), layer(w3), layer(ld), full(fwd)],
        out_specs=[pl.BlockSpec((1, L, 2 * HY_CH), lambda l: (l, 0, 0))] * 3,
        out_shape=[out, out, out],
        compiler_params=_cparams(("arbitrary",)),
        name=f"hyena_filter_{L}",
    )(z, t, w1p, b1, w2, b2, w3, ld, fwd)


def _hyena_kernel(u_ref, sw_ref, bias_ref, p_ref, q_ref, r_ref, f_ref, g_ref, o_ref, *, L, n_sub):
    c = HY_CH
    u = _conv3(u_ref[...], sw_ref[0], n_sub * L, L)
    for s in range(n_sub):
        rs = slice(s * L, (s + 1) * L)
        z = u[rs, 0:c]
        for o in range(2):
            gate = u[rs, (o + 1) * c:(o + 2) * c]
            zf = jnp.dot(f_ref[...], z.astype(BF16), preferred_element_type=F32)
            a, b = zf[:L], zf[L:]
            p = p_ref[0, :, o * c:(o + 1) * c]
            q = q_ref[0, :, o * c:(o + 1) * c]
            r = r_ref[0, :, o * c:(o + 1) * c]
            y_re = (a * p - b * q).astype(BF16)
            y_im = (a * q + b * r).astype(BF16)
            y = (jnp.dot(g_ref[:, :L], y_re, preferred_element_type=F32)
                 + jnp.dot(g_ref[:, L:], y_im, preferred_element_type=F32))
            z = gate * (y + bias_ref[0, o:o + 1, :] * z)
        o_ref[rs, :] = z


def _hyena(u_hy, L, n_seq, row_block0, sw, bias, p, q, r, l):
    n_sub = max(1, HY_ROWS // L)
    fwd, inv = (jnp.asarray(m).astype(BF16) for m in _dft_mats(L))
    full = lambda a: pl.BlockSpec(a.shape, lambda b: (0,) * a.ndim)
    layer = lambda a: _layer_spec(a.shape[1:], l)
    blk0 = row_block0 // n_sub
    return pl.pallas_call(
        functools.partial(_hyena_kernel, L=L, n_sub=n_sub),
        grid=(n_seq // n_sub,),
        in_specs=[pl.BlockSpec((n_sub * L, 3 * HY_CH), lambda b: (blk0 + b, 0)),
                  layer(sw), layer(bias), layer(p), layer(q), layer(r), full(fwd), full(inv)],
        out_specs=pl.BlockSpec((n_sub * L, HY_CH), lambda b: (b, 0)),
        out_shape=jax.ShapeDtypeStruct((n_seq * L, HY_CH), F32),
        compiler_params=_cparams(("arbitrary",)),
        name=f"hyena_{L}",
    )(u_hy, sw, bias, p, q, r, fwd, inv)


def _cumsum_rows(x, n, reverse):
    row = lax.broadcasted_iota(jnp.int32, (n, 1), 0)
    s = 1
    while s < n:
        if reverse:
            x = x + jnp.where(row < n - s, pltpu.roll(x, n - s, 0), 0.0)
        else:
            x = x + jnp.where(row >= s, pltpu.roll(x, s, 0), 0.0)
        s *= 2
    return x


def _cumsum_lanes(x, n, reverse):
    col = lax.broadcasted_iota(jnp.int32, (1, n), 1)
    s = 1
    while s < n:
        if reverse:
            x = x + jnp.where(col < n - s, pltpu.roll(x, n - s, 1), 0.0)
        else:
            x = x + jnp.where(col >= s, pltpu.roll(x, s, 1), 0.0)
        s *= 2
    return x


def _mlstm_kernel(*refs, L, has_state, emit_state):
    qk_ref, v_ref, o_ref, g_ref, gt_ref, sw_ref, gb_ref, gbt_ref, ng_ref = refs[:9]
    refs = refs[9:]
    if has_state:
        c0_ref, n0_ref, m0_ref = refs[:3]
        refs = refs[3:]
    y_ref = refs[0]
    if emit_state:
        cout_ref, nout_ref, mout_ref = refs[1:4]
        refs = refs[4:]
    else:
        refs = refs[1:]
    q_s, k_s, vt_s, ht_s, ct_s, n_s, m_s = refs
    tc = ML_CHUNK
    nc = L // tc
    nh = ML_HEADS
    dh = ML_DH

    qk = _silu(_conv3(qk_ref[...], sw_ref[0], L, L))
    q_s[...] = qk[:, :ML_W].astype(BF16)
    k_s[...] = (qk[:, ML_W:] * (dh ** -0.5)).astype(BF16)
    for c in range(nc):
        vt_s[c] = v_ref[c * tc:(c + 1) * tc, :].T.astype(BF16)

    for i in range(2 * nh):
        ct_s[i] = c0_ref[0, 0, i].T if has_state else jnp.zeros((dh, dh), F32)
    n_s[...] = n0_ref[0, 0] if has_state else jnp.zeros_like(n_s)
    m_s[...] = m0_ref[0, 0] if has_state else jnp.zeros_like(m_s)

    si = lax.broadcasted_iota(jnp.int32, (tc, tc), 0)
    ti = lax.broadcasted_iota(jnp.int32, (tc, tc), 1)

    for d in range(2):
        rev = d == 1
        mask = (si >= ti) if rev else (si <= ti)
        edge = 0 if rev else tc - 1

        def chunk(j, carry, d=d, rev=rev, mask=mask, edge=edge):
            cidx = (nc - 1 - j) if rev else j
            r0 = pl.multiple_of(cidx * tc, tc)
            pre = g_ref[pl.ds(r0, tc), :] + gb_ref[0]
            pre_t = gt_ref[cidx] + gbt_ref[0]
            cum = _cumsum_rows(_log_sigmoid(pre), tc, rev)
            cum_t = _cumsum_lanes(_log_sigmoid(pre_t), tc, rev)
            key_all = cum - pltpu.roll(pre, 8, 1)
            for h in range(nh):
                col = d * nh + h
                hs = slice(h * dh, (h + 1) * dh)
                key = key_all[:, 8 + col:9 + col]
                b_row = cum_t[8 + col:9 + col, :]
                b_end = b_row[:, edge:edge + 1]
                m_prev = m_s[col:col + 1, 0:1]
                dmat = jnp.where(mask, b_row - key, -jnp.inf)
                inter = b_row + m_prev
                m_row = jnp.maximum(inter, jnp.max(dmat, axis=0, keepdims=True))
                w_intra = jnp.exp(dmat - m_row)
                w_state = jnp.exp(inter - m_row)
                qh = q_s[pl.ds(r0, tc), hs]
                kh = k_s[pl.ds(r0, tc), hs]
                vt = vt_s[cidx, hs, :]
                ct_prev = ct_s[col]
                n_prev = n_s[col:col + 1, :]
                s = lax.dot_general(kh, qh, _NT, preferred_element_type=F32) * w_intra
                num = (jnp.dot(vt, s.astype(BF16), preferred_element_type=F32)
                       + w_state * lax.dot_general(ct_prev.astype(BF16), qh, _NT, preferred_element_type=F32))
                qn = lax.dot_general(jnp.broadcast_to(n_prev, (8, dh)).astype(BF16), qh, _NT,
                                     preferred_element_type=F32)[0:1]
                den = jnp.sum(s, axis=0, keepdims=True) + w_state * qn
                hout = num * (1.0 / jnp.maximum(jnp.abs(den), jnp.exp(-m_row)))
                if d == 0:
                    ht_s[cidx, hs, :] = hout
                else:
                    ht_s[cidx, hs, :] += hout
                m_new = jnp.maximum(b_end + m_prev, b_end - jnp.min(key, axis=0, keepdims=True))
                wg = jnp.exp(b_end - key - m_new)
                decay = jnp.exp(b_end + m_prev - m_new)
                kw = kh.astype(F32) * wg
                ct_s[col] = decay * ct_prev + jnp.dot(vt, kw.astype(BF16), preferred_element_type=F32)
                n_s[col:col + 1, :] = decay * n_prev + jnp.sum(kw, axis=0, keepdims=True)
                m_s[col:col + 1, :] = jnp.broadcast_to(m_new, (1, LANES))
            return carry

        lax.fori_loop(0, nc, chunk, 0)

    for c in range(nc):
        for h in range(nh):
            hs = slice(h * dh, (h + 1) * dh)
            rs = slice(c * tc, (c + 1) * tc)
            ht = ht_s[c, hs, :]
            hn = ht * lax.rsqrt(jnp.mean(ht * ht, axis=0, keepdims=True) + EPS)
            y_ref[rs, hs] = jax.nn.sigmoid(o_ref[rs, hs]) * (hn.T * ng_ref[0, :, hs])
    if emit_state:
        for i in range(2 * nh):
            cout_ref[0, i] = ct_s[i].T
        nout_ref[0] = n_s[...]
        mout_ref[0] = m_s[...]


def _mlstm(qk, v, o, gates, gates_t, L, n_seq, row_block0, sw, gb, gbt, ng, state, l, emit_state=False):
    nc = L // ML_CHUNK
    has_state = state is not None
    layer = lambda a: _layer_spec(a.shape[1:], l)
    in_specs = [pl.BlockSpec((L, 2 * ML_W), lambda b: (row_block0 + b, 0)),
                pl.BlockSpec((L, ML_W), lambda b: (row_block0 + b, 0)),
                pl.BlockSpec((L, ML_W), lambda b: (row_block0 + b, 0)),
                pl.BlockSpec((L, GATE_PAD), lambda b: (row_block0 + b, 0)),
                pl.BlockSpec((nc, 16, ML_CHUNK), lambda b: (row_block0 + b, 0, 0)),
                layer(sw), layer(gb), layer(gbt), layer(ng)]
    args = [qk, v, o, gates, gates_t, sw, gb, gbt, ng]
    if has_state:
        c0, n0, m0 = state
        in_specs += [pl.BlockSpec((1, 1, 2 * ML_HEADS, ML_DH, ML_DH), lambda b: (b, l, 0, 0, 0)),
                     pl.BlockSpec((1, 1, 2 * ML_HEADS, ML_DH), lambda b: (b, l, 0, 0)),
                     pl.BlockSpec((1, 1, 2 * ML_HEADS, LANES), lambda b: (b, l, 0, 0))]
        args += [c0, n0, m0]
    out_specs = [pl.BlockSpec((L, ML_W), lambda b: (b, 0))]
    out_shape = [jax.ShapeDtypeStruct((n_seq * L, ML_W), F32)]
    if emit_state:
        tails = ((2 * ML_HEADS, ML_DH, ML_DH), (2 * ML_HEADS, ML_DH), (2 * ML_HEADS, LANES))
        for t in tails:
            out_specs.append(pl.BlockSpec((1,) + t, lambda b, n=len(t): (b,) + (0,) * n))
            out_shape.append(jax.ShapeDtypeStruct((n_seq,) + t, F32))
    return pl.pallas_call(
        functools.partial(_mlstm_kernel, L=L, has_state=has_state, emit_state=emit_state),
        grid=(n_seq,),
        in_specs=in_specs,
        out_specs=out_specs,
        out_shape=out_shape,
        scratch_shapes=[pltpu.VMEM((L, ML_W), BF16), pltpu.VMEM((L, ML_W), BF16),
                        pltpu.VMEM((nc, ML_W, ML_CHUNK), BF16),
                        pltpu.VMEM((nc, ML_W, ML_CHUNK), F32),
                        pltpu.VMEM((2 * ML_HEADS, ML_DH, ML_DH), F32),
                        pltpu.VMEM((2 * ML_HEADS, ML_DH), F32),
                        pltpu.VMEM((2 * ML_HEADS, LANES), F32)],
        compiler_params=_cparams(("arbitrary",)),
        name=f"mlstm_{L}",
    )(*args)


def _cmul(ar, ai, br, bi):
    return ar * br - ai * bi, ar * bi + ai * br


def _s5_kernel(*refs, segmented):
    if segmented:
        (u_ref, bb_ref, cc_ref, eb_ref, ec_ref, lam_ref, dsk_ref, wglu_ref, s0_ref, y_ref,
         sbuf, yacc, bmat, cmat, pw) = refs
    else:
        (u_ref, bb_ref, cc_ref, eb_ref, ec_ref, lam_ref, dsk_ref, wglu_ref, y_ref, fin_ref,
         sbuf, yacc, bmat, cmat) = refs
    n = S5_STATE
    rows = S5_ROWS
    steps = S5_STEPS
    blk = 256
    n_blk = steps * rows // blk
    n_seg = 4

    yacc[...] = u_ref[0] * dsk_ref[0]
    ub = u_ref[0].astype(BF16)
    b_keep = (lax.broadcasted_iota(jnp.int32, (S5_CH, 2 * n), 0) // S5_GROUP
              == (lax.broadcasted_iota(jnp.int32, (S5_CH, 2 * n), 1) % n) // S5_P)
    c_keep = ((lax.broadcasted_iota(jnp.int32, (2 * n, S5_CH), 0) % n) // S5_P
              == lax.broadcasted_iota(jnp.int32, (2 * n, S5_CH), 1) // S5_GROUP)

    for d in range(2):
        rev = d == 1
        bmat[...] = jnp.where(b_keep, jnp.dot(bb_ref[0, d].astype(BF16), eb_ref[...], preferred_element_type=F32),
                              0.0).astype(BF16)
        cmat[...] = jnp.where(c_keep, jnp.dot(cc_ref[0, d].astype(BF16), ec_ref[...], preferred_element_type=F32),
                              0.0).astype(BF16)
        for i in range(n_blk):
            sbuf[i * blk:(i + 1) * blk, :] = jnp.dot(ub[i * blk:(i + 1) * blk], bmat[...],
                                                     preferred_element_type=F32)
        lam = lam_ref[0, d]
        lr = jnp.broadcast_to(lam[:, :n], (rows, n))
        li = jnp.broadcast_to(lam[:, n:], (rows, n))

        def step(i, carry, rev=rev, lr=lr, li=li):
            sr, si = carry
            t = (steps - 1 - i) if rev else i
            off = pl.multiple_of(t * rows, rows)
            pr, pi = _cmul(lr, li, sr, si)
            nr = pr + sbuf[pl.ds(off, rows), 0:n]
            ni = pi + sbuf[pl.ds(off, rows), n:2 * n]
            sbuf[pl.ds(off, rows), 0:n] = nr
            sbuf[pl.ds(off, rows), n:2 * n] = ni
            return nr, ni

        zero = jnp.zeros((rows, n), F32)
        sr, si = lax.fori_loop(0, steps, step, (zero, zero), unroll=2)

        if not segmented:
            fin_ref[0, d, :, 0:n] = sr
            fin_ref[0, d, :, n:2 * n] = si
        else:
            lam_r, lam_i = lam[:, :n], lam[:, n:]
            row8 = lax.broadcasted_iota(jnp.int32, (rows, 1), 0)
            cr, ci = lam_r, lam_i
            acc_r = jnp.broadcast_to(cr, (rows, n))
            acc_i = jnp.broadcast_to(ci, (rows, n))
            for j in range(1, rows):
                cr, ci = _cmul(cr, ci, lam_r, lam_i)
                acc_r = jnp.where(row8 >= j, jnp.broadcast_to(cr, (rows, n)), acc_r)
                acc_i = jnp.where(row8 >= j, jnp.broadcast_to(ci, (rows, n)), acc_i)
            pw[0:rows, 0:n] = acc_r
            pw[0:rows, n:2 * n] = acc_i
            size = rows
            while size < steps:
                tr = pw[size - 1:size, 0:n]
                ti = pw[size - 1:size, n:2 * n]
                xr, xi = _cmul(pw[0:size, 0:n], pw[0:size, n:2 * n], tr, ti)
                pw[size:2 * size, 0:n] = xr
                pw[size:2 * size, n:2 * n] = xi
                size *= 2
            end_off = 0 if rev else (steps - 1) * rows
            loc_r = sbuf[end_off:end_off + rows, 0:n]
            loc_i = sbuf[end_off:end_off + rows, n:2 * n]
            pl_r = pw[steps - 1:steps, 0:n]
            pl_i = pw[steps - 1:steps, n:2 * n]
            s0r = s0_ref[0, d, :, 0:n]
            s0i = s0_ref[0, d, :, n:2 * n]
            seg = row8 // 2
            first = (seg == n_seg - 1) if rev else (seg == 0)
            shift = (rows - 2) if rev else 2
            cin_r, cin_i = s0r, s0i
            for _ in range(n_seg - 1):
                fr, fi = _cmul(jnp.broadcast_to(pl_r, (rows, n)), jnp.broadcast_to(pl_i, (rows, n)), cin_r, cin_i)
                tru_r = loc_r + fr
                tru_i = loc_i + fi
                cin_r = jnp.where(first, s0r, pltpu.roll(tru_r, shift, 0))
                cin_i = jnp.where(first, s0i, pltpu.roll(tru_i, shift, 0))

            def fix(tb, carry, rev=rev, cin_r=cin_r, cin_i=cin_i):
                pb = (steps // rows - 1 - tb) if rev else tb
                poff = pl.multiple_of(pb * rows, rows)
                p_r = pw[pl.ds(poff, rows), 0:n]
                p_i = pw[pl.ds(poff, rows), n:2 * n]
                for j in range(rows):
                    jj = rows - 1 - j if rev else j
                    off = pl.multiple_of((tb * rows + j) * rows, rows)
                    fr, fi = _cmul(jnp.broadcast_to(p_r[jj:jj + 1], (rows, n)),
                                   jnp.broadcast_to(p_i[jj:jj + 1], (rows, n)), cin_r, cin_i)
                    sbuf[pl.ds(off, rows), 0:n] += fr
                    sbuf[pl.ds(off, rows), n:2 * n] += fi
                return carry

            lax.fori_loop(0, steps // rows, fix, 0)

        for i in range(n_blk):
            yacc[i * blk:(i + 1) * blk, :] += jnp.dot(sbuf[i * blk:(i + 1) * blk, :].astype(BF16), cmat[...],
                                                      preferred_element_type=F32)

    g = jax.nn.gelu(yacc[...], approximate=True)
    y_ref[0] = g * jax.nn.sigmoid(_bdot(g, wglu_ref[0]))


@functools.lru_cache(None)
def _s5_spread():
    eb = np.zeros((2 * S5_P, 2 * S5_STATE), np.float32)
    for half in range(2):
        for g in range(S5_G):
            c0 = half * S5_STATE + g * S5_P
            eb[half * S5_P:(half + 1) * S5_P, c0:c0 + S5_P] = np.eye(S5_P)
    ec = np.zeros((LANES, S5_CH), np.float32)
    for g in range(S5_G):
        ec[:S5_GROUP, g * S5_GROUP:(g + 1) * S5_GROUP] = np.eye(S5_GROUP)
    return eb, ec


def _s5(u_tm, bb, cc, lam, dskip, wglu, s0, l):
    n_grp = u_tm.shape[0]
    n_rows = S5_STEPS * S5_ROWS
    segmented = s0 is not None
    eb, ec = (jnp.asarray(m).astype(BF16) for m in _s5_spread())
    full = lambda a: pl.BlockSpec(a.shape, lambda g: (0,) * a.ndim)
    layer = lambda a: _layer_spec(a.shape[1:], l)
    in_specs = [pl.BlockSpec((1, n_rows, S5_CH), lambda g: (g, 0, 0)),
                layer(bb), layer(cc), full(eb), full(ec), layer(lam), layer(dskip), layer(wglu)]
    args = [u_tm, bb, cc, eb, ec, lam, dskip, wglu]
    out_specs = [pl.BlockSpec((1, n_rows, S5_CH), lambda g: (g, 0, 0))]
    out_shape = [jax.ShapeDtypeStruct((n_grp, n_rows, S5_CH), F32)]
    scratch = [pltpu.VMEM((n_rows, 2 * S5_STATE), F32), pltpu.VMEM((n_rows, S5_CH), F32),
               pltpu.VMEM((S5_CH, 2 * S5_STATE), BF16), pltpu.VMEM((2 * S5_STATE, S5_CH), BF16)]
    if segmented:
        in_specs.append(layer(s0))
        args.append(s0)
        scratch.append(pltpu.VMEM((S5_STEPS, 2 * S5_STATE), F32))
    else:
        out_specs.append(pl.BlockSpec((1, 2, S5_ROWS, 2 * S5_STATE), lambda g: (g, 0, 0, 0)))
        out_shape.append(jax.ShapeDtypeStruct((n_grp, 2, S5_ROWS, 2 * S5_STATE), F32))
    return pl.pallas_call(
        functools.partial(_s5_kernel, segmented=segmented),
        grid=(n_grp,),
        in_specs=in_specs,
        out_specs=out_specs,
        out_shape=out_shape,
        scratch_shapes=scratch,
        compiler_params=_cparams(("arbitrary",)),
        name="s5_seg" if segmented else "s5_ctx",
    )(*args)


def _s5_params(a_re, a_im, log_dt, b_re, b_im, c_re, c_im):
    dt = jnp.exp(log_dt)[..., None]
    mag = jnp.exp(a_re * dt)
    lb_re = mag * jnp.cos(a_im * dt)
    lb_im = mag * jnp.sin(a_im * dt)
    den = a_re * a_re + a_im * a_im
    nr, ni = lb_re - 1.0, lb_im
    k_re = (nr * a_re + ni * a_im) / den
    k_im = (ni * a_re - nr * a_im) / den
    bb_re = k_re[..., None] * b_re - k_im[..., None] * b_im
    bb_im = k_re[..., None] * b_im + k_im[..., None] * b_re
    to_gc_p = lambda m: m.transpose(0, 1, 2, 4, 3).reshape(DEPTH, 2, S5_CH, S5_P)
    bb = jnp.concatenate([to_gc_p(bb_re), to_gc_p(bb_im)], axis=3)
    to_gp_c = lambda m: m.transpose(0, 1, 2, 4, 3).reshape(DEPTH, 2, S5_STATE, S5_GROUP)
    cc = jnp.concatenate([to_gp_c(c_re), -to_gp_c(c_im)], axis=2)
    cc = jnp.pad(cc, ((0, 0), (0, 0), (0, 0), (0, LANES - S5_GROUP)))
    lam = jnp.concatenate([lb_re.reshape(DEPTH, 2, 1, S5_STATE), lb_im.reshape(DEPTH, 2, 1, S5_STATE)], axis=3)
    return bb, cc, lam


def _outproj_kernel(*refs, n_x):
    x_refs = refs[:n_x]
    (hyc_ref, hyl_ref, mlc_ref, mll_ref, s5c_ref, s5l_ref, w_ref, mod_ref, g_ref,
     rw_ref, rb_ref, xn_ref, h2e_ref, best_ref, rank_ref, cnt_ref, wb, cnt_s) = refs[n_x:]
    step = pl.program_id(0)

    @pl.when(step == 0)
    def _():
        wb[...] = w_ref[0].astype(BF16)
        cnt_s[...] = jnp.zeros_like(cnt_s)

    is_ctx = step < T_CTX // TM
    pick = lambda c_ref, l_ref: jnp.where(is_ctx, c_ref[...], l_ref[...]).astype(BF16)
    mod = mod_ref[0, 0]
    a, b = HY_CH, HY_CH + ML_W
    mix = (jnp.dot(pick(hyc_ref, hyl_ref), wb[0:a, :], preferred_element_type=F32)
           + jnp.dot(pick(mlc_ref, mll_ref), wb[a:b, :], preferred_element_type=F32)
           + jnp.dot(pick(s5c_ref, s5l_ref), wb[b:, :], preferred_element_type=F32))
    xn = _x_tile(step, x_refs) + mod[2:3] * mix
    xn_ref[...] = xn
    h2 = _rms(xn, g_ref[0]) * (1.0 + mod[4:5]) + mod[3:4]
    h2e_ref[:, 0:D] = h2
    h_hi, h_lo = _split_bf16(h2)
    r_hi, r_lo = _split_bf16(rw_ref[...])
    logits = (lax.dot_general(r_hi, h_hi, _NT, preferred_element_type=F32)
              + (lax.dot_general(r_hi, h_lo, _NT, preferred_element_type=F32)
                 + lax.dot_general(r_lo, h_hi, _NT, preferred_element_type=F32)))
    ex = jnp.exp(logits - jnp.max(logits, axis=0, keepdims=True))
    probs = ex / jnp.sum(ex, axis=0, keepdims=True)
    sel = probs + rb_ref[...]
    best = None
    best_score = None
    for g in range(N_GROUPS):
        r = [sel[g * GROUP_SIZE + i:g * GROUP_SIZE + i + 1, :] for i in range(GROUP_SIZE)]
        score = None
        for i in range(GROUP_SIZE):
            for j in range(i + 1, GROUP_SIZE):
                pair = r[i] + r[j]
                score = pair if score is None else jnp.maximum(score, pair)
        if g == 0:
            best, best_score = jnp.zeros_like(score, dtype=jnp.int32), score
        else:
            upd = score > best_score
            best = jnp.where(upd, g, best)
            best_score = jnp.where(upd, score, best_score)
    eid = lax.broadcasted_iota(jnp.int32, (N_EXPERTS, 1), 0)
    masked = jnp.where(eid // GROUP_SIZE == best, sel, -jnp.inf)
    m1 = jnp.max(masked, axis=0, keepdims=True)
    i1 = jnp.min(jnp.where(masked == m1, eid, N_EXPERTS), axis=0, keepdims=True)
    masked2 = jnp.where(eid == i1, -jnp.inf, masked)
    m2 = jnp.max(masked2, axis=0, keepdims=True)
    i2 = jnp.min(jnp.where(masked2 == m2, eid, N_EXPERTS), axis=0, keepdims=True)
    p1 = jnp.sum(jnp.where(eid == i1, probs, 0.0), axis=0, keepdims=True)
    p2 = jnp.sum(jnp.where(eid == i2, probs, 0.0), axis=0, keepdims=True)
    tot = p1 + p2
    comb = jnp.where(eid == i1, p1 / tot, 0.0) + jnp.where(eid == i2, p2 / tot, 0.0)
    comb = jnp.concatenate([comb, jnp.zeros((LANES - N_EXPERTS, comb.shape[1]), F32)], axis=0)
    h2e_ref[:, D:] = comb.T
    best_ref[...] = best
    gid = lax.broadcasted_iota(jnp.int32, (8, 1), 0)
    onehot = (gid == best).astype(F32)
    cum = _cumsum_lanes(onehot, TM, False)
    run = cnt_s[:, 0:1]
    rank_ref[...] = jnp.sum(onehot * (cum - onehot + run), axis=0, keepdims=True).astype(jnp.int32)
    cnt_s[...] = jnp.broadcast_to(run + cum[:, TM - 1:TM], cnt_s.shape)
    cnt_ref[...] = cnt_s[...]


def _outproj(x, y_hy, y_ml, y_s5, w_out, l, mod, g2, rw_t, rb):
    full = lambda a: pl.BlockSpec(a.shape, lambda i: (0,) * a.ndim)
    n_ctx = T_CTX // TM
    ctx = lambda w: pl.BlockSpec((TM, w), lambda i: (jnp.minimum(i, n_ctx - 1), 0))
    lat = lambda w: pl.BlockSpec((TM, w), lambda i: (jnp.maximum(i - n_ctx, 0), 0))
    tok = lambda w: pl.BlockSpec((TM, w), lambda i: (i, 0))
    row = pl.BlockSpec((1, TM), lambda i: (0, i))
    return pl.pallas_call(
        functools.partial(_outproj_kernel, n_x=len(x)),
        grid=(T_ALL // TM,),
        in_specs=_x_specs(x) + [ctx(HY_CH), lat(HY_CH), ctx(ML_W), lat(ML_W), ctx(S5_CH), lat(S5_CH),
                  pl.BlockSpec((1, D, D), lambda i: (l, 0, 0), pipeline_mode=pl.Buffered(1)),
                  _mod_spec(l, TM), _layer_spec((1, D), l), full(rw_t), full(rb)],
        out_specs=[tok(D), tok(D + LANES), row, row, pl.BlockSpec((8, LANES), lambda i: (0, 0))],
        out_shape=[jax.ShapeDtypeStruct((T_ALL, D), F32),
                   jax.ShapeDtypeStruct((T_ALL, D + LANES), F32),
                   jax.ShapeDtypeStruct((1, T_ALL), jnp.int32),
                   jax.ShapeDtypeStruct((1, T_ALL), jnp.int32),
                   jax.ShapeDtypeStruct((8, LANES), F32)],
        scratch_shapes=[pltpu.VMEM((D, D), BF16), pltpu.VMEM((8, LANES), F32)],
        compiler_params=_cparams(("arbitrary",)),
        name="outproj_router",
    )(*x, *y_hy, *y_ml, *y_s5, w_out, mod, g2, rw_t, rb)


def _gather_rows(idx_ref, idx_base, src_ref, dst_ref, n_rows):
    def body(r8, carry):
        base = pl.multiple_of(r8 * 8, 8)
        for k in range(8):
            idx = idx_ref[idx_base + base + k]
            dst_ref[pl.ds(base + k, 1), :] = src_ref[pl.ds(idx, 1), :]
        return carry

    lax.fori_loop(0, n_rows // 8, body, 0)


def _dispatch(best, rank, cnt):
    tm = TM_MOE
    g = best.reshape(T_ALL)
    cnt = cnt[:N_GROUPS, 0].astype(jnp.int32)
    n_tile_g = (cnt + tm - 1) // tm
    tile_end = jnp.cumsum(n_tile_g)
    row_off = (tile_end - n_tile_g) * tm
    pos = rank.reshape(T_ALL)
    for k in range(N_GROUPS):
        pos = pos + jnp.where(g == k, row_off[k], 0)
    src = jnp.zeros((MOE_SLOTS,), jnp.int32).at[pos].set(jnp.arange(T_ALL, dtype=jnp.int32))
    tiles = jnp.arange(MOE_SLOTS // tm, dtype=jnp.int32)
    tile_group = jnp.minimum(jnp.sum((tiles[:, None] >= tile_end[None, :]).astype(jnp.int32), axis=1), N_GROUPS - 1)
    return pos, src, tile_group.astype(jnp.int32), tile_end[N_GROUPS - 1:].astype(jnp.int32)


def _moe_kernel(src_ref, tg_ref, nt_ref, h_ref, wg_ref, wu_ref, wd_ref, ys_ref, g_a, x_a, g_b, x_b, acc):
    i = pl.program_id(0)
    j = pl.program_id(1)
    tm = TM_MOE
    quarter = tm // GROUP_SIZE
    n_live = nt_ref[0]

    @pl.when(jnp.logical_and(i == 0, j == 0))
    def _():
        _gather_rows(src_ref, 0, h_ref, g_a, tm)
        x_a[...] = g_a[:, 0:D].astype(BF16)

    def step(cur_g, cur_x, nxt_g, nxt_x):
        @pl.when(j == 0)
        def _():
            acc[...] = jnp.zeros_like(acc)

        e = tg_ref[i] * GROUP_SIZE + j
        lane = lax.broadcasted_iota(jnp.int32, (1, LANES), 1)
        wg = wg_ref[0, 0].astype(BF16)
        wu = wu_ref[0, 0].astype(BF16)
        wd = wd_ref[0, 0].astype(BF16)
        sub = MOE_SUB
        for s in range(tm // sub):
            rs = slice(s * sub, (s + 1) * sub)
            hb = cur_x[rs, :]
            ce = jnp.sum(jnp.where(lane == e, cur_g[rs, D:], 0.0), axis=1, keepdims=True)
            hid = _silu(jnp.dot(hb, wg, preferred_element_type=F32)) * jnp.dot(hb, wu, preferred_element_type=F32)
            acc[rs, :] += jnp.dot((hid * ce).astype(BF16), wd, preferred_element_type=F32)

        r0 = pl.multiple_of(j * quarter, quarter)
        base = jnp.minimum(i + 1, n_live - 1) * tm + r0
        for k in range(quarter):
            nxt_g[pl.ds(r0 + k, 1), :] = h_ref[pl.ds(src_ref[base + k], 1), :]
        nxt_x[pl.ds(r0, quarter), :] = nxt_g[pl.ds(r0, quarter), 0:D].astype(BF16)

        @pl.when(j == GROUP_SIZE - 1)
        def _():
            for c in range(D // LANES):
                ys_ref[pl.ds(c, tm, stride=D // LANES), :] = acc[:, c * LANES:(c + 1) * LANES]

    live = i < n_live

    @pl.when(jnp.logical_and(live, i % 2 == 0))
    def _():
        step(g_a, x_a, g_b, x_b)

    @pl.when(jnp.logical_and(live, i % 2 == 1))
    def _():
        step(g_b, x_b, g_a, x_a)

    @pl.when(jnp.logical_and(jnp.logical_not(live), j == GROUP_SIZE - 1))
    def _():
        ys_ref[...] = jnp.zeros_like(ys_ref)


def _moe(h2e, src, tile_group, n_tiles, wg, wu, wd, l):
    tm = TM_MOE

    def w_map(i, j, src, tg, nt):
        live = i < nt[0]
        ii = jnp.minimum(i, nt[0] - 1)
        return (l, tg[ii] * GROUP_SIZE + jnp.where(live, j, GROUP_SIZE - 1), 0, 0)

    return pl.pallas_call(
        _moe_kernel,
        grid_spec=pltpu.PrefetchScalarGridSpec(
            num_scalar_prefetch=3,
            grid=(MOE_SLOTS // tm, GROUP_SIZE),
            in_specs=[pl.BlockSpec(memory_space=pltpu.VMEM),
                      pl.BlockSpec((1, 1, D, D_EXPERT), w_map),
                      pl.BlockSpec((1, 1, D, D_EXPERT), w_map),
                      pl.BlockSpec((1, 1, D_EXPERT, D), w_map)],
            out_specs=pl.BlockSpec((tm * (D // LANES), LANES), lambda i, j, src, tg, nt: (i, 0)),
            scratch_shapes=[pltpu.VMEM((tm, D + LANES), F32), pltpu.VMEM((tm, D), BF16),
                            pltpu.VMEM((tm, D + LANES), F32), pltpu.VMEM((tm, D), BF16),
                            pltpu.VMEM((tm, D), F32)]),
        out_shape=jax.ShapeDtypeStruct((MOE_SLOTS * (D // LANES), LANES), F32),
        compiler_params=_cparams(("arbitrary", "arbitrary")),
        name="moe_experts",
    )(src, tile_group, n_tiles, h2e, wg, wu, wd)


def _combine_kernel(pos_ref, ys_ref, xn_ref, mod_ref, fg_ref, *rest, final):
    step = pl.program_id(0)
    if final:
        yc_ref, yl_ref, gbuf = rest
    else:
        out_ref, gbuf = rest
    n_lt = D // LANES

    def body(r8, carry):
        base = pl.multiple_of(r8 * 8, 8)
        for k in range(8):
            src = pl.multiple_of(pos_ref[step * TM + base + k] * n_lt, n_lt)
            gbuf[pl.ds(pl.multiple_of((base + k) * n_lt, n_lt), n_lt), :] = ys_ref[pl.ds(src, n_lt), :]
        return carry

    lax.fori_loop(0, TM // 8, body, 0)
    moe = jnp.concatenate([gbuf[pl.ds(c, TM, stride=n_lt), :] for c in range(n_lt)], axis=1)
    out = xn_ref[...] + mod_ref[0, 0][5:6] * moe
    if final:
        y = _rms(out, fg_ref[...])

        @pl.when(step < T_CTX // TM)
        def _():
            yc_ref[...] = y

        @pl.when(step >= T_CTX // TM)
        def _():
            yl_ref[...] = y
    else:
        out_ref[...] = out


def _combine(pos, ys, xn, mod, fg, l, final):
    spec = pl.BlockSpec((TM, D), lambda i, pos: (i, 0))
    n_ctx = T_CTX // TM
    if final:
        out_specs = [pl.BlockSpec((TM, D), lambda i, pos: (jnp.minimum(i, n_ctx - 1), 0)),
                     pl.BlockSpec((TM, D), lambda i, pos: (jnp.maximum(i - n_ctx, 0), 0))]
        out_shape = [jax.ShapeDtypeStruct((T_CTX, D), F32), jax.ShapeDtypeStruct((T_LAT, D), F32)]
    else:
        out_specs = [spec]
        out_shape = [jax.ShapeDtypeStruct((T_ALL, D), F32)]
    return pl.pallas_call(
        functools.partial(_combine_kernel, final=final),
        grid_spec=pltpu.PrefetchScalarGridSpec(
            num_scalar_prefetch=1,
            grid=(T_ALL // TM,),
            in_specs=[pl.BlockSpec(memory_space=pltpu.VMEM),
                      spec,
                      _mod_spec(l, TM),
                      pl.BlockSpec((1, D), lambda i, pos: (0, 0))],
            out_specs=out_specs,
            scratch_shapes=[pltpu.VMEM((TM * (D // LANES), LANES), F32)]),
        out_shape=out_shape,
        compiler_params=_cparams(("arbitrary",)),
        name="moe_combine",
    )(pos, ys, xn, mod, fg)


@functools.lru_cache(None)
def _pos_embed():
    rows = L_LAT // GRID_W
    r = np.repeat(np.arange(rows, dtype=np.float64), GRID_W)
    col = np.tile(np.arange(GRID_W, dtype=np.float64), rows)
    quarter = D // 4
    freq = np.exp(-math.log(POS_BASE) * np.arange(quarter, dtype=np.float64) / quarter)
    ar = r[:, None] * freq[None]
    ac = col[:, None] * freq[None]
    emb = np.concatenate([np.sin(ar), np.cos(ar), np.sin(ac), np.cos(ac)], axis=-1)
    return emb.astype(np.float32)


def _to_time_major_ctx(a):
    c = a.shape[-1]
    a = a.reshape(2, S5_ROWS, L_CTX, c).transpose(0, 2, 1, 3)
    return a.reshape(2, L_CTX * S5_ROWS, c)


def _from_time_major_ctx(a):
    c = a.shape[-1]
    a = a.reshape(2, L_CTX, S5_ROWS, c).transpose(0, 2, 1, 3)
    return a.reshape(T_CTX, c)


def _to_time_major_lat(a):
    c = a.shape[-1]
    a = a.reshape(N_LAT_SEQ, 4, S5_STEPS, c).transpose(2, 1, 0, 3)
    return a.reshape(1, S5_STEPS * S5_ROWS, c)


def _from_time_major_lat(a):
    c = a.shape[-1]
    a = a.reshape(S5_STEPS, 4, N_LAT_SEQ, c).transpose(2, 1, 0, 3)
    return a.reshape(T_LAT, c)


def kernel(x_prompt, x_sample, c, state_mlstm_C, state_mlstm_n, state_mlstm_m, state_s5_re, state_s5_im, c_ctx, w_ada, b_ada, norm1_g, norm2_g, final_g, w_in, w_out, hy_short, hy_fw1, hy_fb1, hy_fw2, hy_fb2, hy_fw3, hy_log_decay, hy_bias, ml_short, ml_gate_bias, ml_norm_g, s5_a_re, s5_a_im, s5_log_dt, s5_b_re, s5_b_im, s5_c_re, s5_c_im, s5_d, s5_w_glu, router_w, router_b, moe_w_gate, moe_w_up, moe_w_down):
    x = (x_prompt.reshape(T_CTX, D), x_sample.reshape(T_LAT, D), jnp.asarray(_pos_embed()))
    w_in_t = jnp.swapaxes(w_in, 1, 2)
    cc =jnp.concatenate([c_ctx[None], c, jnp.zeros((8 - 1 - N_LAT_SEQ, D), F32)], axis=0)
    mod = _ada(cc, w_ada, b_ada).reshape(DEPTH, 8, 6, D)
    rw_t = router_w.T
    rb = router_b.reshape(N_EXPERTS, 1)
    fg = final_g.reshape(1, D)
    lat_blk = T_CTX // L_LAT
    g1 = norm1_g.reshape(DEPTH, 1, D)
    g2 = norm2_g.reshape(DEPTH, 1, D)

    w1p = jnp.pad(hy_fw1, ((0, 0), (0, LANES - HY_EMB), (0, 0)))
    b1 = hy_fb1.reshape(DEPTH, 1, HY_FILTER_W)
    b2 = hy_fb2.reshape(DEPTH, 1, HY_FILTER_W)
    ld = hy_log_decay.reshape(DEPTH, 1, 4 * HY_CH)
    hy_spec = {L: _hy_filter(L, w1p, b1, hy_fw2, b2, hy_fw3, ld) for L in (L_CTX, L_LAT)}
    gb = jnp.pad(ml_gate_bias.reshape(DEPTH, 1, 16), ((0, 0), (0, 0), (0, GATE_PAD - 16)))
    gbt = ml_gate_bias.reshape(DEPTH, 16, 1)
    ng = ml_norm_g.reshape(DEPTH, 1, ML_W)
    ml_state = (state_mlstm_C.reshape(N_LAT_SEQ, DEPTH, 2 * ML_HEADS, ML_DH, ML_DH),
                state_mlstm_n.reshape(N_LAT_SEQ, DEPTH, 2 * ML_HEADS, ML_DH),
                jnp.broadcast_to(state_mlstm_m.reshape(N_LAT_SEQ, DEPTH, 2 * ML_HEADS, 1),
                                 (N_LAT_SEQ, DEPTH, 2 * ML_HEADS, LANES)))
    bb, cc_s5, lam = _s5_params(s5_a_re, s5_a_im, s5_log_dt, s5_b_re, s5_b_im, s5_c_re, s5_c_im)
    dsk = s5_d.reshape(DEPTH, 1, S5_CH)
    wglu = s5_w_glu.astype(BF16)
    s0 = jnp.concatenate([state_s5_re.reshape(N_LAT_SEQ, DEPTH, 2, S5_STATE),
                          state_s5_im.reshape(N_LAT_SEQ, DEPTH, 2, S5_STATE)], axis=-1)
    s0 = jnp.tile(s0.transpose(1, 2, 0, 3), (1, 1, 4, 1))

    new_c, new_n, new_m, new_re, new_im = [], [], [], [], []
    y_prompt = y_sample = None
    for l in range(DEPTH):
        u_hy, qk, v, o, u_s5, gates, gates_t = _inproj(x, g1, mod, w_in_t, l)

        y_hy = [_hyena(u_hy, L, n_seq, blk0, hy_short, hy_bias, *hy_spec[L], l)
                for L, n_seq, blk0 in ((L_CTX, N_CTX_SEQ, 0), (L_LAT, N_LAT_SEQ, lat_blk))]

        yc, cc_, nc_, mc_ = _mlstm(qk, v, o, gates, gates_t, L_CTX, N_CTX_SEQ, 0, ml_short, gb, gbt, ng, None, l,
                                   emit_state=True)
        (yl,) = _mlstm(qk, v, o, gates, gates_t, L_LAT, N_LAT_SEQ, lat_blk, ml_short, gb, gbt, ng, ml_state, l)
        y_ml = (yc, yl)
        new_c.append(cc_.reshape(N_CTX_SEQ, 2, ML_HEADS, ML_DH, ML_DH))
        new_n.append(nc_.reshape(N_CTX_SEQ, 2, ML_HEADS, ML_DH))
        new_m.append(mc_[:, :, 0].reshape(N_CTX_SEQ, 2, ML_HEADS))

        ys_c, fin = _s5(_to_time_major_ctx(u_s5[:T_CTX]), bb, cc_s5, lam, dsk, wglu, None, l)
        (ys_l,) = _s5(_to_time_major_lat(u_s5[T_CTX:]), bb, cc_s5, lam, dsk, wglu, s0, l)
        y_s5 = (_from_time_major_ctx(ys_c), _from_time_major_lat(ys_l))
        fin = fin.transpose(0, 2, 1, 3).reshape(N_CTX_SEQ, 2, 2 * S5_STATE)
        new_re.append(fin[..., :S5_STATE].reshape(N_CTX_SEQ, 2, S5_G, S5_P))
        new_im.append(fin[..., S5_STATE:].reshape(N_CTX_SEQ, 2, S5_G, S5_P))

        xn, h2e, best, rank, cnt = _outproj(x, y_hy, y_ml, y_s5, w_out, l, mod, g2, rw_t, rb)
        pos, src, tile_group, n_tiles = _dispatch(best, rank, cnt)
        ys = _moe(h2e, src, tile_group, n_tiles, moe_w_gate, moe_w_up, moe_w_down, l)
        res = _combine(pos, ys, xn, mod, fg, l, l == DEPTH - 1)
        if l == DEPTH - 1:
            y_prompt = res[0].reshape(N_CTX_SEQ, L_CTX, D)
            y_sample = res[1].reshape(N_LAT_SEQ, L_LAT, D)
        else:
            x = (res[0],)

    return (y_prompt, y_sample, jnp.stack(new_c, axis=1), jnp.stack(new_n, axis=1), jnp.stack(new_m, axis=1),
            jnp.stack(new_re, axis=1), jnp.stack(new_im, axis=1))
```

```python
import functools
import math

import numpy as np
import jax
import jax.numpy as jnp
from jax import lax
from jax.experimental import pallas as pl
from jax.experimental.pallas import tpu as pltpu

F32 = jnp.float32
BF16 = jnp.bfloat16
HIGHEST = lax.Precision.HIGHEST

D = 1024
N_CTX_SEQ, L_CTX = 16, 256
N_LAT_SEQ, L_LAT = 2, 1024
T_CTX = N_CTX_SEQ * L_CTX
T_LAT = N_LAT_SEQ * L_LAT
T_ALL = T_CTX + T_LAT
DEPTH = 2
EPS = 1e-6
GRID_W = 64
POS_BASE = 10000.0
HY_CH = 256
HY_EMB = 33
HY_FILTER_W = 64
ML_HEADS = 4
ML_DH = 128
ML_W = ML_HEADS * ML_DH
S5_CH = 256
S5_G = 16
S5_GROUP = 16
S5_P = 64
S5_STATE = S5_G * S5_P
N_EXPERTS = 16
N_GROUPS = 4
GROUP_SIZE = N_EXPERTS // N_GROUPS
D_EXPERT = 512
OFF_HY = 0
OFF_QK = 3 * HY_CH
OFF_V = OFF_QK + 2 * ML_W
OFF_O = OFF_V + ML_W
OFF_G = OFF_O + ML_W
OFF_S5 = OFF_G + 16
IN_W = OFF_S5 + S5_CH
LANES = 128
GATE_PAD = LANES

TM = 512
TM_MOE = 512
MOE_SLOTS = T_ALL + N_GROUPS * TM_MOE
MOE_SUB = 512
HY_ROWS = 1024
ML_CHUNK = 256
S5_ROWS = 8
S5_STEPS = 256
VMEM_LIMIT = 56 * 1024 * 1024


def _cparams(sem, vmem=VMEM_LIMIT):
    if sem is None:
        return pltpu.CompilerParams(vmem_limit_bytes=vmem)
    return pltpu.CompilerParams(dimension_semantics=sem, vmem_limit_bytes=vmem)


def _bdot(a, b):
    return jnp.dot(a.astype(BF16), b.astype(BF16), preferred_element_type=F32)


def _split_bf16(x):
    hi = x.astype(BF16)
    return hi, (x - hi.astype(F32)).astype(BF16)


def _silu(x):
    return x * jax.nn.sigmoid(x)


def _rms(x, g):
    return x * lax.rsqrt(jnp.mean(x * x, axis=-1, keepdims=True) + EPS) * g


def _log_sigmoid(x):
    return jnp.minimum(x, 0.0) - jnp.log1p(jnp.exp(-jnp.abs(x)))


def _conv3(u, w, n_rows, seq_len):
    row = lax.broadcasted_iota(jnp.int32, (n_rows, 1), 0) % seq_len
    prev = jnp.where(row == 0, 0.0, pltpu.roll(u, 1, 0))
    nxt = jnp.where(row == seq_len - 1, 0.0, pltpu.roll(u, n_rows - 1, 0))
    return prev * w[0:1] + u * w[1:2] + nxt * w[2:3]


def _ada_kernel(c_ref, w_ref, b_ref, o_ref):
    o_ref[0] = _bdot(_silu(c_ref[...]), w_ref[0]) + b_ref[0]


def _ada(cc, w_ada, b_ada):
    tn = 1536
    return pl.pallas_call(
        _ada_kernel,
        grid=(DEPTH, 6 * D // tn),
        in_specs=[pl.BlockSpec((8, D), lambda l, j: (0, 0)),
                  pl.BlockSpec((1, D, tn), lambda l, j: (l, 0, j)),
                  pl.BlockSpec((1, 1, tn), lambda l, j: (l, 0, j))],
        out_specs=pl.BlockSpec((1, 8, tn), lambda l, j: (l, 0, j)),
        out_shape=jax.ShapeDtypeStruct((DEPTH, 8, 6 * D), F32),
        compiler_params=_cparams(("arbitrary", "arbitrary")),
        name="ada_mod",
    )(cc, w_ada, b_ada.reshape(DEPTH, 1, 6 * D))


def _mod_row(i, tm):
    n_ctx = T_CTX // tm
    return jnp.where(i < n_ctx, 0, 1 + (i - n_ctx) // (L_LAT // tm))


_SEG = ((OFF_HY, OFF_QK - OFF_HY), (OFF_QK, OFF_V - OFF_QK), (OFF_V, OFF_O - OFF_V), (OFF_O, OFF_G - OFF_O))
TAIL_W = IN_W - OFF_G


N_CTX_TILES = T_CTX // TM
_NT = (((1,), (1,)), ((), ()))


def _x_specs(x):
    if len(x) == 1:
        return [pl.BlockSpec((TM, D), lambda i, *_: (i, 0))]
    per_seq = L_LAT // TM
    return [pl.BlockSpec((TM, D), lambda i, *_: (jnp.minimum(i, N_CTX_TILES - 1), 0)),
            pl.BlockSpec((TM, D), lambda i, *_: (jnp.maximum(i - N_CTX_TILES, 0), 0)),
            pl.BlockSpec((TM, D), lambda i, *_: (jnp.maximum(i - N_CTX_TILES, 0) % per_seq, 0))]


def _x_tile(step, x_refs):
    if len(x_refs) == 1:
        return x_refs[0][...]
    xc_ref, xl_ref, pos_ref = x_refs
    return jnp.where(step < N_CTX_TILES, xc_ref[...], xl_ref[...] + pos_ref[...])


def _inproj_kernel(*refs, n_x):
    x_refs = refs[:n_x]
    g_ref, mod_ref, w_ref, hy_ref, qk_ref, v_ref, o_ref, s5_ref, gt_ref, gtt_ref, wb = refs[n_x:]
    step = pl.program_id(0)

    @pl.when(step == 0)
    def _():
        wb[...] = w_ref[0].astype(BF16)

    mod = mod_ref[0, 0]
    h = _rms(_x_tile(step, x_refs), g_ref[0]) * (1.0 + mod[1:2]) + mod[0:1]
    hb = h.astype(BF16)
    for (a, w), ref in zip(_SEG, (hy_ref, qk_ref, v_ref, o_ref)):
        ref[...] = lax.dot_general(hb, wb[a:a + w, :], _NT, preferred_element_type=F32)
    tail = lax.dot_general(hb, wb[OFF_G:IN_W, :], _NT, preferred_element_type=F32)
    gates = tail[:, 0:GATE_PAD]
    gt_ref[...] = gates
    for c in range(TM // ML_CHUNK):
        gtt_ref[c] = gates[c * ML_CHUNK:(c + 1) * ML_CHUNK, :].T[0:16, :]
    s5_ref[...] = tail[:, OFF_S5 - OFF_G:TAIL_W]


def _layer_spec(shape, l):
    return pl.BlockSpec((1,) + tuple(shape), lambda *_: (l,) + (0,) * len(shape))


def _mod_spec(l, tm):
    return pl.BlockSpec((1, 1, 6, D), lambda i, *_: (l, _mod_row(i, tm), 0, 0))


def _inproj(x, g, mod, w_in_t, l):
    widths = [w for _, w in _SEG] + [S5_CH, GATE_PAD]
    cpt = TM // ML_CHUNK
    return pl.pallas_call(
        functools.partial(_inproj_kernel, n_x=len(x)),
        grid=(T_ALL // TM,),
        in_specs=_x_specs(x) + [
            _layer_spec((1, D), l), _mod_spec(l, TM),
            pl.BlockSpec((1, IN_W, D), lambda i: (l, 0, 0), pipeline_mode=pl.Buffered(1))],
        out_specs=[pl.BlockSpec((TM, w), lambda i: (i, 0)) for w in widths]
        + [pl.BlockSpec((cpt, 16, ML_CHUNK), lambda i: (i, 0, 0))],
        out_shape=[jax.ShapeDtypeStruct((T_ALL, w), F32) for w in widths]
        + [jax.ShapeDtypeStruct((T_ALL // ML_CHUNK, 16, ML_CHUNK), F32)],
        scratch_shapes=[pltpu.VMEM((IN_W, D), BF16)],
        compiler_params=_cparams(("arbitrary",)),
        name="norm_inproj",
    )(*x, g, mod, w_in_t)


@functools.lru_cache(None)
def _dft_mats(L):
    n = 2 * L
    k = np.arange(L)[:, None]
    t = np.arange(L)[None, :]
    ang = 2.0 * np.pi * ((k * t) % n) / n
    top = np.cos(ang)
    bot = -np.sin(ang)
    bot[0] = np.cos(np.pi * np.arange(L))
    fwd = np.concatenate([top, bot], 0)
    s = np.full((n, 1), 2.0 / n)
    s[0] = s[L] = 1.0 / n
    inv = (fwd * s).T
    return fwd.astype(np.float32), inv.astype(np.float32)


@functools.lru_cache(None)
def _hy_positions(L):
    t = np.linspace(0.0, 1.0, L)
    bands = (HY_EMB - 1) // 2
    f = np.linspace(1e-4, bands - 1, bands)
    w = 2.0 * np.pi * np.arange(L) / L
    ang = w[:, None] * f[None, :]
    z = np.concatenate([t[:, None], np.cos(ang), -np.sin(ang)], -1)
    zp = np.zeros((L, LANES))
    zp[:, :HY_EMB] = z
    return zp.astype(np.float32), t[:, None].astype(np.float32)


def _hy_filter_kernel(z_ref, t_ref, w1_ref, b1_ref, w2_ref, b2_ref, w3_ref, ld_ref, f_ref,
                      p_ref, q_ref, r_ref, *, L):
    h = jnp.sin(jnp.dot(z_ref[...], w1_ref[0], precision=HIGHEST, preferred_element_type=F32) + b1_ref[0])
    h = jnp.sin(jnp.dot(h, w2_ref[0], precision=HIGHEST, preferred_element_type=F32) + b2_ref[0])
    filt = jnp.dot(h, w3_ref[0], precision=HIGHEST, preferred_element_type=F32)
    filt = filt * jnp.exp(-t_ref[...] * jnp.exp(ld_ref[0]))
    c = HY_CH
    h_fwd = jnp.concatenate([filt[:, 0:c], filt[:, 2 * c:3 * c]], axis=1)
    h_bwd = jnp.concatenate([filt[:, c:2 * c], filt[:, 3 * c:4 * c]], axis=1)
    row = lax.broadcasted_iota(jnp.int32, (L, 1), 0)
    h_bwd = jnp.where(row == 0, 0.0, h_bwd)
    a = jnp.dot(f_ref[...], h_fwd.astype(BF16), preferred_element_type=F32)
    b = jnp.dot(f_ref[...], h_bwd.astype(BF16), preferred_element_type=F32)
    re = a[:L] + b[:L]
    im = a[L:] - b[L:]
    nyq = a[L:L + 1] + b[L:L + 1]
    p_ref[0] = re
    q_ref[0] = jnp.where(row == 0, 0.0, im)
    r_ref[0] = jnp.where(row == 0, nyq, re)


def _hy_filter(L, w1p, b1, w2, b2, w3, ld):
    z, t = _hy_positions(L)
    fwd = jnp.asarray(_dft_mats(L)[0]).astype(BF16)
    out = jax.ShapeDtypeStruct((DEPTH, L, 2 * HY_CH), F32)
    full = lambda a: pl.BlockSpec(a.shape, lambda l: (0,) * a.ndim)
    layer = lambda a: pl.BlockSpec((1,) + a.shape[1:], lambda l: (l,) + (0,) * (a.ndim - 1))
    return pl.pallas_call(
        functools.partial(_hy_filter_kernel, L=L),
        grid=(DEPTH,),
        in_specs=[full(z), full(t), layer(w1p), layer(b1), layer(w2), layer(b2), layer(w3), layer(ld), full(fwd)],
        out_specs=[pl.BlockSpec((1, L, 2 * HY_CH), lambda l: (l, 0, 0))] * 3,
        out_shape=[out, out, out],
        compiler_params=_cparams(("arbitrary",)),
        name=f"hyena_filter_{L}",
    )(z, t, w1p, b1, w2, b2, w3, ld, fwd)


def _hyena_kernel(u_ref, sw_ref, bias_ref, p_ref, q_ref, r_ref, f_ref, g_ref, o_ref, *, L, n_sub):
    c = HY_CH
    u = _conv3(u_ref[...], sw_ref[0], n_sub * L, L)
    for s in range(n_sub):
        rs = slice(s * L, (s + 1) * L)
        z = u[rs, 0:c]
        for o in range(2):
            gate = u[rs, (o + 1) * c:(o + 2) * c]
            zf = jnp.dot(f_ref[...], z.astype(BF16), preferred_element_type=F32)
            a, b = zf[:L], zf[L:]
            p = p_ref[0, :, o * c:(o + 1) * c]
            q = q_ref[0, :, o * c:(o + 1) * c]
            r = r_ref[0, :, o * c:(o + 1) * c]
            y_re = (a * p - b * q).astype(BF16)
            y_im = (a * q + b * r).astype(BF16)
            y = (jnp.dot(g_ref[:, :L], y_re, preferred_element_type=F32)
                 + jnp.dot(g_ref[:, L:], y_im, preferred_element_type=F32))
            z = gate * (y + bias_ref[0, o:o + 1, :] * z)
        o_ref[rs, :] = z


def _hyena(u_hy, L, n_seq, row_block0, sw, bias, p, q, r, l):
    n_sub = max(1, HY_ROWS // L)
    fwd, inv = (jnp.asarray(m).astype(BF16) for m in _dft_mats(L))
    full = lambda a: pl.BlockSpec(a.shape, lambda b: (0,) * a.ndim)
    layer = lambda a: _layer_spec(a.shape[1:], l)
    blk0 = row_block0 // n_sub
    return pl.pallas_call(
        functools.partial(_hyena_kernel, L=L, n_sub=n_sub),
        grid=(n_seq // n_sub,),
        in_specs=[pl.BlockSpec((n_sub * L, 3 * HY_CH), lambda b: (blk0 + b, 0)),
                  layer(sw), layer(bias), layer(p), layer(q), layer(r), full(fwd), full(inv)],
        out_specs=pl.BlockSpec((n_sub * L, HY_CH), lambda b: (b, 0)),
        out_shape=jax.ShapeDtypeStruct((n_seq * L, HY_CH), F32),
        compiler_params=_cparams(("arbitrary",)),
        name=f"hyena_{L}",
    )(u_hy, sw, bias, p, q, r, fwd, inv)


def _cumsum_rows(x, n, reverse):
    row = lax.broadcasted_iota(jnp.int32, (n, 1), 0)
    s = 1
    while s < n:
        if reverse:
            x = x + jnp.where(row < n - s, pltpu.roll(x, n - s, 0), 0.0)
        else:
            x = x + jnp.where(row >= s, pltpu.roll(x, s, 0), 0.0)
        s *= 2
    return x


def _cumsum_lanes(x, n, reverse):
    col = lax.broadcasted_iota(jnp.int32, (1, n), 1)
    s = 1
    while s < n:
        if reverse:
            x = x + jnp.where(col < n - s, pltpu.roll(x, n - s, 1), 0.0)
        else:
            x = x + jnp.where(col >= s, pltpu.roll(x, s, 1), 0.0)
        s *= 2
    return x


def _mlstm_kernel(*refs, L, has_state, emit_state):
    qk_ref, v_ref, o_ref, g_ref, gt_ref, sw_ref, gb_ref, gbt_ref, ng_ref = refs[:9]
    refs = refs[9:]
    if has_state:
        c0_ref, n0_ref, m0_ref = refs[:3]
        refs = refs[3:]
    y_ref = refs[0]
    if emit_state:
        cout_ref, nout_ref, mout_ref = refs[1:4]
        refs = refs[4:]
    else:
        refs = refs[1:]
    q_s, k_s, vt_s, ht_s, ct_s, n_s, m_s = refs
    tc = ML_CHUNK
    nc = L // tc
    nh = ML_HEADS
    dh = ML_DH

    qk = _silu(_conv3(qk_ref[...], sw_ref[0], L, L))
    q_s[...] = qk[:, :ML_W].astype(BF16)
    k_s[...] = (qk[:, ML_W:] * (dh ** -0.5)).astype(BF16)
    for c in range(nc):
        vt_s[c] = v_ref[c * tc:(c + 1) * tc, :].T.astype(BF16)

    for i in range(2 * nh):
        ct_s[i] = c0_ref[0, 0, i].T if has_state else jnp.zeros((dh, dh), F32)
    n_s[...] = n0_ref[0, 0] if has_state else jnp.zeros_like(n_s)
    m_s[...] = m0_ref[0, 0] if has_state else jnp.zeros_like(m_s)

    si = lax.broadcasted_iota(jnp.int32, (tc, tc), 0)
    ti = lax.broadcasted_iota(jnp.int32, (tc, tc), 1)

    for d in range(2):
        rev = d == 1
        mask = (si >= ti) if rev else (si <= ti)
        edge = 0 if rev else tc - 1

        def chunk(j, carry, d=d, rev=rev, mask=mask, edge=edge):
            cidx = (nc - 1 - j) if rev else j
            r0 = pl.multiple_of(cidx * tc, tc)
            pre = g_ref[pl.ds(r0, tc), :] + gb_ref[0]
            pre_t = gt_ref[cidx] + gbt_ref[0]
            cum = _cumsum_rows(_log_sigmoid(pre), tc, rev)
            cum_t = _cumsum_lanes(_log_sigmoid(pre_t), tc, rev)
            key_all = cum - pltpu.roll(pre, 8, 1)
            for h in range(nh):
                col = d * nh + h
                hs = slice(h * dh, (h + 1) * dh)
                key = key_all[:, 8 + col:9 + col]
                b_row = cum_t[8 + col:9 + col, :]
                b_end = b_row[:, edge:edge + 1]
                m_prev = m_s[col:col + 1, 0:1]
                dmat = jnp.where(mask, b_row - key, -jnp.inf)
                inter = b_row + m_prev
                m_row = jnp.maximum(inter, jnp.max(dmat, axis=0, keepdims=True))
                w_intra = jnp.exp(dmat - m_row)
                w_state = jnp.exp(inter - m_row)
                qh = q_s[pl.ds(r0, tc), hs]
                kh = k_s[pl.ds(r0, tc), hs]
                vt = vt_s[cidx, hs, :]
                ct_prev = ct_s[col]
                n_prev = n_s[col:col + 1, :]
                s = lax.dot_general(kh, qh, _NT, preferred_element_type=F32) * w_intra
                num = (jnp.dot(vt, s.astype(BF16), preferred_element_type=F32)
                       + w_state * lax.dot_general(ct_prev.astype(BF16), qh, _NT, preferred_element_type=F32))
                qn = lax.dot_general(jnp.broadcast_to(n_prev, (8, dh)).astype(BF16), qh, _NT,
                                     preferred_element_type=F32)[0:1]
                den = jnp.sum(s, axis=0, keepdims=True) + w_state * qn
                hout = num * (1.0 / jnp.maximum(jnp.abs(den), jnp.exp(-m_row)))
                if d == 0:
                    ht_s[cidx, hs, :] = hout
                else:
                    ht_s[cidx, hs, :] += hout
                m_new = jnp.maximum(b_end + m_prev, b_end - jnp.min(key, axis=0, keepdims=True))
                wg = jnp.exp(b_end - key - m_new)
                decay = jnp.exp(b_end + m_prev - m_new)
                kw = kh.astype(F32) * wg
                ct_s[col] = decay * ct_prev + jnp.dot(vt, kw.astype(BF16), preferred_element_type=F32)
                n_s[col:col + 1, :] = decay * n_prev + jnp.sum(kw, axis=0, keepdims=True)
                m_s[col:col + 1, :] = jnp.broadcast_to(m_new, (1, LANES))
            return carry

        lax.fori_loop(0, nc, chunk, 0)

    for c in range(nc):
        for h in range(nh):
            hs = slice(h * dh, (h + 1) * dh)
            rs = slice(c * tc, (c + 1) * tc)
            ht = ht_s[c, hs, :]
            hn = ht * lax.rsqrt(jnp.mean(ht * ht, axis=0, keepdims=True) + EPS)
            y_ref[rs, hs] = jax.nn.sigmoid(o_ref[rs, hs]) * (hn.T * ng_ref[0, :, hs])
    if emit_state:
        for i in range(2 * nh):
            cout_ref[0, i] = ct_s[i].T
        nout_ref[0] = n_s[...]
        mout_ref[0] = m_s[...]


def _mlstm(qk, v, o, gates, gates_t, L, n_seq, row_block0, sw, gb, gbt, ng, state, l, emit_state=False):
    nc = L // ML_CHUNK
    has_state = state is not None
    layer = lambda a: _layer_spec(a.shape[1:], l)
    in_specs = [pl.BlockSpec((L, 2 * ML_W), lambda b: (row_block0 + b, 0)),
                pl.BlockSpec((L, ML_W), lambda b: (row_block0 + b, 0)),
                pl.BlockSpec((L, ML_W), lambda b: (row_block0 + b, 0)),
                pl.BlockSpec((L, GATE_PAD), lambda b: (row_block0 + b, 0)),
                pl.BlockSpec((nc, 16, ML_CHUNK), lambda b: (row_block0 + b, 0, 0)),
                layer(sw), layer(gb), layer(gbt), layer(ng)]
    args = [qk, v, o, gates, gates_t, sw, gb, gbt, ng]
    if has_state:
        c0, n0, m0 = state
        in_specs += [pl.BlockSpec((1, 1, 2 * ML_HEADS, ML_DH, ML_DH), lambda b: (b, l, 0, 0, 0)),
                     pl.BlockSpec((1, 1, 2 * ML_HEADS, ML_DH), lambda b: (b, l, 0, 0)),
                     pl.BlockSpec((1, 1, 2 * ML_HEADS, LANES), lambda b: (b, l, 0, 0))]
        args += [c0, n0, m0]
    out_specs = [pl.BlockSpec((L, ML_W), lambda b: (b, 0))]
    out_shape = [jax.ShapeDtypeStruct((n_seq * L, ML_W), F32)]
    if emit_state:
        tails = ((2 * ML_HEADS, ML_DH, ML_DH), (2 * ML_HEADS, ML_DH), (2 * ML_HEADS, LANES))
        for t in tails:
            out_specs.append(pl.BlockSpec((1,) + t, lambda b, n=len(t): (b,) + (0,) * n))
            out_shape.append(jax.ShapeDtypeStruct((n_seq,) + t, F32))
    return pl.pallas_call(
        functools.partial(_mlstm_kernel, L=L, has_state=has_state, emit_state=emit_state),
        grid=(n_seq,),
        in_specs=in_specs,
        out_specs=out_specs,
        out_shape=out_shape,
        scratch_shapes=[pltpu.VMEM((L, ML_W), BF16), pltpu.VMEM((L, ML_W), BF16),
                        pltpu.VMEM((nc, ML_W, ML_CHUNK), BF16),
                        pltpu.VMEM((nc, ML_W, ML_CHUNK), F32),
                        pltpu.VMEM((2 * ML_HEADS, ML_DH, ML_DH), F32),
                        pltpu.VMEM((2 * ML_HEADS, ML_DH), F32),
                        pltpu.VMEM((2 * ML_HEADS, LANES), F32)],
        compiler_params=_cparams(("arbitrary",)),
        name=f"mlstm_{L}",
    )(*args)


def _cmul(ar, ai, br, bi):
    return ar * br - ai * bi, ar * bi + ai * br


def _s5_kernel(*refs, segmented):
    if segmented:
        (u_ref, bb_ref, cc_ref, eb_ref, ec_ref, lam_ref, dsk_ref, wglu_ref, s0_ref, y_ref,
         sbuf, yacc, bmat, cmat, pw) = refs
    else:
        (u_ref, bb_ref, cc_ref, eb_ref, ec_ref, lam_ref, dsk_ref, wglu_ref, y_ref, fin_ref,
         sbuf, yacc, bmat, cmat) = refs
    n = S5_STATE
    rows = S5_ROWS
    steps = S5_STEPS
    blk = 256
    n_blk = steps * rows // blk
    n_seg = 4

    yacc[...] = u_ref[0] * dsk_ref[0]
    ub = u_ref[0].astype(BF16)
    b_keep = (lax.broadcasted_iota(jnp.int32, (S5_CH, 2 * n), 0) // S5_GROUP
              == (lax.broadcasted_iota(jnp.int32, (S5_CH, 2 * n), 1) % n) // S5_P)
    c_keep = ((lax.broadcasted_iota(jnp.int32, (2 * n, S5_CH), 0) % n) // S5_P
              == lax.broadcasted_iota(jnp.int32, (2 * n, S5_CH), 1) // S5_GROUP)

    for d in range(2):
        rev = d == 1
        bmat[...] = jnp.where(b_keep, jnp.dot(bb_ref[0, d].astype(BF16), eb_ref[...], preferred_element_type=F32),
                              0.0).astype(BF16)
        cmat[...] = jnp.where(c_keep, jnp.dot(cc_ref[0, d].astype(BF16), ec_ref[...], preferred_element_type=F32),
                              0.0).astype(BF16)
        for i in range(n_blk):
            sbuf[i * blk:(i + 1) * blk, :] = jnp.dot(ub[i * blk:(i + 1) * blk], bmat[...],
                                                     preferred_element_type=F32)
        lam = lam_ref[0, d]
        lr = jnp.broadcast_to(lam[:, :n], (rows, n))
        li = jnp.broadcast_to(lam[:, n:], (rows, n))

        def step(i, carry, rev=rev, lr=lr, li=li):
            sr, si = carry
            t = (steps - 1 - i) if rev else i
            off = pl.multiple_of(t * rows, rows)
            pr, pi = _cmul(lr, li, sr, si)
            nr = pr + sbuf[pl.ds(off, rows), 0:n]
            ni = pi + sbuf[pl.ds(off, rows), n:2 * n]
            sbuf[pl.ds(off, rows), 0:n] = nr
            sbuf[pl.ds(off, rows), n:2 * n] = ni
            return nr, ni

        zero = jnp.zeros((rows, n), F32)
        sr, si = lax.fori_loop(0, steps, step, (zero, zero), unroll=2)

        if not segmented:
            fin_ref[0, d, :, 0:n] = sr
            fin_ref[0, d, :, n:2 * n] = si
        else:
            lam_r, lam_i = lam[:, :n], lam[:, n:]
            row8 = lax.broadcasted_iota(jnp.int32, (rows, 1), 0)
            cr, ci = lam_r, lam_i
            acc_r = jnp.broadcast_to(cr, (rows, n))
            acc_i = jnp.broadcast_to(ci, (rows, n))
            for j in range(1, rows):
                cr, ci = _cmul(cr, ci, lam_r, lam_i)
                acc_r = jnp.where(row8 >= j, jnp.broadcast_to(cr, (rows, n)), acc_r)
                acc_i = jnp.where(row8 >= j, jnp.broadcast_to(ci, (rows, n)), acc_i)
            pw[0:rows, 0:n] = acc_r
            pw[0:rows, n:2 * n] = acc_i
            size = rows
            while size < steps:
                tr = pw[size - 1:size, 0:n]
                ti = pw[size - 1:size, n:2 * n]
                xr, xi = _cmul(pw[0:size, 0:n], pw[0:size, n:2 * n], tr, ti)
                pw[size:2 * size, 0:n] = xr
                pw[size:2 * size, n:2 * n] = xi
                size *= 2
            end_off = 0 if rev else (steps - 1) * rows
            loc_r = sbuf[end_off:end_off + rows, 0:n]
            loc_i = sbuf[end_off:end_off + rows, n:2 * n]
            pl_r = pw[steps - 1:steps, 0:n]
            pl_i = pw[steps - 1:steps, n:2 * n]
            s0r = s0_ref[0, d, :, 0:n]
            s0i = s0_ref[0, d, :, n:2 * n]
            seg = row8 // 2
            first = (seg == n_seg - 1) if rev else (seg == 0)
            shift = (rows - 2) if rev else 2
            cin_r, cin_i = s0r, s0i
            for _ in range(n_seg - 1):
                fr, fi = _cmul(jnp.broadcast_to(pl_r, (rows, n)), jnp.broadcast_to(pl_i, (rows, n)), cin_r, cin_i)
                tru_r = loc_r + fr
                tru_i = loc_i + fi
                cin_r = jnp.where(first, s0r, pltpu.roll(tru_r, shift, 0))
                cin_i = jnp.where(first, s0i, pltpu.roll(tru_i, shift, 0))

            def fix(tb, carry, rev=rev, cin_r=cin_r, cin_i=cin_i):
                pb = (steps // rows - 1 - tb) if rev else tb
                poff = pl.multiple_of(pb * rows, rows)
                p_r = pw[pl.ds(poff, rows), 0:n]
                p_i = pw[pl.ds(poff, rows), n:2 * n]
                for j in range(rows):
                    jj = rows - 1 - j if rev else j
                    off = pl.multiple_of((tb * rows + j) * rows, rows)
                    fr, fi = _cmul(jnp.broadcast_to(p_r[jj:jj + 1], (rows, n)),
                                   jnp.broadcast_to(p_i[jj:jj + 1], (rows, n)), cin_r, cin_i)
                    sbuf[pl.ds(off, rows), 0:n] += fr
                    sbuf[pl.ds(off, rows), n:2 * n] += fi
                return carry

            lax.fori_loop(0, steps // rows, fix, 0)

        for i in range(n_blk):
            yacc[i * blk:(i + 1) * blk, :] += jnp.dot(sbuf[i * blk:(i + 1) * blk, :].astype(BF16), cmat[...],
                                                      preferred_element_type=F32)

    g = jax.nn.gelu(yacc[...], approximate=True)
    y_ref[0] = g * jax.nn.sigmoid(_bdot(g, wglu_ref[0]))


@functools.lru_cache(None)
def _s5_spread():
    eb = np.zeros((2 * S5_P, 2 * S5_STATE), np.float32)
    for half in range(2):
        for g in range(S5_G):
            c0 = half * S5_STATE + g * S5_P
            eb[half * S5_P:(half + 1) * S5_P, c0:c0 + S5_P] = np.eye(S5_P)
    ec = np.zeros((LANES, S5_CH), np.float32)
    for g in range(S5_G):
        ec[:S5_GROUP, g * S5_GROUP:(g + 1) * S5_GROUP] = np.eye(S5_GROUP)
    return eb, ec


def _s5(u_tm, bb, cc, lam, dskip, wglu, s0, l):
    n_grp = u_tm.shape[0]
    n_rows = S5_STEPS * S5_ROWS
    segmented = s0 is not None
    eb, ec = (jnp.asarray(m).astype(BF16) for m in _s5_spread())
    full = lambda a: pl.BlockSpec(a.shape, lambda g: (0,) * a.ndim)
    layer = lambda a: _layer_spec(a.shape[1:], l)
    in_specs = [pl.BlockSpec((1, n_rows, S5_CH), lambda g: (g, 0, 0)),
                layer(bb), layer(cc), full(eb), full(ec), layer(lam), layer(dskip), layer(wglu)]
    args = [u_tm, bb, cc, eb, ec, lam, dskip, wglu]
    out_specs = [pl.BlockSpec((1, n_rows, S5_CH), lambda g: (g, 0, 0))]
    out_shape = [jax.ShapeDtypeStruct((n_grp, n_rows, S5_CH), F32)]
    scratch = [pltpu.VMEM((n_rows, 2 * S5_STATE), F32), pltpu.VMEM((n_rows, S5_CH), F32),
               pltpu.VMEM((S5_CH, 2 * S5_STATE), BF16), pltpu.VMEM((2 * S5_STATE, S5_CH), BF16)]
    if segmented:
        in_specs.append(layer(s0))
        args.append(s0)
        scratch.append(pltpu.VMEM((S5_STEPS, 2 * S5_STATE), F32))
    else:
        out_specs.append(pl.BlockSpec((1, 2, S5_ROWS, 2 * S5_STATE), lambda g: (g, 0, 0, 0)))
        out_shape.append(jax.ShapeDtypeStruct((n_grp, 2, S5_ROWS, 2 * S5_STATE), F32))
    return pl.pallas_call(
        functools.partial(_s5_kernel, segmented=segmented),
        grid=(n_grp,),
        in_specs=in_specs,
        out_specs=out_specs,
        out_shape=out_shape,
        scratch_shapes=scratch,
        compiler_params=_cparams(("arbitrary",)),
        name="s5_seg" if segmented else "s5_ctx",
    )(*args)


def _s5_params(a_re, a_im, log_dt, b_re, b_im, c_re, c_im):
    dt = jnp.exp(log_dt)[..., None]
    mag = jnp.exp(a_re * dt)
    lb_re = mag * jnp.cos(a_im * dt)
    lb_im = mag * jnp.sin(a_im * dt)
    den = a_re * a_re + a_im * a_im
    nr, ni = lb_re - 1.0, lb_im
    k_re = (nr * a_re + ni * a_im) / den
    k_im = (ni * a_re - nr * a_im) / den
    bb_re = k_re[..., None] * b_re - k_im[..., None] * b_im
    bb_im = k_re[..., None] * b_im + k_im[..., None] * b_re
    to_gc_p = lambda m: m.transpose(0, 1, 2, 4, 3).reshape(DEPTH, 2, S5_CH, S5_P)
    bb = jnp.concatenate([to_gc_p(bb_re), to_gc_p(bb_im)], axis=3)
    to_gp_c = lambda m: m.transpose(0, 1, 2, 4, 3).reshape(DEPTH, 2, S5_STATE, S5_GROUP)
    cc = jnp.concatenate([to_gp_c(c_re), -to_gp_c(c_im)], axis=2)
    cc = jnp.pad(cc, ((0, 0), (0, 0), (0, 0), (0, LANES - S5_GROUP)))
    lam = jnp.concatenate([lb_re.reshape(DEPTH, 2, 1, S5_STATE), lb_im.reshape(DEPTH, 2, 1, S5_STATE)], axis=3)
    return bb, cc, lam


def _outproj_kernel(*refs, n_x):
    x_refs = refs[:n_x]
    (hyc_ref, hyl_ref, mlc_ref, mll_ref, s5c_ref, s5l_ref, w_ref, mod_ref, g_ref,
     rw_ref, rb_ref, xn_ref, h2e_ref, best_ref, rank_ref, cnt_ref, wb, cnt_s) = refs[n_x:]
    step = pl.program_id(0)

    @pl.when(step == 0)
    def _():
        wb[...] = w_ref[0].astype(BF16)
        cnt_s[...] = jnp.zeros_like(cnt_s)

    is_ctx = step < T_CTX // TM
    pick = lambda c_ref, l_ref: jnp.where(is_ctx, c_ref[...], l_ref[...]).astype(BF16)
    mod = mod_ref[0, 0]
    a, b = HY_CH, HY_CH + ML_W
    mix = (jnp.dot(pick(hyc_ref, hyl_ref), wb[0:a, :], preferred_element_type=F32)
           + jnp.dot(pick(mlc_ref, mll_ref), wb[a:b, :], preferred_element_type=F32)
           + jnp.dot(pick(s5c_ref, s5l_ref), wb[b:, :], preferred_element_type=F32))
    xn = _x_tile(step, x_refs) + mod[2:3] * mix
    xn_ref[...] = xn
    h2 = _rms(xn, g_ref[0]) * (1.0 + mod[4:5]) + mod[3:4]
    h2e_ref[:, 0:D] = h2
    h_hi, h_lo = _split_bf16(h2)
    r_hi, r_lo = _split_bf16(rw_ref[...])
    logits = (lax.dot_general(r_hi, h_hi, _NT, preferred_element_type=F32)
              + (lax.dot_general(r_hi, h_lo, _NT, preferred_element_type=F32)
                 + lax.dot_general(r_lo, h_hi, _NT, preferred_element_type=F32)))
    ex = jnp.exp(logits - jnp.max(logits, axis=0, keepdims=True))
    probs = ex / jnp.sum(ex, axis=0, keepdims=True)
    sel = probs + rb_ref[...]
    best = None
    best_score = None
    for g in range(N_GROUPS):
        r = [sel[g * GROUP_SIZE + i:g * GROUP_SIZE + i + 1, :] for i in range(GROUP_SIZE)]
        score = None
        for i in range(GROUP_SIZE):
            for j in range(i + 1, GROUP_SIZE):
                pair = r[i] + r[j]
                score = pair if score is None else jnp.maximum(score, pair)
        if g == 0:
            best, best_score = jnp.zeros_like(score, dtype=jnp.int32), score
        else:
            upd = score > best_score
            best = jnp.where(upd, g, best)
            best_score = jnp.where(upd, score, best_score)
    eid = lax.broadcasted_iota(jnp.int32, (N_EXPERTS, 1), 0)
    masked = jnp.where(eid // GROUP_SIZE == best, sel, -jnp.inf)
    m1 = jnp.max(masked, axis=0, keepdims=True)
    i1 = jnp.min(jnp.where(masked == m1, eid, N_EXPERTS), axis=0, keepdims=True)
    masked2 = jnp.where(eid == i1, -jnp.inf, masked)
    m2 = jnp.max(masked2, axis=0, keepdims=True)
    i2 = jnp.min(jnp.where(masked2 == m2, eid, N_EXPERTS), axis=0, keepdims=True)
    p1 = jnp.sum(jnp.where(eid == i1, probs, 0.0), axis=0, keepdims=True)
    p2 = jnp.sum(jnp.where(eid == i2, probs, 0.0), axis=0, keepdims=True)
    tot = p1 + p2
    comb = jnp.where(eid == i1, p1 / tot, 0.0) + jnp.where(eid == i2, p2 / tot, 0.0)
    comb = jnp.concatenate([comb, jnp.zeros((LANES - N_EXPERTS, comb.shape[1]), F32)], axis=0)
    h2e_ref[:, D:] = comb.T
    best_ref[...] = best
    gid = lax.broadcasted_iota(jnp.int32, (8, 1), 0)
    onehot = (gid == best).astype(F32)
    cum = _cumsum_lanes(onehot, TM, False)
    run = cnt_s[:, 0:1]
    rank_ref[...] = jnp.sum(onehot * (cum - onehot + run), axis=0, keepdims=True).astype(jnp.int32)
    cnt_s[...] = jnp.broadcast_to(run + cum[:, TM - 1:TM], cnt_s.shape)
    cnt_ref[...] = cnt_s[...].astype(jnp.int32)


def _outproj(x, y_hy, y_ml, y_s5, w_out, l, mod, g2, rw_t, rb):
    full = lambda a: pl.BlockSpec(a.shape, lambda i: (0,) * a.ndim)
    n_ctx = T_CTX // TM
    ctx = lambda w: pl.BlockSpec((TM, w), lambda i: (jnp.minimum(i, n_ctx - 1), 0))
    lat = lambda w: pl.BlockSpec((TM, w), lambda i: (jnp.maximum(i - n_ctx, 0), 0))
    tok = lambda w: pl.BlockSpec((TM, w), lambda i: (i, 0))
    row = pl.BlockSpec((1, TM), lambda i: (0, i))
    return pl.pallas_call(
        functools.partial(_outproj_kernel, n_x=len(x)),
        grid=(T_ALL // TM,),
        in_specs=_x_specs(x) + [ctx(HY_CH), lat(HY_CH), ctx(ML_W), lat(ML_W), ctx(S5_CH), lat(S5_CH),
                  pl.BlockSpec((1, D, D), lambda i: (l, 0, 0), pipeline_mode=pl.Buffered(1)),
                  _mod_spec(l, TM), _layer_spec((1, D), l), full(rw_t), full(rb)],
        out_specs=[tok(D), tok(D + LANES), row, row, pl.BlockSpec((8, LANES), lambda i: (0, 0))],
        out_shape=[jax.ShapeDtypeStruct((T_ALL, D), F32),
                   jax.ShapeDtypeStruct((T_ALL, D + LANES), F32),
                   jax.ShapeDtypeStruct((1, T_ALL), jnp.int32),
                   jax.ShapeDtypeStruct((1, T_ALL), jnp.int32),
                   jax.ShapeDtypeStruct((8, LANES), jnp.int32)],
        scratch_shapes=[pltpu.VMEM((D, D), BF16), pltpu.VMEM((8, LANES), F32)],
        compiler_params=_cparams(("arbitrary",)),
        name="outproj_router",
    )(*x, *y_hy, *y_ml, *y_s5, w_out, mod, g2, rw_t, rb)


def _gather_rows(idx_ref, idx_base, src_ref, dst_ref, n_rows):
    def body(r8, carry):
        base = pl.multiple_of(r8 * 8, 8)
        for k in range(8):
            idx = idx_ref[idx_base + base + k]
            dst_ref[pl.ds(base + k, 1), :] = src_ref[pl.ds(idx, 1), :]
        return carry

    lax.fori_loop(0, n_rows // 8, body, 0)


def _dispatch_kernel(best_ref, rank_ref, cnt_ref, pos_ref, src_ref, tg_ref, nt_ref, off_s):
    tm = TM_MOE
    tiles = jnp.int32(0)
    tile_end = []
    for g in range(N_GROUPS):
        off_s[g] = tiles * tm
        tiles = tiles + (cnt_ref[g, 0] + (tm - 1)) // tm
        tile_end.append(tiles)
    nt_ref[0] = tiles
    for k in range(MOE_SLOTS // tm):
        g = jnp.int32(0)
        for e in tile_end[:-1]:
            g = g + (k >= e).astype(jnp.int32)
        tg_ref[k] = g

    def clear(i8, carry):
        for u in range(8):
            src_ref[i8 * 8 + u] = 0
        return carry

    lax.fori_loop(0, MOE_SLOTS // 8, clear, 0)

    def place(t8, carry):
        for u in range(8):
            t = t8 * 8 + u
            p = rank_ref[t] + off_s[best_ref[t]]
            pos_ref[t] = p
            src_ref[p] = t
        return carry

    lax.fori_loop(0, T_ALL // 8, place, 0)


def _dispatch(best, rank, cnt):
    smem = pl.BlockSpec(memory_space=pltpu.SMEM)
    i32 = lambda n: jax.ShapeDtypeStruct((n,), jnp.int32)
    return pl.pallas_call(
        _dispatch_kernel,
        in_specs=[smem, smem, smem],
        out_specs=[smem, smem, smem, smem],
        out_shape=[i32(T_ALL), i32(MOE_SLOTS), i32(MOE_SLOTS // TM_MOE), i32(1)],
        scratch_shapes=[pltpu.SMEM((N_GROUPS,), jnp.int32)],
        name="moe_dispatch",
    )(best.reshape(T_ALL), rank.reshape(T_ALL), cnt)


def _moe_kernel(src_ref, tg_ref, nt_ref, h_ref, wg_ref, wu_ref, wd_ref, ys_ref, g_a, x_a, g_b, x_b, acc):
    i = pl.program_id(0)
    j = pl.program_id(1)
    tm = TM_MOE
    quarter = tm // GROUP_SIZE
    n_live = nt_ref[0]

    @pl.when(jnp.logical_and(i == 0, j == 0))
    def _():
        _gather_rows(src_ref, 0, h_ref, g_a, tm)
        x_a[...] = g_a[:, 0:D].astype(BF16)

    def step(cur_g, cur_x, nxt_g, nxt_x):
        @pl.when(j == 0)
        def _():
            acc[...] = jnp.zeros_like(acc)

        e = tg_ref[i] * GROUP_SIZE + j
        lane = lax.broadcasted_iota(jnp.int32, (1, LANES), 1)
        wg = wg_ref[0, 0].astype(BF16)
        wu = wu_ref[0, 0].astype(BF16)
        wd = wd_ref[0, 0].astype(BF16)
        sub = MOE_SUB
        for s in range(tm // sub):
            rs = slice(s * sub, (s + 1) * sub)
            hb = cur_x[rs, :]
            ce = jnp.sum(jnp.where(lane == e, cur_g[rs, D:], 0.0), axis=1, keepdims=True)
            hid = _silu(jnp.dot(hb, wg, preferred_element_type=F32)) * jnp.dot(hb, wu, preferred_element_type=F32)
            acc[rs, :] += jnp.dot((hid * ce).astype(BF16), wd, preferred_element_type=F32)

        r0 = pl.multiple_of(j * quarter, quarter)
        base = jnp.minimum(i + 1, n_live - 1) * tm + r0
        for k in range(quarter):
            nxt_g[pl.ds(r0 + k, 1), :] = h_ref[pl.ds(src_ref[base + k], 1), :]
        nxt_x[pl.ds(r0, quarter), :] = nxt_g[pl.ds(r0, quarter), 0:D].astype(BF16)

        @pl.when(j == GROUP_SIZE - 1)
        def _():
            for c in range(D // LANES):
                ys_ref[pl.ds(c, tm, stride=D // LANES), :] = acc[:, c * LANES:(c + 1) * LANES]

    live = i < n_live

    @pl.when(jnp.logical_and(live, i % 2 == 0))
    def _():
        step(g_a, x_a, g_b, x_b)

    @pl.when(jnp.logical_and(live, i % 2 == 1))
    def _():
        step(g_b, x_b, g_a, x_a)

    @pl.when(jnp.logical_and(jnp.logical_not(live), j == GROUP_SIZE - 1))
    def _():
        ys_ref[...] = jnp.zeros_like(ys_ref)


def _moe(h2e, src, tile_group, n_tiles, wg, wu, wd, l):
    tm = TM_MOE

    def w_map(i, j, src, tg, nt):
        live = i < nt[0]
        ii = jnp.minimum(i, nt[0] - 1)
        return (l, tg[ii] * GROUP_SIZE + jnp.where(live, j, GROUP_SIZE - 1), 0, 0)

    return pl.pallas_call(
        _moe_kernel,
        grid_spec=pltpu.PrefetchScalarGridSpec(
            num_scalar_prefetch=3,
            grid=(MOE_SLOTS // tm, GROUP_SIZE),
            in_specs=[pl.BlockSpec(memory_space=pltpu.VMEM),
                      pl.BlockSpec((1, 1, D, D_EXPERT), w_map),
                      pl.BlockSpec((1, 1, D, D_EXPERT), w_map),
                      pl.BlockSpec((1, 1, D_EXPERT, D), w_map)],
            out_specs=pl.BlockSpec((tm * (D // LANES), LANES), lambda i, j, src, tg, nt: (i, 0)),
            scratch_shapes=[pltpu.VMEM((tm, D + LANES), F32), pltpu.VMEM((tm, D), BF16),
                            pltpu.VMEM((tm, D + LANES), F32), pltpu.VMEM((tm, D), BF16),
                            pltpu.VMEM((tm, D), F32)]),
        out_shape=jax.ShapeDtypeStruct((MOE_SLOTS * (D // LANES), LANES), F32),
        compiler_params=_cparams(("arbitrary", "arbitrary")),
        name="moe_experts",
    )(src, tile_group, n_tiles, h2e, wg, wu, wd)


def _combine_kernel(pos_ref, ys_ref, xn_ref, mod_ref, fg_ref, *rest, final):
    step = pl.program_id(0)
    if final:
        yc_ref, yl_ref, gbuf = rest
    else:
        out_ref, gbuf = rest
    n_lt = D // LANES

    def body(r8, carry):
        base = pl.multiple_of(r8 * 8, 8)
        for k in range(8):
            src = pl.multiple_of(pos_ref[step * TM + base + k] * n_lt, n_lt)
            gbuf[pl.ds(pl.multiple_of((base + k) * n_lt, n_lt), n_lt), :] = ys_ref[pl.ds(src, n_lt), :]
        return carry

    lax.fori_loop(0, TM // 8, body, 0)
    moe = jnp.concatenate([gbuf[pl.ds(c, TM, stride=n_lt), :] for c in range(n_lt)], axis=1)
    out = xn_ref[...] + mod_ref[0, 0][5:6] * moe
    if final:
        y = _rms(out, fg_ref[...])

        @pl.when(step < T_CTX // TM)
        def _():
            yc_ref[...] = y

        @pl.when(step >= T_CTX // TM)
        def _():
            yl_ref[...] = y
    else:
        out_ref[...] = out


def _combine(pos, ys, xn, mod, fg, l, final):
    spec = pl.BlockSpec((TM, D), lambda i, pos: (i, 0))
    n_ctx = T_CTX // TM
    if final:
        out_specs = [pl.BlockSpec((TM, D), lambda i, pos: (jnp.minimum(i, n_ctx - 1), 0)),
                     pl.BlockSpec((TM, D), lambda i, pos: (jnp.maximum(i - n_ctx, 0), 0))]
        out_shape = [jax.ShapeDtypeStruct((T_CTX, D), F32), jax.ShapeDtypeStruct((T_LAT, D), F32)]
    else:
        out_specs = [spec]
        out_shape = [jax.ShapeDtypeStruct((T_ALL, D), F32)]
    return pl.pallas_call(
        functools.partial(_combine_kernel, final=final),
        grid_spec=pltpu.PrefetchScalarGridSpec(
            num_scalar_prefetch=1,
            grid=(T_ALL // TM,),
            in_specs=[pl.BlockSpec(memory_space=pltpu.VMEM),
                      spec,
                      _mod_spec(l, TM),
                      pl.BlockSpec((1, D), lambda i, pos: (0, 0))],
            out_specs=out_specs,
            scratch_shapes=[pltpu.VMEM((TM * (D // LANES), LANES), F32)]),
        out_shape=out_shape,
        compiler_params=_cparams(("arbitrary",)),
        name="moe_combine",
    )(pos, ys, xn, mod, fg)


@functools.lru_cache(None)
def _pos_embed():
    rows = L_LAT // GRID_W
    r = np.repeat(np.arange(rows, dtype=np.float64), GRID_W)
    col = np.tile(np.arange(GRID_W, dtype=np.float64), rows)
    quarter = D // 4
    freq = np.exp(-math.log(POS_BASE) * np.arange(quarter, dtype=np.float64) / quarter)
    ar = r[:, None] * freq[None]
    ac = col[:, None] * freq[None]
    emb = np.concatenate([np.sin(ar), np.cos(ar), np.sin(ac), np.cos(ac)], axis=-1)
    return emb.astype(np.float32)


def _to_time_major_ctx(a):
    c = a.shape[-1]
    a = a.reshape(2, S5_ROWS, L_CTX, c).transpose(0, 2, 1, 3)
    return a.reshape(2, L_CTX * S5_ROWS, c)


def _from_time_major_ctx(a):
    c = a.shape[-1]
    a = a.reshape(2, L_CTX, S5_ROWS, c).transpose(0, 2, 1, 3)
    return a.reshape(T_CTX, c)


def _to_time_major_lat(a):
    c = a.shape[-1]
    a = a.reshape(N_LAT_SEQ, 4, S5_STEPS, c).transpose(2, 1, 0, 3)
    return a.reshape(1, S5_STEPS * S5_ROWS, c)


def _from_time_major_lat(a):
    c = a.shape[-1]
    a = a.reshape(S5_STEPS, 4, N_LAT_SEQ, c).transpose(2, 1, 0, 3)
    return a.reshape(T_LAT, c)


def kernel(x_prompt, x_sample, c, state_mlstm_C, state_mlstm_n, state_mlstm_m, state_s5_re, state_s5_im, c_ctx, w_ada, b_ada, norm1_g, norm2_g, final_g, w_in, w_out, hy_short, hy_fw1, hy_fb1, hy_fw2, hy_fb2, hy_fw3, hy_log_decay, hy_bias, ml_short, ml_gate_bias, ml_norm_g, s5_a_re, s5_a_im, s5_log_dt, s5_b_re, s5_b_im, s5_c_re, s5_c_im, s5_d, s5_w_glu, router_w, router_b, moe_w_gate, moe_w_up, moe_w_down):
    x = (x_prompt.reshape(T_CTX, D), x_sample.reshape(T_LAT, D), jnp.asarray(_pos_embed()))
    w_in_t = jnp.swapaxes(w_in, 1, 2)
    cc =jnp.concatenate([c_ctx[None], c, jnp.zeros((8 - 1 - N_LAT_SEQ, D), F32)], axis=0)
    mod = _ada(cc, w_ada, b_ada).reshape(DEPTH, 8, 6, D)
    rw_t = router_w.T
    rb = router_b.reshape(N_EXPERTS, 1)
    fg = final_g.reshape(1, D)
    lat_blk = T_CTX // L_LAT
    g1 = norm1_g.reshape(DEPTH, 1, D)
    g2 = norm2_g.reshape(DEPTH, 1, D)

    w1p = jnp.pad(hy_fw1, ((0, 0), (0, LANES - HY_EMB), (0, 0)))
    b1 = hy_fb1.reshape(DEPTH, 1, HY_FILTER_W)
    b2 = hy_fb2.reshape(DEPTH, 1, HY_FILTER_W)
    ld = hy_log_decay.reshape(DEPTH, 1, 4 * HY_CH)
    hy_spec = {L: _hy_filter(L, w1p, b1, hy_fw2, b2, hy_fw3, ld) for L in (L_CTX, L_LAT)}
    gb = jnp.pad(ml_gate_bias.reshape(DEPTH, 1, 16), ((0, 0), (0, 0), (0, GATE_PAD - 16)))
    gbt = ml_gate_bias.reshape(DEPTH, 16, 1)
    ng = ml_norm_g.reshape(DEPTH, 1, ML_W)
    ml_state = (state_mlstm_C.reshape(N_LAT_SEQ, DEPTH, 2 * ML_HEADS, ML_DH, ML_DH),
                state_mlstm_n.reshape(N_LAT_SEQ, DEPTH, 2 * ML_HEADS, ML_DH),
                jnp.broadcast_to(state_mlstm_m.reshape(N_LAT_SEQ, DEPTH, 2 * ML_HEADS, 1),
                                 (N_LAT_SEQ, DEPTH, 2 * ML_HEADS, LANES)))
    bb, cc_s5, lam = _s5_params(s5_a_re, s5_a_im, s5_log_dt, s5_b_re, s5_b_im, s5_c_re, s5_c_im)
    dsk = s5_d.reshape(DEPTH, 1, S5_CH)
    wglu = s5_w_glu.astype(BF16)
    s0 = jnp.concatenate([state_s5_re.reshape(N_LAT_SEQ, DEPTH, 2, S5_STATE),
                          state_s5_im.reshape(N_LAT_SEQ, DEPTH, 2, S5_STATE)], axis=-1)
    s0 = jnp.tile(s0.transpose(1, 2, 0, 3), (1, 1, 4, 1))

    new_c, new_n, new_m, new_re, new_im = [], [], [], [], []
    y_prompt = y_sample = None
    for l in range(DEPTH):
        u_hy, qk, v, o, u_s5, gates, gates_t = _inproj(x, g1, mod, w_in_t, l)

        y_hy = [_hyena(u_hy, L, n_seq, blk0, hy_short, hy_bias, *hy_spec[L], l)
                for L, n_seq, blk0 in ((L_CTX, N_CTX_SEQ, 0), (L_LAT, N_LAT_SEQ, lat_blk))]

        yc, cc_, nc_, mc_ = _mlstm(qk, v, o, gates, gates_t, L_CTX, N_CTX_SEQ, 0, ml_short, gb, gbt, ng, None, l,
                                   emit_state=True)
        (yl,) = _mlstm(qk, v, o, gates, gates_t, L_LAT, N_LAT_SEQ, lat_blk, ml_short, gb, gbt, ng, ml_state, l)
        y_ml = (yc, yl)
        new_c.append(cc_.reshape(N_CTX_SEQ, 2, ML_HEADS, ML_DH, ML_DH))
        new_n.append(nc_.reshape(N_CTX_SEQ, 2, ML_HEADS, ML_DH))
        new_m.append(mc_[:, :, 0].reshape(N_CTX_SEQ, 2, ML_HEADS))

        ys_c, fin = _s5(_to_time_major_ctx(u_s5[:T_CTX]), bb, cc_s5, lam, dsk, wglu, None, l)
        (ys_l,) = _s5(_to_time_major_lat(u_s5[T_CTX:]), bb, cc_s5, lam, dsk, wglu, s0, l)
        y_s5 = (_from_time_major_ctx(ys_c), _from_time_major_lat(ys_l))
        fin = fin.transpose(0, 2, 1, 3).reshape(N_CTX_SEQ, 2, 2 * S5_STATE)
        new_re.append(fin[..., :S5_STATE].reshape(N_CTX_SEQ, 2, S5_G, S5_P))
        new_im.append(fin[..., S5_STATE:].reshape(N_CTX_SEQ, 2, S5_G, S5_P))

        xn, h2e, best, rank, cnt = _outproj(x, y_hy, y_ml, y_s5, w_out, l, mod, g2, rw_t, rb)
        pos, src, tile_group, n_tiles = _dispatch(best, rank, cnt)
        ys = _moe(h2e, src, tile_group, n_tiles, moe_w_gate, moe_w_up, moe_w_down, l)
        res = _combine(pos, ys, xn, mod, fg, l, l == DEPTH - 1)
        if l == DEPTH - 1:
            y_prompt = res[0].reshape(N_CTX_SEQ, L_CTX, D)
            y_sample = res[1].reshape(N_LAT_SEQ, L_LAT, D)
        else:
            x = (res[0],)

    return (y_prompt, y_sample, jnp.stack(new_c, axis=1), jnp.stack(new_n, axis=1), jnp.stack(new_m, axis=1),
            jnp.stack(new_re, axis=1), jnp.stack(new_im, axis=1))
```

```python
import functools
import math

import numpy as np
import jax
import jax.numpy as jnp
from jax import lax
from jax.experimental import pallas as pl
from jax.experimental.pallas import tpu as pltpu

F32 = jnp.float32
BF16 = jnp.bfloat16

D = 1024
N_CTX_SEQ, L_CTX = 16, 256
N_LAT_SEQ, L_LAT = 2, 1024
T_CTX = N_CTX_SEQ * L_CTX
T_LAT = N_LAT_SEQ * L_LAT
T_ALL = T_CTX + T_LAT
DEPTH = 2
EPS = 1e-6
GRID_W = 64
POS_BASE = 10000.0
HY_CH = 256
HY_EMB = 33
HY_FILTER_W = 64
ML_HEADS = 4
ML_DH = 128
ML_W = ML_HEADS * ML_DH
S5_CH = 256
S5_G = 16
S5_GROUP = 16
S5_P = 64
S5_STATE = S5_G * S5_P
N_EXPERTS = 16
N_GROUPS = 4
GROUP_SIZE = N_EXPERTS // N_GROUPS
D_EXPERT = 512
OFF_HY = 0
OFF_QK = 3 * HY_CH
OFF_V = OFF_QK + 2 * ML_W
OFF_O = OFF_V + ML_W
OFF_G = OFF_O + ML_W
OFF_S5 = OFF_G + 16
IN_W = OFF_S5 + S5_CH
LANES = 128
GATE_PAD = LANES

TM = 512
TM_MOE = 512
MOE_SLOTS = T_ALL + N_GROUPS * TM_MOE
MOE_SUB = 512
HY_ROWS = 1024
ML_CHUNK = 256
S5_ROWS = 8
S5_STEPS = 256
VMEM_LIMIT = 56 * 1024 * 1024


def _cparams(sem, vmem=VMEM_LIMIT):
    if sem is None:
        return pltpu.CompilerParams(vmem_limit_bytes=vmem)
    return pltpu.CompilerParams(dimension_semantics=sem, vmem_limit_bytes=vmem)


def _bdot(a, b):
    return jnp.dot(a.astype(BF16), b.astype(BF16), preferred_element_type=F32)


def _split_bf16(x):
    hi = x.astype(BF16)
    return hi, (x - hi.astype(F32)).astype(BF16)


def _dot3(a, b):
    a_hi, a_lo = _split_bf16(a)
    b_hi, b_lo = _split_bf16(b)
    dot = functools.partial(jnp.dot, preferred_element_type=F32)
    return dot(a_hi, b_hi) + (dot(a_hi, b_lo) + dot(a_lo, b_hi))


def _silu(x):
    return x * jax.nn.sigmoid(x)


def _rms(x, g):
    return x * lax.rsqrt(jnp.mean(x * x, axis=-1, keepdims=True) + EPS) * g


def _log_sigmoid(x):
    return jnp.minimum(x, 0.0) - jnp.log1p(jnp.exp(-jnp.abs(x)))


def _conv3(u, w, n_rows, seq_len):
    row = lax.broadcasted_iota(jnp.int32, (n_rows, 1), 0) % seq_len
    prev = jnp.where(row == 0, 0.0, pltpu.roll(u, 1, 0))
    nxt = jnp.where(row == seq_len - 1, 0.0, pltpu.roll(u, n_rows - 1, 0))
    return prev * w[0:1] + u * w[1:2] + nxt * w[2:3]


def _ada_kernel(c_ref, w_ref, b_ref, o_ref):
    o_ref[0] = _bdot(_silu(c_ref[...]), w_ref[0]) + b_ref[0]


def _ada(cc, w_ada, b_ada):
    tn = 1536
    return pl.pallas_call(
        _ada_kernel,
        grid=(DEPTH, 6 * D // tn),
        in_specs=[pl.BlockSpec((8, D), lambda l, j: (0, 0)),
                  pl.BlockSpec((1, D, tn), lambda l, j: (l, 0, j)),
                  pl.BlockSpec((1, 1, tn), lambda l, j: (l, 0, j))],
        out_specs=pl.BlockSpec((1, 8, tn), lambda l, j: (l, 0, j)),
        out_shape=jax.ShapeDtypeStruct((DEPTH, 8, 6 * D), F32),
        compiler_params=_cparams(("arbitrary", "arbitrary")),
        name="ada_mod",
    )(cc, w_ada, b_ada.reshape(DEPTH, 1, 6 * D))


def _mod_row(i, tm):
    n_ctx = T_CTX // tm
    return jnp.where(i < n_ctx, 0, 1 + (i - n_ctx) // (L_LAT // tm))


_SEG = ((OFF_HY, OFF_QK - OFF_HY), (OFF_QK, OFF_V - OFF_QK), (OFF_V, OFF_O - OFF_V), (OFF_O, OFF_G - OFF_O))
TAIL_W = IN_W - OFF_G


N_CTX_TILES = T_CTX // TM
_NT = (((1,), (1,)), ((), ()))


def _x_specs(x):
    if len(x) == 1:
        return [pl.BlockSpec((TM, D), lambda i, *_: (i, 0))]
    per_seq = L_LAT // TM
    return [pl.BlockSpec((TM, D), lambda i, *_: (jnp.minimum(i, N_CTX_TILES - 1), 0)),
            pl.BlockSpec((TM, D), lambda i, *_: (jnp.maximum(i - N_CTX_TILES, 0), 0)),
            pl.BlockSpec((TM, D), lambda i, *_: (jnp.maximum(i - N_CTX_TILES, 0) % per_seq, 0))]


def _x_tile(step, x_refs):
    if len(x_refs) == 1:
        return x_refs[0][...]
    xc_ref, xl_ref, pos_ref = x_refs
    return jnp.where(step < N_CTX_TILES, xc_ref[...], xl_ref[...] + pos_ref[...])


def _inproj_kernel(*refs, n_x):
    x_refs = refs[:n_x]
    g_ref, mod_ref, w_ref, hy_ref, qk_ref, v_ref, o_ref, s5_ref, gt_ref, gtt_ref, wb = refs[n_x:]
    step = pl.program_id(0)

    @pl.when(step == 0)
    def _():
        wb[...] = w_ref[0].astype(BF16)

    mod = mod_ref[0, 0]
    h = _rms(_x_tile(step, x_refs), g_ref[0]) * (1.0 + mod[1:2]) + mod[0:1]
    hb = h.astype(BF16)
    for (a, w), ref in zip(_SEG, (hy_ref, qk_ref, v_ref, o_ref)):
        ref[...] = lax.dot_general(hb, wb[a:a + w, :], _NT, preferred_element_type=F32)
    tail = lax.dot_general(hb, wb[OFF_G:IN_W, :], _NT, preferred_element_type=F32)
    gates = tail[:, 0:GATE_PAD]
    gt_ref[...] = gates
    for c in range(TM // ML_CHUNK):
        gtt_ref[c] = gates[c * ML_CHUNK:(c + 1) * ML_CHUNK, :].T[0:16, :]
    s5_ref[...] = tail[:, OFF_S5 - OFF_G:TAIL_W]


def _layer_spec(shape, l):
    return pl.BlockSpec((1,) + tuple(shape), lambda *_: (l,) + (0,) * len(shape))


def _mod_spec(l, tm):
    return pl.BlockSpec((1, 1, 6, D), lambda i, *_: (l, _mod_row(i, tm), 0, 0))


def _inproj(x, g, mod, w_in_t, l):
    widths = [w for _, w in _SEG] + [S5_CH, GATE_PAD]
    cpt = TM // ML_CHUNK
    return pl.pallas_call(
        functools.partial(_inproj_kernel, n_x=len(x)),
        grid=(T_ALL // TM,),
        in_specs=_x_specs(x) + [
            _layer_spec((1, D), l), _mod_spec(l, TM),
            pl.BlockSpec((1, IN_W, D), lambda i: (l, 0, 0), pipeline_mode=pl.Buffered(1))],
        out_specs=[pl.BlockSpec((TM, w), lambda i: (i, 0)) for w in widths]
        + [pl.BlockSpec((cpt, 16, ML_CHUNK), lambda i: (i, 0, 0))],
        out_shape=[jax.ShapeDtypeStruct((T_ALL, w), F32) for w in widths]
        + [jax.ShapeDtypeStruct((T_ALL // ML_CHUNK, 16, ML_CHUNK), F32)],
        scratch_shapes=[pltpu.VMEM((IN_W, D), BF16)],
        compiler_params=_cparams(("arbitrary",)),
        name="norm_inproj",
    )(*x, g, mod, w_in_t)


@functools.lru_cache(None)
def _dft_mats(L):
    n = 2 * L
    k = np.arange(L)[:, None]
    t = np.arange(L)[None, :]
    ang = 2.0 * np.pi * ((k * t) % n) / n
    top = np.cos(ang)
    bot = -np.sin(ang)
    bot[0] = np.cos(np.pi * np.arange(L))
    fwd = np.concatenate([top, bot], 0)
    s = np.full((n, 1), 2.0 / n)
    s[0] = s[L] = 1.0 / n
    inv = (fwd * s).T
    return fwd.astype(np.float32), inv.astype(np.float32)


@functools.lru_cache(None)
def _hy_positions(L):
    t = np.linspace(0.0, 1.0, L)
    bands = (HY_EMB - 1) // 2
    f = np.linspace(1e-4, bands - 1, bands)
    w = 2.0 * np.pi * np.arange(L) / L
    ang = w[:, None] * f[None, :]
    z = np.concatenate([t[:, None], np.cos(ang), -np.sin(ang)], -1)
    zp = np.zeros((L, LANES))
    zp[:, :HY_EMB] = z
    return zp.astype(np.float32), t[:, None].astype(np.float32)


def _hy_filter_kernel(z_ref, t_ref, w1_ref, b1_ref, w2_ref, b2_ref, w3_ref, ld_ref, f_ref,
                      p_ref, q_ref, r_ref, *, L):
    h = jnp.sin(_dot3(z_ref[...], w1_ref[0]) + b1_ref[0])
    h = jnp.sin(_dot3(h, w2_ref[0]) + b2_ref[0])
    filt = _dot3(h, w3_ref[0])
    filt = filt * jnp.exp(-t_ref[...] * jnp.exp(ld_ref[0]))
    c = HY_CH
    h_fwd = jnp.concatenate([filt[:, 0:c], filt[:, 2 * c:3 * c]], axis=1)
    h_bwd = jnp.concatenate([filt[:, c:2 * c], filt[:, 3 * c:4 * c]], axis=1)
    row = lax.broadcasted_iota(jnp.int32, (L, 1), 0)
    h_bwd = jnp.where(row == 0, 0.0, h_bwd)
    a = jnp.dot(f_ref[...], h_fwd.astype(BF16), preferred_element_type=F32)
    b = jnp.dot(f_ref[...], h_bwd.astype(BF16), preferred_element_type=F32)
    re = a[:L] + b[:L]
    im = a[L:] - b[L:]
    nyq = a[L:L + 1] + b[L:L + 1]
    p_ref[0] = re
    q_ref[0] = jnp.where(row == 0, 0.0, im)
    r_ref[0] = jnp.where(row == 0, nyq, re)


def _hy_filter(L, w1p, b1, w2, b2, w3, ld):
    z, t = _hy_positions(L)
    fwd = jnp.asarray(_dft_mats(L)[0]).astype(BF16)
    out = jax.ShapeDtypeStruct((DEPTH, L, 2 * HY_CH), F32)
    full = lambda a: pl.BlockSpec(a.shape, lambda l: (0,) * a.ndim)
    layer = lambda a: pl.BlockSpec((1,) + a.shape[1:], lambda l: (l,) + (0,) * (a.ndim - 1))
    return pl.pallas_call(
        functools.partial(_hy_filter_kernel, L=L),
        grid=(DEPTH,),
        in_specs=[full(z), full(t), layer(w1p), layer(b1), layer(w2), layer(b2), layer(w3), layer(ld), full(fwd)],
        out_specs=[pl.BlockSpec((1, L, 2 * HY_CH), lambda l: (l, 0, 0))] * 3,
        out_shape=[out, out, out],
        compiler_params=_cparams(("arbitrary",)),
        name=f"hyena_filter_{L}",
    )(z, t, w1p, b1, w2, b2, w3, ld, fwd)


def _hyena_kernel(u_ref, sw_ref, bias_ref, p_ref, q_ref, r_ref, f_ref, g_ref, o_ref, *, L, n_sub):
    c = HY_CH
    u = _conv3(u_ref[...], sw_ref[0], n_sub * L, L)
    for s in range(n_sub):
        rs = slice(s * L, (s + 1) * L)
        z = u[rs, 0:c]
        for o in range(2):
            gate = u[rs, (o + 1) * c:(o + 2) * c]
            zf = jnp.dot(f_ref[...], z.astype(BF16), preferred_element_type=F32)
            a, b = zf[:L], zf[L:]
            p = p_ref[0, :, o * c:(o + 1) * c]
            q = q_ref[0, :, o * c:(o + 1) * c]
            r = r_ref[0, :, o * c:(o + 1) * c]
            y_re = (a * p - b * q).astype(BF16)
            y_im = (a * q + b * r).astype(BF16)
            y = (jnp.dot(g_ref[:, :L], y_re, preferred_element_type=F32)
                 + jnp.dot(g_ref[:, L:], y_im, preferred_element_type=F32))
            z = gate * (y + bias_ref[0, o:o + 1, :] * z)
        o_ref[rs, :] = z


def _hyena(u_hy, L, n_seq, row_block0, sw, bias, p, q, r, l):
    n_sub = max(1, HY_ROWS // L)
    fwd, inv = (jnp.asarray(m).astype(BF16) for m in _dft_mats(L))
    full = lambda a: pl.BlockSpec(a.shape, lambda b: (0,) * a.ndim)
    layer = lambda a: _layer_spec(a.shape[1:], l)
    blk0 = row_block0 // n_sub
    return pl.pallas_call(
        functools.partial(_hyena_kernel, L=L, n_sub=n_sub),
        grid=(n_seq // n_sub,),
        in_specs=[pl.BlockSpec((n_sub * L, 3 * HY_CH), lambda b: (blk0 + b, 0)),
                  layer(sw), layer(bias), layer(p), layer(q), layer(r), full(fwd), full(inv)],
        out_specs=pl.BlockSpec((n_sub * L, HY_CH), lambda b: (b, 0)),
        out_shape=jax.ShapeDtypeStruct((n_seq * L, HY_CH), F32),
        compiler_params=_cparams(("arbitrary",)),
        name=f"hyena_{L}",
    )(u_hy, sw, bias, p, q, r, fwd, inv)


def _cumsum_rows(x, n, reverse):
    row = lax.broadcasted_iota(jnp.int32, (n, 1), 0)
    s = 1
    while s < n:
        if reverse:
            x = x + jnp.where(row < n - s, pltpu.roll(x, n - s, 0), 0.0)
        else:
            x = x + jnp.where(row >= s, pltpu.roll(x, s, 0), 0.0)
        s *= 2
    return x


def _cumsum_lanes(x, n, reverse):
    col = lax.broadcasted_iota(jnp.int32, (1, n), 1)
    s = 1
    while s < n:
        if reverse:
            x = x + jnp.where(col < n - s, pltpu.roll(x, n - s, 1), 0.0)
        else:
            x = x + jnp.where(col >= s, pltpu.roll(x, s, 1), 0.0)
        s *= 2
    return x


def _mlstm_kernel(*refs, L, has_state, emit_state, n_prev):
    qk_ref, v_ref, o_ref, g_ref, gt_ref, sw_ref, gb_ref, gbt_ref, ng_ref = refs[:9]
    refs = refs[9:]
    if has_state:
        c0_ref, n0_ref, m0_ref = refs[:3]
        refs = refs[3:]
    prev_c_refs = refs[:n_prev]
    refs = refs[n_prev:]
    y_ref = refs[0]
    if emit_state:
        cout_ref, nout_ref, mout_ref = refs[1:4]
        refs = refs[4:]
    else:
        refs = refs[1:]
    q_s, k_s, vt_s, ht_s, ct_s, n_s, m_s, lf_s, lft_s = refs
    tc = ML_CHUNK
    nc = L // tc
    nh = ML_HEADS
    dh = ML_DH

    lf_s[...] = _log_sigmoid(g_ref[...] + gb_ref[0])
    lft_s[...] = _log_sigmoid(gt_ref[...] + gbt_ref[0])

    qk = _silu(_conv3(qk_ref[...], sw_ref[0], L, L))
    q_s[...] = qk[:, :ML_W].astype(BF16)
    k_s[...] = (qk[:, ML_W:] * (dh ** -0.5)).astype(BF16)
    for c in range(nc):
        vt_s[c] = v_ref[c * tc:(c + 1) * tc, :].T.astype(BF16)

    for i in range(2 * nh):
        ct_s[i] = c0_ref[0, 0, i].T if has_state else jnp.zeros((dh, dh), F32)
    n_s[...] = n0_ref[0, 0] if has_state else jnp.zeros_like(n_s)
    m_s[...] = m0_ref[0, 0] if has_state else jnp.zeros_like(m_s)

    si = lax.broadcasted_iota(jnp.int32, (tc, tc), 0)
    ti = lax.broadcasted_iota(jnp.int32, (tc, tc), 1)

    for d in range(2):
        rev = d == 1
        mask = (si >= ti) if rev else (si <= ti)
        edge = 0 if rev else tc - 1

        def chunk(j, carry, d=d, rev=rev, mask=mask, edge=edge):
            cidx = (nc - 1 - j) if rev else j
            r0 = pl.multiple_of(cidx * tc, tc)
            pre = g_ref[pl.ds(r0, tc), :] + gb_ref[0]
            pre_t = gt_ref[cidx] + gbt_ref[0]
            cum = _cumsum_rows(lf_s[pl.ds(r0, tc), :], tc, rev)
            cum_t = _cumsum_lanes(lft_s[cidx], tc, rev)
            key_all = cum - pltpu.roll(pre, 8, 1)
            for h in range(nh):
                col = d * nh + h
                hs = slice(h * dh, (h + 1) * dh)
                key = key_all[:, 8 + col:9 + col]
                b_row = cum_t[8 + col:9 + col, :]
                b_end = b_row[:, edge:edge + 1]
                m_prev = m_s[col:col + 1, 0:1]
                dmat = jnp.where(mask, b_row - key, -jnp.inf)
                inter = b_row + m_prev
                m_row = jnp.maximum(inter, jnp.max(dmat, axis=0, keepdims=True))
                w_intra = jnp.exp(dmat - m_row)
                w_state = jnp.exp(inter - m_row)
                qh = q_s[pl.ds(r0, tc), hs]
                kh = k_s[pl.ds(r0, tc), hs]
                vt = vt_s[cidx, hs, :]
                ct_prev = ct_s[col]
                n_prev = n_s[col:col + 1, :]
                s = lax.dot_general(kh, qh, _NT, preferred_element_type=F32) * w_intra
                num = (jnp.dot(vt, s.astype(BF16), preferred_element_type=F32)
                       + w_state * lax.dot_general(ct_prev.astype(BF16), qh, _NT, preferred_element_type=F32))
                qn = lax.dot_general(jnp.broadcast_to(n_prev, (8, dh)).astype(BF16), qh, _NT,
                                     preferred_element_type=F32)[0:1]
                den = jnp.sum(s, axis=0, keepdims=True) + w_state * qn
                hout = num * (1.0 / jnp.maximum(jnp.abs(den), jnp.exp(-m_row)))
                if d == 0:
                    ht_s[cidx, hs, :] = hout
                else:
                    ht_s[cidx, hs, :] += hout
                m_new = jnp.maximum(b_end + m_prev, b_end - jnp.min(key, axis=0, keepdims=True))
                wg = jnp.exp(b_end - key - m_new)
                decay = jnp.exp(b_end + m_prev - m_new)
                kw = kh.astype(F32) * wg
                ct_s[col] = decay * ct_prev + jnp.dot(vt, kw.astype(BF16), preferred_element_type=F32)
                n_s[col:col + 1, :] = decay * n_prev + jnp.sum(kw, axis=0, keepdims=True)
                m_s[col:col + 1, :] = jnp.broadcast_to(m_new, (1, LANES))
            return carry

        lax.fori_loop(0, nc, chunk, 0)

    for c in range(nc):
        for h in range(nh):
            hs = slice(h * dh, (h + 1) * dh)
            rs = slice(c * tc, (c + 1) * tc)
            ht = ht_s[c, hs, :]
            hn = ht * lax.rsqrt(jnp.mean(ht * ht, axis=0, keepdims=True) + EPS)
            y_ref[rs, hs] = jax.nn.sigmoid(o_ref[rs, hs]) * (hn.T * ng_ref[0, :, hs])
    if emit_state:
        for k, prev_ref in enumerate(prev_c_refs):
            cout_ref[0, k] = prev_ref[0]
        for i in range(2 * nh):
            cout_ref[0, n_prev, i] = ct_s[i].T
        nout_ref[0] = n_s[...]
        mout_ref[0] = m_s[...]


def _mlstm(qk, v, o, gates, gates_t, L, n_seq, row_block0, sw, gb, gbt, ng, state, l, emit_state=False,
           prev_c=()):
    nc = L // ML_CHUNK
    has_state = state is not None
    layer = lambda a: _layer_spec(a.shape[1:], l)
    in_specs = [pl.BlockSpec((L, 2 * ML_W), lambda b: (row_block0 + b, 0)),
                pl.BlockSpec((L, ML_W), lambda b: (row_block0 + b, 0)),
                pl.BlockSpec((L, ML_W), lambda b: (row_block0 + b, 0)),
                pl.BlockSpec((L, GATE_PAD), lambda b: (row_block0 + b, 0)),
                pl.BlockSpec((nc, 16, ML_CHUNK), lambda b: (row_block0 + b, 0, 0)),
                layer(sw), layer(gb), layer(gbt), layer(ng)]
    args = [qk, v, o, gates, gates_t, sw, gb, gbt, ng]
    if has_state:
        c0, n0, m0 = state
        in_specs += [pl.BlockSpec((1, 1, 2 * ML_HEADS, ML_DH, ML_DH), lambda b: (b, l, 0, 0, 0)),
                     pl.BlockSpec((1, 1, 2 * ML_HEADS, ML_DH), lambda b: (b, l, 0, 0)),
                     pl.BlockSpec((1, 1, 2 * ML_HEADS, LANES), lambda b: (b, l, 0, 0))]
        args += [c0, n0, m0]
    out_specs = [pl.BlockSpec((L, ML_W), lambda b: (b, 0))]
    out_shape = [jax.ShapeDtypeStruct((n_seq * L, ML_W), F32)]
    if emit_state:
        c_tail = (2 * ML_HEADS, ML_DH, ML_DH)
        for a in prev_c:
            in_specs.append(pl.BlockSpec((1,) + c_tail, lambda b: (b, 0, 0, 0)))
            args.append(a)
        tails = ((len(prev_c) + 1,) + c_tail, (2 * ML_HEADS, ML_DH), (2 * ML_HEADS, LANES))
        for t in tails:
            out_specs.append(pl.BlockSpec((1,) + t, lambda b, n=len(t): (b,) + (0,) * n))
            out_shape.append(jax.ShapeDtypeStruct((n_seq,) + t, F32))
    return pl.pallas_call(
        functools.partial(_mlstm_kernel, L=L, has_state=has_state, emit_state=emit_state,
                          n_prev=len(prev_c) if emit_state else 0),
        grid=(n_seq,),
        in_specs=in_specs,
        out_specs=out_specs,
        out_shape=out_shape,
        scratch_shapes=[pltpu.VMEM((L, ML_W), BF16), pltpu.VMEM((L, ML_W), BF16),
                        pltpu.VMEM((nc, ML_W, ML_CHUNK), BF16),
                        pltpu.VMEM((nc, ML_W, ML_CHUNK), F32),
                        pltpu.VMEM((2 * ML_HEADS, ML_DH, ML_DH), F32),
                        pltpu.VMEM((2 * ML_HEADS, ML_DH), F32),
                        pltpu.VMEM((2 * ML_HEADS, LANES), F32),
                        pltpu.VMEM((L, GATE_PAD), F32),
                        pltpu.VMEM((nc, 16, ML_CHUNK), F32)],
        compiler_params=_cparams(("arbitrary",)),
        name=f"mlstm_{L}",
    )(*args)


def _cmul(ar, ai, br, bi):
    return ar * br - ai * bi, ar * bi + ai * br


def _s5_kernel(*refs, segmented):
    if segmented:
        (u_ref, bb_ref, cc_ref, eb_ref, ec_ref, lam_ref, dsk_ref, wglu_ref, s0_ref, y_ref,
         sbuf, yacc, bmat, cmat, pw) = refs
    else:
        (u_ref, bb_ref, cc_ref, eb_ref, ec_ref, lam_ref, dsk_ref, wglu_ref, y_ref, fin_ref,
         sbuf, yacc, bmat, cmat) = refs
    n = S5_STATE
    rows = S5_ROWS
    steps = S5_STEPS
    blk = 256
    n_blk = steps * rows // blk
    n_seg = 4

    yacc[...] = u_ref[0] * dsk_ref[0]
    ub = u_ref[0].astype(BF16)
    b_keep = (lax.broadcasted_iota(jnp.int32, (S5_CH, 2 * n), 0) // S5_GROUP
              == (lax.broadcasted_iota(jnp.int32, (S5_CH, 2 * n), 1) % n) // S5_P)
    c_keep = ((lax.broadcasted_iota(jnp.int32, (2 * n, S5_CH), 0) % n) // S5_P
              == lax.broadcasted_iota(jnp.int32, (2 * n, S5_CH), 1) // S5_GROUP)

    for d in range(2):
        rev = d == 1
        bmat[...] = jnp.where(b_keep, jnp.dot(bb_ref[0, d].astype(BF16), eb_ref[...], preferred_element_type=F32),
                              0.0).astype(BF16)
        cmat[...] = jnp.where(c_keep, jnp.dot(cc_ref[0, d].astype(BF16), ec_ref[...], preferred_element_type=F32),
                              0.0).astype(BF16)
        for i in range(n_blk):
            sbuf[i * blk:(i + 1) * blk, :] = jnp.dot(ub[i * blk:(i + 1) * blk], bmat[...],
                                                     preferred_element_type=F32)
        lam = lam_ref[0, d]
        lr = jnp.broadcast_to(lam[:, :n], (rows, n))
        li = jnp.broadcast_to(lam[:, n:], (rows, n))

        def step(i, carry, rev=rev, lr=lr, li=li):
            sr, si = carry
            t = (steps - 1 - i) if rev else i
            off = pl.multiple_of(t * rows, rows)
            pr, pi = _cmul(lr, li, sr, si)
            nr = pr + sbuf[pl.ds(off, rows), 0:n]
            ni = pi + sbuf[pl.ds(off, rows), n:2 * n]
            sbuf[pl.ds(off, rows), 0:n] = nr
            sbuf[pl.ds(off, rows), n:2 * n] = ni
            return nr, ni

        zero = jnp.zeros((rows, n), F32)
        sr, si = lax.fori_loop(0, steps, step, (zero, zero), unroll=2)

        if not segmented:
            fin_ref[0, d, :, 0:n] = sr
            fin_ref[0, d, :, n:2 * n] = si
        else:
            lam_r, lam_i = lam[:, :n], lam[:, n:]
            row8 = lax.broadcasted_iota(jnp.int32, (rows, 1), 0)
            cr, ci = lam_r, lam_i
            acc_r = jnp.broadcast_to(cr, (rows, n))
            acc_i = jnp.broadcast_to(ci, (rows, n))
            for j in range(1, rows):
                cr, ci = _cmul(cr, ci, lam_r, lam_i)
                acc_r = jnp.where(row8 >= j, jnp.broadcast_to(cr, (rows, n)), acc_r)
                acc_i = jnp.where(row8 >= j, jnp.broadcast_to(ci, (rows, n)), acc_i)
            pw[0:rows, 0:n] = acc_r
            pw[0:rows, n:2 * n] = acc_i
            size = rows
            while size < steps:
                tr = pw[size - 1:size, 0:n]
                ti = pw[size - 1:size, n:2 * n]
                xr, xi = _cmul(pw[0:size, 0:n], pw[0:size, n:2 * n], tr, ti)
                pw[size:2 * size, 0:n] = xr
                pw[size:2 * size, n:2 * n] = xi
                size *= 2
            end_off = 0 if rev else (steps - 1) * rows
            loc_r = sbuf[end_off:end_off + rows, 0:n]
            loc_i = sbuf[end_off:end_off + rows, n:2 * n]
            pl_r = pw[steps - 1:steps, 0:n]
            pl_i = pw[steps - 1:steps, n:2 * n]
            s0r = s0_ref[0, d, :, 0:n]
            s0i = s0_ref[0, d, :, n:2 * n]
            seg = row8 // 2
            first = (seg == n_seg - 1) if rev else (seg == 0)
            shift = (rows - 2) if rev else 2
            cin_r, cin_i = s0r, s0i
            for _ in range(n_seg - 1):
                fr, fi = _cmul(jnp.broadcast_to(pl_r, (rows, n)), jnp.broadcast_to(pl_i, (rows, n)), cin_r, cin_i)
                tru_r = loc_r + fr
                tru_i = loc_i + fi
                cin_r = jnp.where(first, s0r, pltpu.roll(tru_r, shift, 0))
                cin_i = jnp.where(first, s0i, pltpu.roll(tru_i, shift, 0))

            def fix(tb, carry, rev=rev, cin_r=cin_r, cin_i=cin_i):
                pb = (steps // rows - 1 - tb) if rev else tb
                poff = pl.multiple_of(pb * rows, rows)
                p_r = pw[pl.ds(poff, rows), 0:n]
                p_i = pw[pl.ds(poff, rows), n:2 * n]
                for j in range(rows):
                    jj = rows - 1 - j if rev else j
                    off = pl.multiple_of((tb * rows + j) * rows, rows)
                    fr, fi = _cmul(jnp.broadcast_to(p_r[jj:jj + 1], (rows, n)),
                                   jnp.broadcast_to(p_i[jj:jj + 1], (rows, n)), cin_r, cin_i)
                    sbuf[pl.ds(off, rows), 0:n] += fr
                    sbuf[pl.ds(off, rows), n:2 * n] += fi
                return carry

            lax.fori_loop(0, steps // rows, fix, 0)

        for i in range(n_blk):
            yacc[i * blk:(i + 1) * blk, :] += jnp.dot(sbuf[i * blk:(i + 1) * blk, :].astype(BF16), cmat[...],
                                                      preferred_element_type=F32)

    g = jax.nn.gelu(yacc[...], approximate=True)
    y_ref[0] = g * jax.nn.sigmoid(_bdot(g, wglu_ref[0]))


@functools.lru_cache(None)
def _s5_spread():
    eb = np.zeros((2 * S5_P, 2 * S5_STATE), np.float32)
    for half in range(2):
        for g in range(S5_G):
            c0 = half * S5_STATE + g * S5_P
            eb[half * S5_P:(half + 1) * S5_P, c0:c0 + S5_P] = np.eye(S5_P)
    ec = np.zeros((LANES, S5_CH), np.float32)
    for g in range(S5_G):
        ec[:S5_GROUP, g * S5_GROUP:(g + 1) * S5_GROUP] = np.eye(S5_GROUP)
    return eb, ec


def _s5(u_tm, bb, cc, lam, dskip, wglu, s0, l):
    n_grp = u_tm.shape[0]
    n_rows = S5_STEPS * S5_ROWS
    segmented = s0 is not None
    eb, ec = (jnp.asarray(m).astype(BF16) for m in _s5_spread())
    full = lambda a: pl.BlockSpec(a.shape, lambda g: (0,) * a.ndim)
    layer = lambda a: _layer_spec(a.shape[1:], l)
    in_specs = [pl.BlockSpec((1, n_rows, S5_CH), lambda g: (g, 0, 0)),
                layer(bb), layer(cc), full(eb), full(ec), layer(lam), layer(dskip), layer(wglu)]
    args = [u_tm, bb, cc, eb, ec, lam, dskip, wglu]
    out_specs = [pl.BlockSpec((1, n_rows, S5_CH), lambda g: (g, 0, 0))]
    out_shape = [jax.ShapeDtypeStruct((n_grp, n_rows, S5_CH), F32)]
    scratch = [pltpu.VMEM((n_rows, 2 * S5_STATE), F32), pltpu.VMEM((n_rows, S5_CH), F32),
               pltpu.VMEM((S5_CH, 2 * S5_STATE), BF16), pltpu.VMEM((2 * S5_STATE, S5_CH), BF16)]
    if segmented:
        in_specs.append(layer(s0))
        args.append(s0)
        scratch.append(pltpu.VMEM((S5_STEPS, 2 * S5_STATE), F32))
    else:
        out_specs.append(pl.BlockSpec((1, 2, S5_ROWS, 2 * S5_STATE), lambda g: (g, 0, 0, 0)))
        out_shape.append(jax.ShapeDtypeStruct((n_grp, 2, S5_ROWS, 2 * S5_STATE), F32))
    return pl.pallas_call(
        functools.partial(_s5_kernel, segmented=segmented),
        grid=(n_grp,),
        in_specs=in_specs,
        out_specs=out_specs,
        out_shape=out_shape,
        scratch_shapes=scratch,
        compiler_params=_cparams(("arbitrary",)),
        name="s5_seg" if segmented else "s5_ctx",
    )(*args)


def _s5_params(a_re, a_im, log_dt, b_re, b_im, c_re, c_im):
    dt = jnp.exp(log_dt)[..., None]
    mag = jnp.exp(a_re * dt)
    lb_re = mag * jnp.cos(a_im * dt)
    lb_im = mag * jnp.sin(a_im * dt)
    den = a_re * a_re + a_im * a_im
    nr, ni = lb_re - 1.0, lb_im
    k_re = (nr * a_re + ni * a_im) / den
    k_im = (ni * a_re - nr * a_im) / den
    bb_re = k_re[..., None] * b_re - k_im[..., None] * b_im
    bb_im = k_re[..., None] * b_im + k_im[..., None] * b_re
    to_gc_p = lambda m: m.transpose(0, 1, 2, 4, 3).reshape(DEPTH, 2, S5_CH, S5_P)
    bb = jnp.concatenate([to_gc_p(bb_re), to_gc_p(bb_im)], axis=3)
    to_gp_c = lambda m: m.transpose(0, 1, 2, 4, 3).reshape(DEPTH, 2, S5_STATE, S5_GROUP)
    cc = jnp.concatenate([to_gp_c(c_re), -to_gp_c(c_im)], axis=2)
    cc = jnp.pad(cc, ((0, 0), (0, 0), (0, 0), (0, LANES - S5_GROUP)))
    lam = jnp.concatenate([lb_re.reshape(DEPTH, 2, 1, S5_STATE), lb_im.reshape(DEPTH, 2, 1, S5_STATE)], axis=3)
    return bb, cc, lam


def _outproj_kernel(*refs, n_x):
    x_refs = refs[:n_x]
    (hyc_ref, hyl_ref, mlc_ref, mll_ref, s5c_ref, s5l_ref, w_ref, mod_ref, g_ref,
     rw_ref, rb_ref, xn_ref, h2e_ref, best_ref, rank_ref, cnt_ref, wb, cnt_s) = refs[n_x:]
    step = pl.program_id(0)

    @pl.when(step == 0)
    def _():
        wb[...] = w_ref[0].astype(BF16)
        cnt_s[...] = jnp.zeros_like(cnt_s)

    is_ctx = step < T_CTX // TM
    pick = lambda c_ref, l_ref: jnp.where(is_ctx, c_ref[...], l_ref[...]).astype(BF16)
    mod = mod_ref[0, 0]
    a, b = HY_CH, HY_CH + ML_W
    mix = (jnp.dot(pick(hyc_ref, hyl_ref), wb[0:a, :], preferred_element_type=F32)
           + jnp.dot(pick(mlc_ref, mll_ref), wb[a:b, :], preferred_element_type=F32)
           + jnp.dot(pick(s5c_ref, s5l_ref), wb[b:, :], preferred_element_type=F32))
    xn = _x_tile(step, x_refs) + mod[2:3] * mix
    xn_ref[...] = xn
    h2 = _rms(xn, g_ref[0]) * (1.0 + mod[4:5]) + mod[3:4]
    h2e_ref[:, 0:D] = h2
    h_hi, h_lo = _split_bf16(h2)
    r_hi, r_lo = _split_bf16(rw_ref[...])
    logits = (lax.dot_general(r_hi, h_hi, _NT, preferred_element_type=F32)
              + (lax.dot_general(r_hi, h_lo, _NT, preferred_element_type=F32)
                 + lax.dot_general(r_lo, h_hi, _NT, preferred_element_type=F32)))
    ex = jnp.exp(logits - jnp.max(logits, axis=0, keepdims=True))
    probs = ex / jnp.sum(ex, axis=0, keepdims=True)
    sel = probs + rb_ref[...]
    best = None
    best_score = None
    for g in range(N_GROUPS):
        r = [sel[g * GROUP_SIZE + i:g * GROUP_SIZE + i + 1, :] for i in range(GROUP_SIZE)]
        score = None
        for i in range(GROUP_SIZE):
            for j in range(i + 1, GROUP_SIZE):
                pair = r[i] + r[j]
                score = pair if score is None else jnp.maximum(score, pair)
        if g == 0:
            best, best_score = jnp.zeros_like(score, dtype=jnp.int32), score
        else:
            upd = score > best_score
            best = jnp.where(upd, g, best)
            best_score = jnp.where(upd, score, best_score)
    eid = lax.broadcasted_iota(jnp.int32, (N_EXPERTS, 1), 0)
    masked = jnp.where(eid // GROUP_SIZE == best, sel, -jnp.inf)
    m1 = jnp.max(masked, axis=0, keepdims=True)
    i1 = jnp.min(jnp.where(masked == m1, eid, N_EXPERTS), axis=0, keepdims=True)
    masked2 = jnp.where(eid == i1, -jnp.inf, masked)
    m2 = jnp.max(masked2, axis=0, keepdims=True)
    i2 = jnp.min(jnp.where(masked2 == m2, eid, N_EXPERTS), axis=0, keepdims=True)
    p1 = jnp.sum(jnp.where(eid == i1, probs, 0.0), axis=0, keepdims=True)
    p2 = jnp.sum(jnp.where(eid == i2, probs, 0.0), axis=0, keepdims=True)
    tot = p1 + p2
    comb = jnp.where(eid == i1, p1 / tot, 0.0) + jnp.where(eid == i2, p2 / tot, 0.0)
    comb = jnp.concatenate([comb, jnp.zeros((LANES - N_EXPERTS, comb.shape[1]), F32)], axis=0)
    h2e_ref[:, D:] = comb.T
    best_ref[...] = best
    gid = lax.broadcasted_iota(jnp.int32, (8, 1), 0)
    onehot = (gid == best).astype(F32)
    cum = _cumsum_lanes(onehot, TM, False)
    run = cnt_s[:, 0:1]
    rank_ref[...] = jnp.sum(onehot * (cum - onehot + run), axis=0, keepdims=True).astype(jnp.int32)
    cnt_s[...] = jnp.broadcast_to(run + cum[:, TM - 1:TM], cnt_s.shape)
    cnt_ref[...] = cnt_s[...].astype(jnp.int32)


def _outproj(x, y_hy, y_ml, y_s5, w_out, l, mod, g2, rw_t, rb):
    full = lambda a: pl.BlockSpec(a.shape, lambda i: (0,) * a.ndim)
    n_ctx = T_CTX // TM
    ctx = lambda w: pl.BlockSpec((TM, w), lambda i: (jnp.minimum(i, n_ctx - 1), 0))
    lat = lambda w: pl.BlockSpec((TM, w), lambda i: (jnp.maximum(i - n_ctx, 0), 0))
    tok = lambda w: pl.BlockSpec((TM, w), lambda i: (i, 0))
    row = pl.BlockSpec((1, TM), lambda i: (0, i))
    return pl.pallas_call(
        functools.partial(_outproj_kernel, n_x=len(x)),
        grid=(T_ALL // TM,),
        in_specs=_x_specs(x) + [ctx(HY_CH), lat(HY_CH), ctx(ML_W), lat(ML_W), ctx(S5_CH), lat(S5_CH),
                  pl.BlockSpec((1, D, D), lambda i: (l, 0, 0), pipeline_mode=pl.Buffered(1)),
                  _mod_spec(l, TM), _layer_spec((1, D), l), full(rw_t), full(rb)],
        out_specs=[tok(D), tok(D + LANES), row, row, pl.BlockSpec((8, LANES), lambda i: (0, 0))],
        out_shape=[jax.ShapeDtypeStruct((T_ALL, D), F32),
                   jax.ShapeDtypeStruct((T_ALL, D + LANES), F32),
                   jax.ShapeDtypeStruct((1, T_ALL), jnp.int32),
                   jax.ShapeDtypeStruct((1, T_ALL), jnp.int32),
                   jax.ShapeDtypeStruct((8, LANES), jnp.int32)],
        scratch_shapes=[pltpu.VMEM((D, D), BF16), pltpu.VMEM((8, LANES), F32)],
        compiler_params=_cparams(("arbitrary",)),
        name="outproj_router",
    )(*x, *y_hy, *y_ml, *y_s5, w_out, mod, g2, rw_t, rb)


def _gather_rows(idx_ref, idx_base, src_ref, dst_ref, n_rows):
    def body(r8, carry):
        base = pl.multiple_of(r8 * 8, 8)
        for k in range(8):
            idx = idx_ref[idx_base + base + k]
            dst_ref[pl.ds(base + k, 1), :] = src_ref[pl.ds(idx, 1), :]
        return carry

    lax.fori_loop(0, n_rows // 8, body, 0)


def _dispatch_kernel(best_ref, rank_ref, cnt_ref, pos_ref, src_ref, tg_ref, nt_ref, off_s):
    tm = TM_MOE
    tiles = jnp.int32(0)
    tile_end = []
    for g in range(N_GROUPS):
        off_s[g] = tiles * tm
        tiles = tiles + (cnt_ref[g, 0] + (tm - 1)) // tm
        tile_end.append(tiles)
    nt_ref[0] = tiles
    for k in range(MOE_SLOTS // tm):
        g = jnp.int32(0)
        for e in tile_end[:-1]:
            g = g + (k >= e).astype(jnp.int32)
        tg_ref[k] = g

    def clear(i8, carry):
        for u in range(8):
            src_ref[i8 * 8 + u] = 0
        return carry

    lax.fori_loop(0, MOE_SLOTS // 8, clear, 0)

    def place(t8, carry):
        for u in range(8):
            t = t8 * 8 + u
            p = rank_ref[t] + off_s[best_ref[t]]
            pos_ref[t] = p
            src_ref[p] = t
        return carry

    lax.fori_loop(0, T_ALL // 8, place, 0)


def _dispatch(best, rank, cnt):
    smem = pl.BlockSpec(memory_space=pltpu.SMEM)
    i32 = lambda n: jax.ShapeDtypeStruct((n,), jnp.int32)
    return pl.pallas_call(
        _dispatch_kernel,
        in_specs=[smem, smem, smem],
        out_specs=[smem, smem, smem, smem],
        out_shape=[i32(T_ALL), i32(MOE_SLOTS), i32(MOE_SLOTS // TM_MOE), i32(1)],
        scratch_shapes=[pltpu.SMEM((N_GROUPS,), jnp.int32)],
        name="moe_dispatch",
    )(best.reshape(T_ALL), rank.reshape(T_ALL), cnt)


def _moe_kernel(src_ref, tg_ref, nt_ref, h_ref, wg_ref, wu_ref, wd_ref, ys_ref, g_a, x_a, g_b, x_b, acc):
    i = pl.program_id(0)
    j = pl.program_id(1)
    tm = TM_MOE
    quarter = tm // GROUP_SIZE
    n_live = nt_ref[0]

    @pl.when(jnp.logical_and(i == 0, j == 0))
    def _():
        _gather_rows(src_ref, 0, h_ref, g_a, tm)
        x_a[...] = g_a[:, 0:D].astype(BF16)

    def step(cur_g, cur_x, nxt_g, nxt_x):
        @pl.when(j == 0)
        def _():
            acc[...] = jnp.zeros_like(acc)

        e = tg_ref[i] * GROUP_SIZE + j
        lane = lax.broadcasted_iota(jnp.int32, (1, LANES), 1)
        wg = wg_ref[0, 0].astype(BF16)
        wu = wu_ref[0, 0].astype(BF16)
        wd = wd_ref[0, 0].astype(BF16)
        sub = MOE_SUB
        for s in range(tm // sub):
            rs = slice(s * sub, (s + 1) * sub)
            hb = cur_x[rs, :]
            ce = jnp.sum(jnp.where(lane == e, cur_g[rs, D:], 0.0), axis=1, keepdims=True)
            hid = _silu(jnp.dot(hb, wg, preferred_element_type=F32)) * jnp.dot(hb, wu, preferred_element_type=F32)
            acc[rs, :] += jnp.dot((hid * ce).astype(BF16), wd, preferred_element_type=F32)

        r0 = pl.multiple_of(j * quarter, quarter)
        base = jnp.minimum(i + 1, n_live - 1) * tm + r0
        for k in range(quarter):
            nxt_g[pl.ds(r0 + k, 1), :] = h_ref[pl.ds(src_ref[base + k], 1), :]
        nxt_x[pl.ds(r0, quarter), :] = nxt_g[pl.ds(r0, quarter), 0:D].astype(BF16)

        @pl.when(j == GROUP_SIZE - 1)
        def _():
            for c in range(D // LANES):
                ys_ref[pl.ds(c, tm, stride=D // LANES), :] = acc[:, c * LANES:(c + 1) * LANES]

    live = i < n_live

    @pl.when(jnp.logical_and(live, i % 2 == 0))
    def _():
        step(g_a, x_a, g_b, x_b)

    @pl.when(jnp.logical_and(live, i % 2 == 1))
    def _():
        step(g_b, x_b, g_a, x_a)

    @pl.when(jnp.logical_and(jnp.logical_not(live), j == GROUP_SIZE - 1))
    def _():
        ys_ref[...] = jnp.zeros_like(ys_ref)


def _moe(h2e, src, tile_group, n_tiles, wg, wu, wd, l):
    tm = TM_MOE

    def w_map(i, j, src, tg, nt):
        live = i < nt[0]
        ii = jnp.minimum(i, nt[0] - 1)
        return (l, tg[ii] * GROUP_SIZE + jnp.where(live, j, GROUP_SIZE - 1), 0, 0)

    return pl.pallas_call(
        _moe_kernel,
        grid_spec=pltpu.PrefetchScalarGridSpec(
            num_scalar_prefetch=3,
            grid=(MOE_SLOTS // tm, GROUP_SIZE),
            in_specs=[pl.BlockSpec(memory_space=pltpu.VMEM),
                      pl.BlockSpec((1, 1, D, D_EXPERT), w_map),
                      pl.BlockSpec((1, 1, D, D_EXPERT), w_map),
                      pl.BlockSpec((1, 1, D_EXPERT, D), w_map)],
            out_specs=pl.BlockSpec((tm * (D // LANES), LANES), lambda i, j, src, tg, nt: (i, 0)),
            scratch_shapes=[pltpu.VMEM((tm, D + LANES), F32), pltpu.VMEM((tm, D), BF16),
                            pltpu.VMEM((tm, D + LANES), F32), pltpu.VMEM((tm, D), BF16),
                            pltpu.VMEM((tm, D), F32)]),
        out_shape=jax.ShapeDtypeStruct((MOE_SLOTS * (D // LANES), LANES), F32),
        compiler_params=_cparams(("arbitrary", "arbitrary")),
        name="moe_experts",
    )(src, tile_group, n_tiles, h2e, wg, wu, wd)


def _combine_kernel(pos_ref, ys_ref, xn_ref, mod_ref, fg_ref, *rest, final):
    step = pl.program_id(0)
    if final:
        yc_ref, yl_ref, gbuf = rest
    else:
        out_ref, gbuf = rest
    n_lt = D // LANES

    def body(r8, carry):
        base = pl.multiple_of(r8 * 8, 8)
        for k in range(8):
            src = pl.multiple_of(pos_ref[step * TM + base + k] * n_lt, n_lt)
            gbuf[pl.ds(pl.multiple_of((base + k) * n_lt, n_lt), n_lt), :] = ys_ref[pl.ds(src, n_lt), :]
        return carry

    lax.fori_loop(0, TM // 8, body, 0)
    moe = jnp.concatenate([gbuf[pl.ds(c, TM, stride=n_lt), :] for c in range(n_lt)], axis=1)
    out = xn_ref[...] + mod_ref[0, 0][5:6] * moe
    if final:
        y = _rms(out, fg_ref[...])

        @pl.when(step < T_CTX // TM)
        def _():
            yc_ref[...] = y

        @pl.when(step >= T_CTX // TM)
        def _():
            yl_ref[...] = y
    else:
        out_ref[...] = out


def _combine(pos, ys, xn, mod, fg, l, final):
    spec = pl.BlockSpec((TM, D), lambda i, pos: (i, 0))
    n_ctx = T_CTX // TM
    if final:
        out_specs = [pl.BlockSpec((TM, D), lambda i, pos: (jnp.minimum(i, n_ctx - 1), 0)),
                     pl.BlockSpec((TM, D), lambda i, pos: (jnp.maximum(i - n_ctx, 0), 0))]
        out_shape = [jax.ShapeDtypeStruct((T_CTX, D), F32), jax.ShapeDtypeStruct((T_LAT, D), F32)]
    else:
        out_specs = [spec]
        out_shape = [jax.ShapeDtypeStruct((T_ALL, D), F32)]
    return pl.pallas_call(
        functools.partial(_combine_kernel, final=final),
        grid_spec=pltpu.PrefetchScalarGridSpec(
            num_scalar_prefetch=1,
            grid=(T_ALL // TM,),
            in_specs=[pl.BlockSpec(memory_space=pltpu.VMEM),
                      spec,
                      _mod_spec(l, TM),
                      pl.BlockSpec((1, D), lambda i, pos: (0, 0))],
            out_specs=out_specs,
            scratch_shapes=[pltpu.VMEM((TM * (D // LANES), LANES), F32)]),
        out_shape=out_shape,
        compiler_params=_cparams(("arbitrary",)),
        name="moe_combine",
    )(pos, ys, xn, mod, fg)


@functools.lru_cache(None)
def _pos_embed():
    rows = L_LAT // GRID_W
    r = np.repeat(np.arange(rows, dtype=np.float64), GRID_W)
    col = np.tile(np.arange(GRID_W, dtype=np.float64), rows)
    quarter = D // 4
    freq = np.exp(-math.log(POS_BASE) * np.arange(quarter, dtype=np.float64) / quarter)
    ar = r[:, None] * freq[None]
    ac = col[:, None] * freq[None]
    emb = np.concatenate([np.sin(ar), np.cos(ar), np.sin(ac), np.cos(ac)], axis=-1)
    return emb.astype(np.float32)


def _to_time_major_ctx(a):
    c = a.shape[-1]
    a = a.reshape(2, S5_ROWS, L_CTX, c).transpose(0, 2, 1, 3)
    return a.reshape(2, L_CTX * S5_ROWS, c)


def _from_time_major_ctx(a):
    c = a.shape[-1]
    a = a.reshape(2, L_CTX, S5_ROWS, c).transpose(0, 2, 1, 3)
    return a.reshape(T_CTX, c)


def _to_time_major_lat(a):
    c = a.shape[-1]
    a = a.reshape(N_LAT_SEQ, 4, S5_STEPS, c).transpose(2, 1, 0, 3)
    return a.reshape(1, S5_STEPS * S5_ROWS, c)


def _from_time_major_lat(a):
    c = a.shape[-1]
    a = a.reshape(S5_STEPS, 4, N_LAT_SEQ, c).transpose(2, 1, 0, 3)
    return a.reshape(T_LAT, c)


def kernel(x_prompt, x_sample, c, state_mlstm_C, state_mlstm_n, state_mlstm_m, state_s5_re, state_s5_im, c_ctx, w_ada, b_ada, norm1_g, norm2_g, final_g, w_in, w_out, hy_short, hy_fw1, hy_fb1, hy_fw2, hy_fb2, hy_fw3, hy_log_decay, hy_bias, ml_short, ml_gate_bias, ml_norm_g, s5_a_re, s5_a_im, s5_log_dt, s5_b_re, s5_b_im, s5_c_re, s5_c_im, s5_d, s5_w_glu, router_w, router_b, moe_w_gate, moe_w_up, moe_w_down):
    x = (x_prompt.reshape(T_CTX, D), x_sample.reshape(T_LAT, D), jnp.asarray(_pos_embed()))
    w_in_t = jnp.swapaxes(w_in, 1, 2)
    cc =jnp.concatenate([c_ctx[None], c, jnp.zeros((8 - 1 - N_LAT_SEQ, D), F32)], axis=0)
    mod = _ada(cc, w_ada, b_ada).reshape(DEPTH, 8, 6, D)
    rw_t = router_w.T
    rb = router_b.reshape(N_EXPERTS, 1)
    fg = final_g.reshape(1, D)
    lat_blk = T_CTX // L_LAT
    g1 = norm1_g.reshape(DEPTH, 1, D)
    g2 = norm2_g.reshape(DEPTH, 1, D)

    w1p = jnp.pad(hy_fw1, ((0, 0), (0, LANES - HY_EMB), (0, 0)))
    b1 = hy_fb1.reshape(DEPTH, 1, HY_FILTER_W)
    b2 = hy_fb2.reshape(DEPTH, 1, HY_FILTER_W)
    ld = hy_log_decay.reshape(DEPTH, 1, 4 * HY_CH)
    hy_spec = {L: _hy_filter(L, w1p, b1, hy_fw2, b2, hy_fw3, ld) for L in (L_CTX, L_LAT)}
    gb = jnp.pad(ml_gate_bias.reshape(DEPTH, 1, 16), ((0, 0), (0, 0), (0, GATE_PAD - 16)))
    gbt = ml_gate_bias.reshape(DEPTH, 16, 1)
    ng = ml_norm_g.reshape(DEPTH, 1, ML_W)
    ml_state = (state_mlstm_C.reshape(N_LAT_SEQ, DEPTH, 2 * ML_HEADS, ML_DH, ML_DH),
                state_mlstm_n.reshape(N_LAT_SEQ, DEPTH, 2 * ML_HEADS, ML_DH),
                jnp.broadcast_to(state_mlstm_m.reshape(N_LAT_SEQ, DEPTH, 2 * ML_HEADS, 1),
                                 (N_LAT_SEQ, DEPTH, 2 * ML_HEADS, LANES)))
    bb, cc_s5, lam = _s5_params(s5_a_re, s5_a_im, s5_log_dt, s5_b_re, s5_b_im, s5_c_re, s5_c_im)
    dsk = s5_d.reshape(DEPTH, 1, S5_CH)
    wglu = s5_w_glu.astype(BF16)
    s0 = jnp.concatenate([state_s5_re.reshape(N_LAT_SEQ, DEPTH, 2, S5_STATE),
                          state_s5_im.reshape(N_LAT_SEQ, DEPTH, 2, S5_STATE)], axis=-1)
    s0 = jnp.tile(s0.transpose(1, 2, 0, 3), (1, 1, 4, 1))

    new_n, new_m, new_re, new_im = [], [], [], []
    prev_c, c_all = [], None
    y_prompt = y_sample = None
    for l in range(DEPTH):
        u_hy, qk, v, o, u_s5, gates, gates_t = _inproj(x, g1, mod, w_in_t, l)

        y_hy = [_hyena(u_hy, L, n_seq, blk0, hy_short, hy_bias, *hy_spec[L], l)
                for L, n_seq, blk0 in ((L_CTX, N_CTX_SEQ, 0), (L_LAT, N_LAT_SEQ, lat_blk))]

        yc, c_all, nc_, mc_ = _mlstm(qk, v, o, gates, gates_t, L_CTX, N_CTX_SEQ, 0, ml_short, gb, gbt, ng, None, l,
                                     emit_state=True, prev_c=prev_c)
        prev_c = [c_all[:, k] for k in range(l + 1)] if l + 1 < DEPTH else None
        (yl,) = _mlstm(qk, v, o, gates, gates_t, L_LAT, N_LAT_SEQ, lat_blk, ml_short, gb, gbt, ng, ml_state, l)
        y_ml = (yc, yl)
        new_n.append(nc_.reshape(N_CTX_SEQ, 2, ML_HEADS, ML_DH))
        new_m.append(mc_[:, :, 0].reshape(N_CTX_SEQ, 2, ML_HEADS))

        ys_c, fin = _s5(_to_time_major_ctx(u_s5[:T_CTX]), bb, cc_s5, lam, dsk, wglu, None, l)
        (ys_l,) = _s5(_to_time_major_lat(u_s5[T_CTX:]), bb, cc_s5, lam, dsk, wglu, s0, l)
        y_s5 = (_from_time_major_ctx(ys_c), _from_time_major_lat(ys_l))
        fin = fin.transpose(0, 2, 1, 3).reshape(N_CTX_SEQ, 2, 2 * S5_STATE)
        new_re.append(fin[..., :S5_STATE].reshape(N_CTX_SEQ, 2, S5_G, S5_P))
        new_im.append(fin[..., S5_STATE:].reshape(N_CTX_SEQ, 2, S5_G, S5_P))

        xn, h2e, best, rank, cnt = _outproj(x, y_hy, y_ml, y_s5, w_out, l, mod, g2, rw_t, rb)
        pos, src, tile_group, n_tiles = _dispatch(best, rank, cnt)
        ys = _moe(h2e, src, tile_group, n_tiles, moe_w_gate, moe_w_up, moe_w_down, l)
        res = _combine(pos, ys, xn, mod, fg, l, l == DEPTH - 1)
        if l == DEPTH - 1:
            y_prompt = res[0].reshape(N_CTX_SEQ, L_CTX, D)
            y_sample = res[1].reshape(N_LAT_SEQ, L_LAT, D)
        else:
            x = (res[0],)

    new_c = c_all.reshape(N_CTX_SEQ, DEPTH, 2, ML_HEADS, ML_DH, ML_DH)
    return (y_prompt, y_sample, new_c, jnp.stack(new_n, axis=1), jnp.stack(new_m, axis=1),
            jnp.stack(new_re, axis=1), jnp.stack(new_im, axis=1))
```

```python
import functools
import math

import numpy as np
import jax
import jax.numpy as jnp
from jax import lax
from jax.experimental import pallas as pl
from jax.experimental.pallas import tpu as pltpu

F32 = jnp.float32
BF16 = jnp.bfloat16

D = 1024
N_CTX_SEQ, L_CTX = 16, 256
N_LAT_SEQ, L_LAT = 2, 1024
T_CTX = N_CTX_SEQ * L_CTX
T_LAT = N_LAT_SEQ * L_LAT
T_ALL = T_CTX + T_LAT
DEPTH = 2
EPS = 1e-6
GRID_W = 64
POS_BASE = 10000.0
HY_CH = 256
HY_EMB = 33
HY_FILTER_W = 64
ML_HEADS = 4
ML_DH = 128
ML_W = ML_HEADS * ML_DH
S5_CH = 256
S5_G = 16
S5_GROUP = 16
S5_P = 64
S5_STATE = S5_G * S5_P
N_EXPERTS = 16
N_GROUPS = 4
GROUP_SIZE = N_EXPERTS // N_GROUPS
D_EXPERT = 512
OFF_HY = 0
OFF_QK = 3 * HY_CH
OFF_V = OFF_QK + 2 * ML_W
OFF_O = OFF_V + ML_W
OFF_G = OFF_O + ML_W
OFF_S5 = OFF_G + 16
IN_W = OFF_S5 + S5_CH
LANES = 128
SUBLANES = 8
GATE_PAD = LANES

TM = 512
TM_MOE = 512
MOE_SLOTS = T_ALL + N_GROUPS * TM_MOE
MOE_SUB = 512
HY_ROWS = 1024
CONV_ROWS = 128
ML_CHUNK = 256
S5_ROWS = 8
S5_STEPS = 256
VMEM_LIMIT = 56 * 1024 * 1024


def _cparams(sem, vmem=VMEM_LIMIT):
    if sem is None:
        return pltpu.CompilerParams(vmem_limit_bytes=vmem)
    return pltpu.CompilerParams(dimension_semantics=sem, vmem_limit_bytes=vmem)


def _bdot(a, b):
    return jnp.dot(a.astype(BF16), b.astype(BF16), preferred_element_type=F32)


def _split_bf16(x):
    hi = x.astype(BF16)
    return hi, (x - hi.astype(F32)).astype(BF16)


def _dot3(a, b):
    a_hi, a_lo = _split_bf16(a)
    b_hi, b_lo = _split_bf16(b)
    dot = functools.partial(jnp.dot, preferred_element_type=F32)
    return dot(a_hi, b_hi) + (dot(a_hi, b_lo) + dot(a_lo, b_hi))


def _silu(x):
    return x * jax.nn.sigmoid(x)


def _rms(x, g):
    return x * lax.rsqrt(jnp.mean(x * x, axis=-1, keepdims=True) + EPS) * g


def _log_sigmoid(x):
    return jnp.minimum(x, 0.0) - jnp.log1p(jnp.exp(-jnp.abs(x)))


def _conv3(u, w, n_rows, seq_len):
    row = lax.broadcasted_iota(jnp.int32, (n_rows, 1), 0) % seq_len
    prev = jnp.where(row == 0, 0.0, pltpu.roll(u, 1, 0))
    nxt = jnp.where(row == seq_len - 1, 0.0, pltpu.roll(u, n_rows - 1, 0))
    return prev * w[0:1] + u * w[1:2] + nxt * w[2:3]


def _ada_kernel(c_ref, w_ref, b_ref, o_ref):
    o_ref[0] = _bdot(_silu(c_ref[...]), w_ref[0]) + b_ref[0]


def _ada(cc, w_ada, b_ada):
    tn = 1536
    return pl.pallas_call(
        _ada_kernel,
        grid=(DEPTH, 6 * D // tn),
        in_specs=[pl.BlockSpec((8, D), lambda l, j: (0, 0)),
                  pl.BlockSpec((1, D, tn), lambda l, j: (l, 0, j)),
                  pl.BlockSpec((1, 1, tn), lambda l, j: (l, 0, j))],
        out_specs=pl.BlockSpec((1, 8, tn), lambda l, j: (l, 0, j)),
        out_shape=jax.ShapeDtypeStruct((DEPTH, 8, 6 * D), F32),
        compiler_params=_cparams(("arbitrary", "arbitrary")),
        name="ada_mod",
    )(cc, w_ada, b_ada.reshape(DEPTH, 1, 6 * D))


def _mod_row(i, tm):
    n_ctx = T_CTX // tm
    return jnp.where(i < n_ctx, 0, 1 + (i - n_ctx) // (L_LAT // tm))


_SEG = ((OFF_HY, OFF_QK - OFF_HY), (OFF_QK, OFF_V - OFF_QK), (OFF_V, OFF_O - OFF_V), (OFF_O, OFF_G - OFF_O))
TAIL_W = IN_W - OFF_G


N_CTX_TILES = T_CTX // TM
_NT = (((1,), (1,)), ((), ()))


def _x_specs(x):
    if len(x) == 1:
        return [pl.BlockSpec((TM, D), lambda i, *_: (i, 0))]
    per_seq = L_LAT // TM
    return [pl.BlockSpec((TM, D), lambda i, *_: (jnp.minimum(i, N_CTX_TILES - 1), 0)),
            pl.BlockSpec((TM, D), lambda i, *_: (jnp.maximum(i - N_CTX_TILES, 0), 0)),
            pl.BlockSpec((TM, D), lambda i, *_: (jnp.maximum(i - N_CTX_TILES, 0) % per_seq, 0))]


def _x_tile(step, x_refs):
    if len(x_refs) == 1:
        return x_refs[0][...]
    xc_ref, xl_ref, pos_ref = x_refs
    return jnp.where(step < N_CTX_TILES, xc_ref[...], xl_ref[...] + pos_ref[...])


def _inproj_kernel(*refs, n_x):
    x_refs = refs[:n_x]
    g_ref, mod_ref, w_ref, hy_ref, qk_ref, v_ref, o_ref, s5_ref, gt_ref, gtt_ref, wb = refs[n_x:]
    step = pl.program_id(0)

    @pl.when(step == 0)
    def _():
        wb[...] = w_ref[0].astype(BF16)

    mod = mod_ref[0, 0]
    h = _rms(_x_tile(step, x_refs), g_ref[0]) * (1.0 + mod[1:2]) + mod[0:1]
    hb = h.astype(BF16)
    for (a, w), ref in zip(_SEG, (hy_ref, qk_ref, v_ref, o_ref)):
        ref[...] = lax.dot_general(hb, wb[a:a + w, :], _NT, preferred_element_type=F32)
    tail = lax.dot_general(hb, wb[OFF_G:IN_W, :], _NT, preferred_element_type=F32)
    gates = tail[:, 0:GATE_PAD]
    gt_ref[...] = gates
    for c in range(TM // ML_CHUNK):
        gtt_ref[c] = gates[c * ML_CHUNK:(c + 1) * ML_CHUNK, :].T[0:16, :]
    s5_ref[...] = tail[:, OFF_S5 - OFF_G:TAIL_W]


def _layer_spec(shape, l):
    return pl.BlockSpec((1,) + tuple(shape), lambda *_: (l,) + (0,) * len(shape))


def _mod_spec(l, tm):
    return pl.BlockSpec((1, 1, 6, D), lambda i, *_: (l, _mod_row(i, tm), 0, 0))


def _inproj(x, g, mod, w_in_t, l):
    widths = [w for _, w in _SEG] + [S5_CH, GATE_PAD]
    cpt = TM // ML_CHUNK
    return pl.pallas_call(
        functools.partial(_inproj_kernel, n_x=len(x)),
        grid=(T_ALL // TM,),
        in_specs=_x_specs(x) + [
            _layer_spec((1, D), l), _mod_spec(l, TM),
            pl.BlockSpec((1, IN_W, D), lambda i: (l, 0, 0), pipeline_mode=pl.Buffered(1))],
        out_specs=[pl.BlockSpec((TM, w), lambda i: (i, 0)) for w in widths]
        + [pl.BlockSpec((cpt, 16, ML_CHUNK), lambda i: (i, 0, 0))],
        out_shape=[jax.ShapeDtypeStruct((T_ALL, w), F32) for w in widths]
        + [jax.ShapeDtypeStruct((T_ALL // ML_CHUNK, 16, ML_CHUNK), F32)],
        scratch_shapes=[pltpu.VMEM((IN_W, D), BF16)],
        compiler_params=_cparams(("arbitrary",)),
        name="norm_inproj",
    )(*x, g, mod, w_in_t)


@functools.lru_cache(None)
def _dft_mats(L):
    n = 2 * L
    k = np.arange(L)[:, None]
    t = np.arange(L)[None, :]
    ang = 2.0 * np.pi * ((k * t) % n) / n
    top = np.cos(ang)
    bot = -np.sin(ang)
    bot[0] = np.cos(np.pi * np.arange(L))
    fwd = np.concatenate([top, bot], 0)
    s = np.full((n, 1), 2.0 / n)
    s[0] = s[L] = 1.0 / n
    inv = (fwd * s).T
    return fwd.astype(np.float32), inv.astype(np.float32)


@functools.lru_cache(None)
def _hy_positions(L):
    t = np.linspace(0.0, 1.0, L)
    bands = (HY_EMB - 1) // 2
    f = np.linspace(1e-4, bands - 1, bands)
    w = 2.0 * np.pi * np.arange(L) / L
    ang = w[:, None] * f[None, :]
    z = np.concatenate([t[:, None], np.cos(ang), -np.sin(ang)], -1)
    zp = np.zeros((L, LANES))
    zp[:, :HY_EMB] = z
    return zp.astype(np.float32), t[:, None].astype(np.float32)


def _hy_filter_kernel(z_ref, t_ref, w1_ref, b1_ref, w2_ref, b2_ref, w3_ref, ld_ref, f_ref,
                      p_ref, q_ref, r_ref, *, L):
    h = jnp.sin(_dot3(z_ref[...], w1_ref[0]) + b1_ref[0])
    h = jnp.sin(_dot3(h, w2_ref[0]) + b2_ref[0])
    filt = _dot3(h, w3_ref[0])
    filt = filt * jnp.exp(-t_ref[...] * jnp.exp(ld_ref[0]))
    c = HY_CH
    h_fwd = jnp.concatenate([filt[:, 0:c], filt[:, 2 * c:3 * c]], axis=1)
    h_bwd = jnp.concatenate([filt[:, c:2 * c], filt[:, 3 * c:4 * c]], axis=1)
    row = lax.broadcasted_iota(jnp.int32, (L, 1), 0)
    h_bwd = jnp.where(row == 0, 0.0, h_bwd)
    a = jnp.dot(f_ref[...], h_fwd.astype(BF16), preferred_element_type=F32)
    b = jnp.dot(f_ref[...], h_bwd.astype(BF16), preferred_element_type=F32)
    re = a[:L] + b[:L]
    im = a[L:] - b[L:]
    nyq = a[L:L + 1] + b[L:L + 1]
    p_ref[0] = re
    q_ref[0] = jnp.where(row == 0, 0.0, im)
    r_ref[0] = jnp.where(row == 0, nyq, re)


def _hy_filter(L, w1p, b1, w2, b2, w3, ld):
    z, t = _hy_positions(L)
    fwd = jnp.asarray(_dft_mats(L)[0]).astype(BF16)
    out = jax.ShapeDtypeStruct((DEPTH, L, 2 * HY_CH), F32)
    full = lambda a: pl.BlockSpec(a.shape, lambda l: (0,) * a.ndim)
    layer = lambda a: pl.BlockSpec((1,) + a.shape[1:], lambda l: (l,) + (0,) * (a.ndim - 1))
    return pl.pallas_call(
        functools.partial(_hy_filter_kernel, L=L),
        grid=(DEPTH,),
        in_specs=[full(z), full(t), layer(w1p), layer(b1), layer(w2), layer(b2), layer(w3), layer(ld), full(fwd)],
        out_specs=[pl.BlockSpec((1, L, 2 * HY_CH), lambda l: (l, 0, 0))] * 3,
        out_shape=[out, out, out],
        compiler_params=_cparams(("arbitrary",)),
        name=f"hyena_filter_{L}",
    )(z, t, w1p, b1, w2, b2, w3, ld, fwd)


def _hyena_kernel(u_ref, sw_ref, bias_ref, p_ref, q_ref, r_ref, f_ref, g_ref, o_ref, *, L, n_sub):
    c = HY_CH
    u = _conv3(u_ref[...], sw_ref[0], n_sub * L, L)
    for s in range(n_sub):
        rs = slice(s * L, (s + 1) * L)
        z = u[rs, 0:c]
        for o in range(2):
            gate = u[rs, (o + 1) * c:(o + 2) * c]
            zf = jnp.dot(f_ref[...], z.astype(BF16), preferred_element_type=F32)
            a, b = zf[:L], zf[L:]
            p = p_ref[0, :, o * c:(o + 1) * c]
            q = q_ref[0, :, o * c:(o + 1) * c]
            r = r_ref[0, :, o * c:(o + 1) * c]
            y_re = (a * p - b * q).astype(BF16)
            y_im = (a * q + b * r).astype(BF16)
            y = (jnp.dot(g_ref[:, :L], y_re, preferred_element_type=F32)
                 + jnp.dot(g_ref[:, L:], y_im, preferred_element_type=F32))
            z = gate * (y + bias_ref[0, o:o + 1, :] * z)
        o_ref[rs, :] = z


def _hyena(u_hy, L, n_seq, row_block0, sw, bias, p, q, r, l):
    n_sub = max(1, HY_ROWS // L)
    fwd, inv = (jnp.asarray(m).astype(BF16) for m in _dft_mats(L))
    full = lambda a: pl.BlockSpec(a.shape, lambda b: (0,) * a.ndim)
    layer = lambda a: _layer_spec(a.shape[1:], l)
    blk0 = row_block0 // n_sub
    return pl.pallas_call(
        functools.partial(_hyena_kernel, L=L, n_sub=n_sub),
        grid=(n_seq // n_sub,),
        in_specs=[pl.BlockSpec((n_sub * L, 3 * HY_CH), lambda b: (blk0 + b, 0)),
                  layer(sw), layer(bias), layer(p), layer(q), layer(r), full(fwd), full(inv)],
        out_specs=pl.BlockSpec((n_sub * L, HY_CH), lambda b: (b, 0)),
        out_shape=jax.ShapeDtypeStruct((n_seq * L, HY_CH), F32),
        compiler_params=_cparams(("arbitrary",)),
        name=f"hyena_{L}",
    )(u_hy, sw, bias, p, q, r, fwd, inv)


def _cumsum_rows(x, n, reverse):
    row = lax.broadcasted_iota(jnp.int32, (n, 1), 0)
    s = 1
    while s < n:
        if reverse:
            x = x + jnp.where(row < n - s, pltpu.roll(x, n - s, 0), 0.0)
        else:
            x = x + jnp.where(row >= s, pltpu.roll(x, s, 0), 0.0)
        s *= 2
    return x


def _cumsum_lanes(x, n, reverse):
    col = lax.broadcasted_iota(jnp.int32, (1, n), 1)
    s = 1
    while s < n:
        if reverse:
            x = x + jnp.where(col < n - s, pltpu.roll(x, n - s, 1), 0.0)
        else:
            x = x + jnp.where(col >= s, pltpu.roll(x, s, 1), 0.0)
        s *= 2
    return x


def _mlstm_kernel(*refs, L, has_state, emit_state, n_prev):
    qk_ref, v_ref, o_ref, g_ref, gt_ref, sw_ref, gb_ref, gbt_ref, ng_ref = refs[:9]
    refs = refs[9:]
    if has_state:
        c0_ref, n0_ref, m0_ref = refs[:3]
        refs = refs[3:]
    prev_c_refs = refs[:n_prev]
    refs = refs[n_prev:]
    y_ref = refs[0]
    if emit_state:
        cout_ref, nout_ref, mout_ref = refs[1:4]
        refs = refs[4:]
    else:
        refs = refs[1:]
    q_s, k_s, vt_s, ht_s, ct_s, n_s, m_s, lf_s, lft_s = refs
    tc = ML_CHUNK
    nc = L // tc
    nh = ML_HEADS
    dh = ML_DH

    lf_s[...] = _log_sigmoid(g_ref[...] + gb_ref[0])
    lft_s[...] = _log_sigmoid(gt_ref[...] + gbt_ref[0])

    sw = sw_ref[0]
    for r0 in range(0, L, CONV_ROWS):
        a, b = max(r0 - SUBLANES, 0), min(r0 + CONV_ROWS + SUBLANES, L)
        n = b - a
        u = qk_ref[a:b, :]
        row = lax.broadcasted_iota(jnp.int32, (n, 1), 0) + a
        prev = jnp.where(row == 0, 0.0, pltpu.roll(u, 1, 0))
        nxt = jnp.where(row == L - 1, 0.0, pltpu.roll(u, n - 1, 0))
        y = _silu(prev * sw[0:1] + u * sw[1:2] + nxt * sw[2:3])[r0 - a:r0 - a + CONV_ROWS]
        q_s[r0:r0 + CONV_ROWS, :] = y[:, :ML_W].astype(BF16)
        k_s[r0:r0 + CONV_ROWS, :] = (y[:, ML_W:] * (dh ** -0.5)).astype(BF16)
    for c in range(nc):
        vt_s[c] = v_ref[c * tc:(c + 1) * tc, :].T.astype(BF16)

    for i in range(2 * nh):
        ct_s[i] = c0_ref[0, 0, i].T if has_state else jnp.zeros((dh, dh), F32)
    n_s[...] = n0_ref[0, 0] if has_state else jnp.zeros_like(n_s)
    m_s[...] = m0_ref[0, 0] if has_state else jnp.zeros_like(m_s)

    si = lax.broadcasted_iota(jnp.int32, (tc, tc), 0)
    ti = lax.broadcasted_iota(jnp.int32, (tc, tc), 1)

    for d in range(2):
        rev = d == 1
        mask = (si >= ti) if rev else (si <= ti)
        edge = 0 if rev else tc - 1

        def chunk(j, carry, d=d, rev=rev, mask=mask, edge=edge):
            cidx = (nc - 1 - j) if rev else j
            r0 = pl.multiple_of(cidx * tc, tc)
            pre = g_ref[pl.ds(r0, tc), :] + gb_ref[0]
            pre_t = gt_ref[cidx] + gbt_ref[0]
            cum = _cumsum_rows(lf_s[pl.ds(r0, tc), :], tc, rev)
            cum_t = _cumsum_lanes(lft_s[cidx], tc, rev)
            key_all = cum - pltpu.roll(pre, 8, 1)
            for h in range(nh):
                col = d * nh + h
                hs = slice(h * dh, (h + 1) * dh)
                key = key_all[:, 8 + col:9 + col]
                b_row = cum_t[8 + col:9 + col, :]
                b_end = b_row[:, edge:edge + 1]
                m_prev = m_s[col:col + 1, 0:1]
                dmat = jnp.where(mask, b_row - key, -jnp.inf)
                inter = b_row + m_prev
                m_row = jnp.maximum(inter, jnp.max(dmat, axis=0, keepdims=True))
                w_intra = jnp.exp(dmat - m_row)
                w_state = jnp.exp(inter - m_row)
                qh = q_s[pl.ds(r0, tc), hs]
                kh = k_s[pl.ds(r0, tc), hs]
                vt = vt_s[cidx, hs, :]
                ct_prev = ct_s[col]
                n_prev = n_s[col:col + 1, :]
                s = lax.dot_general(kh, qh, _NT, preferred_element_type=F32) * w_intra
                num = (jnp.dot(vt, s.astype(BF16), preferred_element_type=F32)
                       + w_state * lax.dot_general(ct_prev.astype(BF16), qh, _NT, preferred_element_type=F32))
                qn = lax.dot_general(jnp.broadcast_to(n_prev, (8, dh)).astype(BF16), qh, _NT,
                                     preferred_element_type=F32)[0:1]
                den = jnp.sum(s, axis=0, keepdims=True) + w_state * qn
                hout = num * (1.0 / jnp.maximum(jnp.abs(den), jnp.exp(-m_row)))
                if d == 0:
                    ht_s[cidx, hs, :] = hout
                else:
                    ht_s[cidx, hs, :] += hout
                m_new = jnp.maximum(b_end + m_prev, b_end - jnp.min(key, axis=0, keepdims=True))
                wg = jnp.exp(b_end - key - m_new)
                decay = jnp.exp(b_end + m_prev - m_new)
                kw = kh.astype(F32) * wg
                ct_s[col] = decay * ct_prev + jnp.dot(vt, kw.astype(BF16), preferred_element_type=F32)
                n_s[col:col + 1, :] = decay * n_prev + jnp.sum(kw, axis=0, keepdims=True)
                m_s[col:col + 1, :] = jnp.broadcast_to(m_new, (1, LANES))
            return carry

        lax.fori_loop(0, nc, chunk, 0)

    for c in range(nc):
        for h in range(nh):
            hs = slice(h * dh, (h + 1) * dh)
            rs = slice(c * tc, (c + 1) * tc)
            ht = ht_s[c, hs, :]
            hn = ht * lax.rsqrt(jnp.mean(ht * ht, axis=0, keepdims=True) + EPS)
            y_ref[rs, hs] = jax.nn.sigmoid(o_ref[rs, hs]) * (hn.T * ng_ref[0, :, hs])
    if emit_state:
        for k, prev_ref in enumerate(prev_c_refs):
            cout_ref[0, k] = prev_ref[0]
        for i in range(2 * nh):
            cout_ref[0, n_prev, i] = ct_s[i].T
        nout_ref[0] = n_s[...]
        mout_ref[0] = m_s[...]


def _mlstm(qk, v, o, gates, gates_t, L, n_seq, row_block0, sw, gb, gbt, ng, state, l, emit_state=False,
           prev_c=()):
    nc = L // ML_CHUNK
    has_state = state is not None
    layer = lambda a: _layer_spec(a.shape[1:], l)
    in_specs = [pl.BlockSpec((L, 2 * ML_W), lambda b: (row_block0 + b, 0)),
                pl.BlockSpec((L, ML_W), lambda b: (row_block0 + b, 0)),
                pl.BlockSpec((L, ML_W), lambda b: (row_block0 + b, 0)),
                pl.BlockSpec((L, GATE_PAD), lambda b: (row_block0 + b, 0)),
                pl.BlockSpec((nc, 16, ML_CHUNK), lambda b: (row_block0 + b, 0, 0)),
                layer(sw), layer(gb), layer(gbt), layer(ng)]
    args = [qk, v, o, gates, gates_t, sw, gb, gbt, ng]
    if has_state:
        c0, n0, m0 = state
        in_specs += [pl.BlockSpec((1, 1, 2 * ML_HEADS, ML_DH, ML_DH), lambda b: (b, l, 0, 0, 0)),
                     pl.BlockSpec((1, 1, 2 * ML_HEADS, ML_DH), lambda b: (b, l, 0, 0)),
                     pl.BlockSpec((1, 1, 2 * ML_HEADS, LANES), lambda b: (b, l, 0, 0))]
        args += [c0, n0, m0]
    out_specs = [pl.BlockSpec((L, ML_W), lambda b: (b, 0))]
    out_shape = [jax.ShapeDtypeStruct((n_seq * L, ML_W), F32)]
    if emit_state:
        c_tail = (2 * ML_HEADS, ML_DH, ML_DH)
        for a in prev_c:
            in_specs.append(pl.BlockSpec((1,) + c_tail, lambda b: (b, 0, 0, 0)))
            args.append(a)
        tails = ((len(prev_c) + 1,) + c_tail, (2 * ML_HEADS, ML_DH), (2 * ML_HEADS, LANES))
        for t in tails:
            out_specs.append(pl.BlockSpec((1,) + t, lambda b, n=len(t): (b,) + (0,) * n))
            out_shape.append(jax.ShapeDtypeStruct((n_seq,) + t, F32))
    return pl.pallas_call(
        functools.partial(_mlstm_kernel, L=L, has_state=has_state, emit_state=emit_state,
                          n_prev=len(prev_c) if emit_state else 0),
        grid=(n_seq,),
        in_specs=in_specs,
        out_specs=out_specs,
        out_shape=out_shape,
        scratch_shapes=[pltpu.VMEM((L, ML_W), BF16), pltpu.VMEM((L, ML_W), BF16),
                        pltpu.VMEM((nc, ML_W, ML_CHUNK), BF16),
                        pltpu.VMEM((nc, ML_W, ML_CHUNK), F32),
                        pltpu.VMEM((2 * ML_HEADS, ML_DH, ML_DH), F32),
                        pltpu.VMEM((2 * ML_HEADS, ML_DH), F32),
                        pltpu.VMEM((2 * ML_HEADS, LANES), F32),
                        pltpu.VMEM((L, GATE_PAD), F32),
                        pltpu.VMEM((nc, 16, ML_CHUNK), F32)],
        compiler_params=_cparams(("arbitrary",)),
        name=f"mlstm_{L}",
    )(*args)


def _cmul(ar, ai, br, bi):
    return ar * br - ai * bi, ar * bi + ai * br


def _s5_kernel(*refs, segmented):
    if segmented:
        (u_ref, bb_ref, cc_ref, eb_ref, ec_ref, lam_ref, dsk_ref, wglu_ref, s0_ref, y_ref,
         sbuf, yacc, bmat, cmat, pw) = refs
    else:
        (u_ref, bb_ref, cc_ref, eb_ref, ec_ref, lam_ref, dsk_ref, wglu_ref, y_ref, fin_ref,
         sbuf, yacc, bmat, cmat) = refs
    n = S5_STATE
    rows = S5_ROWS
    steps = S5_STEPS
    blk = 256
    n_blk = steps * rows // blk
    n_seg = 4

    yacc[...] = u_ref[0] * dsk_ref[0]
    ub = u_ref[0].astype(BF16)
    b_keep = (lax.broadcasted_iota(jnp.int32, (S5_CH, 2 * n), 0) // S5_GROUP
              == (lax.broadcasted_iota(jnp.int32, (S5_CH, 2 * n), 1) % n) // S5_P)
    c_keep = ((lax.broadcasted_iota(jnp.int32, (2 * n, S5_CH), 0) % n) // S5_P
              == lax.broadcasted_iota(jnp.int32, (2 * n, S5_CH), 1) // S5_GROUP)

    for d in range(2):
        rev = d == 1
        bmat[...] = jnp.where(b_keep, jnp.dot(bb_ref[0, d].astype(BF16), eb_ref[...], preferred_element_type=F32),
                              0.0).astype(BF16)
        cmat[...] = jnp.where(c_keep, jnp.dot(cc_ref[0, d].astype(BF16), ec_ref[...], preferred_element_type=F32),
                              0.0).astype(BF16)
        for i in range(n_blk):
            sbuf[i * blk:(i + 1) * blk, :] = jnp.dot(ub[i * blk:(i + 1) * blk], bmat[...],
                                                     preferred_element_type=F32)
        lam = lam_ref[0, d]
        lr = jnp.broadcast_to(lam[:, :n], (rows, n))
        li = jnp.broadcast_to(lam[:, n:], (rows, n))

        def step(i, carry, rev=rev, lr=lr, li=li):
            sr, si = carry
            t = (steps - 1 - i) if rev else i
            off = pl.multiple_of(t * rows, rows)
            pr, pi = _cmul(lr, li, sr, si)
            nr = pr + sbuf[pl.ds(off, rows), 0:n]
            ni = pi + sbuf[pl.ds(off, rows), n:2 * n]
            sbuf[pl.ds(off, rows), 0:n] = nr
            sbuf[pl.ds(off, rows), n:2 * n] = ni
            return nr, ni

        zero = jnp.zeros((rows, n), F32)
        sr, si = lax.fori_loop(0, steps, step, (zero, zero), unroll=2)

        if not segmented:
            fin_ref[0, d, :, 0:n] = sr
            fin_ref[0, d, :, n:2 * n] = si
        else:
            lam_r, lam_i = lam[:, :n], lam[:, n:]
            row8 = lax.broadcasted_iota(jnp.int32, (rows, 1), 0)
            cr, ci = lam_r, lam_i
            acc_r = jnp.broadcast_to(cr, (rows, n))
            acc_i = jnp.broadcast_to(ci, (rows, n))
            for j in range(1, rows):
                cr, ci = _cmul(cr, ci, lam_r, lam_i)
                acc_r = jnp.where(row8 >= j, jnp.broadcast_to(cr, (rows, n)), acc_r)
                acc_i = jnp.where(row8 >= j, jnp.broadcast_to(ci, (rows, n)), acc_i)
            pw[0:rows, 0:n] = acc_r
            pw[0:rows, n:2 * n] = acc_i
            size = rows
            while size < steps:
                tr = pw[size - 1:size, 0:n]
                ti = pw[size - 1:size, n:2 * n]
                xr, xi = _cmul(pw[0:size, 0:n], pw[0:size, n:2 * n], tr, ti)
                pw[size:2 * size, 0:n] = xr
                pw[size:2 * size, n:2 * n] = xi
                size *= 2
            end_off = 0 if rev else (steps - 1) * rows
            loc_r = sbuf[end_off:end_off + rows, 0:n]
            loc_i = sbuf[end_off:end_off + rows, n:2 * n]
            pl_r = pw[steps - 1:steps, 0:n]
            pl_i = pw[steps - 1:steps, n:2 * n]
            s0r = s0_ref[0, d, :, 0:n]
            s0i = s0_ref[0, d, :, n:2 * n]
            seg = row8 // 2
            first = (seg == n_seg - 1) if rev else (seg == 0)
            shift = (rows - 2) if rev else 2
            cin_r, cin_i = s0r, s0i
            for _ in range(n_seg - 1):
                fr, fi = _cmul(jnp.broadcast_to(pl_r, (rows, n)), jnp.broadcast_to(pl_i, (rows, n)), cin_r, cin_i)
                tru_r = loc_r + fr
                tru_i = loc_i + fi
                cin_r = jnp.where(first, s0r, pltpu.roll(tru_r, shift, 0))
                cin_i = jnp.where(first, s0i, pltpu.roll(tru_i, shift, 0))

            def fix(tb, carry, rev=rev, cin_r=cin_r, cin_i=cin_i):
                pb = (steps // rows - 1 - tb) if rev else tb
                poff = pl.multiple_of(pb * rows, rows)
                p_r = pw[pl.ds(poff, rows), 0:n]
                p_i = pw[pl.ds(poff, rows), n:2 * n]
                for j in range(rows):
                    jj = rows - 1 - j if rev else j
                    off = pl.multiple_of((tb * rows + j) * rows, rows)
                    fr, fi = _cmul(jnp.broadcast_to(p_r[jj:jj + 1], (rows, n)),
                                   jnp.broadcast_to(p_i[jj:jj + 1], (rows, n)), cin_r, cin_i)
                    sbuf[pl.ds(off, rows), 0:n] += fr
                    sbuf[pl.ds(off, rows), n:2 * n] += fi
                return carry

            lax.fori_loop(0, steps // rows, fix, 0)

        for i in range(n_blk):
            yacc[i * blk:(i + 1) * blk, :] += jnp.dot(sbuf[i * blk:(i + 1) * blk, :].astype(BF16), cmat[...],
                                                      preferred_element_type=F32)

    g = jax.nn.gelu(yacc[...], approximate=True)
    y_ref[0] = g * jax.nn.sigmoid(_bdot(g, wglu_ref[0]))


@functools.lru_cache(None)
def _s5_spread():
    eb = np.zeros((2 * S5_P, 2 * S5_STATE), np.float32)
    for half in range(2):
        for g in range(S5_G):
            c0 = half * S5_STATE + g * S5_P
            eb[half * S5_P:(half + 1) * S5_P, c0:c0 + S5_P] = np.eye(S5_P)
    ec = np.zeros((LANES, S5_CH), np.float32)
    for g in range(S5_G):
        ec[:S5_GROUP, g * S5_GROUP:(g + 1) * S5_GROUP] = np.eye(S5_GROUP)
    return eb, ec


def _s5(u_tm, bb, cc, lam, dskip, wglu, s0, l):
    n_grp = u_tm.shape[0]
    n_rows = S5_STEPS * S5_ROWS
    segmented = s0 is not None
    eb, ec = (jnp.asarray(m).astype(BF16) for m in _s5_spread())
    full = lambda a: pl.BlockSpec(a.shape, lambda g: (0,) * a.ndim)
    layer = lambda a: _layer_spec(a.shape[1:], l)
    in_specs = [pl.BlockSpec((1, n_rows, S5_CH), lambda g: (g, 0, 0)),
                layer(bb), layer(cc), full(eb), full(ec), layer(lam), layer(dskip), layer(wglu)]
    args = [u_tm, bb, cc, eb, ec, lam, dskip, wglu]
    out_specs = [pl.BlockSpec((1, n_rows, S5_CH), lambda g: (g, 0, 0))]
    out_shape = [jax.ShapeDtypeStruct((n_grp, n_rows, S5_CH), F32)]
    scratch = [pltpu.VMEM((n_rows, 2 * S5_STATE), F32), pltpu.VMEM((n_rows, S5_CH), F32),
               pltpu.VMEM((S5_CH, 2 * S5_STATE), BF16), pltpu.VMEM((2 * S5_STATE, S5_CH), BF16)]
    if segmented:
        in_specs.append(layer(s0))
        args.append(s0)
        scratch.append(pltpu.VMEM((S5_STEPS, 2 * S5_STATE), F32))
    else:
        out_specs.append(pl.BlockSpec((1, 2, S5_ROWS, 2 * S5_STATE), lambda g: (g, 0, 0, 0)))
        out_shape.append(jax.ShapeDtypeStruct((n_grp, 2, S5_ROWS, 2 * S5_STATE), F32))
    return pl.pallas_call(
        functools.partial(_s5_kernel, segmented=segmented),
        grid=(n_grp,),
        in_specs=in_specs,
        out_specs=out_specs,
        out_shape=out_shape,
        scratch_shapes=scratch,
        compiler_params=_cparams(("arbitrary",)),
        name="s5_seg" if segmented else "s5_ctx",
    )(*args)


def _s5_params(a_re, a_im, log_dt, b_re, b_im, c_re, c_im):
    dt = jnp.exp(log_dt)[..., None]
    mag = jnp.exp(a_re * dt)
    lb_re = mag * jnp.cos(a_im * dt)
    lb_im = mag * jnp.sin(a_im * dt)
    den = a_re * a_re + a_im * a_im
    nr, ni = lb_re - 1.0, lb_im
    k_re = (nr * a_re + ni * a_im) / den
    k_im = (ni * a_re - nr * a_im) / den
    bb_re = k_re[..., None] * b_re - k_im[..., None] * b_im
    bb_im = k_re[..., None] * b_im + k_im[..., None] * b_re
    to_gc_p = lambda m: m.transpose(0, 1, 2, 4, 3).reshape(DEPTH, 2, S5_CH, S5_P)
    bb = jnp.concatenate([to_gc_p(bb_re), to_gc_p(bb_im)], axis=3)
    to_gp_c = lambda m: m.transpose(0, 1, 2, 4, 3).reshape(DEPTH, 2, S5_STATE, S5_GROUP)
    cc = jnp.concatenate([to_gp_c(c_re), -to_gp_c(c_im)], axis=2)
    cc = jnp.pad(cc, ((0, 0), (0, 0), (0, 0), (0, LANES - S5_GROUP)))
    lam = jnp.concatenate([lb_re.reshape(DEPTH, 2, 1, S5_STATE), lb_im.reshape(DEPTH, 2, 1, S5_STATE)], axis=3)
    return bb, cc, lam


def _outproj_kernel(*refs, n_x):
    x_refs = refs[:n_x]
    (hyc_ref, hyl_ref, mlc_ref, mll_ref, s5c_ref, s5l_ref, w_ref, mod_ref, g_ref,
     rw_ref, rb_ref, xn_ref, h2e_ref, best_ref, rank_ref, cnt_ref, wb, cnt_s) = refs[n_x:]
    step = pl.program_id(0)

    @pl.when(step == 0)
    def _():
        wb[...] = w_ref[0].astype(BF16)
        cnt_s[...] = jnp.zeros_like(cnt_s)

    is_ctx = step < T_CTX // TM
    pick = lambda c_ref, l_ref: jnp.where(is_ctx, c_ref[...], l_ref[...]).astype(BF16)
    mod = mod_ref[0, 0]
    a, b = HY_CH, HY_CH + ML_W
    mix = (jnp.dot(pick(hyc_ref, hyl_ref), wb[0:a, :], preferred_element_type=F32)
           + jnp.dot(pick(mlc_ref, mll_ref), wb[a:b, :], preferred_element_type=F32)
           + jnp.dot(pick(s5c_ref, s5l_ref), wb[b:, :], preferred_element_type=F32))
    xn = _x_tile(step, x_refs) + mod[2:3] * mix
    xn_ref[...] = xn
    h2 = _rms(xn, g_ref[0]) * (1.0 + mod[4:5]) + mod[3:4]
    h2e_ref[:, 0:D] = h2
    h_hi, h_lo = _split_bf16(h2)
    r_hi, r_lo = _split_bf16(rw_ref[...])
    logits = (lax.dot_general(r_hi, h_hi, _NT, preferred_element_type=F32)
              + (lax.dot_general(r_hi, h_lo, _NT, preferred_element_type=F32)
                 + lax.dot_general(r_lo, h_hi, _NT, preferred_element_type=F32)))
    ex = jnp.exp(logits - jnp.max(logits, axis=0, keepdims=True))
    probs = ex / jnp.sum(ex, axis=0, keepdims=True)
    sel = probs + rb_ref[...]
    best = None
    best_score = None
    for g in range(N_GROUPS):
        r = [sel[g * GROUP_SIZE + i:g * GROUP_SIZE + i + 1, :] for i in range(GROUP_SIZE)]
        score = None
        for i in range(GROUP_SIZE):
            for j in range(i + 1, GROUP_SIZE):
                pair = r[i] + r[j]
                score = pair if score is None else jnp.maximum(score, pair)
        if g == 0:
            best, best_score = jnp.zeros_like(score, dtype=jnp.int32), score
        else:
            upd = score > best_score
            best = jnp.where(upd, g, best)
            best_score = jnp.where(upd, score, best_score)
    eid = lax.broadcasted_iota(jnp.int32, (N_EXPERTS, 1), 0)
    masked = jnp.where(eid // GROUP_SIZE == best, sel, -jnp.inf)
    m1 = jnp.max(masked, axis=0, keepdims=True)
    i1 = jnp.min(jnp.where(masked == m1, eid, N_EXPERTS), axis=0, keepdims=True)
    masked2 = jnp.where(eid == i1, -jnp.inf, masked)
    m2 = jnp.max(masked2, axis=0, keepdims=True)
    i2 = jnp.min(jnp.where(masked2 == m2, eid, N_EXPERTS), axis=0, keepdims=True)
    p1 = jnp.sum(jnp.where(eid == i1, probs, 0.0), axis=0, keepdims=True)
    p2 = jnp.sum(jnp.where(eid == i2, probs, 0.0), axis=0, keepdims=True)
    tot = p1 + p2
    comb = jnp.where(eid == i1, p1 / tot, 0.0) + jnp.where(eid == i2, p2 / tot, 0.0)
    comb = jnp.concatenate([comb, jnp.zeros((LANES - N_EXPERTS, comb.shape[1]), F32)], axis=0)
    h2e_ref[:, D:] = comb.T
    gid = lax.broadcasted_iota(jnp.int32, (SUBLANES, 1), 0)
    onehot = (gid == best).astype(F32)
    cum = _cumsum_lanes(onehot, TM, False)
    run = cnt_s[:, 0:1]
    best_ref[...] = best
    rank_ref[...] = jnp.sum(onehot * (cum - onehot + run), axis=0, keepdims=True).astype(jnp.int32)
    cnt_s[...] = jnp.broadcast_to(run + cum[:, TM - 1:TM], cnt_s.shape)
    cnt_ref[...] = cnt_s[...].astype(jnp.int32)


def _outproj(x, y_hy, y_ml, y_s5, w_out, l, mod, g2, rw_t, rb):
    full = lambda a: pl.BlockSpec(a.shape, lambda i: (0,) * a.ndim)
    n_ctx = T_CTX // TM
    ctx = lambda w: pl.BlockSpec((TM, w), lambda i: (jnp.minimum(i, n_ctx - 1), 0))
    lat = lambda w: pl.BlockSpec((TM, w), lambda i: (jnp.maximum(i - n_ctx, 0), 0))
    tok = lambda w: pl.BlockSpec((TM, w), lambda i: (i, 0))
    row = pl.BlockSpec((1, TM), lambda i: (0, i))
    return pl.pallas_call(
        functools.partial(_outproj_kernel, n_x=len(x)),
        grid=(T_ALL // TM,),
        in_specs=_x_specs(x) + [ctx(HY_CH), lat(HY_CH), ctx(ML_W), lat(ML_W), ctx(S5_CH), lat(S5_CH),
                  pl.BlockSpec((1, D, D), lambda i: (l, 0, 0), pipeline_mode=pl.Buffered(1)),
                  _mod_spec(l, TM), _layer_spec((1, D), l), full(rw_t), full(rb)],
        out_specs=[tok(D), tok(D + LANES), row, row, pl.BlockSpec((SUBLANES, LANES), lambda i: (0, 0))],
        out_shape=[jax.ShapeDtypeStruct((T_ALL, D), F32),
                   jax.ShapeDtypeStruct((T_ALL, D + LANES), F32),
                   jax.ShapeDtypeStruct((1, T_ALL), jnp.int32),
                   jax.ShapeDtypeStruct((1, T_ALL), jnp.int32),
                   jax.ShapeDtypeStruct((SUBLANES, LANES), jnp.int32)],
        scratch_shapes=[pltpu.VMEM((D, D), BF16), pltpu.VMEM((SUBLANES, LANES), F32)],
        compiler_params=_cparams(("arbitrary",)),
        name="outproj_router",
    )(*x, *y_hy, *y_ml, *y_s5, w_out, mod, g2, rw_t, rb)


def _gather_rows(idx_ref, idx_base, src_ref, dst_ref, n_rows):
    def body(r8, carry):
        base = pl.multiple_of(r8 * SUBLANES, SUBLANES)
        for k in range(SUBLANES):
            idx = idx_ref[idx_base + base + k]
            dst_ref[pl.ds(base + k, 1), :] = src_ref[pl.ds(idx, 1), :]
        return carry

    lax.fori_loop(0, n_rows // SUBLANES, body, 0)


def _dispatch_kernel(best_ref, rank_ref, cnt_ref, pos_ref, src_ref, tg_ref, nt_ref, off_s):
    tm = TM_MOE
    tiles = jnp.int32(0)
    tile_end = []
    for g in range(N_GROUPS):
        off_s[g] = tiles * tm
        tiles = tiles + (cnt_ref[g, 0] + (tm - 1)) // tm
        tile_end.append(tiles)
    nt_ref[0] = tiles
    for k in range(MOE_SLOTS // tm):
        g = jnp.int32(0)
        for e in tile_end[:-1]:
            g = g + (k >= e).astype(jnp.int32)
        tg_ref[k] = g

    unroll = SUBLANES

    def clear(i8, carry):
        for u in range(unroll):
            src_ref[i8 * unroll + u] = 0
        return carry

    lax.fori_loop(0, MOE_SLOTS // unroll, clear, 0)

    def place(t8, carry):
        for u in range(unroll):
            t = t8 * unroll + u
            p = rank_ref[t] + off_s[best_ref[t]]
            pos_ref[t] = p
            src_ref[p] = t
        return carry

    lax.fori_loop(0, T_ALL // unroll, place, 0)


def _dispatch(best, rank, cnt):
    smem = pl.BlockSpec(memory_space=pltpu.SMEM)
    i32 = lambda n: jax.ShapeDtypeStruct((n,), jnp.int32)
    return pl.pallas_call(
        _dispatch_kernel,
        in_specs=[smem, smem, smem],
        out_specs=[smem, smem, smem, smem],
        out_shape=[i32(T_ALL), i32(MOE_SLOTS), i32(MOE_SLOTS // TM_MOE), i32(1)],
        scratch_shapes=[pltpu.SMEM((N_GROUPS,), jnp.int32)],
        name="moe_dispatch",
    )(best.reshape(T_ALL), rank.reshape(T_ALL), cnt)


def _moe_kernel(src_ref, tg_ref, nt_ref, h_ref, wg_ref, wu_ref, wd_ref, ys_ref, g_a, x_a, g_b, x_b, acc):
    i = pl.program_id(0)
    j = pl.program_id(1)
    tm = TM_MOE
    quarter = tm // GROUP_SIZE
    n_live = nt_ref[0]

    @pl.when(jnp.logical_and(i == 0, j == 0))
    def _():
        _gather_rows(src_ref, 0, h_ref, g_a, tm)
        x_a[...] = g_a[:, 0:D].astype(BF16)

    def step(cur_g, cur_x, nxt_g, nxt_x):
        @pl.when(j == 0)
        def _():
            acc[...] = jnp.zeros_like(acc)

        e = tg_ref[i] * GROUP_SIZE + j
        lane = lax.broadcasted_iota(jnp.int32, (1, LANES), 1)
        wg = wg_ref[0, 0].astype(BF16)
        wu = wu_ref[0, 0].astype(BF16)
        wd = wd_ref[0, 0].astype(BF16)
        sub = MOE_SUB
        for s in range(tm // sub):
            rs = slice(s * sub, (s + 1) * sub)
            hb = cur_x[rs, :]
            ce = jnp.sum(jnp.where(lane == e, cur_g[rs, D:], 0.0), axis=1, keepdims=True)
            hid = _silu(jnp.dot(hb, wg, preferred_element_type=F32)) * jnp.dot(hb, wu, preferred_element_type=F32)
            acc[rs, :] += jnp.dot((hid * ce).astype(BF16), wd, preferred_element_type=F32)

        r0 = pl.multiple_of(j * quarter, quarter)
        base = jnp.minimum(i + 1, n_live - 1) * tm + r0
        for k in range(quarter):
            nxt_g[pl.ds(r0 + k, 1), :] = h_ref[pl.ds(src_ref[base + k], 1), :]
        nxt_x[pl.ds(r0, quarter), :] = nxt_g[pl.ds(r0, quarter), 0:D].astype(BF16)

        @pl.when(j == GROUP_SIZE - 1)
        def _():
            for c in range(D // LANES):
                ys_ref[pl.ds(c, tm, stride=D // LANES), :] = acc[:, c * LANES:(c + 1) * LANES]

    live = i < n_live

    @pl.when(jnp.logical_and(live, i % 2 == 0))
    def _():
        step(g_a, x_a, g_b, x_b)

    @pl.when(jnp.logical_and(live, i % 2 == 1))
    def _():
        step(g_b, x_b, g_a, x_a)

    @pl.when(jnp.logical_and(jnp.logical_not(live), j == GROUP_SIZE - 1))
    def _():
        ys_ref[...] = jnp.zeros_like(ys_ref)


def _moe(h2e, src, tile_group, n_tiles, wg, wu, wd, l):
    tm = TM_MOE

    def w_map(i, j, src, tg, nt):
        live = i < nt[0]
        ii = jnp.minimum(i, nt[0] - 1)
        return (l, tg[ii] * GROUP_SIZE + jnp.where(live, j, GROUP_SIZE - 1), 0, 0)

    return pl.pallas_call(
        _moe_kernel,
        grid_spec=pltpu.PrefetchScalarGridSpec(
            num_scalar_prefetch=3,
            grid=(MOE_SLOTS // tm, GROUP_SIZE),
            in_specs=[pl.BlockSpec(memory_space=pltpu.VMEM),
                      pl.BlockSpec((1, 1, D, D_EXPERT), w_map),
                      pl.BlockSpec((1, 1, D, D_EXPERT), w_map),
                      pl.BlockSpec((1, 1, D_EXPERT, D), w_map)],
            out_specs=pl.BlockSpec((tm * (D // LANES), LANES), lambda i, j, src, tg, nt: (i, 0)),
            scratch_shapes=[pltpu.VMEM((tm, D + LANES), F32), pltpu.VMEM((tm, D), BF16),
                            pltpu.VMEM((tm, D + LANES), F32), pltpu.VMEM((tm, D), BF16),
                            pltpu.VMEM((tm, D), F32)]),
        out_shape=jax.ShapeDtypeStruct((MOE_SLOTS * (D // LANES), LANES), F32),
        compiler_params=_cparams(("arbitrary", "arbitrary")),
        name="moe_experts",
    )(src, tile_group, n_tiles, h2e, wg, wu, wd)


def _combine_kernel(pos_ref, ys_ref, xn_ref, mod_ref, fg_ref, *rest, final):
    step = pl.program_id(0)
    if final:
        yc_ref, yl_ref, gbuf = rest
    else:
        out_ref, gbuf = rest
    n_lt = D // LANES

    def body(r8, carry):
        base = pl.multiple_of(r8 * SUBLANES, SUBLANES)
        for k in range(SUBLANES):
            src = pl.multiple_of(pos_ref[step * TM + base + k] * n_lt, n_lt)
            gbuf[pl.ds(pl.multiple_of((base + k) * n_lt, n_lt), n_lt), :] = ys_ref[pl.ds(src, n_lt), :]
        return carry

    lax.fori_loop(0, TM // SUBLANES, body, 0)
    moe = jnp.concatenate([gbuf[pl.ds(c, TM, stride=n_lt), :] for c in range(n_lt)], axis=1)
    out = xn_ref[...] + mod_ref[0, 0][5:6] * moe
    if final:
        y = _rms(out, fg_ref[...])

        @pl.when(step < T_CTX // TM)
        def _():
            yc_ref[...] = y

        @pl.when(step >= T_CTX // TM)
        def _():
            yl_ref[...] = y
    else:
        out_ref[...] = out


def _combine(pos, ys, xn, mod, fg, l, final):
    spec = pl.BlockSpec((TM, D), lambda i, pos: (i, 0))
    n_ctx = T_CTX // TM
    if final:
        out_specs = [pl.BlockSpec((TM, D), lambda i, pos: (jnp.minimum(i, n_ctx - 1), 0)),
                     pl.BlockSpec((TM, D), lambda i, pos: (jnp.maximum(i - n_ctx, 0), 0))]
        out_shape = [jax.ShapeDtypeStruct((T_CTX, D), F32), jax.ShapeDtypeStruct((T_LAT, D), F32)]
    else:
        out_specs = [spec]
        out_shape = [jax.ShapeDtypeStruct((T_ALL, D), F32)]
    return pl.pallas_call(
        functools.partial(_combine_kernel, final=final),
        grid_spec=pltpu.PrefetchScalarGridSpec(
            num_scalar_prefetch=1,
            grid=(T_ALL // TM,),
            in_specs=[pl.BlockSpec(memory_space=pltpu.VMEM),
                      spec,
                      _mod_spec(l, TM),
                      pl.BlockSpec((1, D), lambda i, pos: (0, 0))],
            out_specs=out_specs,
            scratch_shapes=[pltpu.VMEM((TM * (D // LANES), LANES), F32)]),
        out_shape=out_shape,
        compiler_params=_cparams(("arbitrary",)),
        name="moe_combine",
    )(pos, ys, xn, mod, fg)


@functools.lru_cache(None)
def _pos_embed():
    rows = L_LAT // GRID_W
    r = np.repeat(np.arange(rows, dtype=np.float64), GRID_W)
    col = np.tile(np.arange(GRID_W, dtype=np.float64), rows)
    quarter = D // 4
    freq = np.exp(-math.log(POS_BASE) * np.arange(quarter, dtype=np.float64) / quarter)
    ar = r[:, None] * freq[None]
    ac = col[:, None] * freq[None]
    emb = np.concatenate([np.sin(ar), np.cos(ar), np.sin(ac), np.cos(ac)], axis=-1)
    return emb.astype(np.float32)


def _to_time_major_ctx(a):
    c = a.shape[-1]
    a = a.reshape(2, S5_ROWS, L_CTX, c).transpose(0, 2, 1, 3)
    return a.reshape(2, L_CTX * S5_ROWS, c)


def _from_time_major_ctx(a):
    c = a.shape[-1]
    a = a.reshape(2, L_CTX, S5_ROWS, c).transpose(0, 2, 1, 3)
    return a.reshape(T_CTX, c)


def _to_time_major_lat(a):
    c = a.shape[-1]
    a = a.reshape(N_LAT_SEQ, 4, S5_STEPS, c).transpose(2, 1, 0, 3)
    return a.reshape(1, S5_STEPS * S5_ROWS, c)


def _from_time_major_lat(a):
    c = a.shape[-1]
    a = a.reshape(S5_STEPS, 4, N_LAT_SEQ, c).transpose(2, 1, 0, 3)
    return a.reshape(T_LAT, c)


def kernel(x_prompt, x_sample, c, state_mlstm_C, state_mlstm_n, state_mlstm_m, state_s5_re, state_s5_im, c_ctx, w_ada, b_ada, norm1_g, norm2_g, final_g, w_in, w_out, hy_short, hy_fw1, hy_fb1, hy_fw2, hy_fb2, hy_fw3, hy_log_decay, hy_bias, ml_short, ml_gate_bias, ml_norm_g, s5_a_re, s5_a_im, s5_log_dt, s5_b_re, s5_b_im, s5_c_re, s5_c_im, s5_d, s5_w_glu, router_w, router_b, moe_w_gate, moe_w_up, moe_w_down):
    x = (x_prompt.reshape(T_CTX, D), x_sample.reshape(T_LAT, D), jnp.asarray(_pos_embed()))
    w_in_t = jnp.swapaxes(w_in, 1, 2)
    cc =jnp.concatenate([c_ctx[None], c, jnp.zeros((8 - 1 - N_LAT_SEQ, D), F32)], axis=0)
    mod = _ada(cc, w_ada, b_ada).reshape(DEPTH, 8, 6, D)
    rw_t = router_w.T
    rb = router_b.reshape(N_EXPERTS, 1)
    fg = final_g.reshape(1, D)
    lat_blk = T_CTX // L_LAT
    g1 = norm1_g.reshape(DEPTH, 1, D)
    g2 = norm2_g.reshape(DEPTH, 1, D)

    w1p = jnp.pad(hy_fw1, ((0, 0), (0, LANES - HY_EMB), (0, 0)))
    b1 = hy_fb1.reshape(DEPTH, 1, HY_FILTER_W)
    b2 = hy_fb2.reshape(DEPTH, 1, HY_FILTER_W)
    ld = hy_log_decay.reshape(DEPTH, 1, 4 * HY_CH)
    hy_spec = {L: _hy_filter(L, w1p, b1, hy_fw2, b2, hy_fw3, ld) for L in (L_CTX, L_LAT)}
    gb = jnp.pad(ml_gate_bias.reshape(DEPTH, 1, 16), ((0, 0), (0, 0), (0, GATE_PAD - 16)))
    gbt = ml_gate_bias.reshape(DEPTH, 16, 1)
    ng = ml_norm_g.reshape(DEPTH, 1, ML_W)
    ml_state = (state_mlstm_C.reshape(N_LAT_SEQ, DEPTH, 2 * ML_HEADS, ML_DH, ML_DH),
                state_mlstm_n.reshape(N_LAT_SEQ, DEPTH, 2 * ML_HEADS, ML_DH),
                jnp.broadcast_to(state_mlstm_m.reshape(N_LAT_SEQ, DEPTH, 2 * ML_HEADS, 1),
                                 (N_LAT_SEQ, DEPTH, 2 * ML_HEADS, LANES)))
    bb, cc_s5, lam = _s5_params(s5_a_re, s5_a_im, s5_log_dt, s5_b_re, s5_b_im, s5_c_re, s5_c_im)
    dsk = s5_d.reshape(DEPTH, 1, S5_CH)
    wglu = s5_w_glu.astype(BF16)
    s0 = jnp.concatenate([state_s5_re.reshape(N_LAT_SEQ, DEPTH, 2, S5_STATE),
                          state_s5_im.reshape(N_LAT_SEQ, DEPTH, 2, S5_STATE)], axis=-1)
    s0 = jnp.tile(s0.transpose(1, 2, 0, 3), (1, 1, 4, 1))

    new_n, new_m, new_re, new_im = [], [], [], []
    prev_c, c_all = [], None
    y_prompt = y_sample = None
    for l in range(DEPTH):
        u_hy, qk, v, o, u_s5, gates, gates_t = _inproj(x, g1, mod, w_in_t, l)

        y_hy = [_hyena(u_hy, L, n_seq, blk0, hy_short, hy_bias, *hy_spec[L], l)
                for L, n_seq, blk0 in ((L_CTX, N_CTX_SEQ, 0), (L_LAT, N_LAT_SEQ, lat_blk))]

        yc, c_all, nc_, mc_ = _mlstm(qk, v, o, gates, gates_t, L_CTX, N_CTX_SEQ, 0, ml_short, gb, gbt, ng, None, l,
                                     emit_state=True, prev_c=prev_c)
        prev_c = [c_all[:, k] for k in range(l + 1)] if l + 1 < DEPTH else None
        (yl,) = _mlstm(qk, v, o, gates, gates_t, L_LAT, N_LAT_SEQ, lat_blk, ml_short, gb, gbt, ng, ml_state, l)
        y_ml = (yc, yl)
        new_n.append(nc_.reshape(N_CTX_SEQ, 2, ML_HEADS, ML_DH))
        new_m.append(mc_[:, :, 0].reshape(N_CTX_SEQ, 2, ML_HEADS))

        ys_c, fin = _s5(_to_time_major_ctx(u_s5[:T_CTX]), bb, cc_s5, lam, dsk, wglu, None, l)
        (ys_l,) = _s5(_to_time_major_lat(u_s5[T_CTX:]), bb, cc_s5, lam, dsk, wglu, s0, l)
        y_s5 = (_from_time_major_ctx(ys_c), _from_time_major_lat(ys_l))
        fin = fin.transpose(0, 2, 1, 3).reshape(N_CTX_SEQ, 2, 2 * S5_STATE)
        new_re.append(fin[..., :S5_STATE].reshape(N_CTX_SEQ, 2, S5_G, S5_P))
        new_im.append(fin[..., S5_STATE:].reshape(N_CTX_SEQ, 2, S5_G, S5_P))

        xn, h2e, best, rank, cnt = _outproj(x, y_hy, y_ml, y_s5, w_out, l, mod, g2, rw_t, rb)
        pos, src, tile_group, n_tiles = _dispatch(best, rank, cnt)
        ys = _moe(h2e, src, tile_group, n_tiles, moe_w_gate, moe_w_up, moe_w_down, l)
        res = _combine(pos, ys, xn, mod, fg, l, l == DEPTH - 1)
        if l == DEPTH - 1:
            y_prompt = res[0].reshape(N_CTX_SEQ, L_CTX, D)
            y_sample = res[1].reshape(N_LAT_SEQ, L_LAT, D)
        else:
            x = (res[0],)

    new_c = c_all.reshape(N_CTX_SEQ, DEPTH, 2, ML_HEADS, ML_DH, ML_DH)
    return (y_prompt, y_sample, new_c, jnp.stack(new_n, axis=1), jnp.stack(new_m, axis=1),
            jnp.stack(new_re, axis=1), jnp.stack(new_im, axis=1))
```

```python
import functools
import math

import numpy as np
import jax
import jax.numpy as jnp
from jax import lax
from jax.experimental import pallas as pl
from jax.experimental.pallas import tpu as pltpu

F32 = jnp.float32
BF16 = jnp.bfloat16

D = 1024
N_CTX_SEQ, L_CTX = 16, 256
N_LAT_SEQ, L_LAT = 2, 1024
T_CTX = N_CTX_SEQ * L_CTX
T_LAT = N_LAT_SEQ * L_LAT
T_ALL = T_CTX + T_LAT
DEPTH = 2
EPS = 1e-6
GRID_W = 64
POS_BASE = 10000.0
HY_CH = 256
HY_EMB = 33
HY_FILTER_W = 64
ML_HEADS = 4
ML_DH = 128
ML_W = ML_HEADS * ML_DH
S5_CH = 256
S5_G = 16
S5_GROUP = 16
S5_P = 64
S5_STATE = S5_G * S5_P
N_EXPERTS = 16
N_GROUPS = 4
GROUP_SIZE = N_EXPERTS // N_GROUPS
D_EXPERT = 512
OFF_HY = 0
OFF_QK = 3 * HY_CH
OFF_V = OFF_QK + 2 * ML_W
OFF_O = OFF_V + ML_W
OFF_G = OFF_O + ML_W
OFF_S5 = OFF_G + 16
IN_W = OFF_S5 + S5_CH
LANES = 128
SUBLANES = 8
GATE_PAD = LANES

TM = 512
TM_MOE = 512
MOE_SLOTS = T_ALL + N_GROUPS * TM_MOE
MOE_SUB = 512
MOE_HID = 256
HY_ROWS = 1024
CONV_ROWS = 128
ML_CHUNK = 256
S5_ROWS = 8
S5_STEPS = 256
VMEM_LIMIT = 56 * 1024 * 1024


def _cparams(sem, vmem=VMEM_LIMIT):
    if sem is None:
        return pltpu.CompilerParams(vmem_limit_bytes=vmem)
    return pltpu.CompilerParams(dimension_semantics=sem, vmem_limit_bytes=vmem)


def _bdot(a, b):
    return jnp.dot(a.astype(BF16), b.astype(BF16), preferred_element_type=F32)


def _split_bf16(x):
    hi = x.astype(BF16)
    return hi, (x - hi.astype(F32)).astype(BF16)


def _dot3(a, b):
    a_hi, a_lo = _split_bf16(a)
    b_hi, b_lo = _split_bf16(b)
    dot = functools.partial(jnp.dot, preferred_element_type=F32)
    return dot(a_hi, b_hi) + (dot(a_hi, b_lo) + dot(a_lo, b_hi))


def _silu(x):
    return x * jax.nn.sigmoid(x)


def _rms(x, g):
    return x * lax.rsqrt(jnp.mean(x * x, axis=-1, keepdims=True) + EPS) * g


def _log_sigmoid(x):
    return jnp.minimum(x, 0.0) - jnp.log1p(jnp.exp(-jnp.abs(x)))


def _conv3(u, w, n_rows, seq_len):
    row = lax.broadcasted_iota(jnp.int32, (n_rows, 1), 0) % seq_len
    prev = jnp.where(row == 0, 0.0, pltpu.roll(u, 1, 0))
    nxt = jnp.where(row == seq_len - 1, 0.0, pltpu.roll(u, n_rows - 1, 0))
    return prev * w[0:1] + u * w[1:2] + nxt * w[2:3]


def _ada_kernel(c_ref, w_ref, b_ref, o_ref):
    o_ref[0] = _bdot(_silu(c_ref[...]), w_ref[0]) + b_ref[0]


def _ada(cc, w_ada, b_ada):
    tn = 1536
    return pl.pallas_call(
        _ada_kernel,
        grid=(DEPTH, 6 * D // tn),
        in_specs=[pl.BlockSpec((8, D), lambda l, j: (0, 0)),
                  pl.BlockSpec((1, D, tn), lambda l, j: (l, 0, j)),
                  pl.BlockSpec((1, 1, tn), lambda l, j: (l, 0, j))],
        out_specs=pl.BlockSpec((1, 8, tn), lambda l, j: (l, 0, j)),
        out_shape=jax.ShapeDtypeStruct((DEPTH, 8, 6 * D), F32),
        compiler_params=_cparams(("arbitrary", "arbitrary")),
        name="ada_mod",
    )(cc, w_ada, b_ada.reshape(DEPTH, 1, 6 * D))


def _mod_row(i, tm):
    n_ctx = T_CTX // tm
    return jnp.where(i < n_ctx, 0, 1 + (i - n_ctx) // (L_LAT // tm))


_SEG = ((OFF_HY, OFF_QK - OFF_HY), (OFF_QK, OFF_V - OFF_QK), (OFF_V, OFF_O - OFF_V), (OFF_O, OFF_G - OFF_O))
TAIL_W = IN_W - OFF_G


N_CTX_TILES = T_CTX // TM
_NT = (((1,), (1,)), ((), ()))


def _x_specs(x):
    if len(x) == 1:
        return [pl.BlockSpec((TM, D), lambda i, *_: (i, 0))]
    per_seq = L_LAT // TM
    return [pl.BlockSpec((TM, D), lambda i, *_: (jnp.minimum(i, N_CTX_TILES - 1), 0)),
            pl.BlockSpec((TM, D), lambda i, *_: (jnp.maximum(i - N_CTX_TILES, 0), 0)),
            pl.BlockSpec((TM, D), lambda i, *_: (jnp.maximum(i - N_CTX_TILES, 0) % per_seq, 0))]


def _x_tile(step, x_refs):
    if len(x_refs) == 1:
        return x_refs[0][...]
    xc_ref, xl_ref, pos_ref = x_refs
    return jnp.where(step < N_CTX_TILES, xc_ref[...], xl_ref[...] + pos_ref[...])


def _inproj_kernel(*refs, n_x):
    x_refs = refs[:n_x]
    g_ref, mod_ref, w_ref, hy_ref, qk_ref, v_ref, o_ref, s5_ref, gt_ref, gtt_ref, wb = refs[n_x:]
    step = pl.program_id(0)

    @pl.when(step == 0)
    def _():
        wb[...] = w_ref[0].astype(BF16)

    mod = mod_ref[0, 0]
    h = _rms(_x_tile(step, x_refs), g_ref[0]) * (1.0 + mod[1:2]) + mod[0:1]
    hb = h.astype(BF16)
    for (a, w), ref in zip(_SEG, (hy_ref, qk_ref, v_ref, o_ref)):
        ref[...] = lax.dot_general(hb, wb[a:a + w, :], _NT, preferred_element_type=F32)
    tail = lax.dot_general(hb, wb[OFF_G:IN_W, :], _NT, preferred_element_type=F32)
    gates = tail[:, 0:GATE_PAD]
    gt_ref[...] = gates
    for c in range(TM // ML_CHUNK):
        gtt_ref[c] = gates[c * ML_CHUNK:(c + 1) * ML_CHUNK, :].T[0:16, :]
    s5_ref[...] = tail[:, OFF_S5 - OFF_G:TAIL_W]


def _layer_spec(shape, l):
    return pl.BlockSpec((1,) + tuple(shape), lambda *_: (l,) + (0,) * len(shape))


def _mod_spec(l, tm):
    return pl.BlockSpec((1, 1, 6, D), lambda i, *_: (l, _mod_row(i, tm), 0, 0))


def _inproj(x, g, mod, w_in_t, l):
    widths = [w for _, w in _SEG] + [S5_CH, GATE_PAD]
    cpt = TM // ML_CHUNK
    return pl.pallas_call(
        functools.partial(_inproj_kernel, n_x=len(x)),
        grid=(T_ALL // TM,),
        in_specs=_x_specs(x) + [
            _layer_spec((1, D), l), _mod_spec(l, TM),
            pl.BlockSpec((1, IN_W, D), lambda i: (l, 0, 0), pipeline_mode=pl.Buffered(1))],
        out_specs=[pl.BlockSpec((TM, w), lambda i: (i, 0)) for w in widths]
        + [pl.BlockSpec((cpt, 16, ML_CHUNK), lambda i: (i, 0, 0))],
        out_shape=[jax.ShapeDtypeStruct((T_ALL, w), F32) for w in widths]
        + [jax.ShapeDtypeStruct((T_ALL // ML_CHUNK, 16, ML_CHUNK), F32)],
        scratch_shapes=[pltpu.VMEM((IN_W, D), BF16)],
        compiler_params=_cparams(("arbitrary",)),
        name="norm_inproj",
    )(*x, g, mod, w_in_t)


@functools.lru_cache(None)
def _dft_mats(L):
    n = 2 * L
    k = np.arange(L)[:, None]
    t = np.arange(L)[None, :]
    ang = 2.0 * np.pi * ((k * t) % n) / n
    top = np.cos(ang)
    bot = -np.sin(ang)
    bot[0] = np.cos(np.pi * np.arange(L))
    fwd = np.concatenate([top, bot], 0)
    s = np.full((n, 1), 2.0 / n)
    s[0] = s[L] = 1.0 / n
    inv = (fwd * s).T
    return fwd.astype(np.float32), inv.astype(np.float32)


@functools.lru_cache(None)
def _hy_positions(L):
    t = np.linspace(0.0, 1.0, L)
    bands = (HY_EMB - 1) // 2
    f = np.linspace(1e-4, bands - 1, bands)
    w = 2.0 * np.pi * np.arange(L) / L
    ang = w[:, None] * f[None, :]
    z = np.concatenate([t[:, None], np.cos(ang), -np.sin(ang)], -1)
    zp = np.zeros((L, LANES))
    zp[:, :HY_EMB] = z
    return zp.astype(np.float32), t[:, None].astype(np.float32)


def _hy_filter_kernel(z_ref, t_ref, w1_ref, b1_ref, w2_ref, b2_ref, w3_ref, ld_ref, f_ref,
                      p_ref, q_ref, r_ref, *, L):
    h = jnp.sin(_dot3(z_ref[...], w1_ref[0]) + b1_ref[0])
    h = jnp.sin(_dot3(h, w2_ref[0]) + b2_ref[0])
    filt = _dot3(h, w3_ref[0])
    filt = filt * jnp.exp(-t_ref[...] * jnp.exp(ld_ref[0]))
    c = HY_CH
    h_fwd = jnp.concatenate([filt[:, 0:c], filt[:, 2 * c:3 * c]], axis=1)
    h_bwd = jnp.concatenate([filt[:, c:2 * c], filt[:, 3 * c:4 * c]], axis=1)
    row = lax.broadcasted_iota(jnp.int32, (L, 1), 0)
    h_bwd = jnp.where(row == 0, 0.0, h_bwd)
    a = jnp.dot(f_ref[...], h_fwd.astype(BF16), preferred_element_type=F32)
    b = jnp.dot(f_ref[...], h_bwd.astype(BF16), preferred_element_type=F32)
    re = a[:L] + b[:L]
    im = a[L:] - b[L:]
    nyq = a[L:L + 1] + b[L:L + 1]
    p_ref[0] = re
    q_ref[0] = jnp.where(row == 0, 0.0, im)
    r_ref[0] = jnp.where(row == 0, nyq, re)


def _hy_filter(L, w1p, b1, w2, b2, w3, ld):
    z, t = _hy_positions(L)
    fwd = jnp.asarray(_dft_mats(L)[0]).astype(BF16)
    out = jax.ShapeDtypeStruct((DEPTH, L, 2 * HY_CH), F32)
    full = lambda a: pl.BlockSpec(a.shape, lambda l: (0,) * a.ndim)
    layer = lambda a: pl.BlockSpec((1,) + a.shape[1:], lambda l: (l,) + (0,) * (a.ndim - 1))
    return pl.pallas_call(
        functools.partial(_hy_filter_kernel, L=L),
        grid=(DEPTH,),
        in_specs=[full(z), full(t), layer(w1p), layer(b1), layer(w2), layer(b2), layer(w3), layer(ld), full(fwd)],
        out_specs=[pl.BlockSpec((1, L, 2 * HY_CH), lambda l: (l, 0, 0))] * 3,
        out_shape=[out, out, out],
        compiler_params=_cparams(("arbitrary",)),
        name=f"hyena_filter_{L}",
    )(z, t, w1p, b1, w2, b2, w3, ld, fwd)


def _hyena_kernel(u_ref, sw_ref, bias_ref, p_ref, q_ref, r_ref, f_ref, g_ref, o_ref, *, L, n_sub):
    c = HY_CH
    u = _conv3(u_ref[...], sw_ref[0], n_sub * L, L)
    for s in range(n_sub):
        rs = slice(s * L, (s + 1) * L)
        z = u[rs, 0:c]
        for o in range(2):
            gate = u[rs, (o + 1) * c:(o + 2) * c]
            zf = jnp.dot(f_ref[...], z.astype(BF16), preferred_element_type=F32)
            a, b = zf[:L], zf[L:]
            p = p_ref[0, :, o * c:(o + 1) * c]
            q = q_ref[0, :, o * c:(o + 1) * c]
            r = r_ref[0, :, o * c:(o + 1) * c]
            y_re = (a * p - b * q).astype(BF16)
            y_im = (a * q + b * r).astype(BF16)
            y = (jnp.dot(g_ref[:, :L], y_re, preferred_element_type=F32)
                 + jnp.dot(g_ref[:, L:], y_im, preferred_element_type=F32))
            z = gate * (y + bias_ref[0, o:o + 1, :] * z)
        o_ref[rs, :] = z


def _hyena(u_hy, L, n_seq, row_block0, sw, bias, p, q, r, l):
    n_sub = max(1, HY_ROWS // L)
    fwd, inv = (jnp.asarray(m).astype(BF16) for m in _dft_mats(L))
    full = lambda a: pl.BlockSpec(a.shape, lambda b: (0,) * a.ndim)
    layer = lambda a: _layer_spec(a.shape[1:], l)
    blk0 = row_block0 // n_sub
    return pl.pallas_call(
        functools.partial(_hyena_kernel, L=L, n_sub=n_sub),
        grid=(n_seq // n_sub,),
        in_specs=[pl.BlockSpec((n_sub * L, 3 * HY_CH), lambda b: (blk0 + b, 0)),
                  layer(sw), layer(bias), layer(p), layer(q), layer(r), full(fwd), full(inv)],
        out_specs=pl.BlockSpec((n_sub * L, HY_CH), lambda b: (b, 0)),
        out_shape=jax.ShapeDtypeStruct((n_seq * L, HY_CH), F32),
        compiler_params=_cparams(("arbitrary",)),
        name=f"hyena_{L}",
    )(u_hy, sw, bias, p, q, r, fwd, inv)


def _cumsum_rows(x, n, reverse):
    row = lax.broadcasted_iota(jnp.int32, (n, 1), 0)
    s = 1
    while s < n:
        if reverse:
            x = x + jnp.where(row < n - s, pltpu.roll(x, n - s, 0), 0.0)
        else:
            x = x + jnp.where(row >= s, pltpu.roll(x, s, 0), 0.0)
        s *= 2
    return x


def _cumsum_lanes(x, n, reverse):
    col = lax.broadcasted_iota(jnp.int32, (1, n), 1)
    s = 1
    while s < n:
        if reverse:
            x = x + jnp.where(col < n - s, pltpu.roll(x, n - s, 1), 0.0)
        else:
            x = x + jnp.where(col >= s, pltpu.roll(x, s, 1), 0.0)
        s *= 2
    return x


def _mlstm_kernel(*refs, L, has_state, emit_state, n_prev):
    qk_ref, v_ref, o_ref, g_ref, gt_ref, sw_ref, gb_ref, gbt_ref, ng_ref = refs[:9]
    refs = refs[9:]
    if has_state:
        c0_ref, n0_ref, m0_ref = refs[:3]
        refs = refs[3:]
    prev_c_refs = refs[:n_prev]
    refs = refs[n_prev:]
    y_ref = refs[0]
    if emit_state:
        cout_ref, nout_ref, mout_ref = refs[1:4]
        refs = refs[4:]
    else:
        refs = refs[1:]
    q_s, k_s, vt_s, ht_s, ct_s, n_s, m_s, lf_s, lft_s = refs
    tc = ML_CHUNK
    nc = L // tc
    nh = ML_HEADS
    dh = ML_DH

    lf_s[...] = _log_sigmoid(g_ref[...] + gb_ref[0])
    lft_s[...] = _log_sigmoid(gt_ref[...] + gbt_ref[0])

    sw = sw_ref[0]
    for r0 in range(0, L, CONV_ROWS):
        a, b = max(r0 - SUBLANES, 0), min(r0 + CONV_ROWS + SUBLANES, L)
        n = b - a
        u = qk_ref[a:b, :]
        row = lax.broadcasted_iota(jnp.int32, (n, 1), 0) + a
        prev = jnp.where(row == 0, 0.0, pltpu.roll(u, 1, 0))
        nxt = jnp.where(row == L - 1, 0.0, pltpu.roll(u, n - 1, 0))
        y = _silu(prev * sw[0:1] + u * sw[1:2] + nxt * sw[2:3])[r0 - a:r0 - a + CONV_ROWS]
        q_s[r0:r0 + CONV_ROWS, :] = y[:, :ML_W].astype(BF16)
        k_s[r0:r0 + CONV_ROWS, :] = (y[:, ML_W:] * (dh ** -0.5)).astype(BF16)
    for c in range(nc):
        vt_s[c] = v_ref[c * tc:(c + 1) * tc, :].T.astype(BF16)

    for i in range(2 * nh):
        ct_s[i] = c0_ref[0, 0, i].T if has_state else jnp.zeros((dh, dh), F32)
    n_s[...] = n0_ref[0, 0] if has_state else jnp.zeros_like(n_s)
    m_s[...] = m0_ref[0, 0] if has_state else jnp.zeros_like(m_s)

    si = lax.broadcasted_iota(jnp.int32, (tc, tc), 0)
    ti = lax.broadcasted_iota(jnp.int32, (tc, tc), 1)

    for d in range(2):
        rev = d == 1
        mask = (si >= ti) if rev else (si <= ti)
        edge = 0 if rev else tc - 1

        def chunk(j, carry, d=d, rev=rev, mask=mask, edge=edge):
            cidx = (nc - 1 - j) if rev else j
            r0 = pl.multiple_of(cidx * tc, tc)
            pre = g_ref[pl.ds(r0, tc), :] + gb_ref[0]
            pre_t = gt_ref[cidx] + gbt_ref[0]
            cum = _cumsum_rows(lf_s[pl.ds(r0, tc), :], tc, rev)
            cum_t = _cumsum_lanes(lft_s[cidx], tc, rev)
            key_all = cum - pltpu.roll(pre, 8, 1)
            for h in range(nh):
                col = d * nh + h
                hs = slice(h * dh, (h + 1) * dh)
                key = key_all[:, 8 + col:9 + col]
                b_row = cum_t[8 + col:9 + col, :]
                b_end = b_row[:, edge:edge + 1]
                m_prev = m_s[col:col + 1, 0:1]
                dmat = jnp.where(mask, b_row - key, -jnp.inf)
                inter = b_row + m_prev
                m_row = jnp.maximum(inter, jnp.max(dmat, axis=0, keepdims=True))
                w_intra = jnp.exp(dmat - m_row)
                w_state = jnp.exp(inter - m_row)
                qh = q_s[pl.ds(r0, tc), hs]
                kh = k_s[pl.ds(r0, tc), hs]
                vt = vt_s[cidx, hs, :]
                ct_prev = ct_s[col]
                n_prev = n_s[col:col + 1, :]
                s = lax.dot_general(kh, qh, _NT, preferred_element_type=F32) * w_intra
                num = (jnp.dot(vt, s.astype(BF16), preferred_element_type=F32)
                       + w_state * lax.dot_general(ct_prev.astype(BF16), qh, _NT, preferred_element_type=F32))
                qn = lax.dot_general(jnp.broadcast_to(n_prev, (8, dh)).astype(BF16), qh, _NT,
                                     preferred_element_type=F32)[0:1]
                den = jnp.sum(s, axis=0, keepdims=True) + w_state * qn
                hout = num * (1.0 / jnp.maximum(jnp.abs(den), jnp.exp(-m_row)))
                if d == 0:
                    ht_s[cidx, hs, :] = hout
                else:
                    ht_s[cidx, hs, :] += hout
                m_new = jnp.maximum(b_end + m_prev, b_end - jnp.min(key, axis=0, keepdims=True))
                wg = jnp.exp(b_end - key - m_new)
                decay = jnp.exp(b_end + m_prev - m_new)
                kw = kh.astype(F32) * wg
                ct_s[col] = decay * ct_prev + jnp.dot(vt, kw.astype(BF16), preferred_element_type=F32)
                n_s[col:col + 1, :] = decay * n_prev + jnp.sum(kw, axis=0, keepdims=True)
                m_s[col:col + 1, :] = jnp.broadcast_to(m_new, (1, LANES))
            return carry

        lax.fori_loop(0, nc, chunk, 0)

    for c in range(nc):
        for h in range(nh):
            hs = slice(h * dh, (h + 1) * dh)
            rs = slice(c * tc, (c + 1) * tc)
            ht = ht_s[c, hs, :]
            hn = ht * lax.rsqrt(jnp.mean(ht * ht, axis=0, keepdims=True) + EPS)
            y_ref[rs, hs] = jax.nn.sigmoid(o_ref[rs, hs]) * (hn.T * ng_ref[0, :, hs])
    if emit_state:
        for k, prev_ref in enumerate(prev_c_refs):
            cout_ref[0, k] = prev_ref[0]
        for i in range(2 * nh):
            cout_ref[0, n_prev, i] = ct_s[i].T
        nout_ref[0] = n_s[...]
        mout_ref[0] = m_s[...]


def _mlstm(qk, v, o, gates, gates_t, L, n_seq, row_block0, sw, gb, gbt, ng, state, l, emit_state=False,
           prev_c=()):
    nc = L // ML_CHUNK
    has_state = state is not None
    layer = lambda a: _layer_spec(a.shape[1:], l)
    in_specs = [pl.BlockSpec((L, 2 * ML_W), lambda b: (row_block0 + b, 0)),
                pl.BlockSpec((L, ML_W), lambda b: (row_block0 + b, 0)),
                pl.BlockSpec((L, ML_W), lambda b: (row_block0 + b, 0)),
                pl.BlockSpec((L, GATE_PAD), lambda b: (row_block0 + b, 0)),
                pl.BlockSpec((nc, 16, ML_CHUNK), lambda b: (row_block0 + b, 0, 0)),
                layer(sw), layer(gb), layer(gbt), layer(ng)]
    args = [qk, v, o, gates, gates_t, sw, gb, gbt, ng]
    if has_state:
        c0, n0, m0 = state
        in_specs += [pl.BlockSpec((1, 1, 2 * ML_HEADS, ML_DH, ML_DH), lambda b: (b, l, 0, 0, 0)),
                     pl.BlockSpec((1, 1, 2 * ML_HEADS, ML_DH), lambda b: (b, l, 0, 0)),
                     pl.BlockSpec((1, 1, 2 * ML_HEADS, LANES), lambda b: (b, l, 0, 0))]
        args += [c0, n0, m0]
    out_specs = [pl.BlockSpec((L, ML_W), lambda b: (b, 0))]
    out_shape = [jax.ShapeDtypeStruct((n_seq * L, ML_W), F32)]
    if emit_state:
        c_tail = (2 * ML_HEADS, ML_DH, ML_DH)
        for a in prev_c:
            in_specs.append(pl.BlockSpec((1,) + c_tail, lambda b: (b, 0, 0, 0)))
            args.append(a)
        tails = ((len(prev_c) + 1,) + c_tail, (2 * ML_HEADS, ML_DH), (2 * ML_HEADS, LANES))
        for t in tails:
            out_specs.append(pl.BlockSpec((1,) + t, lambda b, n=len(t): (b,) + (0,) * n))
            out_shape.append(jax.ShapeDtypeStruct((n_seq,) + t, F32))
    return pl.pallas_call(
        functools.partial(_mlstm_kernel, L=L, has_state=has_state, emit_state=emit_state,
                          n_prev=len(prev_c) if emit_state else 0),
        grid=(n_seq,),
        in_specs=in_specs,
        out_specs=out_specs,
        out_shape=out_shape,
        scratch_shapes=[pltpu.VMEM((L, ML_W), BF16), pltpu.VMEM((L, ML_W), BF16),
                        pltpu.VMEM((nc, ML_W, ML_CHUNK), BF16),
                        pltpu.VMEM((nc, ML_W, ML_CHUNK), F32),
                        pltpu.VMEM((2 * ML_HEADS, ML_DH, ML_DH), F32),
                        pltpu.VMEM((2 * ML_HEADS, ML_DH), F32),
                        pltpu.VMEM((2 * ML_HEADS, LANES), F32),
                        pltpu.VMEM((L, GATE_PAD), F32),
                        pltpu.VMEM((nc, 16, ML_CHUNK), F32)],
        compiler_params=_cparams(("arbitrary",)),
        name=f"mlstm_{L}",
    )(*args)


def _cmul(ar, ai, br, bi):
    return ar * br - ai * bi, ar * bi + ai * br


def _s5_kernel(*refs, segmented):
    if segmented:
        (u_ref, bb_ref, cc_ref, eb_ref, ec_ref, lam_ref, dsk_ref, wglu_ref, s0_ref, y_ref,
         sbuf, yacc, bmat, cmat, pw) = refs
    else:
        (u_ref, bb_ref, cc_ref, eb_ref, ec_ref, lam_ref, dsk_ref, wglu_ref, y_ref, fin_ref,
         sbuf, yacc, bmat, cmat) = refs
    n = S5_STATE
    rows = S5_ROWS
    steps = S5_STEPS
    blk = 256
    n_blk = steps * rows // blk
    n_seg = 4

    yacc[...] = u_ref[0] * dsk_ref[0]
    ub = u_ref[0].astype(BF16)
    b_keep = (lax.broadcasted_iota(jnp.int32, (S5_CH, 2 * n), 0) // S5_GROUP
              == (lax.broadcasted_iota(jnp.int32, (S5_CH, 2 * n), 1) % n) // S5_P)
    c_keep = ((lax.broadcasted_iota(jnp.int32, (2 * n, S5_CH), 0) % n) // S5_P
              == lax.broadcasted_iota(jnp.int32, (2 * n, S5_CH), 1) // S5_GROUP)

    for d in range(2):
        rev = d == 1
        bmat[...] = jnp.where(b_keep, jnp.dot(bb_ref[0, d].astype(BF16), eb_ref[...], preferred_element_type=F32),
                              0.0).astype(BF16)
        cmat[...] = jnp.where(c_keep, jnp.dot(cc_ref[0, d].astype(BF16), ec_ref[...], preferred_element_type=F32),
                              0.0).astype(BF16)
        for i in range(n_blk):
            sbuf[i * blk:(i + 1) * blk, :] = jnp.dot(ub[i * blk:(i + 1) * blk], bmat[...],
                                                     preferred_element_type=F32)
        lam = lam_ref[0, d]
        lr = jnp.broadcast_to(lam[:, :n], (rows, n))
        li = jnp.broadcast_to(lam[:, n:], (rows, n))

        def step(i, carry, rev=rev, lr=lr, li=li):
            sr, si = carry
            t = (steps - 1 - i) if rev else i
            off = pl.multiple_of(t * rows, rows)
            pr, pi = _cmul(lr, li, sr, si)
            nr = pr + sbuf[pl.ds(off, rows), 0:n]
            ni = pi + sbuf[pl.ds(off, rows), n:2 * n]
            sbuf[pl.ds(off, rows), 0:n] = nr
            sbuf[pl.ds(off, rows), n:2 * n] = ni
            return nr, ni

        zero = jnp.zeros((rows, n), F32)
        sr, si = lax.fori_loop(0, steps, step, (zero, zero), unroll=2)

        if not segmented:
            fin_ref[0, d, :, 0:n] = sr
            fin_ref[0, d, :, n:2 * n] = si
        else:
            lam_r, lam_i = lam[:, :n], lam[:, n:]
            row8 = lax.broadcasted_iota(jnp.int32, (rows, 1), 0)
            cr, ci = lam_r, lam_i
            acc_r = jnp.broadcast_to(cr, (rows, n))
            acc_i = jnp.broadcast_to(ci, (rows, n))
            for j in range(1, rows):
                cr, ci = _cmul(cr, ci, lam_r, lam_i)
                acc_r = jnp.where(row8 >= j, jnp.broadcast_to(cr, (rows, n)), acc_r)
                acc_i = jnp.where(row8 >= j, jnp.broadcast_to(ci, (rows, n)), acc_i)
            pw[0:rows, 0:n] = acc_r
            pw[0:rows, n:2 * n] = acc_i
            size = rows
            while size < steps:
                tr = pw[size - 1:size, 0:n]
                ti = pw[size - 1:size, n:2 * n]
                xr, xi = _cmul(pw[0:size, 0:n], pw[0:size, n:2 * n], tr, ti)
                pw[size:2 * size, 0:n] = xr
                pw[size:2 * size, n:2 * n] = xi
                size *= 2
            end_off = 0 if rev else (steps - 1) * rows
            loc_r = sbuf[end_off:end_off + rows, 0:n]
            loc_i = sbuf[end_off:end_off + rows, n:2 * n]
            pl_r = pw[steps - 1:steps, 0:n]
            pl_i = pw[steps - 1:steps, n:2 * n]
            s0r = s0_ref[0, d, :, 0:n]
            s0i = s0_ref[0, d, :, n:2 * n]
            seg = row8 // 2
            first = (seg == n_seg - 1) if rev else (seg == 0)
            shift = (rows - 2) if rev else 2
            cin_r, cin_i = s0r, s0i
            for _ in range(n_seg - 1):
                fr, fi = _cmul(jnp.broadcast_to(pl_r, (rows, n)), jnp.broadcast_to(pl_i, (rows, n)), cin_r, cin_i)
                tru_r = loc_r + fr
                tru_i = loc_i + fi
                cin_r = jnp.where(first, s0r, pltpu.roll(tru_r, shift, 0))
                cin_i = jnp.where(first, s0i, pltpu.roll(tru_i, shift, 0))

            def fix(tb, carry, rev=rev, cin_r=cin_r, cin_i=cin_i):
                pb = (steps // rows - 1 - tb) if rev else tb
                poff = pl.multiple_of(pb * rows, rows)
                p_r = pw[pl.ds(poff, rows), 0:n]
                p_i = pw[pl.ds(poff, rows), n:2 * n]
                for j in range(rows):
                    jj = rows - 1 - j if rev else j
                    off = pl.multiple_of((tb * rows + j) * rows, rows)
                    fr, fi = _cmul(jnp.broadcast_to(p_r[jj:jj + 1], (rows, n)),
                                   jnp.broadcast_to(p_i[jj:jj + 1], (rows, n)), cin_r, cin_i)
                    sbuf[pl.ds(off, rows), 0:n] += fr
                    sbuf[pl.ds(off, rows), n:2 * n] += fi
                return carry

            lax.fori_loop(0, steps // rows, fix, 0)

        for i in range(n_blk):
            yacc[i * blk:(i + 1) * blk, :] += jnp.dot(sbuf[i * blk:(i + 1) * blk, :].astype(BF16), cmat[...],
                                                      preferred_element_type=F32)

    g = jax.nn.gelu(yacc[...], approximate=True)
    y_ref[0] = g * jax.nn.sigmoid(_bdot(g, wglu_ref[0]))


@functools.lru_cache(None)
def _s5_spread():
    eb = np.zeros((2 * S5_P, 2 * S5_STATE), np.float32)
    for half in range(2):
        for g in range(S5_G):
            c0 = half * S5_STATE + g * S5_P
            eb[half * S5_P:(half + 1) * S5_P, c0:c0 + S5_P] = np.eye(S5_P)
    ec = np.zeros((LANES, S5_CH), np.float32)
    for g in range(S5_G):
        ec[:S5_GROUP, g * S5_GROUP:(g + 1) * S5_GROUP] = np.eye(S5_GROUP)
    return eb, ec


def _s5(u_tm, bb, cc, lam, dskip, wglu, s0, l):
    n_grp = u_tm.shape[0]
    n_rows = S5_STEPS * S5_ROWS
    segmented = s0 is not None
    eb, ec = (jnp.asarray(m).astype(BF16) for m in _s5_spread())
    full = lambda a: pl.BlockSpec(a.shape, lambda g: (0,) * a.ndim)
    layer = lambda a: _layer_spec(a.shape[1:], l)
    in_specs = [pl.BlockSpec((1, n_rows, S5_CH), lambda g: (g, 0, 0)),
                layer(bb), layer(cc), full(eb), full(ec), layer(lam), layer(dskip), layer(wglu)]
    args = [u_tm, bb, cc, eb, ec, lam, dskip, wglu]
    out_specs = [pl.BlockSpec((1, n_rows, S5_CH), lambda g: (g, 0, 0))]
    out_shape = [jax.ShapeDtypeStruct((n_grp, n_rows, S5_CH), F32)]
    scratch = [pltpu.VMEM((n_rows, 2 * S5_STATE), F32), pltpu.VMEM((n_rows, S5_CH), F32),
               pltpu.VMEM((S5_CH, 2 * S5_STATE), BF16), pltpu.VMEM((2 * S5_STATE, S5_CH), BF16)]
    if segmented:
        in_specs.append(layer(s0))
        args.append(s0)
        scratch.append(pltpu.VMEM((S5_STEPS, 2 * S5_STATE), F32))
    else:
        out_specs.append(pl.BlockSpec((1, 2, S5_ROWS, 2 * S5_STATE), lambda g: (g, 0, 0, 0)))
        out_shape.append(jax.ShapeDtypeStruct((n_grp, 2, S5_ROWS, 2 * S5_STATE), F32))
    return pl.pallas_call(
        functools.partial(_s5_kernel, segmented=segmented),
        grid=(n_grp,),
        in_specs=in_specs,
        out_specs=out_specs,
        out_shape=out_shape,
        scratch_shapes=scratch,
        compiler_params=_cparams(("arbitrary",)),
        name="s5_seg" if segmented else "s5_ctx",
    )(*args)


def _s5_params(a_re, a_im, log_dt, b_re, b_im, c_re, c_im):
    dt = jnp.exp(log_dt)[..., None]
    mag = jnp.exp(a_re * dt)
    lb_re = mag * jnp.cos(a_im * dt)
    lb_im = mag * jnp.sin(a_im * dt)
    den = a_re * a_re + a_im * a_im
    nr, ni = lb_re - 1.0, lb_im
    k_re = (nr * a_re + ni * a_im) / den
    k_im = (ni * a_re - nr * a_im) / den
    bb_re = k_re[..., None] * b_re - k_im[..., None] * b_im
    bb_im = k_re[..., None] * b_im + k_im[..., None] * b_re
    to_gc_p = lambda m: m.transpose(0, 1, 2, 4, 3).reshape(DEPTH, 2, S5_CH, S5_P)
    bb = jnp.concatenate([to_gc_p(bb_re), to_gc_p(bb_im)], axis=3)
    to_gp_c = lambda m: m.transpose(0, 1, 2, 4, 3).reshape(DEPTH, 2, S5_STATE, S5_GROUP)
    cc = jnp.concatenate([to_gp_c(c_re), -to_gp_c(c_im)], axis=2)
    cc = jnp.pad(cc, ((0, 0), (0, 0), (0, 0), (0, LANES - S5_GROUP)))
    lam = jnp.concatenate([lb_re.reshape(DEPTH, 2, 1, S5_STATE), lb_im.reshape(DEPTH, 2, 1, S5_STATE)], axis=3)
    return bb, cc, lam


def _outproj_kernel(*refs, n_x):
    x_refs = refs[:n_x]
    (hyc_ref, hyl_ref, mlc_ref, mll_ref, s5c_ref, s5l_ref, w_ref, mod_ref, g_ref,
     rw_ref, rb_ref, xn_ref, h2e_ref, best_ref, rank_ref, cnt_ref, wb, cnt_s) = refs[n_x:]
    step = pl.program_id(0)

    @pl.when(step == 0)
    def _():
        wb[...] = w_ref[0].astype(BF16)
        cnt_s[...] = jnp.zeros_like(cnt_s)

    is_ctx = step < T_CTX // TM
    pick = lambda c_ref, l_ref: jnp.where(is_ctx, c_ref[...], l_ref[...]).astype(BF16)
    mod = mod_ref[0, 0]
    a, b = HY_CH, HY_CH + ML_W
    mix = (jnp.dot(pick(hyc_ref, hyl_ref), wb[0:a, :], preferred_element_type=F32)
           + jnp.dot(pick(mlc_ref, mll_ref), wb[a:b, :], preferred_element_type=F32)
           + jnp.dot(pick(s5c_ref, s5l_ref), wb[b:, :], preferred_element_type=F32))
    xn = _x_tile(step, x_refs) + mod[2:3] * mix
    xn_ref[...] = xn
    h2 = _rms(xn, g_ref[0]) * (1.0 + mod[4:5]) + mod[3:4]
    h2e_ref[:, 0:D] = h2
    h_hi, h_lo = _split_bf16(h2)
    r_hi, r_lo = _split_bf16(rw_ref[...])
    logits = (lax.dot_general(r_hi, h_hi, _NT, preferred_element_type=F32)
              + (lax.dot_general(r_hi, h_lo, _NT, preferred_element_type=F32)
                 + lax.dot_general(r_lo, h_hi, _NT, preferred_element_type=F32)))
    ex = jnp.exp(logits - jnp.max(logits, axis=0, keepdims=True))
    probs = ex / jnp.sum(ex, axis=0, keepdims=True)
    sel = probs + rb_ref[...]
    best = None
    best_score = None
    for g in range(N_GROUPS):
        r = [sel[g * GROUP_SIZE + i:g * GROUP_SIZE + i + 1, :] for i in range(GROUP_SIZE)]
        score = None
        for i in range(GROUP_SIZE):
            for j in range(i + 1, GROUP_SIZE):
                pair = r[i] + r[j]
                score = pair if score is None else jnp.maximum(score, pair)
        if g == 0:
            best, best_score = jnp.zeros_like(score, dtype=jnp.int32), score
        else:
            upd = score > best_score
            best = jnp.where(upd, g, best)
            best_score = jnp.where(upd, score, best_score)
    eid = lax.broadcasted_iota(jnp.int32, (N_EXPERTS, 1), 0)
    masked = jnp.where(eid // GROUP_SIZE == best, sel, -jnp.inf)
    m1 = jnp.max(masked, axis=0, keepdims=True)
    i1 = jnp.min(jnp.where(masked == m1, eid, N_EXPERTS), axis=0, keepdims=True)
    masked2 = jnp.where(eid == i1, -jnp.inf, masked)
    m2 = jnp.max(masked2, axis=0, keepdims=True)
    i2 = jnp.min(jnp.where(masked2 == m2, eid, N_EXPERTS), axis=0, keepdims=True)
    p1 = jnp.sum(jnp.where(eid == i1, probs, 0.0), axis=0, keepdims=True)
    p2 = jnp.sum(jnp.where(eid == i2, probs, 0.0), axis=0, keepdims=True)
    tot = p1 + p2
    comb = jnp.where(eid == i1, p1 / tot, 0.0) + jnp.where(eid == i2, p2 / tot, 0.0)
    comb = jnp.concatenate([comb, jnp.zeros((LANES - N_EXPERTS, comb.shape[1]), F32)], axis=0)
    h2e_ref[:, D:] = comb.T
    gid = lax.broadcasted_iota(jnp.int32, (SUBLANES, 1), 0)
    onehot = (gid == best).astype(F32)
    cum = _cumsum_lanes(onehot, TM, False)
    run = cnt_s[:, 0:1]
    best_ref[...] = best
    rank_ref[...] = jnp.sum(onehot * (cum - onehot + run), axis=0, keepdims=True).astype(jnp.int32)
    cnt_s[...] = jnp.broadcast_to(run + cum[:, TM - 1:TM], cnt_s.shape)
    cnt_ref[...] = cnt_s[...].astype(jnp.int32)


def _outproj(x, y_hy, y_ml, y_s5, w_out, l, mod, g2, rw_t, rb):
    full = lambda a: pl.BlockSpec(a.shape, lambda i: (0,) * a.ndim)
    n_ctx = T_CTX // TM
    ctx = lambda w: pl.BlockSpec((TM, w), lambda i: (jnp.minimum(i, n_ctx - 1), 0))
    lat = lambda w: pl.BlockSpec((TM, w), lambda i: (jnp.maximum(i - n_ctx, 0), 0))
    tok = lambda w: pl.BlockSpec((TM, w), lambda i: (i, 0))
    row = pl.BlockSpec((1, TM), lambda i: (0, i))
    return pl.pallas_call(
        functools.partial(_outproj_kernel, n_x=len(x)),
        grid=(T_ALL // TM,),
        in_specs=_x_specs(x) + [ctx(HY_CH), lat(HY_CH), ctx(ML_W), lat(ML_W), ctx(S5_CH), lat(S5_CH),
                  pl.BlockSpec((1, D, D), lambda i: (l, 0, 0), pipeline_mode=pl.Buffered(1)),
                  _mod_spec(l, TM), _layer_spec((1, D), l), full(rw_t), full(rb)],
        out_specs=[tok(D), tok(D + LANES), row, row, pl.BlockSpec((SUBLANES, LANES), lambda i: (0, 0))],
        out_shape=[jax.ShapeDtypeStruct((T_ALL, D), F32),
                   jax.ShapeDtypeStruct((T_ALL, D + LANES), F32),
                   jax.ShapeDtypeStruct((1, T_ALL), jnp.int32),
                   jax.ShapeDtypeStruct((1, T_ALL), jnp.int32),
                   jax.ShapeDtypeStruct((SUBLANES, LANES), jnp.int32)],
        scratch_shapes=[pltpu.VMEM((D, D), BF16), pltpu.VMEM((SUBLANES, LANES), F32)],
        compiler_params=_cparams(("arbitrary",)),
        name="outproj_router",
    )(*x, *y_hy, *y_ml, *y_s5, w_out, mod, g2, rw_t, rb)


def _gather_rows(idx_ref, idx_base, src_ref, dst_ref, n_rows):
    def body(r8, carry):
        base = pl.multiple_of(r8 * SUBLANES, SUBLANES)
        for k in range(SUBLANES):
            idx = idx_ref[idx_base + base + k]
            dst_ref[pl.ds(base + k, 1), :] = src_ref[pl.ds(idx, 1), :]
        return carry

    lax.fori_loop(0, n_rows // SUBLANES, body, 0)


def _dispatch_kernel(best_ref, rank_ref, cnt_ref, pos_ref, src_ref, tg_ref, nt_ref, off_s):
    tm = TM_MOE
    tiles = jnp.int32(0)
    tile_end = []
    for g in range(N_GROUPS):
        off_s[g] = tiles * tm
        tiles = tiles + (cnt_ref[g, 0] + (tm - 1)) // tm
        tile_end.append(tiles)
    nt_ref[0] = tiles
    for k in range(MOE_SLOTS // tm):
        g = jnp.int32(0)
        for e in tile_end[:-1]:
            g = g + (k >= e).astype(jnp.int32)
        tg_ref[k] = g

    unroll = SUBLANES

    def clear(i8, carry):
        for u in range(unroll):
            src_ref[i8 * unroll + u] = 0
        return carry

    lax.fori_loop(0, MOE_SLOTS // unroll, clear, 0)

    def place(t8, carry):
        for u in range(unroll):
            t = t8 * unroll + u
            p = rank_ref[t] + off_s[best_ref[t]]
            pos_ref[t] = p
            src_ref[p] = t
        return carry

    lax.fori_loop(0, T_ALL // unroll, place, 0)


def _dispatch(best, rank, cnt):
    smem = pl.BlockSpec(memory_space=pltpu.SMEM)
    i32 = lambda n: jax.ShapeDtypeStruct((n,), jnp.int32)
    return pl.pallas_call(
        _dispatch_kernel,
        in_specs=[smem, smem, smem],
        out_specs=[smem, smem, smem, smem],
        out_shape=[i32(T_ALL), i32(MOE_SLOTS), i32(MOE_SLOTS // TM_MOE), i32(1)],
        scratch_shapes=[pltpu.SMEM((N_GROUPS,), jnp.int32)],
        name="moe_dispatch",
    )(best.reshape(T_ALL), rank.reshape(T_ALL), cnt)


def _moe_kernel(src_ref, tg_ref, nt_ref, h_ref, wg_ref, wu_ref, wd_ref, ys_ref, g_a, x_a, g_b, x_b, acc):
    i = pl.program_id(0)
    j = pl.program_id(1)
    tm = TM_MOE
    quarter = tm // GROUP_SIZE
    n_live = nt_ref[0]

    @pl.when(jnp.logical_and(i == 0, j == 0))
    def _():
        _gather_rows(src_ref, 0, h_ref, g_a, tm)
        x_a[...] = g_a[:, 0:D].astype(BF16)

    def step(cur_g, cur_x, nxt_g, nxt_x):
        @pl.when(j == 0)
        def _():
            acc[...] = jnp.zeros_like(acc)

        e = tg_ref[i] * GROUP_SIZE + j
        lane = lax.broadcasted_iota(jnp.int32, (1, LANES), 1)
        wg = wg_ref[0, 0].astype(BF16)
        wu = wu_ref[0, 0].astype(BF16)
        wd = wd_ref[0, 0].astype(BF16)
        sub = MOE_SUB
        for s in range(tm // sub):
            rs = slice(s * sub, (s + 1) * sub)
            hb = cur_x[rs, :]
            ce = jnp.sum(jnp.where(lane == e, cur_g[rs, D:], 0.0), axis=1, keepdims=True)
            part = None
            for c0 in range(0, D_EXPERT, MOE_HID):
                cs = slice(c0, c0 + MOE_HID)
                hid = (_silu(jnp.dot(hb, wg[:, cs], preferred_element_type=F32))
                       * jnp.dot(hb, wu[:, cs], preferred_element_type=F32))
                p = jnp.dot((hid * ce).astype(BF16), wd[cs, :], preferred_element_type=F32)
                part = p if part is None else part + p
            acc[rs, :] += part

        r0 = pl.multiple_of(j * quarter, quarter)
        base = jnp.minimum(i + 1, n_live - 1) * tm + r0
        for k in range(quarter):
            nxt_g[pl.ds(r0 + k, 1), :] = h_ref[pl.ds(src_ref[base + k], 1), :]
        nxt_x[pl.ds(r0, quarter), :] = nxt_g[pl.ds(r0, quarter), 0:D].astype(BF16)

        @pl.when(j == GROUP_SIZE - 1)
        def _():
            for c in range(D // LANES):
                ys_ref[pl.ds(c, tm, stride=D // LANES), :] = acc[:, c * LANES:(c + 1) * LANES]

    live = i < n_live

    @pl.when(jnp.logical_and(live, i % 2 == 0))
    def _():
        step(g_a, x_a, g_b, x_b)

    @pl.when(jnp.logical_and(live, i % 2 == 1))
    def _():
        step(g_b, x_b, g_a, x_a)

    @pl.when(jnp.logical_and(jnp.logical_not(live), j == GROUP_SIZE - 1))
    def _():
        ys_ref[...] = jnp.zeros_like(ys_ref)


def _moe(h2e, src, tile_group, n_tiles, wg, wu, wd, l):
    tm = TM_MOE

    def w_map(i, j, src, tg, nt):
        live = i < nt[0]
        ii = jnp.minimum(i, nt[0] - 1)
        return (l, tg[ii] * GROUP_SIZE + jnp.where(live, j, GROUP_SIZE - 1), 0, 0)

    return pl.pallas_call(
        _moe_kernel,
        grid_spec=pltpu.PrefetchScalarGridSpec(
            num_scalar_prefetch=3,
            grid=(MOE_SLOTS // tm, GROUP_SIZE),
            in_specs=[pl.BlockSpec(memory_space=pltpu.VMEM),
                      pl.BlockSpec((1, 1, D, D_EXPERT), w_map),
                      pl.BlockSpec((1, 1, D, D_EXPERT), w_map),
                      pl.BlockSpec((1, 1, D_EXPERT, D), w_map)],
            out_specs=pl.BlockSpec((tm * (D // LANES), LANES), lambda i, j, src, tg, nt: (i, 0)),
            scratch_shapes=[pltpu.VMEM((tm, D + LANES), F32), pltpu.VMEM((tm, D), BF16),
                            pltpu.VMEM((tm, D + LANES), F32), pltpu.VMEM((tm, D), BF16),
                            pltpu.VMEM((tm, D), F32)]),
        out_shape=jax.ShapeDtypeStruct((MOE_SLOTS * (D // LANES), LANES), F32),
        compiler_params=_cparams(("arbitrary", "arbitrary")),
        name="moe_experts",
    )(src, tile_group, n_tiles, h2e, wg, wu, wd)


def _combine_kernel(pos_ref, ys_ref, xn_ref, mod_ref, fg_ref, *rest, final):
    step = pl.program_id(0)
    if final:
        yc_ref, yl_ref, gbuf = rest
    else:
        out_ref, gbuf = rest
    n_lt = D // LANES

    def body(r8, carry):
        base = pl.multiple_of(r8 * SUBLANES, SUBLANES)
        for k in range(SUBLANES):
            src = pl.multiple_of(pos_ref[step * TM + base + k] * n_lt, n_lt)
            gbuf[pl.ds(pl.multiple_of((base + k) * n_lt, n_lt), n_lt), :] = ys_ref[pl.ds(src, n_lt), :]
        return carry

    lax.fori_loop(0, TM // SUBLANES, body, 0)
    moe = jnp.concatenate([gbuf[pl.ds(c, TM, stride=n_lt), :] for c in range(n_lt)], axis=1)
    out = xn_ref[...] + mod_ref[0, 0][5:6] * moe
    if final:
        y = _rms(out, fg_ref[...])

        @pl.when(step < T_CTX // TM)
        def _():
            yc_ref[...] = y

        @pl.when(step >= T_CTX // TM)
        def _():
            yl_ref[...] = y
    else:
        out_ref[...] = out


def _combine(pos, ys, xn, mod, fg, l, final):
    spec = pl.BlockSpec((TM, D), lambda i, pos: (i, 0))
    n_ctx = T_CTX // TM
    if final:
        out_specs = [pl.BlockSpec((TM, D), lambda i, pos: (jnp.minimum(i, n_ctx - 1), 0)),
                     pl.BlockSpec((TM, D), lambda i, pos: (jnp.maximum(i - n_ctx, 0), 0))]
        out_shape = [jax.ShapeDtypeStruct((T_CTX, D), F32), jax.ShapeDtypeStruct((T_LAT, D), F32)]
    else:
        out_specs = [spec]
        out_shape = [jax.ShapeDtypeStruct((T_ALL, D), F32)]
    return pl.pallas_call(
        functools.partial(_combine_kernel, final=final),
        grid_spec=pltpu.PrefetchScalarGridSpec(
            num_scalar_prefetch=1,
            grid=(T_ALL // TM,),
            in_specs=[pl.BlockSpec(memory_space=pltpu.VMEM),
                      spec,
                      _mod_spec(l, TM),
                      pl.BlockSpec((1, D), lambda i, pos: (0, 0))],
            out_specs=out_specs,
            scratch_shapes=[pltpu.VMEM((TM * (D // LANES), LANES), F32)]),
        out_shape=out_shape,
        compiler_params=_cparams(("arbitrary",)),
        name="moe_combine",
    )(pos, ys, xn, mod, fg)


@functools.lru_cache(None)
def _pos_embed():
    rows = L_LAT // GRID_W
    r = np.repeat(np.arange(rows, dtype=np.float64), GRID_W)
    col = np.tile(np.arange(GRID_W, dtype=np.float64), rows)
    quarter = D // 4
    freq = np.exp(-math.log(POS_BASE) * np.arange(quarter, dtype=np.float64) / quarter)
    ar = r[:, None] * freq[None]
    ac = col[:, None] * freq[None]
    emb = np.concatenate([np.sin(ar), np.cos(ar), np.sin(ac), np.cos(ac)], axis=-1)
    return emb.astype(np.float32)


def _to_time_major_ctx(a):
    c = a.shape[-1]
    a = a.reshape(2, S5_ROWS, L_CTX, c).transpose(0, 2, 1, 3)
    return a.reshape(2, L_CTX * S5_ROWS, c)


def _from_time_major_ctx(a):
    c = a.shape[-1]
    a = a.reshape(2, L_CTX, S5_ROWS, c).transpose(0, 2, 1, 3)
    return a.reshape(T_CTX, c)


def _to_time_major_lat(a):
    c = a.shape[-1]
    a = a.reshape(N_LAT_SEQ, 4, S5_STEPS, c).transpose(2, 1, 0, 3)
    return a.reshape(1, S5_STEPS * S5_ROWS, c)


def _from_time_major_lat(a):
    c = a.shape[-1]
    a = a.reshape(S5_STEPS, 4, N_LAT_SEQ, c).transpose(2, 1, 0, 3)
    return a.reshape(T_LAT, c)


def kernel(x_prompt, x_sample, c, state_mlstm_C, state_mlstm_n, state_mlstm_m, state_s5_re, state_s5_im, c_ctx, w_ada, b_ada, norm1_g, norm2_g, final_g, w_in, w_out, hy_short, hy_fw1, hy_fb1, hy_fw2, hy_fb2, hy_fw3, hy_log_decay, hy_bias, ml_short, ml_gate_bias, ml_norm_g, s5_a_re, s5_a_im, s5_log_dt, s5_b_re, s5_b_im, s5_c_re, s5_c_im, s5_d, s5_w_glu, router_w, router_b, moe_w_gate, moe_w_up, moe_w_down):
    x = (x_prompt.reshape(T_CTX, D), x_sample.reshape(T_LAT, D), jnp.asarray(_pos_embed()))
    w_in_t = jnp.swapaxes(w_in, 1, 2)
    cc =jnp.concatenate([c_ctx[None], c, jnp.zeros((8 - 1 - N_LAT_SEQ, D), F32)], axis=0)
    mod = _ada(cc, w_ada, b_ada).reshape(DEPTH, 8, 6, D)
    rw_t = router_w.T
    rb = router_b.reshape(N_EXPERTS, 1)
    fg = final_g.reshape(1, D)
    lat_blk = T_CTX // L_LAT
    g1 = norm1_g.reshape(DEPTH, 1, D)
    g2 = norm2_g.reshape(DEPTH, 1, D)

    w1p = jnp.pad(hy_fw1, ((0, 0), (0, LANES - HY_EMB), (0, 0)))
    b1 = hy_fb1.reshape(DEPTH, 1, HY_FILTER_W)
    b2 = hy_fb2.reshape(DEPTH, 1, HY_FILTER_W)
    ld = hy_log_decay.reshape(DEPTH, 1, 4 * HY_CH)
    hy_spec = {L: _hy_filter(L, w1p, b1, hy_fw2, b2, hy_fw3, ld) for L in (L_CTX, L_LAT)}
    gb = jnp.pad(ml_gate_bias.reshape(DEPTH, 1, 16), ((0, 0), (0, 0), (0, GATE_PAD - 16)))
    gbt = ml_gate_bias.reshape(DEPTH, 16, 1)
    ng = ml_norm_g.reshape(DEPTH, 1, ML_W)
    ml_state = (state_mlstm_C.reshape(N_LAT_SEQ, DEPTH, 2 * ML_HEADS, ML_DH, ML_DH),
                state_mlstm_n.reshape(N_LAT_SEQ, DEPTH, 2 * ML_HEADS, ML_DH),
                jnp.broadcast_to(state_mlstm_m.reshape(N_LAT_SEQ, DEPTH, 2 * ML_HEADS, 1),
                                 (N_LAT_SEQ, DEPTH, 2 * ML_HEADS, LANES)))
    bb, cc_s5, lam = _s5_params(s5_a_re, s5_a_im, s5_log_dt, s5_b_re, s5_b_im, s5_c_re, s5_c_im)
    dsk = s5_d.reshape(DEPTH, 1, S5_CH)
    wglu = s5_w_glu.astype(BF16)
    s0 = jnp.concatenate([state_s5_re.reshape(N_LAT_SEQ, DEPTH, 2, S5_STATE),
                          state_s5_im.reshape(N_LAT_SEQ, DEPTH, 2, S5_STATE)], axis=-1)
    s0 = jnp.tile(s0.transpose(1, 2, 0, 3), (1, 1, 4, 1))

    new_n, new_m, new_re, new_im = [], [], [], []
    prev_c, c_all = [], None
    y_prompt = y_sample = None
    for l in range(DEPTH):
        u_hy, qk, v, o, u_s5, gates, gates_t = _inproj(x, g1, mod, w_in_t, l)

        y_hy = [_hyena(u_hy, L, n_seq, blk0, hy_short, hy_bias, *hy_spec[L], l)
                for L, n_seq, blk0 in ((L_CTX, N_CTX_SEQ, 0), (L_LAT, N_LAT_SEQ, lat_blk))]

        yc, c_all, nc_, mc_ = _mlstm(qk, v, o, gates, gates_t, L_CTX, N_CTX_SEQ, 0, ml_short, gb, gbt, ng, None, l,
                                     emit_state=True, prev_c=prev_c)
        prev_c = [c_all[:, k] for k in range(l + 1)] if l + 1 < DEPTH else None
        (yl,) = _mlstm(qk, v, o, gates, gates_t, L_LAT, N_LAT_SEQ, lat_blk, ml_short, gb, gbt, ng, ml_state, l)
        y_ml = (yc, yl)
        new_n.append(nc_.reshape(N_CTX_SEQ, 2, ML_HEADS, ML_DH))
        new_m.append(mc_[:, :, 0].reshape(N_CTX_SEQ, 2, ML_HEADS))

        ys_c, fin = _s5(_to_time_major_ctx(u_s5[:T_CTX]), bb, cc_s5, lam, dsk, wglu, None, l)
        (ys_l,) = _s5(_to_time_major_lat(u_s5[T_CTX:]), bb, cc_s5, lam, dsk, wglu, s0, l)
        y_s5 = (_from_time_major_ctx(ys_c), _from_time_major_lat(ys_l))
        fin = fin.transpose(0, 2, 1, 3).reshape(N_CTX_SEQ, 2, 2 * S5_STATE)
        new_re.append(fin[..., :S5_STATE].reshape(N_CTX_SEQ, 2, S5_G, S5_P))
        new_im.append(fin[..., S5_STATE:].reshape(N_CTX_SEQ, 2, S5_G, S5_P))

        xn, h2e, best, rank, cnt = _outproj(x, y_hy, y_ml, y_s5, w_out, l, mod, g2, rw_t, rb)
        pos, src, tile_group, n_tiles = _dispatch(best, rank, cnt)
        ys = _moe(h2e, src, tile_group, n_tiles, moe_w_gate, moe_w_up, moe_w_down, l)
        res = _combine(pos, ys, xn, mod, fg, l, l == DEPTH - 1)
        if l == DEPTH - 1:
            y_prompt = res[0].reshape(N_CTX_SEQ, L_CTX, D)
            y_sample = res[1].reshape(N_LAT_SEQ, L_LAT, D)
        else:
            x = (res[0],)

    new_c = c_all.reshape(N_CTX_SEQ, DEPTH, 2, ML_HEADS, ML_DH, ML_DH)
    return (y_prompt, y_sample, new_c, jnp.stack(new_n, axis=1), jnp.stack(new_m, axis=1),
            jnp.stack(new_re, axis=1), jnp.stack(new_im, axis=1))
```

```python
import functools
import math

import numpy as np
import jax
import jax.numpy as jnp
from jax import lax
from jax.experimental import pallas as pl
from jax.experimental.pallas import tpu as pltpu

F32 = jnp.float32
BF16 = jnp.bfloat16

D = 1024
N_CTX_SEQ, L_CTX = 16, 256
N_LAT_SEQ, L_LAT = 2, 1024
T_CTX = N_CTX_SEQ * L_CTX
T_LAT = N_LAT_SEQ * L_LAT
T_ALL = T_CTX + T_LAT
DEPTH = 2
EPS = 1e-6
GRID_W = 64
POS_BASE = 10000.0
HY_CH = 256
HY_EMB = 33
HY_FILTER_W = 64
ML_HEADS = 4
ML_DH = 128
ML_W = ML_HEADS * ML_DH
S5_CH = 256
S5_G = 16
S5_GROUP = 16
S5_P = 64
S5_STATE = S5_G * S5_P
N_EXPERTS = 16
N_GROUPS = 4
GROUP_SIZE = N_EXPERTS // N_GROUPS
D_EXPERT = 512
OFF_HY = 0
OFF_QK = 3 * HY_CH
OFF_V = OFF_QK + 2 * ML_W
OFF_O = OFF_V + ML_W
OFF_G = OFF_O + ML_W
OFF_S5 = OFF_G + 16
IN_W = OFF_S5 + S5_CH
LANES = 128
SUBLANES = 8
GATE_PAD = LANES

TM = 512
TM_MOE = 512
MOE_SLOTS = T_ALL + N_GROUPS * TM_MOE
MOE_SUB = 512
MOE_HID = 256
HY_ROWS = 1024
HY_FREQ = 256
CONV_ROWS = 128
ML_CHUNK = 256
S5_ROWS = 8
S5_STEPS = 256
VMEM_LIMIT = 56 * 1024 * 1024


def _cparams(sem, vmem=VMEM_LIMIT):
    if sem is None:
        return pltpu.CompilerParams(vmem_limit_bytes=vmem)
    return pltpu.CompilerParams(dimension_semantics=sem, vmem_limit_bytes=vmem)


def _bdot(a, b):
    return jnp.dot(a.astype(BF16), b.astype(BF16), preferred_element_type=F32)


def _split_bf16(x):
    hi = x.astype(BF16)
    return hi, (x - hi.astype(F32)).astype(BF16)


def _dot3(a, b):
    a_hi, a_lo = _split_bf16(a)
    b_hi, b_lo = _split_bf16(b)
    dot = functools.partial(jnp.dot, preferred_element_type=F32)
    return dot(a_hi, b_hi) + (dot(a_hi, b_lo) + dot(a_lo, b_hi))


def _silu(x):
    return x * jax.nn.sigmoid(x)


def _rms(x, g):
    return x * lax.rsqrt(jnp.mean(x * x, axis=-1, keepdims=True) + EPS) * g


def _log_sigmoid(x):
    return jnp.minimum(x, 0.0) - jnp.log1p(jnp.exp(-jnp.abs(x)))


def _conv3(u, w, n_rows, seq_len):
    row = lax.broadcasted_iota(jnp.int32, (n_rows, 1), 0) % seq_len
    prev = jnp.where(row == 0, 0.0, pltpu.roll(u, 1, 0))
    nxt = jnp.where(row == seq_len - 1, 0.0, pltpu.roll(u, n_rows - 1, 0))
    return prev * w[0:1] + u * w[1:2] + nxt * w[2:3]


def _ada_kernel(c_ref, w_ref, b_ref, o_ref):
    o_ref[0] = _bdot(_silu(c_ref[...]), w_ref[0]) + b_ref[0]


def _ada(cc, w_ada, b_ada):
    tn = 1536
    return pl.pallas_call(
        _ada_kernel,
        grid=(DEPTH, 6 * D // tn),
        in_specs=[pl.BlockSpec((8, D), lambda l, j: (0, 0)),
                  pl.BlockSpec((1, D, tn), lambda l, j: (l, 0, j)),
                  pl.BlockSpec((1, 1, tn), lambda l, j: (l, 0, j))],
        out_specs=pl.BlockSpec((1, 8, tn), lambda l, j: (l, 0, j)),
        out_shape=jax.ShapeDtypeStruct((DEPTH, 8, 6 * D), F32),
        compiler_params=_cparams(("arbitrary", "arbitrary")),
        name="ada_mod",
    )(cc, w_ada, b_ada.reshape(DEPTH, 1, 6 * D))


def _mod_row(i, tm):
    n_ctx = T_CTX // tm
    return jnp.where(i < n_ctx, 0, 1 + (i - n_ctx) // (L_LAT // tm))


_SEG = ((OFF_HY, OFF_QK - OFF_HY), (OFF_QK, OFF_V - OFF_QK), (OFF_V, OFF_O - OFF_V), (OFF_O, OFF_G - OFF_O))
TAIL_W = IN_W - OFF_G


N_CTX_TILES = T_CTX // TM
_NT = (((1,), (1,)), ((), ()))


def _x_specs(x):
    if len(x) == 1:
        return [pl.BlockSpec((TM, D), lambda i, *_: (i, 0))]
    per_seq = L_LAT // TM
    return [pl.BlockSpec((TM, D), lambda i, *_: (jnp.minimum(i, N_CTX_TILES - 1), 0)),
            pl.BlockSpec((TM, D), lambda i, *_: (jnp.maximum(i - N_CTX_TILES, 0), 0)),
            pl.BlockSpec((TM, D), lambda i, *_: (jnp.maximum(i - N_CTX_TILES, 0) % per_seq, 0))]


def _x_tile(step, x_refs):
    if len(x_refs) == 1:
        return x_refs[0][...]
    xc_ref, xl_ref, pos_ref = x_refs
    return jnp.where(step < N_CTX_TILES, xc_ref[...], xl_ref[...] + pos_ref[...])


def _inproj_kernel(*refs, n_x):
    x_refs = refs[:n_x]
    g_ref, mod_ref, w_ref, hy_ref, qk_ref, v_ref, o_ref, s5_ref, gt_ref, gtt_ref, wb = refs[n_x:]
    step = pl.program_id(0)

    @pl.when(step == 0)
    def _():
        wb[...] = w_ref[0].astype(BF16)

    mod = mod_ref[0, 0]
    h = _rms(_x_tile(step, x_refs), g_ref[0]) * (1.0 + mod[1:2]) + mod[0:1]
    hb = h.astype(BF16)
    for (a, w), ref in zip(_SEG, (hy_ref, qk_ref, v_ref, o_ref)):
        ref[...] = lax.dot_general(hb, wb[a:a + w, :], _NT, preferred_element_type=F32)
    tail = lax.dot_general(hb, wb[OFF_G:IN_W, :], _NT, preferred_element_type=F32)
    gates = tail[:, 0:GATE_PAD]
    gt_ref[...] = gates
    for c in range(TM // ML_CHUNK):
        gtt_ref[c] = gates[c * ML_CHUNK:(c + 1) * ML_CHUNK, :].T[0:16, :]
    s5_ref[...] = tail[:, OFF_S5 - OFF_G:TAIL_W]


def _layer_spec(shape, l):
    return pl.BlockSpec((1,) + tuple(shape), lambda *_: (l,) + (0,) * len(shape))


def _mod_spec(l, tm):
    return pl.BlockSpec((1, 1, 6, D), lambda i, *_: (l, _mod_row(i, tm), 0, 0))


def _inproj(x, g, mod, w_in_t, l):
    widths = [w for _, w in _SEG] + [S5_CH, GATE_PAD]
    cpt = TM // ML_CHUNK
    return pl.pallas_call(
        functools.partial(_inproj_kernel, n_x=len(x)),
        grid=(T_ALL // TM,),
        in_specs=_x_specs(x) + [
            _layer_spec((1, D), l), _mod_spec(l, TM),
            pl.BlockSpec((1, IN_W, D), lambda i: (l, 0, 0), pipeline_mode=pl.Buffered(1))],
        out_specs=[pl.BlockSpec((TM, w), lambda i: (i, 0)) for w in widths]
        + [pl.BlockSpec((cpt, 16, ML_CHUNK), lambda i: (i, 0, 0))],
        out_shape=[jax.ShapeDtypeStruct((T_ALL, w), F32) for w in widths]
        + [jax.ShapeDtypeStruct((T_ALL // ML_CHUNK, 16, ML_CHUNK), F32)],
        scratch_shapes=[pltpu.VMEM((IN_W, D), BF16)],
        compiler_params=_cparams(("arbitrary",)),
        name="norm_inproj",
    )(*x, g, mod, w_in_t)


@functools.lru_cache(None)
def _dft_mats(L):
    n = 2 * L
    k = np.arange(L)[:, None]
    t = np.arange(L)[None, :]
    ang = 2.0 * np.pi * ((k * t) % n) / n
    top = np.cos(ang)
    bot = -np.sin(ang)
    bot[0] = np.cos(np.pi * np.arange(L))
    fwd = np.concatenate([top, bot], 0)
    s = np.full((n, 1), 2.0 / n)
    s[0] = s[L] = 1.0 / n
    inv = (fwd * s).T
    return fwd.astype(np.float32), inv.astype(np.float32)


@functools.lru_cache(None)
def _hy_positions(L):
    t = np.linspace(0.0, 1.0, L)
    bands = (HY_EMB - 1) // 2
    f = np.linspace(1e-4, bands - 1, bands)
    w = 2.0 * np.pi * np.arange(L) / L
    ang = w[:, None] * f[None, :]
    z = np.concatenate([t[:, None], np.cos(ang), -np.sin(ang)], -1)
    zp = np.zeros((L, LANES))
    zp[:, :HY_EMB] = z
    return zp.astype(np.float32), t[:, None].astype(np.float32)


def _hy_filter_kernel(z_ref, t_ref, w1_ref, b1_ref, w2_ref, b2_ref, w3_ref, ld_ref, f_ref,
                      p_ref, q_ref, r_ref, *, L):
    h = jnp.sin(_dot3(z_ref[...], w1_ref[0]) + b1_ref[0])
    h = jnp.sin(_dot3(h, w2_ref[0]) + b2_ref[0])
    filt = _dot3(h, w3_ref[0])
    filt = filt * jnp.exp(-t_ref[...] * jnp.exp(ld_ref[0]))
    c = HY_CH
    h_fwd = jnp.concatenate([filt[:, 0:c], filt[:, 2 * c:3 * c]], axis=1)
    h_bwd = jnp.concatenate([filt[:, c:2 * c], filt[:, 3 * c:4 * c]], axis=1)
    row = lax.broadcasted_iota(jnp.int32, (L, 1), 0)
    h_bwd = jnp.where(row == 0, 0.0, h_bwd)
    h_sum = (h_fwd + h_bwd).astype(BF16)
    h_dif = (h_fwd - h_bwd).astype(BF16)
    re = jnp.dot(f_ref[0:L, :], h_sum, preferred_element_type=F32)
    im = jnp.dot(f_ref[L:2 * L, :], h_dif, preferred_element_type=F32)
    nyq = jnp.dot(f_ref[L:L + SUBLANES, :], h_sum, preferred_element_type=F32)[0:1]
    p_ref[0] = re
    q_ref[0] = jnp.where(row == 0, 0.0, im)
    r_ref[0] = jnp.where(row == 0, nyq, re)


def _hy_filter(L, w1p, b1, w2, b2, w3, ld):
    z, t = _hy_positions(L)
    fwd = jnp.asarray(_dft_mats(L)[0]).astype(BF16)
    out = jax.ShapeDtypeStruct((DEPTH, L, 2 * HY_CH), F32)
    full = lambda a: pl.BlockSpec(a.shape, lambda l: (0,) * a.ndim)
    layer = lambda a: pl.BlockSpec((1,) + a.shape[1:], lambda l: (l,) + (0,) * (a.ndim - 1))
    return pl.pallas_call(
        functools.partial(_hy_filter_kernel, L=L),
        grid=(DEPTH,),
        in_specs=[full(z), full(t), layer(w1p), layer(b1), layer(w2), layer(b2), layer(w3), layer(ld), full(fwd)],
        out_specs=[pl.BlockSpec((1, L, 2 * HY_CH), lambda l: (l, 0, 0))] * 3,
        out_shape=[out, out, out],
        compiler_params=_cparams(("arbitrary",)),
        name=f"hyena_filter_{L}",
    )(z, t, w1p, b1, w2, b2, w3, ld, fwd)


def _hyena_kernel(u_ref, sw_ref, bias_ref, p_ref, q_ref, r_ref, f_ref, g_ref, o_ref, *, L, n_sub):
    c = HY_CH
    u = _conv3(u_ref[...], sw_ref[0], n_sub * L, L)
    for s in range(n_sub):
        rs = slice(s * L, (s + 1) * L)
        z = u[rs, 0:c]
        for o in range(2):
            gate = u[rs, (o + 1) * c:(o + 2) * c]
            zb = z.astype(BF16)
            y = None
            for k0 in range(0, L, HY_FREQ):
                ks = slice(k0, k0 + HY_FREQ)
                a = jnp.dot(f_ref[k0:k0 + HY_FREQ, :], zb, preferred_element_type=F32)
                b = jnp.dot(f_ref[L + k0:L + k0 + HY_FREQ, :], zb, preferred_element_type=F32)
                p = p_ref[0, ks, o * c:(o + 1) * c]
                q = q_ref[0, ks, o * c:(o + 1) * c]
                r = r_ref[0, ks, o * c:(o + 1) * c]
                y_re = (a * p - b * q).astype(BF16)
                y_im = (a * q + b * r).astype(BF16)
                part = (jnp.dot(g_ref[:, k0:k0 + HY_FREQ], y_re, preferred_element_type=F32)
                        + jnp.dot(g_ref[:, L + k0:L + k0 + HY_FREQ], y_im, preferred_element_type=F32))
                y = part if y is None else y + part
            z = gate * (y + bias_ref[0, o:o + 1, :] * z)
        o_ref[rs, :] = z


def _hyena(u_hy, L, n_seq, row_block0, sw, bias, p, q, r, l):
    n_sub = max(1, HY_ROWS // L)
    fwd, inv = (jnp.asarray(m).astype(BF16) for m in _dft_mats(L))
    full = lambda a: pl.BlockSpec(a.shape, lambda b: (0,) * a.ndim)
    layer = lambda a: _layer_spec(a.shape[1:], l)
    blk0 = row_block0 // n_sub
    return pl.pallas_call(
        functools.partial(_hyena_kernel, L=L, n_sub=n_sub),
        grid=(n_seq // n_sub,),
        in_specs=[pl.BlockSpec((n_sub * L, 3 * HY_CH), lambda b: (blk0 + b, 0)),
                  layer(sw), layer(bias), layer(p), layer(q), layer(r), full(fwd), full(inv)],
        out_specs=pl.BlockSpec((n_sub * L, HY_CH), lambda b: (b, 0)),
        out_shape=jax.ShapeDtypeStruct((n_seq * L, HY_CH), F32),
        compiler_params=_cparams(("arbitrary",)),
        name=f"hyena_{L}",
    )(u_hy, sw, bias, p, q, r, fwd, inv)


def _cumsum_rows(x, n, reverse):
    row = lax.broadcasted_iota(jnp.int32, (n, 1), 0)
    s = 1
    while s < n:
        if reverse:
            x = x + jnp.where(row < n - s, pltpu.roll(x, n - s, 0), 0.0)
        else:
            x = x + jnp.where(row >= s, pltpu.roll(x, s, 0), 0.0)
        s *= 2
    return x


def _cumsum_lanes(x, n, reverse):
    col = lax.broadcasted_iota(jnp.int32, (1, n), 1)
    s = 1
    while s < n:
        if reverse:
            x = x + jnp.where(col < n - s, pltpu.roll(x, n - s, 1), 0.0)
        else:
            x = x + jnp.where(col >= s, pltpu.roll(x, s, 1), 0.0)
        s *= 2
    return x


def _mlstm_kernel(*refs, L, has_state, emit_state, n_prev):
    qk_ref, v_ref, o_ref, g_ref, gt_ref, sw_ref, gb_ref, gbt_ref, ng_ref = refs[:9]
    refs = refs[9:]
    if has_state:
        c0_ref, n0_ref, m0_ref = refs[:3]
        refs = refs[3:]
    prev_c_refs = refs[:n_prev]
    refs = refs[n_prev:]
    y_ref = refs[0]
    if emit_state:
        cout_ref, nout_ref, mout_ref = refs[1:4]
        refs = refs[4:]
    else:
        refs = refs[1:]
    q_s, k_s, vt_s, ht_s, ct_s, n_s, m_s, lf_s, lft_s = refs
    tc = ML_CHUNK
    nc = L // tc
    nh = ML_HEADS
    dh = ML_DH

    lf_s[...] = _log_sigmoid(g_ref[...] + gb_ref[0])
    lft_s[...] = _log_sigmoid(gt_ref[...] + gbt_ref[0])

    sw = sw_ref[0]
    for r0 in range(0, L, CONV_ROWS):
        a, b = max(r0 - SUBLANES, 0), min(r0 + CONV_ROWS + SUBLANES, L)
        n = b - a
        u = qk_ref[a:b, :]
        row = lax.broadcasted_iota(jnp.int32, (n, 1), 0) + a
        prev = jnp.where(row == 0, 0.0, pltpu.roll(u, 1, 0))
        nxt = jnp.where(row == L - 1, 0.0, pltpu.roll(u, n - 1, 0))
        y = _silu(prev * sw[0:1] + u * sw[1:2] + nxt * sw[2:3])[r0 - a:r0 - a + CONV_ROWS]
        q_s[r0:r0 + CONV_ROWS, :] = y[:, :ML_W].astype(BF16)
        k_s[r0:r0 + CONV_ROWS, :] = (y[:, ML_W:] * (dh ** -0.5)).astype(BF16)
    for c in range(nc):
        vt_s[c] = v_ref[c * tc:(c + 1) * tc, :].T.astype(BF16)

    for i in range(2 * nh):
        ct_s[i] = c0_ref[0, 0, i].T if has_state else jnp.zeros((dh, dh), F32)
    n_s[...] = n0_ref[0, 0] if has_state else jnp.zeros_like(n_s)
    m_s[...] = m0_ref[0, 0] if has_state else jnp.zeros_like(m_s)

    si = lax.broadcasted_iota(jnp.int32, (tc, tc), 0)
    ti = lax.broadcasted_iota(jnp.int32, (tc, tc), 1)

    for d in range(2):
        rev = d == 1
        mask = (si >= ti) if rev else (si <= ti)
        edge = 0 if rev else tc - 1

        def chunk(j, carry, d=d, rev=rev, mask=mask, edge=edge):
            cidx = (nc - 1 - j) if rev else j
            r0 = pl.multiple_of(cidx * tc, tc)
            pre = g_ref[pl.ds(r0, tc), :] + gb_ref[0]
            pre_t = gt_ref[cidx] + gbt_ref[0]
            cum = _cumsum_rows(lf_s[pl.ds(r0, tc), :], tc, rev)
            cum_t = _cumsum_lanes(lft_s[cidx], tc, rev)
            key_all = cum - pltpu.roll(pre, 8, 1)
            for h in range(nh):
                col = d * nh + h
                hs = slice(h * dh, (h + 1) * dh)
                key = key_all[:, 8 + col:9 + col]
                b_row = cum_t[8 + col:9 + col, :]
                b_end = b_row[:, edge:edge + 1]
                m_prev = m_s[col:col + 1, 0:1]
                dmat = jnp.where(mask, b_row - key, -jnp.inf)
                inter = b_row + m_prev
                m_row = jnp.maximum(inter, jnp.max(dmat, axis=0, keepdims=True))
                w_intra = jnp.exp(dmat - m_row)
                w_state = jnp.exp(inter - m_row)
                qh = q_s[pl.ds(r0, tc), hs]
                kh = k_s[pl.ds(r0, tc), hs]
                vt = vt_s[cidx, hs, :]
                ct_prev = ct_s[col]
                n_prev = n_s[col:col + 1, :]
                s = lax.dot_general(kh, qh, _NT, preferred_element_type=F32) * w_intra
                num = (jnp.dot(vt, s.astype(BF16), preferred_element_type=F32)
                       + w_state * lax.dot_general(ct_prev.astype(BF16), qh, _NT, preferred_element_type=F32))
                qn = lax.dot_general(jnp.broadcast_to(n_prev, (8, dh)).astype(BF16), qh, _NT,
                                     preferred_element_type=F32)[0:1]
                den = jnp.sum(s, axis=0, keepdims=True) + w_state * qn
                hout = num * (1.0 / jnp.maximum(jnp.abs(den), jnp.exp(-m_row)))
                if d == 0:
                    ht_s[cidx, hs, :] = hout
                else:
                    ht_s[cidx, hs, :] += hout
                m_new = jnp.maximum(b_end + m_prev, b_end - jnp.min(key, axis=0, keepdims=True))
                wg = jnp.exp(b_end - key - m_new)
                decay = jnp.exp(b_end + m_prev - m_new)
                kw = kh.astype(F32) * wg
                ct_s[col] = decay * ct_prev + jnp.dot(vt, kw.astype(BF16), preferred_element_type=F32)
                n_s[col:col + 1, :] = decay * n_prev + jnp.sum(kw, axis=0, keepdims=True)
                m_s[col:col + 1, :] = jnp.broadcast_to(m_new, (1, LANES))
            return carry

        lax.fori_loop(0, nc, chunk, 0)

    for c in range(nc):
        for h in range(nh):
            hs = slice(h * dh, (h + 1) * dh)
            rs = slice(c * tc, (c + 1) * tc)
            ht = ht_s[c, hs, :]
            hn = ht * lax.rsqrt(jnp.mean(ht * ht, axis=0, keepdims=True) + EPS)
            y_ref[rs, hs] = jax.nn.sigmoid(o_ref[rs, hs]) * (hn.T * ng_ref[0, :, hs])
    if emit_state:
        for k, prev_ref in enumerate(prev_c_refs):
            cout_ref[0, k] = prev_ref[0]
        for i in range(2 * nh):
            cout_ref[0, n_prev, i] = ct_s[i].T
        nout_ref[0] = n_s[...]
        mout_ref[0] = m_s[...]


def _mlstm(qk, v, o, gates, gates_t, L, n_seq, row_block0, sw, gb, gbt, ng, state, l, emit_state=False,
           prev_c=()):
    nc = L // ML_CHUNK
    has_state = state is not None
    layer = lambda a: _layer_spec(a.shape[1:], l)
    in_specs = [pl.BlockSpec((L, 2 * ML_W), lambda b: (row_block0 + b, 0)),
                pl.BlockSpec((L, ML_W), lambda b: (row_block0 + b, 0)),
                pl.BlockSpec((L, ML_W), lambda b: (row_block0 + b, 0)),
                pl.BlockSpec((L, GATE_PAD), lambda b: (row_block0 + b, 0)),
                pl.BlockSpec((nc, 16, ML_CHUNK), lambda b: (row_block0 + b, 0, 0)),
                layer(sw), layer(gb), layer(gbt), layer(ng)]
    args = [qk, v, o, gates, gates_t, sw, gb, gbt, ng]
    if has_state:
        c0, n0, m0 = state
        in_specs += [pl.BlockSpec((1, 1, 2 * ML_HEADS, ML_DH, ML_DH), lambda b: (b, l, 0, 0, 0)),
                     pl.BlockSpec((1, 1, 2 * ML_HEADS, ML_DH), lambda b: (b, l, 0, 0)),
                     pl.BlockSpec((1, 1, 2 * ML_HEADS, LANES), lambda b: (b, l, 0, 0))]
        args += [c0, n0, m0]
    out_specs = [pl.BlockSpec((L, ML_W), lambda b: (b, 0))]
    out_shape = [jax.ShapeDtypeStruct((n_seq * L, ML_W), F32)]
    if emit_state:
        c_tail = (2 * ML_HEADS, ML_DH, ML_DH)
        for a in prev_c:
            in_specs.append(pl.BlockSpec((1,) + c_tail, lambda b: (b, 0, 0, 0)))
            args.append(a)
        tails = ((len(prev_c) + 1,) + c_tail, (2 * ML_HEADS, ML_DH), (2 * ML_HEADS, LANES))
        for t in tails:
            out_specs.append(pl.BlockSpec((1,) + t, lambda b, n=len(t): (b,) + (0,) * n))
            out_shape.append(jax.ShapeDtypeStruct((n_seq,) + t, F32))
    return pl.pallas_call(
        functools.partial(_mlstm_kernel, L=L, has_state=has_state, emit_state=emit_state,
                          n_prev=len(prev_c) if emit_state else 0),
        grid=(n_seq,),
        in_specs=in_specs,
        out_specs=out_specs,
        out_shape=out_shape,
        scratch_shapes=[pltpu.VMEM((L, ML_W), BF16), pltpu.VMEM((L, ML_W), BF16),
                        pltpu.VMEM((nc, ML_W, ML_CHUNK), BF16),
                        pltpu.VMEM((nc, ML_W, ML_CHUNK), F32),
                        pltpu.VMEM((2 * ML_HEADS, ML_DH, ML_DH), F32),
                        pltpu.VMEM((2 * ML_HEADS, ML_DH), F32),
                        pltpu.VMEM((2 * ML_HEADS, LANES), F32),
                        pltpu.VMEM((L, GATE_PAD), F32),
                        pltpu.VMEM((nc, 16, ML_CHUNK), F32)],
        compiler_params=_cparams(("arbitrary",)),
        name=f"mlstm_{L}",
    )(*args)


def _cmul(ar, ai, br, bi):
    return ar * br - ai * bi, ar * bi + ai * br


def _s5_kernel(*refs, segmented):
    if segmented:
        (u_ref, bb_ref, cc_ref, eb_ref, ec_ref, lam_ref, dsk_ref, wglu_ref, s0_ref, y_ref,
         sbuf, yacc, bmat, cmat, pw) = refs
    else:
        (u_ref, bb_ref, cc_ref, eb_ref, ec_ref, lam_ref, dsk_ref, wglu_ref, y_ref, fin_ref,
         sbuf, yacc, bmat, cmat) = refs
    n = S5_STATE
    rows = S5_ROWS
    steps = S5_STEPS
    blk = 256
    n_blk = steps * rows // blk
    n_seg = 4

    yacc[...] = u_ref[0] * dsk_ref[0]
    ub = u_ref[0].astype(BF16)
    b_keep = (lax.broadcasted_iota(jnp.int32, (S5_CH, 2 * n), 0) // S5_GROUP
              == (lax.broadcasted_iota(jnp.int32, (S5_CH, 2 * n), 1) % n) // S5_P)
    c_keep = ((lax.broadcasted_iota(jnp.int32, (2 * n, S5_CH), 0) % n) // S5_P
              == lax.broadcasted_iota(jnp.int32, (2 * n, S5_CH), 1) // S5_GROUP)

    for d in range(2):
        rev = d == 1
        bmat[...] = jnp.where(b_keep, jnp.dot(bb_ref[0, d].astype(BF16), eb_ref[...], preferred_element_type=F32),
                              0.0).astype(BF16)
        cmat[...] = jnp.where(c_keep, jnp.dot(cc_ref[0, d].astype(BF16), ec_ref[...], preferred_element_type=F32),
                              0.0).astype(BF16)
        for i in range(n_blk):
            sbuf[i * blk:(i + 1) * blk, :] = jnp.dot(ub[i * blk:(i + 1) * blk], bmat[...],
                                                     preferred_element_type=F32)
        lam = lam_ref[0, d]
        lr = jnp.broadcast_to(lam[:, :n], (rows, n))
        li = jnp.broadcast_to(lam[:, n:], (rows, n))

        def step(i, carry, rev=rev, lr=lr, li=li):
            sr, si = carry
            t = (steps - 1 - i) if rev else i
            off = pl.multiple_of(t * rows, rows)
            pr, pi = _cmul(lr, li, sr, si)
            nr = pr + sbuf[pl.ds(off, rows), 0:n]
            ni = pi + sbuf[pl.ds(off, rows), n:2 * n]
            sbuf[pl.ds(off, rows), 0:n] = nr
            sbuf[pl.ds(off, rows), n:2 * n] = ni
            return nr, ni

        zero = jnp.zeros((rows, n), F32)
        sr, si = lax.fori_loop(0, steps, step, (zero, zero), unroll=4)

        if not segmented:
            fin_ref[0, d, :, 0:n] = sr
            fin_ref[0, d, :, n:2 * n] = si
        else:
            lam_r, lam_i = lam[:, :n], lam[:, n:]
            row8 = lax.broadcasted_iota(jnp.int32, (rows, 1), 0)
            cr, ci = lam_r, lam_i
            acc_r = jnp.broadcast_to(cr, (rows, n))
            acc_i = jnp.broadcast_to(ci, (rows, n))
            for j in range(1, rows):
                cr, ci = _cmul(cr, ci, lam_r, lam_i)
                acc_r = jnp.where(row8 >= j, jnp.broadcast_to(cr, (rows, n)), acc_r)
                acc_i = jnp.where(row8 >= j, jnp.broadcast_to(ci, (rows, n)), acc_i)
            pw[0:rows, 0:n] = acc_r
            pw[0:rows, n:2 * n] = acc_i
            size = rows
            while size < steps:
                tr = pw[size - 1:size, 0:n]
                ti = pw[size - 1:size, n:2 * n]
                xr, xi = _cmul(pw[0:size, 0:n], pw[0:size, n:2 * n], tr, ti)
                pw[size:2 * size, 0:n] = xr
                pw[size:2 * size, n:2 * n] = xi
                size *= 2
            end_off = 0 if rev else (steps - 1) * rows
            loc_r = sbuf[end_off:end_off + rows, 0:n]
            loc_i = sbuf[end_off:end_off + rows, n:2 * n]
            pl_r = pw[steps - 1:steps, 0:n]
            pl_i = pw[steps - 1:steps, n:2 * n]
            s0r = s0_ref[0, d, :, 0:n]
            s0i = s0_ref[0, d, :, n:2 * n]
            seg = row8 // 2
            first = (seg == n_seg - 1) if rev else (seg == 0)
            shift = (rows - 2) if rev else 2
            cin_r, cin_i = s0r, s0i
            for _ in range(n_seg - 1):
                fr, fi = _cmul(jnp.broadcast_to(pl_r, (rows, n)), jnp.broadcast_to(pl_i, (rows, n)), cin_r, cin_i)
                tru_r = loc_r + fr
                tru_i = loc_i + fi
                cin_r = jnp.where(first, s0r, pltpu.roll(tru_r, shift, 0))
                cin_i = jnp.where(first, s0i, pltpu.roll(tru_i, shift, 0))

            def fix(tb, carry, rev=rev, cin_r=cin_r, cin_i=cin_i):
                pb = (steps // rows - 1 - tb) if rev else tb
                poff = pl.multiple_of(pb * rows, rows)
                p_r = pw[pl.ds(poff, rows), 0:n]
                p_i = pw[pl.ds(poff, rows), n:2 * n]
                for j in range(rows):
                    jj = rows - 1 - j if rev else j
                    off = pl.multiple_of((tb * rows + j) * rows, rows)
                    fr, fi = _cmul(jnp.broadcast_to(p_r[jj:jj + 1], (rows, n)),
                                   jnp.broadcast_to(p_i[jj:jj + 1], (rows, n)), cin_r, cin_i)
                    sbuf[pl.ds(off, rows), 0:n] += fr
                    sbuf[pl.ds(off, rows), n:2 * n] += fi
                return carry

            lax.fori_loop(0, steps // rows, fix, 0)

        for i in range(n_blk):
            yacc[i * blk:(i + 1) * blk, :] += jnp.dot(sbuf[i * blk:(i + 1) * blk, :].astype(BF16), cmat[...],
                                                      preferred_element_type=F32)

    g = jax.nn.gelu(yacc[...], approximate=True)
    y_ref[0] = g * jax.nn.sigmoid(_bdot(g, wglu_ref[0]))


@functools.lru_cache(None)
def _s5_spread():
    eb = np.zeros((2 * S5_P, 2 * S5_STATE), np.float32)
    for half in range(2):
        for g in range(S5_G):
            c0 = half * S5_STATE + g * S5_P
            eb[half * S5_P:(half + 1) * S5_P, c0:c0 + S5_P] = np.eye(S5_P)
    ec = np.zeros((LANES, S5_CH), np.float32)
    for g in range(S5_G):
        ec[:S5_GROUP, g * S5_GROUP:(g + 1) * S5_GROUP] = np.eye(S5_GROUP)
    return eb, ec


def _s5(u_tm, bb, cc, lam, dskip, wglu, s0, l):
    n_grp = u_tm.shape[0]
    n_rows = S5_STEPS * S5_ROWS
    segmented = s0 is not None
    eb, ec = (jnp.asarray(m).astype(BF16) for m in _s5_spread())
    full = lambda a: pl.BlockSpec(a.shape, lambda g: (0,) * a.ndim)
    layer = lambda a: _layer_spec(a.shape[1:], l)
    in_specs = [pl.BlockSpec((1, n_rows, S5_CH), lambda g: (g, 0, 0)),
                layer(bb), layer(cc), full(eb), full(ec), layer(lam), layer(dskip), layer(wglu)]
    args = [u_tm, bb, cc, eb, ec, lam, dskip, wglu]
    out_specs = [pl.BlockSpec((1, n_rows, S5_CH), lambda g: (g, 0, 0))]
    out_shape = [jax.ShapeDtypeStruct((n_grp, n_rows, S5_CH), F32)]
    scratch = [pltpu.VMEM((n_rows, 2 * S5_STATE), F32), pltpu.VMEM((n_rows, S5_CH), F32),
               pltpu.VMEM((S5_CH, 2 * S5_STATE), BF16), pltpu.VMEM((2 * S5_STATE, S5_CH), BF16)]
    if segmented:
        in_specs.append(layer(s0))
        args.append(s0)
        scratch.append(pltpu.VMEM((S5_STEPS, 2 * S5_STATE), F32))
    else:
        out_specs.append(pl.BlockSpec((1, 2, S5_ROWS, 2 * S5_STATE), lambda g: (g, 0, 0, 0)))
        out_shape.append(jax.ShapeDtypeStruct((n_grp, 2, S5_ROWS, 2 * S5_STATE), F32))
    return pl.pallas_call(
        functools.partial(_s5_kernel, segmented=segmented),
        grid=(n_grp,),
        in_specs=in_specs,
        out_specs=out_specs,
        out_shape=out_shape,
        scratch_shapes=scratch,
        compiler_params=_cparams(("arbitrary",)),
        name="s5_seg" if segmented else "s5_ctx",
    )(*args)


def _s5_params(a_re, a_im, log_dt, b_re, b_im, c_re, c_im):
    dt = jnp.exp(log_dt)[..., None]
    mag = jnp.exp(a_re * dt)
    lb_re = mag * jnp.cos(a_im * dt)
    lb_im = mag * jnp.sin(a_im * dt)
    den = a_re * a_re + a_im * a_im
    nr, ni = lb_re - 1.0, lb_im
    k_re = (nr * a_re + ni * a_im) / den
    k_im = (ni * a_re - nr * a_im) / den
    bb_re = k_re[..., None] * b_re - k_im[..., None] * b_im
    bb_im = k_re[..., None] * b_im + k_im[..., None] * b_re
    to_gc_p = lambda m: m.transpose(0, 1, 2, 4, 3).reshape(DEPTH, 2, S5_CH, S5_P)
    bb = jnp.concatenate([to_gc_p(bb_re), to_gc_p(bb_im)], axis=3)
    to_gp_c = lambda m: m.transpose(0, 1, 2, 4, 3).reshape(DEPTH, 2, S5_STATE, S5_GROUP)
    cc = jnp.concatenate([to_gp_c(c_re), -to_gp_c(c_im)], axis=2)
    cc = jnp.pad(cc, ((0, 0), (0, 0), (0, 0), (0, LANES - S5_GROUP)))
    lam = jnp.concatenate([lb_re.reshape(DEPTH, 2, 1, S5_STATE), lb_im.reshape(DEPTH, 2, 1, S5_STATE)], axis=3)
    return bb, cc, lam


def _outproj_kernel(*refs, n_x):
    x_refs = refs[:n_x]
    (hyc_ref, hyl_ref, mlc_ref, mll_ref, s5c_ref, s5l_ref, w_ref, mod_ref, g_ref,
     rw_ref, rb_ref, xn_ref, h2e_ref, best_ref, rank_ref, cnt_ref, wb, cnt_s) = refs[n_x:]
    step = pl.program_id(0)

    @pl.when(step == 0)
    def _():
        wb[...] = w_ref[0].astype(BF16)
        cnt_s[...] = jnp.zeros_like(cnt_s)

    is_ctx = step < T_CTX // TM
    pick = lambda c_ref, l_ref: jnp.where(is_ctx, c_ref[...], l_ref[...]).astype(BF16)
    mod = mod_ref[0, 0]
    a, b = HY_CH, HY_CH + ML_W
    mix = (jnp.dot(pick(hyc_ref, hyl_ref), wb[0:a, :], preferred_element_type=F32)
           + jnp.dot(pick(mlc_ref, mll_ref), wb[a:b, :], preferred_element_type=F32)
           + jnp.dot(pick(s5c_ref, s5l_ref), wb[b:, :], preferred_element_type=F32))
    xn = _x_tile(step, x_refs) + mod[2:3] * mix
    xn_ref[...] = xn
    h2 = _rms(xn, g_ref[0]) * (1.0 + mod[4:5]) + mod[3:4]
    h2e_ref[:, 0:D] = h2
    h_hi, h_lo = _split_bf16(h2)
    r_hi, r_lo = _split_bf16(rw_ref[...])
    logits = (lax.dot_general(r_hi, h_hi, _NT, preferred_element_type=F32)
              + (lax.dot_general(r_hi, h_lo, _NT, preferred_element_type=F32)
                 + lax.dot_general(r_lo, h_hi, _NT, preferred_element_type=F32)))
    ex = jnp.exp(logits - jnp.max(logits, axis=0, keepdims=True))
    probs = ex / jnp.sum(ex, axis=0, keepdims=True)
    sel = probs + rb_ref[...]
    best = None
    best_score = None
    for g in range(N_GROUPS):
        r = [sel[g * GROUP_SIZE + i:g * GROUP_SIZE + i + 1, :] for i in range(GROUP_SIZE)]
        score = None
        for i in range(GROUP_SIZE):
            for j in range(i + 1, GROUP_SIZE):
                pair = r[i] + r[j]
                score = pair if score is None else jnp.maximum(score, pair)
        if g == 0:
            best, best_score = jnp.zeros_like(score, dtype=jnp.int32), score
        else:
            upd = score > best_score
            best = jnp.where(upd, g, best)
            best_score = jnp.where(upd, score, best_score)
    eid = lax.broadcasted_iota(jnp.int32, (N_EXPERTS, 1), 0)
    masked = jnp.where(eid // GROUP_SIZE == best, sel, -jnp.inf)
    m1 = jnp.max(masked, axis=0, keepdims=True)
    i1 = jnp.min(jnp.where(masked == m1, eid, N_EXPERTS), axis=0, keepdims=True)
    masked2 = jnp.where(eid == i1, -jnp.inf, masked)
    m2 = jnp.max(masked2, axis=0, keepdims=True)
    i2 = jnp.min(jnp.where(masked2 == m2, eid, N_EXPERTS), axis=0, keepdims=True)
    p1 = jnp.sum(jnp.where(eid == i1, probs, 0.0), axis=0, keepdims=True)
    p2 = jnp.sum(jnp.where(eid == i2, probs, 0.0), axis=0, keepdims=True)
    tot = p1 + p2
    comb = jnp.where(eid == i1, p1 / tot, 0.0) + jnp.where(eid == i2, p2 / tot, 0.0)
    comb = jnp.concatenate([comb, jnp.zeros((LANES - N_EXPERTS, comb.shape[1]), F32)], axis=0)
    h2e_ref[:, D:] = comb.T
    gid = lax.broadcasted_iota(jnp.int32, (SUBLANES, 1), 0)
    onehot = (gid == best).astype(F32)
    cum = _cumsum_lanes(onehot, TM, False)
    run = cnt_s[:, 0:1]
    best_ref[...] = best
    rank_ref[...] = jnp.sum(onehot * (cum - onehot + run), axis=0, keepdims=True).astype(jnp.int32)
    cnt_s[...] = jnp.broadcast_to(run + cum[:, TM - 1:TM], cnt_s.shape)
    cnt_ref[...] = cnt_s[...].astype(jnp.int32)


def _outproj(x, y_hy, y_ml, y_s5, w_out, l, mod, g2, rw_t, rb):
    full = lambda a: pl.BlockSpec(a.shape, lambda i: (0,) * a.ndim)
    n_ctx = T_CTX // TM
    ctx = lambda w: pl.BlockSpec((TM, w), lambda i: (jnp.minimum(i, n_ctx - 1), 0))
    lat = lambda w: pl.BlockSpec((TM, w), lambda i: (jnp.maximum(i - n_ctx, 0), 0))
    tok = lambda w: pl.BlockSpec((TM, w), lambda i: (i, 0))
    row = pl.BlockSpec((1, TM), lambda i: (0, i))
    return pl.pallas_call(
        functools.partial(_outproj_kernel, n_x=len(x)),
        grid=(T_ALL // TM,),
        in_specs=_x_specs(x) + [ctx(HY_CH), lat(HY_CH), ctx(ML_W), lat(ML_W), ctx(S5_CH), lat(S5_CH),
                  pl.BlockSpec((1, D, D), lambda i: (l, 0, 0), pipeline_mode=pl.Buffered(1)),
                  _mod_spec(l, TM), _layer_spec((1, D), l), full(rw_t), full(rb)],
        out_specs=[tok(D), tok(D + LANES), row, row, pl.BlockSpec((SUBLANES, LANES), lambda i: (0, 0))],
        out_shape=[jax.ShapeDtypeStruct((T_ALL, D), F32),
                   jax.ShapeDtypeStruct((T_ALL, D + LANES), F32),
                   jax.ShapeDtypeStruct((1, T_ALL), jnp.int32),
                   jax.ShapeDtypeStruct((1, T_ALL), jnp.int32),
                   jax.ShapeDtypeStruct((SUBLANES, LANES), jnp.int32)],
        scratch_shapes=[pltpu.VMEM((D, D), BF16), pltpu.VMEM((SUBLANES, LANES), F32)],
        compiler_params=_cparams(("arbitrary",)),
        name="outproj_router",
    )(*x, *y_hy, *y_ml, *y_s5, w_out, mod, g2, rw_t, rb)


def _gather_rows(idx_ref, idx_base, src_ref, dst_ref, n_rows):
    def body(r8, carry):
        base = pl.multiple_of(r8 * SUBLANES, SUBLANES)
        for k in range(SUBLANES):
            idx = idx_ref[idx_base + base + k]
            dst_ref[pl.ds(base + k, 1), :] = src_ref[pl.ds(idx, 1), :]
        return carry

    lax.fori_loop(0, n_rows // SUBLANES, body, 0)


def _dispatch_kernel(best_ref, rank_ref, cnt_ref, pos_ref, src_ref, tg_ref, nt_ref, off_s):
    tm = TM_MOE
    tiles = jnp.int32(0)
    tile_end = []
    for g in range(N_GROUPS):
        off_s[g] = tiles * tm
        tiles = tiles + (cnt_ref[g, 0] + (tm - 1)) // tm
        tile_end.append(tiles)
    nt_ref[0] = tiles
    for k in range(MOE_SLOTS // tm):
        g = jnp.int32(0)
        for e in tile_end[:-1]:
            g = g + (k >= e).astype(jnp.int32)
        tg_ref[k] = g

    unroll = SUBLANES

    def clear(i8, carry):
        for u in range(unroll):
            src_ref[i8 * unroll + u] = 0
        return carry

    lax.fori_loop(0, MOE_SLOTS // unroll, clear, 0)

    def place(t8, carry):
        for u in range(unroll):
            t = t8 * unroll + u
            p = rank_ref[t] + off_s[best_ref[t]]
            pos_ref[t] = p
            src_ref[p] = t
        return carry

    lax.fori_loop(0, T_ALL // unroll, place, 0)


def _dispatch(best, rank, cnt):
    smem = pl.BlockSpec(memory_space=pltpu.SMEM)
    i32 = lambda n: jax.ShapeDtypeStruct((n,), jnp.int32)
    return pl.pallas_call(
        _dispatch_kernel,
        in_specs=[smem, smem, smem],
        out_specs=[smem, smem, smem, smem],
        out_shape=[i32(T_ALL), i32(MOE_SLOTS), i32(MOE_SLOTS // TM_MOE), i32(1)],
        scratch_shapes=[pltpu.SMEM((N_GROUPS,), jnp.int32)],
        name="moe_dispatch",
    )(best.reshape(T_ALL), rank.reshape(T_ALL), cnt)


def _moe_kernel(src_ref, tg_ref, nt_ref, h_ref, wg_ref, wu_ref, wd_ref, ys_ref, g_a, x_a, g_b, x_b, acc):
    i = pl.program_id(0)
    j = pl.program_id(1)
    tm = TM_MOE
    quarter = tm // GROUP_SIZE
    n_live = nt_ref[0]

    @pl.when(jnp.logical_and(i == 0, j == 0))
    def _():
        _gather_rows(src_ref, 0, h_ref, g_a, tm)
        x_a[...] = g_a[:, 0:D].astype(BF16)

    def step(cur_g, cur_x, nxt_g, nxt_x):
        @pl.when(j == 0)
        def _():
            acc[...] = jnp.zeros_like(acc)

        e = tg_ref[i] * GROUP_SIZE + j
        lane = lax.broadcasted_iota(jnp.int32, (1, LANES), 1)
        wg = wg_ref[0, 0].astype(BF16)
        wu = wu_ref[0, 0].astype(BF16)
        wd = wd_ref[0, 0].astype(BF16)
        sub = MOE_SUB
        for s in range(tm // sub):
            rs = slice(s * sub, (s + 1) * sub)
            hb = cur_x[rs, :]
            ce = jnp.sum(jnp.where(lane == e, cur_g[rs, D:], 0.0), axis=1, keepdims=True)
            part = None
            for c0 in range(0, D_EXPERT, MOE_HID):
                cs = slice(c0, c0 + MOE_HID)
                hid = (_silu(jnp.dot(hb, wg[:, cs], preferred_element_type=F32))
                       * jnp.dot(hb, wu[:, cs], preferred_element_type=F32))
                p = jnp.dot((hid * ce).astype(BF16), wd[cs, :], preferred_element_type=F32)
                part = p if part is None else part + p
            acc[rs, :] += part

        r0 = pl.multiple_of(j * quarter, quarter)
        base = jnp.minimum(i + 1, n_live - 1) * tm + r0
        for k in range(quarter):
            nxt_g[pl.ds(r0 + k, 1), :] = h_ref[pl.ds(src_ref[base + k], 1), :]
        nxt_x[pl.ds(r0, quarter), :] = nxt_g[pl.ds(r0, quarter), 0:D].astype(BF16)

        @pl.when(j == GROUP_SIZE - 1)
        def _():
            for c in range(D // LANES):
                ys_ref[pl.ds(c, tm, stride=D // LANES), :] = acc[:, c * LANES:(c + 1) * LANES]

    live = i < n_live

    @pl.when(jnp.logical_and(live, i % 2 == 0))
    def _():
        step(g_a, x_a, g_b, x_b)

    @pl.when(jnp.logical_and(live, i % 2 == 1))
    def _():
        step(g_b, x_b, g_a, x_a)

    @pl.when(jnp.logical_and(jnp.logical_not(live), j == GROUP_SIZE - 1))
    def _():
        ys_ref[...] = jnp.zeros_like(ys_ref)


def _moe(h2e, src, tile_group, n_tiles, wg, wu, wd, l):
    tm = TM_MOE

    def w_map(i, j, src, tg, nt):
        live = i < nt[0]
        ii = jnp.minimum(i, nt[0] - 1)
        return (l, tg[ii] * GROUP_SIZE + jnp.where(live, j, GROUP_SIZE - 1), 0, 0)

    return pl.pallas_call(
        _moe_kernel,
        grid_spec=pltpu.PrefetchScalarGridSpec(
            num_scalar_prefetch=3,
            grid=(MOE_SLOTS // tm, GROUP_SIZE),
            in_specs=[pl.BlockSpec(memory_space=pltpu.VMEM),
                      pl.BlockSpec((1, 1, D, D_EXPERT), w_map),
                      pl.BlockSpec((1, 1, D, D_EXPERT), w_map),
                      pl.BlockSpec((1, 1, D_EXPERT, D), w_map)],
            out_specs=pl.BlockSpec((tm * (D // LANES), LANES), lambda i, j, src, tg, nt: (i, 0)),
            scratch_shapes=[pltpu.VMEM((tm, D + LANES), F32), pltpu.VMEM((tm, D), BF16),
                            pltpu.VMEM((tm, D + LANES), F32), pltpu.VMEM((tm, D), BF16),
                            pltpu.VMEM((tm, D), F32)]),
        out_shape=jax.ShapeDtypeStruct((MOE_SLOTS * (D // LANES), LANES), F32),
        compiler_params=_cparams(("arbitrary", "arbitrary")),
        name="moe_experts",
    )(src, tile_group, n_tiles, h2e, wg, wu, wd)


def _combine_kernel(pos_ref, ys_ref, xn_ref, mod_ref, fg_ref, *rest, final):
    step = pl.program_id(0)
    if final:
        yc_ref, yl_ref, gbuf = rest
    else:
        out_ref, gbuf = rest
    n_lt = D // LANES

    def body(r8, carry):
        base = pl.multiple_of(r8 * SUBLANES, SUBLANES)
        for k in range(SUBLANES):
            src = pl.multiple_of(pos_ref[step * TM + base + k] * n_lt, n_lt)
            gbuf[pl.ds(pl.multiple_of((base + k) * n_lt, n_lt), n_lt), :] = ys_ref[pl.ds(src, n_lt), :]
        return carry

    lax.fori_loop(0, TM // SUBLANES, body, 0)
    moe = jnp.concatenate([gbuf[pl.ds(c, TM, stride=n_lt), :] for c in range(n_lt)], axis=1)
    out = xn_ref[...] + mod_ref[0, 0][5:6] * moe
    if final:
        y = _rms(out, fg_ref[...])

        @pl.when(step < T_CTX // TM)
        def _():
            yc_ref[...] = y

        @pl.when(step >= T_CTX // TM)
        def _():
            yl_ref[...] = y
    else:
        out_ref[...] = out


def _combine(pos, ys, xn, mod, fg, l, final):
    spec = pl.BlockSpec((TM, D), lambda i, pos: (i, 0))
    n_ctx = T_CTX // TM
    if final:
        out_specs = [pl.BlockSpec((TM, D), lambda i, pos: (jnp.minimum(i, n_ctx - 1), 0)),
                     pl.BlockSpec((TM, D), lambda i, pos: (jnp.maximum(i - n_ctx, 0), 0))]
        out_shape = [jax.ShapeDtypeStruct((T_CTX, D), F32), jax.ShapeDtypeStruct((T_LAT, D), F32)]
    else:
        out_specs = [spec]
        out_shape = [jax.ShapeDtypeStruct((T_ALL, D), F32)]
    return pl.pallas_call(
        functools.partial(_combine_kernel, final=final),
        grid_spec=pltpu.PrefetchScalarGridSpec(
            num_scalar_prefetch=1,
            grid=(T_ALL // TM,),
            in_specs=[pl.BlockSpec(memory_space=pltpu.VMEM),
                      spec,
                      _mod_spec(l, TM),
                      pl.BlockSpec((1, D), lambda i, pos: (0, 0))],
            out_specs=out_specs,
            scratch_shapes=[pltpu.VMEM((TM * (D // LANES), LANES), F32)]),
        out_shape=out_shape,
        compiler_params=_cparams(("arbitrary",)),
        name="moe_combine",
    )(pos, ys, xn, mod, fg)


@functools.lru_cache(None)
def _pos_embed():
    rows = L_LAT // GRID_W
    r = np.repeat(np.arange(rows, dtype=np.float64), GRID_W)
    col = np.tile(np.arange(GRID_W, dtype=np.float64), rows)
    quarter = D // 4
    freq = np.exp(-math.log(POS_BASE) * np.arange(quarter, dtype=np.float64) / quarter)
    ar = r[:, None] * freq[None]
    ac = col[:, None] * freq[None]
    emb = np.concatenate([np.sin(ar), np.cos(ar), np.sin(ac), np.cos(ac)], axis=-1)
    return emb.astype(np.float32)


def _to_time_major_ctx(a):
    c = a.shape[-1]
    a = a.reshape(2, S5_ROWS, L_CTX, c).transpose(0, 2, 1, 3)
    return a.reshape(2, L_CTX * S5_ROWS, c)


def _from_time_major_ctx(a):
    c = a.shape[-1]
    a = a.reshape(2, L_CTX, S5_ROWS, c).transpose(0, 2, 1, 3)
    return a.reshape(T_CTX, c)


def _to_time_major_lat(a):
    c = a.shape[-1]
    a = a.reshape(N_LAT_SEQ, 4, S5_STEPS, c).transpose(2, 1, 0, 3)
    return a.reshape(1, S5_STEPS * S5_ROWS, c)


def _from_time_major_lat(a):
    c = a.shape[-1]
    a = a.reshape(S5_STEPS, 4, N_LAT_SEQ, c).transpose(2, 1, 0, 3)
    return a.reshape(T_LAT, c)


def kernel(x_prompt, x_sample, c, state_mlstm_C, state_mlstm_n, state_mlstm_m, state_s5_re, state_s5_im, c_ctx, w_ada, b_ada, norm1_g, norm2_g, final_g, w_in, w_out, hy_short, hy_fw1, hy_fb1, hy_fw2, hy_fb2, hy_fw3, hy_log_decay, hy_bias, ml_short, ml_gate_bias, ml_norm_g, s5_a_re, s5_a_im, s5_log_dt, s5_b_re, s5_b_im, s5_c_re, s5_c_im, s5_d, s5_w_glu, router_w, router_b, moe_w_gate, moe_w_up, moe_w_down):
    x = (x_prompt.reshape(T_CTX, D), x_sample.reshape(T_LAT, D), jnp.asarray(_pos_embed()))
    w_in_t = jnp.swapaxes(w_in, 1, 2)
    cc =jnp.concatenate([c_ctx[None], c, jnp.zeros((8 - 1 - N_LAT_SEQ, D), F32)], axis=0)
    mod = _ada(cc, w_ada, b_ada).reshape(DEPTH, 8, 6, D)
    rw_t = router_w.T
    rb = router_b.reshape(N_EXPERTS, 1)
    fg = final_g.reshape(1, D)
    lat_blk = T_CTX // L_LAT
    g1 = norm1_g.reshape(DEPTH, 1, D)
    g2 = norm2_g.reshape(DEPTH, 1, D)

    w1p = jnp.pad(hy_fw1, ((0, 0), (0, LANES - HY_EMB), (0, 0)))
    b1 = hy_fb1.reshape(DEPTH, 1, HY_FILTER_W)
    b2 = hy_fb2.reshape(DEPTH, 1, HY_FILTER_W)
    ld = hy_log_decay.reshape(DEPTH, 1, 4 * HY_CH)
    hy_spec = {L: _hy_filter(L, w1p, b1, hy_fw2, b2, hy_fw3, ld) for L in (L_CTX, L_LAT)}
    gb = jnp.pad(ml_gate_bias.reshape(DEPTH, 1, 16), ((0, 0), (0, 0), (0, GATE_PAD - 16)))
    gbt = ml_gate_bias.reshape(DEPTH, 16, 1)
    ng = ml_norm_g.reshape(DEPTH, 1, ML_W)
    ml_state = (state_mlstm_C.reshape(N_LAT_SEQ, DEPTH, 2 * ML_HEADS, ML_DH, ML_DH),
                state_mlstm_n.reshape(N_LAT_SEQ, DEPTH, 2 * ML_HEADS, ML_DH),
                jnp.broadcast_to(state_mlstm_m.reshape(N_LAT_SEQ, DEPTH, 2 * ML_HEADS, 1),
                                 (N_LAT_SEQ, DEPTH, 2 * ML_HEADS, LANES)))
    bb, cc_s5, lam = _s5_params(s5_a_re, s5_a_im, s5_log_dt, s5_b_re, s5_b_im, s5_c_re, s5_c_im)
    dsk = s5_d.reshape(DEPTH, 1, S5_CH)
    wglu = s5_w_glu.astype(BF16)
    s0 = jnp.concatenate([state_s5_re.reshape(N_LAT_SEQ, DEPTH, 2, S5_STATE),
                          state_s5_im.reshape(N_LAT_SEQ, DEPTH, 2, S5_STATE)], axis=-1)
    s0 = jnp.tile(s0.transpose(1, 2, 0, 3), (1, 1, 4, 1))

    new_n, new_m, new_re, new_im = [], [], [], []
    prev_c, c_all = [], None
    y_prompt = y_sample = None
    for l in range(DEPTH):
        u_hy, qk, v, o, u_s5, gates, gates_t = _inproj(x, g1, mod, w_in_t, l)

        y_hy = [_hyena(u_hy, L, n_seq, blk0, hy_short, hy_bias, *hy_spec[L], l)
                for L, n_seq, blk0 in ((L_CTX, N_CTX_SEQ, 0), (L_LAT, N_LAT_SEQ, lat_blk))]

        yc, c_all, nc_, mc_ = _mlstm(qk, v, o, gates, gates_t, L_CTX, N_CTX_SEQ, 0, ml_short, gb, gbt, ng, None, l,
                                     emit_state=True, prev_c=prev_c)
        prev_c = [c_all[:, k] for k in range(l + 1)] if l + 1 < DEPTH else None
        (yl,) = _mlstm(qk, v, o, gates, gates_t, L_LAT, N_LAT_SEQ, lat_blk, ml_short, gb, gbt, ng, ml_state, l)
        y_ml = (yc, yl)
        new_n.append(nc_.reshape(N_CTX_SEQ, 2, ML_HEADS, ML_DH))
        new_m.append(mc_[:, :, 0].reshape(N_CTX_SEQ, 2, ML_HEADS))

        ys_c, fin = _s5(_to_time_major_ctx(u_s5[:T_CTX]), bb, cc_s5, lam, dsk, wglu, None, l)
        (ys_l,) = _s5(_to_time_major_lat(u_s5[T_CTX:]), bb, cc_s5, lam, dsk, wglu, s0, l)
        y_s5 = (_from_time_major_ctx(ys_c), _from_time_major_lat(ys_l))
        fin = fin.transpose(0, 2, 1, 3).reshape(N_CTX_SEQ, 2, 2 * S5_STATE)
        new_re.append(fin[..., :S5_STATE].reshape(N_CTX_SEQ, 2, S5_G, S5_P))
        new_im.append(fin[..., S5_STATE:].reshape(N_CTX_SEQ, 2, S5_G, S5_P))

        xn, h2e, best, rank, cnt = _outproj(x, y_hy, y_ml, y_s5, w_out, l, mod, g2, rw_t, rb)
        pos, src, tile_group, n_tiles = _dispatch(best, rank, cnt)
        ys = _moe(h2e, src, tile_group, n_tiles, moe_w_gate, moe_w_up, moe_w_down, l)
        res = _combine(pos, ys, xn, mod, fg, l, l == DEPTH - 1)
        if l == DEPTH - 1:
            y_prompt = res[0].reshape(N_CTX_SEQ, L_CTX, D)
            y_sample = res[1].reshape(N_LAT_SEQ, L_LAT, D)
        else:
            x = (res[0],)

    new_c = c_all.reshape(N_CTX_SEQ, DEPTH, 2, ML_HEADS, ML_DH, ML_DH)
    return (y_prompt, y_sample, new_c, jnp.stack(new_n, axis=1), jnp.stack(new_m, axis=1),
            jnp.stack(new_re, axis=1), jnp.stack(new_im, axis=1))
```

```python
import functools
import math

import numpy as np
import jax
import jax.numpy as jnp
from jax import lax
from jax.experimental import pallas as pl
from jax.experimental.pallas import tpu as pltpu

F32 = jnp.float32
BF16 = jnp.bfloat16

D = 1024
N_CTX_SEQ, L_CTX = 16, 256
N_LAT_SEQ, L_LAT = 2, 1024
T_CTX = N_CTX_SEQ * L_CTX
T_LAT = N_LAT_SEQ * L_LAT
T_ALL = T_CTX + T_LAT
DEPTH = 2
EPS = 1e-6
GRID_W = 64
POS_BASE = 10000.0
HY_CH = 256
HY_EMB = 33
HY_FILTER_W = 64
ML_HEADS = 4
ML_DH = 128
ML_W = ML_HEADS * ML_DH
S5_CH = 256
S5_G = 16
S5_GROUP = 16
S5_P = 64
S5_STATE = S5_G * S5_P
N_EXPERTS = 16
N_GROUPS = 4
GROUP_SIZE = N_EXPERTS // N_GROUPS
D_EXPERT = 512
OFF_HY = 0
OFF_QK = 3 * HY_CH
OFF_V = OFF_QK + 2 * ML_W
OFF_O = OFF_V + ML_W
OFF_G = OFF_O + ML_W
OFF_S5 = OFF_G + 16
IN_W = OFF_S5 + S5_CH
LANES = 128
SUBLANES = 8
GATE_PAD = LANES

TM = 512
TM_MOE = 512
MOE_SLOTS = T_ALL + N_GROUPS * TM_MOE
MOE_SUB = 512
MOE_HID = 256
HY_ROWS = 1024
HY_FREQ = 256
CONV_ROWS = 128
ML_CHUNK = 256
S5_ROWS = 8
S5_STEPS = 256
VMEM_LIMIT = 56 * 1024 * 1024


def _cparams(sem, vmem=VMEM_LIMIT):
    if sem is None:
        return pltpu.CompilerParams(vmem_limit_bytes=vmem)
    return pltpu.CompilerParams(dimension_semantics=sem, vmem_limit_bytes=vmem)


def _bdot(a, b):
    return jnp.dot(a.astype(BF16), b.astype(BF16), preferred_element_type=F32)


def _split_bf16(x):
    hi = x.astype(BF16)
    return hi, (x - hi.astype(F32)).astype(BF16)


def _dot3(a, b):
    a_hi, a_lo = _split_bf16(a)
    b_hi, b_lo = _split_bf16(b)
    dot = functools.partial(jnp.dot, preferred_element_type=F32)
    return dot(a_hi, b_hi) + (dot(a_hi, b_lo) + dot(a_lo, b_hi))


def _silu(x):
    return x * jax.nn.sigmoid(x)


def _rms(x, g):
    return x * lax.rsqrt(jnp.mean(x * x, axis=-1, keepdims=True) + EPS) * g


def _log_sigmoid(x):
    return jnp.minimum(x, 0.0) - jnp.log1p(jnp.exp(-jnp.abs(x)))


def _conv3(u, w, n_rows, seq_len):
    row = lax.broadcasted_iota(jnp.int32, (n_rows, 1), 0) % seq_len
    prev = jnp.where(row == 0, 0.0, pltpu.roll(u, 1, 0))
    nxt = jnp.where(row == seq_len - 1, 0.0, pltpu.roll(u, n_rows - 1, 0))
    return prev * w[0:1] + u * w[1:2] + nxt * w[2:3]


def _ada_kernel(c_ref, w_ref, b_ref, o_ref):
    o_ref[0] = _bdot(_silu(c_ref[...]), w_ref[0]) + b_ref[0]


def _ada(cc, w_ada, b_ada):
    tn = 1536
    return pl.pallas_call(
        _ada_kernel,
        grid=(DEPTH, 6 * D // tn),
        in_specs=[pl.BlockSpec((8, D), lambda l, j: (0, 0)),
                  pl.BlockSpec((1, D, tn), lambda l, j: (l, 0, j)),
                  pl.BlockSpec((1, 1, tn), lambda l, j: (l, 0, j))],
        out_specs=pl.BlockSpec((1, 8, tn), lambda l, j: (l, 0, j)),
        out_shape=jax.ShapeDtypeStruct((DEPTH, 8, 6 * D), F32),
        compiler_params=_cparams(("arbitrary", "arbitrary")),
        name="ada_mod",
    )(cc, w_ada, b_ada.reshape(DEPTH, 1, 6 * D))


def _mod_row(i, tm):
    n_ctx = T_CTX // tm
    return jnp.where(i < n_ctx, 0, 1 + (i - n_ctx) // (L_LAT // tm))


_SEG = ((OFF_HY, OFF_QK - OFF_HY), (OFF_QK, OFF_V - OFF_QK), (OFF_V, OFF_O - OFF_V), (OFF_O, OFF_G - OFF_O))
TAIL_W = IN_W - OFF_G


N_CTX_TILES = T_CTX // TM
_NT = (((1,), (1,)), ((), ()))


def _x_specs(x):
    if len(x) == 1:
        return [pl.BlockSpec((TM, D), lambda i, *_: (i, 0))]
    per_seq = L_LAT // TM
    return [pl.BlockSpec((TM, D), lambda i, *_: (jnp.minimum(i, N_CTX_TILES - 1), 0)),
            pl.BlockSpec((TM, D), lambda i, *_: (jnp.maximum(i - N_CTX_TILES, 0), 0)),
            pl.BlockSpec((TM, D), lambda i, *_: (jnp.maximum(i - N_CTX_TILES, 0) % per_seq, 0))]


def _x_tile(step, x_refs):
    if len(x_refs) == 1:
        return x_refs[0][...]
    xc_ref, xl_ref, pos_ref = x_refs
    return jnp.where(step < N_CTX_TILES, xc_ref[...], xl_ref[...] + pos_ref[...])


def _inproj_kernel(*refs, n_x):
    x_refs = refs[:n_x]
    g_ref, mod_ref, w_ref, hy_ref, qk_ref, v_ref, o_ref, s5_ref, gt_ref, gtt_ref, wb = refs[n_x:]
    step = pl.program_id(0)

    @pl.when(step == 0)
    def _():
        wb[...] = w_ref[0].astype(BF16)

    mod = mod_ref[0, 0]
    h = _rms(_x_tile(step, x_refs), g_ref[0]) * (1.0 + mod[1:2]) + mod[0:1]
    hb = h.astype(BF16)
    for (a, w), ref in zip(_SEG, (hy_ref, qk_ref, v_ref, o_ref)):
        ref[...] = lax.dot_general(hb, wb[a:a + w, :], _NT, preferred_element_type=F32)
    tail = lax.dot_general(hb, wb[OFF_G:IN_W, :], _NT, preferred_element_type=F32)
    gates = tail[:, 0:GATE_PAD]
    gt_ref[...] = gates
    for c in range(TM // ML_CHUNK):
        gtt_ref[c] = gates[c * ML_CHUNK:(c + 1) * ML_CHUNK, :].T[0:16, :]
    s5_ref[...] = tail[:, OFF_S5 - OFF_G:TAIL_W]


def _layer_spec(shape, l):
    return pl.BlockSpec((1,) + tuple(shape), lambda *_: (l,) + (0,) * len(shape))


def _mod_spec(l, tm):
    return pl.BlockSpec((1, 1, 6, D), lambda i, *_: (l, _mod_row(i, tm), 0, 0))


def _inproj(x, g, mod, w_in_t, l):
    widths = [w for _, w in _SEG] + [S5_CH, GATE_PAD]
    cpt = TM // ML_CHUNK
    return pl.pallas_call(
        functools.partial(_inproj_kernel, n_x=len(x)),
        grid=(T_ALL // TM,),
        in_specs=_x_specs(x) + [
            _layer_spec((1, D), l), _mod_spec(l, TM),
            pl.BlockSpec((1, IN_W, D), lambda i: (l, 0, 0), pipeline_mode=pl.Buffered(1))],
        out_specs=[pl.BlockSpec((TM, w), lambda i: (i, 0)) for w in widths]
        + [pl.BlockSpec((cpt, 16, ML_CHUNK), lambda i: (i, 0, 0))],
        out_shape=[jax.ShapeDtypeStruct((T_ALL, w), F32) for w in widths]
        + [jax.ShapeDtypeStruct((T_ALL // ML_CHUNK, 16, ML_CHUNK), F32)],
        scratch_shapes=[pltpu.VMEM((IN_W, D), BF16)],
        compiler_params=_cparams(("arbitrary",)),
        name="norm_inproj",
    )(*x, g, mod, w_in_t)


@functools.lru_cache(None)
def _dft_mats(L):
    n = 2 * L
    k = np.arange(L)[:, None]
    t = np.arange(L)[None, :]
    ang = 2.0 * np.pi * ((k * t) % n) / n
    top = np.cos(ang)
    bot = -np.sin(ang)
    bot[0] = np.cos(np.pi * np.arange(L))
    fwd = np.concatenate([top, bot], 0)
    s = np.full((n, 1), 2.0 / n)
    s[0] = s[L] = 1.0 / n
    inv = (fwd * s).T
    return fwd.astype(np.float32), inv.astype(np.float32)


@functools.lru_cache(None)
def _hy_positions(L):
    t = np.linspace(0.0, 1.0, L)
    bands = (HY_EMB - 1) // 2
    f = np.linspace(1e-4, bands - 1, bands)
    w = 2.0 * np.pi * np.arange(L) / L
    ang = w[:, None] * f[None, :]
    z = np.concatenate([t[:, None], np.cos(ang), -np.sin(ang)], -1)
    zp = np.zeros((L, LANES))
    zp[:, :HY_EMB] = z
    return zp.astype(np.float32), t[:, None].astype(np.float32)


def _hy_filter_kernel(z_ref, t_ref, w1_ref, b1_ref, w2_ref, b2_ref, w3_ref, ld_ref, f_ref,
                      p_ref, q_ref, r_ref, *, L):
    h = jnp.sin(_dot3(z_ref[...], w1_ref[0]) + b1_ref[0])
    h = jnp.sin(_dot3(h, w2_ref[0]) + b2_ref[0])
    filt = _dot3(h, w3_ref[0])
    filt = filt * jnp.exp(-t_ref[...] * jnp.exp(ld_ref[0]))
    c = HY_CH
    h_fwd = jnp.concatenate([filt[:, 0:c], filt[:, 2 * c:3 * c]], axis=1)
    h_bwd = jnp.concatenate([filt[:, c:2 * c], filt[:, 3 * c:4 * c]], axis=1)
    row = lax.broadcasted_iota(jnp.int32, (L, 1), 0)
    h_bwd = jnp.where(row == 0, 0.0, h_bwd)
    h_sum = (h_fwd + h_bwd).astype(BF16)
    h_dif = (h_fwd - h_bwd).astype(BF16)
    re = jnp.dot(f_ref[0:L, :], h_sum, preferred_element_type=F32)
    im = jnp.dot(f_ref[L:2 * L, :], h_dif, preferred_element_type=F32)
    nyq = jnp.dot(f_ref[L:L + SUBLANES, :], h_sum, preferred_element_type=F32)[0:1]
    p_ref[0] = re
    q_ref[0] = jnp.where(row == 0, 0.0, im)
    r_ref[0] = jnp.where(row == 0, nyq, re)


def _hy_filter(L, w1p, b1, w2, b2, w3, ld):
    z, t = _hy_positions(L)
    fwd = jnp.asarray(_dft_mats(L)[0]).astype(BF16)
    out = jax.ShapeDtypeStruct((DEPTH, L, 2 * HY_CH), F32)
    full = lambda a: pl.BlockSpec(a.shape, lambda l: (0,) * a.ndim)
    layer = lambda a: pl.BlockSpec((1,) + a.shape[1:], lambda l: (l,) + (0,) * (a.ndim - 1))
    return pl.pallas_call(
        functools.partial(_hy_filter_kernel, L=L),
        grid=(DEPTH,),
        in_specs=[full(z), full(t), layer(w1p), layer(b1), layer(w2), layer(b2), layer(w3), layer(ld), full(fwd)],
        out_specs=[pl.BlockSpec((1, L, 2 * HY_CH), lambda l: (l, 0, 0))] * 3,
        out_shape=[out, out, out],
        compiler_params=_cparams(("arbitrary",)),
        name=f"hyena_filter_{L}",
    )(z, t, w1p, b1, w2, b2, w3, ld, fwd)


def _hyena_kernel(u_ref, sw_ref, bias_ref, p_ref, q_ref, r_ref, f_ref, g_ref, o_ref, *, L, n_sub):
    c = HY_CH
    u = _conv3(u_ref[...], sw_ref[0], n_sub * L, L)
    for s in range(n_sub):
        rs = slice(s * L, (s + 1) * L)
        z = u[rs, 0:c]
        for o in range(2):
            gate = u[rs, (o + 1) * c:(o + 2) * c]
            zb = z.astype(BF16)
            y = None
            for k0 in range(0, L, HY_FREQ):
                ks = slice(k0, k0 + HY_FREQ)
                a = jnp.dot(f_ref[k0:k0 + HY_FREQ, :], zb, preferred_element_type=F32)
                b = jnp.dot(f_ref[L + k0:L + k0 + HY_FREQ, :], zb, preferred_element_type=F32)
                p = p_ref[0, ks, o * c:(o + 1) * c]
                q = q_ref[0, ks, o * c:(o + 1) * c]
                r = r_ref[0, ks, o * c:(o + 1) * c]
                y_re = (a * p - b * q).astype(BF16)
                y_im = (a * q + b * r).astype(BF16)
                part = (jnp.dot(g_ref[:, k0:k0 + HY_FREQ], y_re, preferred_element_type=F32)
                        + jnp.dot(g_ref[:, L + k0:L + k0 + HY_FREQ], y_im, preferred_element_type=F32))
                y = part if y is None else y + part
            z = gate * (y + bias_ref[0, o:o + 1, :] * z)
        o_ref[rs, :] = z


def _hyena(u_hy, L, n_seq, row_block0, sw, bias, p, q, r, l):
    n_sub = max(1, HY_ROWS // L)
    fwd, inv = (jnp.asarray(m).astype(BF16) for m in _dft_mats(L))
    full = lambda a: pl.BlockSpec(a.shape, lambda b: (0,) * a.ndim)
    layer = lambda a: _layer_spec(a.shape[1:], l)
    blk0 = row_block0 // n_sub
    return pl.pallas_call(
        functools.partial(_hyena_kernel, L=L, n_sub=n_sub),
        grid=(n_seq // n_sub,),
        in_specs=[pl.BlockSpec((n_sub * L, 3 * HY_CH), lambda b: (blk0 + b, 0)),
                  layer(sw), layer(bias), layer(p), layer(q), layer(r), full(fwd), full(inv)],
        out_specs=pl.BlockSpec((n_sub * L, HY_CH), lambda b: (b, 0)),
        out_shape=jax.ShapeDtypeStruct((n_seq * L, HY_CH), F32),
        compiler_params=_cparams(("arbitrary",)),
        name=f"hyena_{L}",
    )(u_hy, sw, bias, p, q, r, fwd, inv)


def _cumsum_rows(x, n, reverse):
    row = lax.broadcasted_iota(jnp.int32, (n, 1), 0)
    s = 1
    while s < n:
        if reverse:
            x = x + jnp.where(row < n - s, pltpu.roll(x, n - s, 0), 0.0)
        else:
            x = x + jnp.where(row >= s, pltpu.roll(x, s, 0), 0.0)
        s *= 2
    return x


def _cumsum_lanes(x, n, reverse):
    col = lax.broadcasted_iota(jnp.int32, (1, n), 1)
    s = 1
    while s < n:
        if reverse:
            x = x + jnp.where(col < n - s, pltpu.roll(x, n - s, 1), 0.0)
        else:
            x = x + jnp.where(col >= s, pltpu.roll(x, s, 1), 0.0)
        s *= 2
    return x


def _mlstm_kernel(*refs, L, has_state, emit_state, n_prev):
    qk_ref, v_ref, o_ref, g_ref, gt_ref, sw_ref, gb_ref, gbt_ref, ng_ref = refs[:9]
    refs = refs[9:]
    if has_state:
        c0_ref, n0_ref, m0_ref = refs[:3]
        refs = refs[3:]
    prev_c_refs = refs[:n_prev]
    refs = refs[n_prev:]
    y_ref = refs[0]
    if emit_state:
        cout_ref, nout_ref, mout_ref = refs[1:4]
        refs = refs[4:]
    else:
        refs = refs[1:]
    q_s, k_s, vt_s, ht_s, ct_s, n_s, m_s, lf_s, lft_s = refs
    tc = ML_CHUNK
    nc = L // tc
    nh = ML_HEADS
    dh = ML_DH

    lf_s[...] = _log_sigmoid(g_ref[...] + gb_ref[0])
    lft_s[...] = _log_sigmoid(gt_ref[...] + gbt_ref[0])

    sw = sw_ref[0]
    for r0 in range(0, L, CONV_ROWS):
        a, b = max(r0 - SUBLANES, 0), min(r0 + CONV_ROWS + SUBLANES, L)
        n = b - a
        u = qk_ref[a:b, :]
        row = lax.broadcasted_iota(jnp.int32, (n, 1), 0) + a
        prev = jnp.where(row == 0, 0.0, pltpu.roll(u, 1, 0))
        nxt = jnp.where(row == L - 1, 0.0, pltpu.roll(u, n - 1, 0))
        y = _silu(prev * sw[0:1] + u * sw[1:2] + nxt * sw[2:3])[r0 - a:r0 - a + CONV_ROWS]
        q_s[r0:r0 + CONV_ROWS, :] = y[:, :ML_W].astype(BF16)
        k_s[r0:r0 + CONV_ROWS, :] = (y[:, ML_W:] * (dh ** -0.5)).astype(BF16)
    for c in range(nc):
        vt_s[c] = v_ref[c * tc:(c + 1) * tc, :].T.astype(BF16)

    for i in range(2 * nh):
        ct_s[i] = c0_ref[0, 0, i].T if has_state else jnp.zeros((dh, dh), F32)
    n_s[...] = n0_ref[0, 0] if has_state else jnp.zeros_like(n_s)
    m_s[...] = m0_ref[0, 0] if has_state else jnp.zeros_like(m_s)

    si = lax.broadcasted_iota(jnp.int32, (tc, tc), 0)
    ti = lax.broadcasted_iota(jnp.int32, (tc, tc), 1)

    for d in range(2):
        rev = d == 1
        mask = (si >= ti) if rev else (si <= ti)
        edge = 0 if rev else tc - 1

        def chunk(j, carry, d=d, rev=rev, mask=mask, edge=edge):
            cidx = (nc - 1 - j) if rev else j
            r0 = pl.multiple_of(cidx * tc, tc)
            pre = g_ref[pl.ds(r0, tc), :] + gb_ref[0]
            pre_t = gt_ref[cidx] + gbt_ref[0]
            cum = _cumsum_rows(lf_s[pl.ds(r0, tc), :], tc, rev)
            cum_t = _cumsum_lanes(lft_s[cidx], tc, rev)
            key_all = cum - pltpu.roll(pre, 8, 1)
            for h in range(nh):
                col = d * nh + h
                hs = slice(h * dh, (h + 1) * dh)
                key = key_all[:, 8 + col:9 + col]
                b_row = cum_t[8 + col:9 + col, :]
                b_end = b_row[:, edge:edge + 1]
                m_prev = m_s[col:col + 1, 0:1]
                dmat = jnp.where(mask, b_row - key, -jnp.inf)
                inter = b_row + m_prev
                m_row = jnp.maximum(inter, jnp.max(dmat, axis=0, keepdims=True))
                w_intra = jnp.exp(dmat - m_row)
                w_state = jnp.exp(inter - m_row)
                qh = q_s[pl.ds(r0, tc), hs]
                kh = k_s[pl.ds(r0, tc), hs]
                vt = vt_s[cidx, hs, :]
                ct_prev = ct_s[col]
                n_prev = n_s[col:col + 1, :]
                s = lax.dot_general(kh, qh, _NT, preferred_element_type=F32) * w_intra
                num = (jnp.dot(vt, s.astype(BF16), preferred_element_type=F32)
                       + w_state * lax.dot_general(ct_prev.astype(BF16), qh, _NT, preferred_element_type=F32))
                qn = lax.dot_general(jnp.broadcast_to(n_prev, (8, dh)).astype(BF16), qh, _NT,
                                     preferred_element_type=F32)[0:1]
                den = jnp.sum(s, axis=0, keepdims=True) + w_state * qn
                hout = num * (1.0 / jnp.maximum(jnp.abs(den), jnp.exp(-m_row)))
                if d == 0:
                    ht_s[cidx, hs, :] = hout
                else:
                    ht_s[cidx, hs, :] += hout
                m_new = jnp.maximum(b_end + m_prev, b_end - jnp.min(key, axis=0, keepdims=True))
                wg = jnp.exp(b_end - key - m_new)
                decay = jnp.exp(b_end + m_prev - m_new)
                kw = kh.astype(F32) * wg
                ct_s[col] = decay * ct_prev + jnp.dot(vt, kw.astype(BF16), preferred_element_type=F32)
                n_s[col:col + 1, :] = decay * n_prev + jnp.sum(kw, axis=0, keepdims=True)
                m_s[col:col + 1, :] = jnp.broadcast_to(m_new, (1, LANES))
            return carry

        lax.fori_loop(0, nc, chunk, 0, unroll=2 if nc > 1 else 1)

    for c in range(nc):
        for h in range(nh):
            hs = slice(h * dh, (h + 1) * dh)
            rs = slice(c * tc, (c + 1) * tc)
            ht = ht_s[c, hs, :]
            hn = ht * lax.rsqrt(jnp.mean(ht * ht, axis=0, keepdims=True) + EPS)
            y_ref[rs, hs] = jax.nn.sigmoid(o_ref[rs, hs]) * (hn.T * ng_ref[0, :, hs])
    if emit_state:
        for k, prev_ref in enumerate(prev_c_refs):
            cout_ref[0, k] = prev_ref[0]
        for i in range(2 * nh):
            cout_ref[0, n_prev, i] = ct_s[i].T
        nout_ref[0] = n_s[...]
        mout_ref[0] = m_s[...]


def _mlstm(qk, v, o, gates, gates_t, L, n_seq, row_block0, sw, gb, gbt, ng, state, l, emit_state=False,
           prev_c=()):
    nc = L // ML_CHUNK
    has_state = state is not None
    layer = lambda a: _layer_spec(a.shape[1:], l)
    in_specs = [pl.BlockSpec((L, 2 * ML_W), lambda b: (row_block0 + b, 0)),
                pl.BlockSpec((L, ML_W), lambda b: (row_block0 + b, 0)),
                pl.BlockSpec((L, ML_W), lambda b: (row_block0 + b, 0)),
                pl.BlockSpec((L, GATE_PAD), lambda b: (row_block0 + b, 0)),
                pl.BlockSpec((nc, 16, ML_CHUNK), lambda b: (row_block0 + b, 0, 0)),
                layer(sw), layer(gb), layer(gbt), layer(ng)]
    args = [qk, v, o, gates, gates_t, sw, gb, gbt, ng]
    if has_state:
        c0, n0, m0 = state
        in_specs += [pl.BlockSpec((1, 1, 2 * ML_HEADS, ML_DH, ML_DH), lambda b: (b, l, 0, 0, 0)),
                     pl.BlockSpec((1, 1, 2 * ML_HEADS, ML_DH), lambda b: (b, l, 0, 0)),
                     pl.BlockSpec((1, 1, 2 * ML_HEADS, LANES), lambda b: (b, l, 0, 0))]
        args += [c0, n0, m0]
    out_specs = [pl.BlockSpec((L, ML_W), lambda b: (b, 0))]
    out_shape = [jax.ShapeDtypeStruct((n_seq * L, ML_W), F32)]
    if emit_state:
        c_tail = (2 * ML_HEADS, ML_DH, ML_DH)
        for a in prev_c:
            in_specs.append(pl.BlockSpec((1,) + c_tail, lambda b: (b, 0, 0, 0)))
            args.append(a)
        tails = ((len(prev_c) + 1,) + c_tail, (2 * ML_HEADS, ML_DH), (2 * ML_HEADS, LANES))
        for t in tails:
            out_specs.append(pl.BlockSpec((1,) + t, lambda b, n=len(t): (b,) + (0,) * n))
            out_shape.append(jax.ShapeDtypeStruct((n_seq,) + t, F32))
    return pl.pallas_call(
        functools.partial(_mlstm_kernel, L=L, has_state=has_state, emit_state=emit_state,
                          n_prev=len(prev_c) if emit_state else 0),
        grid=(n_seq,),
        in_specs=in_specs,
        out_specs=out_specs,
        out_shape=out_shape,
        scratch_shapes=[pltpu.VMEM((L, ML_W), BF16), pltpu.VMEM((L, ML_W), BF16),
                        pltpu.VMEM((nc, ML_W, ML_CHUNK), BF16),
                        pltpu.VMEM((nc, ML_W, ML_CHUNK), F32),
                        pltpu.VMEM((2 * ML_HEADS, ML_DH, ML_DH), F32),
                        pltpu.VMEM((2 * ML_HEADS, ML_DH), F32),
                        pltpu.VMEM((2 * ML_HEADS, LANES), F32),
                        pltpu.VMEM((L, GATE_PAD), F32),
                        pltpu.VMEM((nc, 16, ML_CHUNK), F32)],
        compiler_params=_cparams(("arbitrary",)),
        name=f"mlstm_{L}",
    )(*args)


def _cmul(ar, ai, br, bi):
    return ar * br - ai * bi, ar * bi + ai * br


def _s5_kernel(*refs, segmented):
    if segmented:
        (u_ref, bb_ref, cc_ref, eb_ref, ec_ref, lam_ref, dsk_ref, wglu_ref, s0_ref, y_ref,
         sbuf, yacc, bmat, cmat, pw) = refs
    else:
        (u_ref, bb_ref, cc_ref, eb_ref, ec_ref, lam_ref, dsk_ref, wglu_ref, y_ref, fin_ref,
         sbuf, yacc, bmat, cmat) = refs
    n = S5_STATE
    rows = S5_ROWS
    steps = S5_STEPS
    blk = 256
    n_blk = steps * rows // blk
    n_seg = 4

    yacc[...] = u_ref[0] * dsk_ref[0]
    ub = u_ref[0].astype(BF16)
    b_keep = (lax.broadcasted_iota(jnp.int32, (S5_CH, 2 * n), 0) // S5_GROUP
              == (lax.broadcasted_iota(jnp.int32, (S5_CH, 2 * n), 1) % n) // S5_P)
    c_keep = ((lax.broadcasted_iota(jnp.int32, (2 * n, S5_CH), 0) % n) // S5_P
              == lax.broadcasted_iota(jnp.int32, (2 * n, S5_CH), 1) // S5_GROUP)

    for d in range(2):
        rev = d == 1
        bmat[...] = jnp.where(b_keep, jnp.dot(bb_ref[0, d].astype(BF16), eb_ref[...], preferred_element_type=F32),
                              0.0).astype(BF16)
        cmat[...] = jnp.where(c_keep, jnp.dot(cc_ref[0, d].astype(BF16), ec_ref[...], preferred_element_type=F32),
                              0.0).astype(BF16)
        for i in range(n_blk):
            sbuf[i * blk:(i + 1) * blk, :] = jnp.dot(ub[i * blk:(i + 1) * blk], bmat[...],
                                                     preferred_element_type=F32)
        lam = lam_ref[0, d]
        lr = jnp.broadcast_to(lam[:, :n], (rows, n))
        li = jnp.broadcast_to(lam[:, n:], (rows, n))

        def step(i, carry, rev=rev, lr=lr, li=li):
            sr, si = carry
            t = (steps - 1 - i) if rev else i
            off = pl.multiple_of(t * rows, rows)
            pr, pi = _cmul(lr, li, sr, si)
            nr = pr + sbuf[pl.ds(off, rows), 0:n]
            ni = pi + sbuf[pl.ds(off, rows), n:2 * n]
            sbuf[pl.ds(off, rows), 0:n] = nr
            sbuf[pl.ds(off, rows), n:2 * n] = ni
            return nr, ni

        zero = jnp.zeros((rows, n), F32)
        sr, si = lax.fori_loop(0, steps, step, (zero, zero), unroll=4)

        if not segmented:
            fin_ref[0, d, :, 0:n] = sr
            fin_ref[0, d, :, n:2 * n] = si
        else:
            lam_r, lam_i = lam[:, :n], lam[:, n:]
            row8 = lax.broadcasted_iota(jnp.int32, (rows, 1), 0)
            cr, ci = lam_r, lam_i
            acc_r = jnp.broadcast_to(cr, (rows, n))
            acc_i = jnp.broadcast_to(ci, (rows, n))
            for j in range(1, rows):
                cr, ci = _cmul(cr, ci, lam_r, lam_i)
                acc_r = jnp.where(row8 >= j, jnp.broadcast_to(cr, (rows, n)), acc_r)
                acc_i = jnp.where(row8 >= j, jnp.broadcast_to(ci, (rows, n)), acc_i)
            pw[0:rows, 0:n] = acc_r
            pw[0:rows, n:2 * n] = acc_i
            size = rows
            while size < steps:
                tr = pw[size - 1:size, 0:n]
                ti = pw[size - 1:size, n:2 * n]
                xr, xi = _cmul(pw[0:size, 0:n], pw[0:size, n:2 * n], tr, ti)
                pw[size:2 * size, 0:n] = xr
                pw[size:2 * size, n:2 * n] = xi
                size *= 2
            end_off = 0 if rev else (steps - 1) * rows
            loc_r = sbuf[end_off:end_off + rows, 0:n]
            loc_i = sbuf[end_off:end_off + rows, n:2 * n]
            pl_r = pw[steps - 1:steps, 0:n]
            pl_i = pw[steps - 1:steps, n:2 * n]
            s0r = s0_ref[0, d, :, 0:n]
            s0i = s0_ref[0, d, :, n:2 * n]
            seg = row8 // 2
            first = (seg == n_seg - 1) if rev else (seg == 0)
            shift = (rows - 2) if rev else 2
            cin_r, cin_i = s0r, s0i
            for _ in range(n_seg - 1):
                fr, fi = _cmul(jnp.broadcast_to(pl_r, (rows, n)), jnp.broadcast_to(pl_i, (rows, n)), cin_r, cin_i)
                tru_r = loc_r + fr
                tru_i = loc_i + fi
                cin_r = jnp.where(first, s0r, pltpu.roll(tru_r, shift, 0))
                cin_i = jnp.where(first, s0i, pltpu.roll(tru_i, shift, 0))

            def fix(tb, carry, rev=rev, cin_r=cin_r, cin_i=cin_i):
                pb = (steps // rows - 1 - tb) if rev else tb
                poff = pl.multiple_of(pb * rows, rows)
                p_r = pw[pl.ds(poff, rows), 0:n]
                p_i = pw[pl.ds(poff, rows), n:2 * n]
                for j in range(rows):
                    jj = rows - 1 - j if rev else j
                    off = pl.multiple_of((tb * rows + j) * rows, rows)
                    fr, fi = _cmul(jnp.broadcast_to(p_r[jj:jj + 1], (rows, n)),
                                   jnp.broadcast_to(p_i[jj:jj + 1], (rows, n)), cin_r, cin_i)
                    sbuf[pl.ds(off, rows), 0:n] += fr
                    sbuf[pl.ds(off, rows), n:2 * n] += fi
                return carry

            lax.fori_loop(0, steps // rows, fix, 0)

        for i in range(n_blk):
            yacc[i * blk:(i + 1) * blk, :] += jnp.dot(sbuf[i * blk:(i + 1) * blk, :].astype(BF16), cmat[...],
                                                      preferred_element_type=F32)

    g = jax.nn.gelu(yacc[...], approximate=True)
    y_ref[0] = g * jax.nn.sigmoid(_bdot(g, wglu_ref[0]))


@functools.lru_cache(None)
def _s5_spread():
    eb = np.zeros((2 * S5_P, 2 * S5_STATE), np.float32)
    for half in range(2):
        for g in range(S5_G):
            c0 = half * S5_STATE + g * S5_P
            eb[half * S5_P:(half + 1) * S5_P, c0:c0 + S5_P] = np.eye(S5_P)
    ec = np.zeros((LANES, S5_CH), np.float32)
    for g in range(S5_G):
        ec[:S5_GROUP, g * S5_GROUP:(g + 1) * S5_GROUP] = np.eye(S5_GROUP)
    return eb, ec


def _s5(u_tm, bb, cc, lam, dskip, wglu, s0, l):
    n_grp = u_tm.shape[0]
    n_rows = S5_STEPS * S5_ROWS
    segmented = s0 is not None
    eb, ec = (jnp.asarray(m).astype(BF16) for m in _s5_spread())
    full = lambda a: pl.BlockSpec(a.shape, lambda g: (0,) * a.ndim)
    layer = lambda a: _layer_spec(a.shape[1:], l)
    in_specs = [pl.BlockSpec((1, n_rows, S5_CH), lambda g: (g, 0, 0)),
                layer(bb), layer(cc), full(eb), full(ec), layer(lam), layer(dskip), layer(wglu)]
    args = [u_tm, bb, cc, eb, ec, lam, dskip, wglu]
    out_specs = [pl.BlockSpec((1, n_rows, S5_CH), lambda g: (g, 0, 0))]
    out_shape = [jax.ShapeDtypeStruct((n_grp, n_rows, S5_CH), F32)]
    scratch = [pltpu.VMEM((n_rows, 2 * S5_STATE), F32), pltpu.VMEM((n_rows, S5_CH), F32),
               pltpu.VMEM((S5_CH, 2 * S5_STATE), BF16), pltpu.VMEM((2 * S5_STATE, S5_CH), BF16)]
    if segmented:
        in_specs.append(layer(s0))
        args.append(s0)
        scratch.append(pltpu.VMEM((S5_STEPS, 2 * S5_STATE), F32))
    else:
        out_specs.append(pl.BlockSpec((1, 2, S5_ROWS, 2 * S5_STATE), lambda g: (g, 0, 0, 0)))
        out_shape.append(jax.ShapeDtypeStruct((n_grp, 2, S5_ROWS, 2 * S5_STATE), F32))
    return pl.pallas_call(
        functools.partial(_s5_kernel, segmented=segmented),
        grid=(n_grp,),
        in_specs=in_specs,
        out_specs=out_specs,
        out_shape=out_shape,
        scratch_shapes=scratch,
        compiler_params=_cparams(("arbitrary",)),
        name="s5_seg" if segmented else "s5_ctx",
    )(*args)


def _s5_params(a_re, a_im, log_dt, b_re, b_im, c_re, c_im):
    dt = jnp.exp(log_dt)[..., None]
    mag = jnp.exp(a_re * dt)
    lb_re = mag * jnp.cos(a_im * dt)
    lb_im = mag * jnp.sin(a_im * dt)
    den = a_re * a_re + a_im * a_im
    nr, ni = lb_re - 1.0, lb_im
    k_re = (nr * a_re + ni * a_im) / den
    k_im = (ni * a_re - nr * a_im) / den
    bb_re = k_re[..., None] * b_re - k_im[..., None] * b_im
    bb_im = k_re[..., None] * b_im + k_im[..., None] * b_re
    to_gc_p = lambda m: m.transpose(0, 1, 2, 4, 3).reshape(DEPTH, 2, S5_CH, S5_P)
    bb = jnp.concatenate([to_gc_p(bb_re), to_gc_p(bb_im)], axis=3)
    to_gp_c = lambda m: m.transpose(0, 1, 2, 4, 3).reshape(DEPTH, 2, S5_STATE, S5_GROUP)
    cc = jnp.concatenate([to_gp_c(c_re), -to_gp_c(c_im)], axis=2)
    cc = jnp.pad(cc, ((0, 0), (0, 0), (0, 0), (0, LANES - S5_GROUP)))
    lam = jnp.concatenate([lb_re.reshape(DEPTH, 2, 1, S5_STATE), lb_im.reshape(DEPTH, 2, 1, S5_STATE)], axis=3)
    return bb, cc, lam


def _outproj_kernel(*refs, n_x):
    x_refs = refs[:n_x]
    (hyc_ref, hyl_ref, mlc_ref, mll_ref, s5c_ref, s5l_ref, w_ref, mod_ref, g_ref,
     rw_ref, rb_ref, xn_ref, h2e_ref, best_ref, rank_ref, cnt_ref, wb, cnt_s) = refs[n_x:]
    step = pl.program_id(0)

    @pl.when(step == 0)
    def _():
        wb[...] = w_ref[0].astype(BF16)
        cnt_s[...] = jnp.zeros_like(cnt_s)

    is_ctx = step < T_CTX // TM
    pick = lambda c_ref, l_ref: jnp.where(is_ctx, c_ref[...], l_ref[...]).astype(BF16)
    mod = mod_ref[0, 0]
    a, b = HY_CH, HY_CH + ML_W
    mix = (jnp.dot(pick(hyc_ref, hyl_ref), wb[0:a, :], preferred_element_type=F32)
           + jnp.dot(pick(mlc_ref, mll_ref), wb[a:b, :], preferred_element_type=F32)
           + jnp.dot(pick(s5c_ref, s5l_ref), wb[b:, :], preferred_element_type=F32))
    xn = _x_tile(step, x_refs) + mod[2:3] * mix
    xn_ref[...] = xn
    h2 = _rms(xn, g_ref[0]) * (1.0 + mod[4:5]) + mod[3:4]
    h2e_ref[:, 0:D] = h2
    h_hi, h_lo = _split_bf16(h2)
    r_hi, r_lo = _split_bf16(rw_ref[...])
    logits = (lax.dot_general(r_hi, h_hi, _NT, preferred_element_type=F32)
              + (lax.dot_general(r_hi, h_lo, _NT, preferred_element_type=F32)
                 + lax.dot_general(r_lo, h_hi, _NT, preferred_element_type=F32)))
    ex = jnp.exp(logits - jnp.max(logits, axis=0, keepdims=True))
    probs = ex / jnp.sum(ex, axis=0, keepdims=True)
    sel = probs + rb_ref[...]
    best = None
    best_score = None
    for g in range(N_GROUPS):
        r = [sel[g * GROUP_SIZE + i:g * GROUP_SIZE + i + 1, :] for i in range(GROUP_SIZE)]
        score = None
        for i in range(GROUP_SIZE):
            for j in range(i + 1, GROUP_SIZE):
                pair = r[i] + r[j]
                score = pair if score is None else jnp.maximum(score, pair)
        if g == 0:
            best, best_score = jnp.zeros_like(score, dtype=jnp.int32), score
        else:
            upd = score > best_score
            best = jnp.where(upd, g, best)
            best_score = jnp.where(upd, score, best_score)
    eid = lax.broadcasted_iota(jnp.int32, (N_EXPERTS, 1), 0)
    masked = jnp.where(eid // GROUP_SIZE == best, sel, -jnp.inf)
    m1 = jnp.max(masked, axis=0, keepdims=True)
    i1 = jnp.min(jnp.where(masked == m1, eid, N_EXPERTS), axis=0, keepdims=True)
    masked2 = jnp.where(eid == i1, -jnp.inf, masked)
    m2 = jnp.max(masked2, axis=0, keepdims=True)
    i2 = jnp.min(jnp.where(masked2 == m2, eid, N_EXPERTS), axis=0, keepdims=True)
    p1 = jnp.sum(jnp.where(eid == i1, probs, 0.0), axis=0, keepdims=True)
    p2 = jnp.sum(jnp.where(eid == i2, probs, 0.0), axis=0, keepdims=True)
    tot = p1 + p2
    comb = jnp.where(eid == i1, p1 / tot, 0.0) + jnp.where(eid == i2, p2 / tot, 0.0)
    comb = jnp.concatenate([comb, jnp.zeros((LANES - N_EXPERTS, comb.shape[1]), F32)], axis=0)
    h2e_ref[:, D:] = comb.T
    gid = lax.broadcasted_iota(jnp.int32, (SUBLANES, 1), 0)
    onehot = (gid == best).astype(F32)
    cum = _cumsum_lanes(onehot, TM, False)
    run = cnt_s[:, 0:1]
    best_ref[...] = best
    rank_ref[...] = jnp.sum(onehot * (cum - onehot + run), axis=0, keepdims=True).astype(jnp.int32)
    cnt_s[...] = jnp.broadcast_to(run + cum[:, TM - 1:TM], cnt_s.shape)
    cnt_ref[...] = cnt_s[...].astype(jnp.int32)


def _outproj(x, y_hy, y_ml, y_s5, w_out, l, mod, g2, rw_t, rb):
    full = lambda a: pl.BlockSpec(a.shape, lambda i: (0,) * a.ndim)
    n_ctx = T_CTX // TM
    ctx = lambda w: pl.BlockSpec((TM, w), lambda i: (jnp.minimum(i, n_ctx - 1), 0))
    lat = lambda w: pl.BlockSpec((TM, w), lambda i: (jnp.maximum(i - n_ctx, 0), 0))
    tok = lambda w: pl.BlockSpec((TM, w), lambda i: (i, 0))
    row = pl.BlockSpec((1, TM), lambda i: (0, i))
    return pl.pallas_call(
        functools.partial(_outproj_kernel, n_x=len(x)),
        grid=(T_ALL // TM,),
        in_specs=_x_specs(x) + [ctx(HY_CH), lat(HY_CH), ctx(ML_W), lat(ML_W), ctx(S5_CH), lat(S5_CH),
                  pl.BlockSpec((1, D, D), lambda i: (l, 0, 0), pipeline_mode=pl.Buffered(1)),
                  _mod_spec(l, TM), _layer_spec((1, D), l), full(rw_t), full(rb)],
        out_specs=[tok(D), tok(D + LANES), row, row, pl.BlockSpec((SUBLANES, LANES), lambda i: (0, 0))],
        out_shape=[jax.ShapeDtypeStruct((T_ALL, D), F32),
                   jax.ShapeDtypeStruct((T_ALL, D + LANES), F32),
                   jax.ShapeDtypeStruct((1, T_ALL), jnp.int32),
                   jax.ShapeDtypeStruct((1, T_ALL), jnp.int32),
                   jax.ShapeDtypeStruct((SUBLANES, LANES), jnp.int32)],
        scratch_shapes=[pltpu.VMEM((D, D), BF16), pltpu.VMEM((SUBLANES, LANES), F32)],
        compiler_params=_cparams(("arbitrary",)),
        name="outproj_router",
    )(*x, *y_hy, *y_ml, *y_s5, w_out, mod, g2, rw_t, rb)


def _gather_rows(idx_ref, idx_base, src_ref, dst_ref, n_rows):
    def body(r8, carry):
        base = pl.multiple_of(r8 * SUBLANES, SUBLANES)
        for k in range(SUBLANES):
            idx = idx_ref[idx_base + base + k]
            dst_ref[pl.ds(base + k, 1), :] = src_ref[pl.ds(idx, 1), :]
        return carry

    lax.fori_loop(0, n_rows // SUBLANES, body, 0)


def _dispatch_kernel(best_ref, rank_ref, cnt_ref, pos_ref, src_ref, tg_ref, nt_ref, off_s):
    tm = TM_MOE
    tiles = jnp.int32(0)
    tile_end = []
    for g in range(N_GROUPS):
        off_s[g] = tiles * tm
        tiles = tiles + (cnt_ref[g, 0] + (tm - 1)) // tm
        tile_end.append(tiles)
    nt_ref[0] = tiles
    for k in range(MOE_SLOTS // tm):
        g = jnp.int32(0)
        for e in tile_end[:-1]:
            g = g + (k >= e).astype(jnp.int32)
        tg_ref[k] = g

    unroll = SUBLANES

    def clear(i8, carry):
        for u in range(unroll):
            src_ref[i8 * unroll + u] = 0
        return carry

    lax.fori_loop(0, MOE_SLOTS // unroll, clear, 0)

    def place(t8, carry):
        for u in range(unroll):
            t = t8 * unroll + u
            p = rank_ref[t] + off_s[best_ref[t]]
            pos_ref[t] = p
            src_ref[p] = t
        return carry

    lax.fori_loop(0, T_ALL // unroll, place, 0)


def _dispatch(best, rank, cnt):
    smem = pl.BlockSpec(memory_space=pltpu.SMEM)
    i32 = lambda n: jax.ShapeDtypeStruct((n,), jnp.int32)
    return pl.pallas_call(
        _dispatch_kernel,
        in_specs=[smem, smem, smem],
        out_specs=[smem, smem, smem, smem],
        out_shape=[i32(T_ALL), i32(MOE_SLOTS), i32(MOE_SLOTS // TM_MOE), i32(1)],
        scratch_shapes=[pltpu.SMEM((N_GROUPS,), jnp.int32)],
        name="moe_dispatch",
    )(best.reshape(T_ALL), rank.reshape(T_ALL), cnt)


def _moe_kernel(src_ref, tg_ref, nt_ref, h_ref, wg_ref, wu_ref, wd_ref, ys_ref, g_a, x_a, g_b, x_b, acc):
    i = pl.program_id(0)
    j = pl.program_id(1)
    tm = TM_MOE
    quarter = tm // GROUP_SIZE
    n_live = nt_ref[0]

    @pl.when(jnp.logical_and(i == 0, j == 0))
    def _():
        _gather_rows(src_ref, 0, h_ref, g_a, tm)
        x_a[...] = g_a[:, 0:D].astype(BF16)

    def step(cur_g, cur_x, nxt_g, nxt_x):
        @pl.when(j == 0)
        def _():
            acc[...] = jnp.zeros_like(acc)

        e = tg_ref[i] * GROUP_SIZE + j
        lane = lax.broadcasted_iota(jnp.int32, (1, LANES), 1)
        wg = wg_ref[0, 0].astype(BF16)
        wu = wu_ref[0, 0].astype(BF16)
        wd = wd_ref[0, 0].astype(BF16)
        sub = MOE_SUB
        for s in range(tm // sub):
            rs = slice(s * sub, (s + 1) * sub)
            hb = cur_x[rs, :]
            ce = jnp.sum(jnp.where(lane == e, cur_g[rs, D:], 0.0), axis=1, keepdims=True)
            part = None
            for c0 in range(0, D_EXPERT, MOE_HID):
                cs = slice(c0, c0 + MOE_HID)
                hid = (_silu(jnp.dot(hb, wg[:, cs], preferred_element_type=F32))
                       * jnp.dot(hb, wu[:, cs], preferred_element_type=F32))
                p = jnp.dot((hid * ce).astype(BF16), wd[cs, :], preferred_element_type=F32)
                part = p if part is None else part + p
            acc[rs, :] += part

        r0 = pl.multiple_of(j * quarter, quarter)
        base = jnp.minimum(i + 1, n_live - 1) * tm + r0
        for k in range(quarter):
            nxt_g[pl.ds(r0 + k, 1), :] = h_ref[pl.ds(src_ref[base + k], 1), :]
        nxt_x[pl.ds(r0, quarter), :] = nxt_g[pl.ds(r0, quarter), 0:D].astype(BF16)

        @pl.when(j == GROUP_SIZE - 1)
        def _():
            for c in range(D // LANES):
                ys_ref[pl.ds(c, tm, stride=D // LANES), :] = acc[:, c * LANES:(c + 1) * LANES]

    live = i < n_live

    @pl.when(jnp.logical_and(live, i % 2 == 0))
    def _():
        step(g_a, x_a, g_b, x_b)

    @pl.when(jnp.logical_and(live, i % 2 == 1))
    def _():
        step(g_b, x_b, g_a, x_a)

    @pl.when(jnp.logical_and(jnp.logical_not(live), j == GROUP_SIZE - 1))
    def _():
        ys_ref[...] = jnp.zeros_like(ys_ref)


def _moe(h2e, src, tile_group, n_tiles, wg, wu, wd, l):
    tm = TM_MOE

    def w_map(i, j, src, tg, nt):
        live = i < nt[0]
        ii = jnp.minimum(i, nt[0] - 1)
        return (l, tg[ii] * GROUP_SIZE + jnp.where(live, j, GROUP_SIZE - 1), 0, 0)

    return pl.pallas_call(
        _moe_kernel,
        grid_spec=pltpu.PrefetchScalarGridSpec(
            num_scalar_prefetch=3,
            grid=(MOE_SLOTS // tm, GROUP_SIZE),
            in_specs=[pl.BlockSpec(memory_space=pltpu.VMEM),
                      pl.BlockSpec((1, 1, D, D_EXPERT), w_map),
                      pl.BlockSpec((1, 1, D, D_EXPERT), w_map),
                      pl.BlockSpec((1, 1, D_EXPERT, D), w_map)],
            out_specs=pl.BlockSpec((tm * (D // LANES), LANES), lambda i, j, src, tg, nt: (i, 0)),
            scratch_shapes=[pltpu.VMEM((tm, D + LANES), F32), pltpu.VMEM((tm, D), BF16),
                            pltpu.VMEM((tm, D + LANES), F32), pltpu.VMEM((tm, D), BF16),
                            pltpu.VMEM((tm, D), F32)]),
        out_shape=jax.ShapeDtypeStruct((MOE_SLOTS * (D // LANES), LANES), F32),
        compiler_params=_cparams(("arbitrary", "arbitrary")),
        name="moe_experts",
    )(src, tile_group, n_tiles, h2e, wg, wu, wd)


def _combine_kernel(pos_ref, ys_ref, xn_ref, mod_ref, fg_ref, *rest, final):
    step = pl.program_id(0)
    if final:
        yc_ref, yl_ref, gbuf = rest
    else:
        out_ref, gbuf = rest
    n_lt = D // LANES

    def body(r8, carry):
        base = pl.multiple_of(r8 * SUBLANES, SUBLANES)
        for k in range(SUBLANES):
            src = pl.multiple_of(pos_ref[step * TM + base + k] * n_lt, n_lt)
            gbuf[pl.ds(pl.multiple_of((base + k) * n_lt, n_lt), n_lt), :] = ys_ref[pl.ds(src, n_lt), :]
        return carry

    lax.fori_loop(0, TM // SUBLANES, body, 0)
    moe = jnp.concatenate([gbuf[pl.ds(c, TM, stride=n_lt), :] for c in range(n_lt)], axis=1)
    out = xn_ref[...] + mod_ref[0, 0][5:6] * moe
    if final:
        y = _rms(out, fg_ref[...])

        @pl.when(step < T_CTX // TM)
        def _():
            yc_ref[...] = y

        @pl.when(step >= T_CTX // TM)
        def _():
            yl_ref[...] = y
    else:
        out_ref[...] = out


def _combine(pos, ys, xn, mod, fg, l, final):
    spec = pl.BlockSpec((TM, D), lambda i, pos: (i, 0))
    n_ctx = T_CTX // TM
    if final:
        out_specs = [pl.BlockSpec((TM, D), lambda i, pos: (jnp.minimum(i, n_ctx - 1), 0)),
                     pl.BlockSpec((TM, D), lambda i, pos: (jnp.maximum(i - n_ctx, 0), 0))]
        out_shape = [jax.ShapeDtypeStruct((T_CTX, D), F32), jax.ShapeDtypeStruct((T_LAT, D), F32)]
    else:
        out_specs = [spec]
        out_shape = [jax.ShapeDtypeStruct((T_ALL, D), F32)]
    return pl.pallas_call(
        functools.partial(_combine_kernel, final=final),
        grid_spec=pltpu.PrefetchScalarGridSpec(
            num_scalar_prefetch=1,
            grid=(T_ALL // TM,),
            in_specs=[pl.BlockSpec(memory_space=pltpu.VMEM),
                      spec,
                      _mod_spec(l, TM),
                      pl.BlockSpec((1, D), lambda i, pos: (0, 0))],
            out_specs=out_specs,
            scratch_shapes=[pltpu.VMEM((TM * (D // LANES), LANES), F32)]),
        out_shape=out_shape,
        compiler_params=_cparams(("arbitrary",)),
        name="moe_combine",
    )(pos, ys, xn, mod, fg)


@functools.lru_cache(None)
def _pos_embed():
    rows = L_LAT // GRID_W
    r = np.repeat(np.arange(rows, dtype=np.float64), GRID_W)
    col = np.tile(np.arange(GRID_W, dtype=np.float64), rows)
    quarter = D // 4
    freq = np.exp(-math.log(POS_BASE) * np.arange(quarter, dtype=np.float64) / quarter)
    ar = r[:, None] * freq[None]
    ac = col[:, None] * freq[None]
    emb = np.concatenate([np.sin(ar), np.cos(ar), np.sin(ac), np.cos(ac)], axis=-1)
    return emb.astype(np.float32)


def _to_time_major_ctx(a):
    c = a.shape[-1]
    a = a.reshape(2, S5_ROWS, L_CTX, c).transpose(0, 2, 1, 3)
    return a.reshape(2, L_CTX * S5_ROWS, c)


def _from_time_major_ctx(a):
    c = a.shape[-1]
    a = a.reshape(2, L_CTX, S5_ROWS, c).transpose(0, 2, 1, 3)
    return a.reshape(T_CTX, c)


def _to_time_major_lat(a):
    c = a.shape[-1]
    a = a.reshape(N_LAT_SEQ, 4, S5_STEPS, c).transpose(2, 1, 0, 3)
    return a.reshape(1, S5_STEPS * S5_ROWS, c)


def _from_time_major_lat(a):
    c = a.shape[-1]
    a = a.reshape(S5_STEPS, 4, N_LAT_SEQ, c).transpose(2, 1, 0, 3)
    return a.reshape(T_LAT, c)


def kernel(x_prompt, x_sample, c, state_mlstm_C, state_mlstm_n, state_mlstm_m, state_s5_re, state_s5_im, c_ctx, w_ada, b_ada, norm1_g, norm2_g, final_g, w_in, w_out, hy_short, hy_fw1, hy_fb1, hy_fw2, hy_fb2, hy_fw3, hy_log_decay, hy_bias, ml_short, ml_gate_bias, ml_norm_g, s5_a_re, s5_a_im, s5_log_dt, s5_b_re, s5_b_im, s5_c_re, s5_c_im, s5_d, s5_w_glu, router_w, router_b, moe_w_gate, moe_w_up, moe_w_down):
    x = (x_prompt.reshape(T_CTX, D), x_sample.reshape(T_LAT, D), jnp.asarray(_pos_embed()))
    w_in_t = jnp.swapaxes(w_in, 1, 2)
    cc =jnp.concatenate([c_ctx[None], c, jnp.zeros((8 - 1 - N_LAT_SEQ, D), F32)], axis=0)
    mod = _ada(cc, w_ada, b_ada).reshape(DEPTH, 8, 6, D)
    rw_t = router_w.T
    rb = router_b.reshape(N_EXPERTS, 1)
    fg = final_g.reshape(1, D)
    lat_blk = T_CTX // L_LAT
    g1 = norm1_g.reshape(DEPTH, 1, D)
    g2 = norm2_g.reshape(DEPTH, 1, D)

    w1p = jnp.pad(hy_fw1, ((0, 0), (0, LANES - HY_EMB), (0, 0)))
    b1 = hy_fb1.reshape(DEPTH, 1, HY_FILTER_W)
    b2 = hy_fb2.reshape(DEPTH, 1, HY_FILTER_W)
    ld = hy_log_decay.reshape(DEPTH, 1, 4 * HY_CH)
    hy_spec = {L: _hy_filter(L, w1p, b1, hy_fw2, b2, hy_fw3, ld) for L in (L_CTX, L_LAT)}
    gb = jnp.pad(ml_gate_bias.reshape(DEPTH, 1, 16), ((0, 0), (0, 0), (0, GATE_PAD - 16)))
    gbt = ml_gate_bias.reshape(DEPTH, 16, 1)
    ng = ml_norm_g.reshape(DEPTH, 1, ML_W)
    ml_state = (state_mlstm_C.reshape(N_LAT_SEQ, DEPTH, 2 * ML_HEADS, ML_DH, ML_DH),
                state_mlstm_n.reshape(N_LAT_SEQ, DEPTH, 2 * ML_HEADS, ML_DH),
                jnp.broadcast_to(state_mlstm_m.reshape(N_LAT_SEQ, DEPTH, 2 * ML_HEADS, 1),
                                 (N_LAT_SEQ, DEPTH, 2 * ML_HEADS, LANES)))
    bb, cc_s5, lam = _s5_params(s5_a_re, s5_a_im, s5_log_dt, s5_b_re, s5_b_im, s5_c_re, s5_c_im)
    dsk = s5_d.reshape(DEPTH, 1, S5_CH)
    wglu = s5_w_glu.astype(BF16)
    s0 = jnp.concatenate([state_s5_re.reshape(N_LAT_SEQ, DEPTH, 2, S5_STATE),
                          state_s5_im.reshape(N_LAT_SEQ, DEPTH, 2, S5_STATE)], axis=-1)
    s0 = jnp.tile(s0.transpose(1, 2, 0, 3), (1, 1, 4, 1))

    new_n, new_m, new_re, new_im = [], [], [], []
    prev_c, c_all = [], None
    y_prompt = y_sample = None
    for l in range(DEPTH):
        u_hy, qk, v, o, u_s5, gates, gates_t = _inproj(x, g1, mod, w_in_t, l)

        y_hy = [_hyena(u_hy, L, n_seq, blk0, hy_short, hy_bias, *hy_spec[L], l)
                for L, n_seq, blk0 in ((L_CTX, N_CTX_SEQ, 0), (L_LAT, N_LAT_SEQ, lat_blk))]

        yc, c_all, nc_, mc_ = _mlstm(qk, v, o, gates, gates_t, L_CTX, N_CTX_SEQ, 0, ml_short, gb, gbt, ng, None, l,
                                     emit_state=True, prev_c=prev_c)
        prev_c = [c_all[:, k] for k in range(l + 1)] if l + 1 < DEPTH else None
        (yl,) = _mlstm(qk, v, o, gates, gates_t, L_LAT, N_LAT_SEQ, lat_blk, ml_short, gb, gbt, ng, ml_state, l)
        y_ml = (yc, yl)
        new_n.append(nc_.reshape(N_CTX_SEQ, 2, ML_HEADS, ML_DH))
        new_m.append(mc_[:, :, 0].reshape(N_CTX_SEQ, 2, ML_HEADS))

        ys_c, fin = _s5(_to_time_major_ctx(u_s5[:T_CTX]), bb, cc_s5, lam, dsk, wglu, None, l)
        (ys_l,) = _s5(_to_time_major_lat(u_s5[T_CTX:]), bb, cc_s5, lam, dsk, wglu, s0, l)
        y_s5 = (_from_time_major_ctx(ys_c), _from_time_major_lat(ys_l))
        fin = fin.transpose(0, 2, 1, 3).reshape(N_CTX_SEQ, 2, 2 * S5_STATE)
        new_re.append(fin[..., :S5_STATE].reshape(N_CTX_SEQ, 2, S5_G, S5_P))
        new_im.append(fin[..., S5_STATE:].reshape(N_CTX_SEQ, 2, S5_G, S5_P))

        xn, h2e, best, rank, cnt = _outproj(x, y_hy, y_ml, y_s5, w_out, l, mod, g2, rw_t, rb)
        pos, src, tile_group, n_tiles = _dispatch(best, rank, cnt)
        ys = _moe(h2e, src, tile_group, n_tiles, moe_w_gate, moe_w_up, moe_w_down, l)
        res = _combine(pos, ys, xn, mod, fg, l, l == DEPTH - 1)
        if l == DEPTH - 1:
            y_prompt = res[0].reshape(N_CTX_SEQ, L_CTX, D)
            y_sample = res[1].reshape(N_LAT_SEQ, L_LAT, D)
        else:
            x = (res[0],)

    new_c = c_all.reshape(N_CTX_SEQ, DEPTH, 2, ML_HEADS, ML_DH, ML_DH)
    return (y_prompt, y_sample, new_c, jnp.stack(new_n, axis=1), jnp.stack(new_m, axis=1),
            jnp.stack(new_re, axis=1), jnp.stack(new_im, axis=1))
```
